```python
import jax, jax.numpy as jnp
from jax import lax
import numpy as np

D_MODEL = 2048
BATCH = 1
SEQ = 8192
DEPTH = 1

D_MIX = D_MODEL
RET_HEADS = 4
RET_HEAD_DIM = 256
RET_WIDTH = RET_HEADS * RET_HEAD_DIM
SSM_WIDTH = D_MIX - RET_WIDTH
SSM_HEAD_DIM = 64
SSM_HEADS = SSM_WIDTH // SSM_HEAD_DIM
SSM_GROUPS = 2
SSM_STATE = 128
CONV_WIDTH = 4
XBC_WIDTH = SSM_WIDTH + 2 * SSM_GROUPS * SSM_STATE
D_IN_PROJ = 4 * RET_WIDTH + SSM_WIDTH + XBC_WIDTH + SSM_HEADS
CHUNK = 128
ROPE_BASE = 10000.0
N_EXPERTS = 32
TOP_K = 4
D_FF = D_MODEL
SWIGLU_LIMIT = 7.0
SWIGLU_ALPHA = 1.702
MOE_BLOCK = 128
EPS = 1e-6

kernel_name = "hybrid_retention_ssd_moe_layer"

F32 = jnp.float32


def rmsnorm(x, w):
    xf = x.astype(F32)
    y = xf * lax.rsqrt(jnp.mean(xf * xf, axis=-1, keepdims=True) + EPS)
    return (y * w.astype(F32)).astype(x.dtype)


def apply_rope(t, positions):
    half = t.shape[-1] // 2
    inv_freq = ROPE_BASE ** (-jnp.arange(half, dtype=F32) / half)
    ang = positions.astype(F32)[..., None] * inv_freq
    cos = jnp.cos(ang)[:, :, None, :]
    sin = jnp.sin(ang)[:, :, None, :]
    t1, t2 = t[..., :half], t[..., half:]
    return jnp.concatenate([t1 * cos - t2 * sin, t2 * cos + t1 * sin], axis=-1)


def retention_chunkwise(q, k, v):
    b, L, h, dk = q.shape
    dv = v.shape[-1]
    c = L // CHUNK
    log_gamma = jnp.log1p(-jnp.exp2(-5.0 - jnp.arange(h, dtype=F32)))
    idx = jnp.arange(CHUNK, dtype=F32)
    rel = idx[:, None] - idx[None, :]
    causal = rel >= 0
    intra = jnp.where(causal, jnp.exp(log_gamma[:, None, None] * jnp.where(causal, rel, 0.0)), 0.0)
    q_decay = jnp.exp(log_gamma[:, None] * (idx + 1.0))
    k_decay = jnp.exp(log_gamma[:, None] * (CHUNK - 1.0 - idx))
    chunk_decay = jnp.exp(log_gamma * CHUNK)
    qc = q.reshape(b, c, CHUNK, h, dk)
    kc = k.reshape(b, c, CHUNK, h, dk)
    vc = v.reshape(b, c, CHUNK, h, dv)
    scores = jnp.einsum('bclhd,bcshd->bchls', qc, kc) * intra
    inner = jnp.einsum('bchls,bcshv->bclhv', scores, vc)
    kv = jnp.einsum('bclhd,bclhv,hl->bchdv', kc, vc, k_decay)

    def step(state, kv_c):
        return chunk_decay[None, :, None, None] * state + kv_c, state

    _, prev = lax.scan(step, jnp.zeros((b, h, dk, dv), F32), jnp.moveaxis(kv, 1, 0))
    prev = jnp.moveaxis(prev, 0, 1)
    cross = jnp.einsum('bclhd,bchdv,hl->bclhv', qc, prev, q_decay)
    return (inner + cross).reshape(b, L, h, dv)


def ssd_chunked(x, dt, A, Bm, Cm):
    b, L, H, P = x.shape
    G, N = Bm.shape[2], Bm.shape[3]
    E = H // G
    c = L // CHUNK
    x = x.reshape(b, c, CHUNK, G, E, P)
    dt = dt.reshape(b, c, CHUNK, G, E)
    Bm = Bm.reshape(b, c, CHUNK, G, N)
    Cm = Cm.reshape(b, c, CHUNK, G, N)
    a_cum = jnp.cumsum(dt * A.reshape(G, E), axis=2)
    a_t = jnp.moveaxis(a_cum, 2, -1)
    seg = a_t[..., :, None] - a_t[..., None, :]
    tril = jnp.tril(jnp.ones((CHUNK, CHUNK), dtype=bool))
    decay_ls = jnp.exp(jnp.where(tril, seg, -jnp.inf))
    xdt = x * dt[..., None]
    cb = jnp.einsum('bclgn,bcsgn->bcgls', Cm, Bm)
    y_diag = jnp.einsum('bcgels,bcsgep->bclgep', cb[:, :, :, None] * decay_ls, xdt)
    decay_states = jnp.exp(a_cum[:, :, -1:] - a_cum)
    states = jnp.einsum('bclgn,bclge,bclgep->bcgepn', Bm, decay_states, xdt)
    chunk_decay = jnp.exp(a_cum[:, :, -1])

    def step(state, inp):
        s_c, d_c = inp
        return d_c[..., None, None] * state + s_c, state

    _, prev = lax.scan(step, jnp.zeros((b, G, E, P, N), F32),
                       (jnp.moveaxis(states, 1, 0), jnp.moveaxis(chunk_decay, 1, 0)))
    prev = jnp.moveaxis(prev, 0, 1)
    y_off = jnp.einsum('bclgn,bcgepn,bclge->bclgep', Cm, prev, jnp.exp(a_cum))
    return (y_diag + y_off).reshape(b, L, H, P)


def causal_depthwise_conv(u, w, bias):
    K, ch = w.shape
    out = lax.conv_general_dilated(u, w[:, None, :], window_strides=(1,), padding=[(K - 1, 0)],
                                   dimension_numbers=('NWC', 'WIO', 'NWC'), feature_group_count=ch)
    return out + bias


def hybrid_mixer(u, positions, w_in, conv_w, conv_b, dt_bias, a_log, d_skip, ssm_norm_w, w_out):
    b, L, _ = u.shape
    proj = u @ w_in
    cuts = np.cumsum([RET_WIDTH, RET_WIDTH, RET_WIDTH, RET_WIDTH, SSM_WIDTH, XBC_WIDTH]).tolist()
    q, k, v, g, z, xbc, dt_raw = jnp.split(proj, cuts, axis=-1)

    q = apply_rope(q.astype(F32).reshape(b, L, RET_HEADS, RET_HEAD_DIM), positions)
    k = apply_rope(k.astype(F32).reshape(b, L, RET_HEADS, RET_HEAD_DIM), positions) * (RET_HEAD_DIM ** -0.5)
    v = v.astype(F32).reshape(b, L, RET_HEADS, RET_HEAD_DIM)
    o = retention_chunkwise(q, k, v)
    o = o * lax.rsqrt(jnp.mean(o * o, axis=-1, keepdims=True) + EPS)
    ret_out = o.reshape(b, L, RET_WIDTH) * jax.nn.silu(g.astype(F32))

    xbc = jax.nn.silu(causal_depthwise_conv(xbc, conv_w, conv_b)).astype(F32)
    xs, Bm, Cm = jnp.split(xbc, [SSM_WIDTH, SSM_WIDTH + SSM_GROUPS * SSM_STATE], axis=-1)
    xs = xs.reshape(b, L, SSM_HEADS, SSM_HEAD_DIM)
    Bm = Bm.reshape(b, L, SSM_GROUPS, SSM_STATE)
    Cm = Cm.reshape(b, L, SSM_GROUPS, SSM_STATE)
    dt = jax.nn.softplus(dt_raw.astype(F32) + dt_bias.astype(F32))
    A = -jnp.exp(a_log.astype(F32))
    y = ssd_chunked(xs, dt, A, Bm, Cm) + d_skip.astype(F32)[:, None] * xs
    y = y.reshape(b, L, SSM_WIDTH) * jax.nn.silu(z.astype(F32))
    yg = y.reshape(b, L, SSM_GROUPS, SSM_WIDTH // SSM_GROUPS)
    yg = yg * lax.rsqrt(jnp.mean(yg * yg, axis=-1, keepdims=True) + EPS)
    ssm_out = yg.reshape(b, L, SSM_WIDTH) * ssm_norm_w.astype(F32)

    mixed = jnp.concatenate([ret_out, ssm_out], axis=-1).astype(u.dtype)
    return mixed @ w_out


def moe_ffn(h, w_router, b_router, w_gate_up, b_gate_up, w_down, b_down):
    b, L, Dm = h.shape
    T = b * L
    xt = h.reshape(T, Dm)
    logits = (xt @ w_router + b_router).astype(F32)
    top_logits, top_idx = lax.top_k(logits, TOP_K)
    top_w = jax.nn.softmax(top_logits, axis=-1).astype(h.dtype)

    n_assign = T * TOP_K
    n_blocks = -(-(n_assign + N_EXPERTS * (MOE_BLOCK - 1)) // MOE_BLOCK)
    n_rows = n_blocks * MOE_BLOCK
    expert_flat = top_idx.reshape(-1).astype(jnp.int32)
    token_flat = jnp.arange(n_assign, dtype=jnp.int32) // TOP_K
    weight_flat = top_w.reshape(-1)
    order = jnp.argsort(expert_flat, stable=True)
    sorted_expert = expert_flat[order]
    counts = jnp.bincount(expert_flat, length=N_EXPERTS).astype(jnp.int32)
    padded = (counts + MOE_BLOCK - 1) // MOE_BLOCK * MOE_BLOCK
    padded_end = jnp.cumsum(padded)
    padded_start = padded_end - padded
    start = jnp.cumsum(counts) - counts
    rank = jnp.arange(n_assign, dtype=jnp.int32) - start[sorted_expert]
    dest = padded_start[sorted_expert] + rank
    row_token = jnp.full((n_rows,), T, jnp.int32).at[dest].set(token_flat[order])
    row_weight = jnp.zeros((n_rows,), h.dtype).at[dest].set(weight_flat[order])
    block_expert = jnp.minimum(
        jnp.searchsorted(padded_end, jnp.arange(n_blocks, dtype=jnp.int32) * MOE_BLOCK, side='right'),
        N_EXPERTS - 1).astype(jnp.int32)
    x_rows = jnp.concatenate([xt, jnp.zeros((1, Dm), xt.dtype)], axis=0)[row_token]
    x_rows = x_rows.reshape(n_blocks, MOE_BLOCK, Dm)

    def expert_block(args):
        xb, e = args
        gu = xb @ w_gate_up[e] + b_gate_up[e]
        gate = jnp.minimum(gu[..., ::2], SWIGLU_LIMIT)
        up = jnp.clip(gu[..., 1::2], -SWIGLU_LIMIT, SWIGLU_LIMIT)
        act = (up + 1.0) * (gate * jax.nn.sigmoid(gate * SWIGLU_ALPHA))
        return act @ w_down[e] + b_down[e]

    y_rows = lax.map(expert_block, (x_rows, block_expert)).reshape(n_rows, Dm)
    y = jnp.zeros((T + 1, Dm), h.dtype).at[row_token].add(y_rows * row_weight[:, None])
    return y[:T].reshape(b, L, Dm)


def setup_inputs(seed: int = 0) -> dict:
    key = jax.random.key(seed)
    ks = jax.random.split(key, 20)
    nrm = jax.random.normal
    x = nrm(ks[0], (BATCH, SEQ, D_MODEL), F32)
    positions = jnp.broadcast_to(jnp.arange(SEQ, dtype=jnp.int32), (BATCH, SEQ))
    ln_mix_w = 1.0 + 0.02 * nrm(ks[1], (DEPTH, D_MODEL), F32)
    w_in = nrm(ks[2], (DEPTH, D_MODEL, D_IN_PROJ), F32) * D_MODEL ** -0.5
    conv_w = nrm(ks[3], (DEPTH, CONV_WIDTH, XBC_WIDTH), F32) * CONV_WIDTH ** -0.5
    conv_b = 0.01 * nrm(ks[4], (DEPTH, XBC_WIDTH), F32)
    dt0 = jnp.exp(jax.random.uniform(ks[5], (DEPTH, SSM_HEADS), F32, np.log(1e-3), np.log(1e-1)))
    dt_bias = dt0 + jnp.log(-jnp.expm1(-dt0))
    a_log = jnp.log(jax.random.uniform(ks[6], (DEPTH, SSM_HEADS), F32, 1.0, 16.0))
    d_skip = 1.0 + 0.02 * nrm(ks[7], (DEPTH, SSM_HEADS), F32)
    ssm_norm_w = 1.0 + 0.02 * nrm(ks[8], (DEPTH, SSM_WIDTH), F32)
    w_out = nrm(ks[9], (DEPTH, D_MIX, D_MODEL), F32) * D_MIX ** -0.5
    ln_ffn_w = 1.0 + 0.02 * nrm(ks[10], (DEPTH, D_MODEL), F32)
    w_router = nrm(ks[11], (DEPTH, D_MODEL, N_EXPERTS), F32) * D_MODEL ** -0.5
    b_router = 0.01 * nrm(ks[12], (DEPTH, N_EXPERTS), F32)
    w_gate_up = nrm(ks[13], (DEPTH, N_EXPERTS, D_MODEL, 2 * D_FF), F32) * D_MODEL ** -0.5
    b_gate_up = 0.01 * nrm(ks[14], (DEPTH, N_EXPERTS, 2 * D_FF), F32)
    w_down = nrm(ks[15], (DEPTH, N_EXPERTS, D_FF, D_MODEL), F32) * D_FF ** -0.5
    b_down = 0.01 * nrm(ks[16], (DEPTH, N_EXPERTS, D_MODEL), F32)
    ln_final_w = 1.0 + 0.02 * nrm(ks[17], (D_MODEL,), F32)
    return {"x": x, "positions": positions, "ln_mix_w": ln_mix_w, "w_in": w_in, "conv_w": conv_w,
            "conv_b": conv_b, "dt_bias": dt_bias, "a_log": a_log, "d_skip": d_skip,
            "ssm_norm_w": ssm_norm_w, "w_out": w_out, "ln_ffn_w": ln_ffn_w, "w_router": w_router,
            "b_router": b_router, "w_gate_up": w_gate_up, "b_gate_up": b_gate_up, "w_down": w_down,
            "b_down": b_down, "ln_final_w": ln_final_w}


def reference(x, positions, ln_mix_w, w_in, conv_w, conv_b, dt_bias, a_log, d_skip, ssm_norm_w, w_out,
              ln_ffn_w, w_router, b_router, w_gate_up, b_gate_up, w_down, b_down, ln_final_w):
    h = x
    for layer in range(DEPTH):
        u = rmsnorm(h, ln_mix_w[layer])
        h = h + hybrid_mixer(u, positions, w_in[layer], conv_w[layer], conv_b[layer], dt_bias[layer],
                             a_log[layer], d_skip[layer], ssm_norm_w[layer], w_out[layer])
        u = rmsnorm(h, ln_ffn_w[layer])
        h = h + moe_ffn(u, w_router[layer], b_router[layer], w_gate_up[layer], b_gate_up[layer],
                        w_down[layer], b_down[layer])
    return rmsnorm(h, ln_final_w)
```

```python
import functools

import numpy as np
import jax
import jax.numpy as jnp
from jax import lax
from jax.experimental import pallas as pl
from jax.experimental.pallas import tpu as pltpu

F32 = jnp.float32
BF16 = jnp.bfloat16

D_MODEL = 2048
RET_HEADS = 4
RET_HEAD_DIM = 256
RET_WIDTH = RET_HEADS * RET_HEAD_DIM
SSM_WIDTH = D_MODEL - RET_WIDTH
SSM_HEAD_DIM = 64
SSM_HEADS = SSM_WIDTH // SSM_HEAD_DIM
SSM_GROUPS = 2
SSM_STATE = 128
CONV_WIDTH = 4
XBC_WIDTH = SSM_WIDTH + 2 * SSM_GROUPS * SSM_STATE
D_IN_PROJ = 4 * RET_WIDTH + SSM_WIDTH + XBC_WIDTH + SSM_HEADS
ROPE_BASE = 10000.0
N_EXPERTS = 32
TOP_K = 4
D_FF = D_MODEL
SWIGLU_LIMIT = 7.0
SWIGLU_ALPHA = 1.702
EPS = 1e-6

LANES = 128
VMEM_LIMIT = 56 * 1024 * 1024

RET_CHUNK = 256
SSD_CHUNK = 128
ROW_BLK = 256
ITEM_ROWS = 2048
FF_TILE = 128
ROW_SLABS = D_MODEL // LANES


def _params(sem, **kw):
    return pltpu.CompilerParams(dimension_semantics=sem, vmem_limit_bytes=VMEM_LIMIT, **kw)


def _dot(a, b):
    return jnp.dot(a, b, preferred_element_type=F32)


def _dot_nt(a, b):
    return lax.dot_general(a, b, (((1,), (1,)), ((), ())), preferred_element_type=F32)


def _dot_tn(a, b):
    return lax.dot_general(a, b, (((0,), (0,)), ((), ())), preferred_element_type=F32)


def _split3(x):
    hi = x.astype(BF16)
    r = x - hi.astype(F32)
    mid = r.astype(BF16)
    lo = (r - mid.astype(F32)).astype(BF16)
    return hi, mid, lo


def _dot_exact_rhs01(x, m01):
    hi, mid, lo = _split3(x)
    return _dot(hi, m01) + _dot(mid, m01) + _dot(lo, m01)


def _dot_exact_lhs01(m01, x):
    hi, mid, lo = _split3(x)
    return _dot(m01, hi) + _dot(m01, mid) + _dot(m01, lo)


def _silu(x):
    return x * jax.nn.sigmoid(x)


def _inproj_kernel(x_ref, lnw_ref, w_ref, o_ref, u_ref):
    @pl.when(pl.program_id(1) == 0)
    def _():
        x = x_ref[...]
        ms = jnp.mean(x * x, axis=-1, keepdims=True)
        u_ref[...] = (x * lax.rsqrt(ms + EPS) * lnw_ref[...]).astype(BF16)

    o_ref[...] = _dot(u_ref[...], w_ref[...].astype(BF16))


def _inproj(x2, ln_w, w_in):
    T = x2.shape[0]
    tm, tn = 1024, 512
    return pl.pallas_call(
        _inproj_kernel,
        grid=(T // tm, pl.cdiv(D_IN_PROJ, tn)),
        in_specs=[pl.BlockSpec((tm, D_MODEL), lambda i, j: (i, 0)),
                  pl.BlockSpec((1, D_MODEL), lambda i, j: (0, 0)),
                  pl.BlockSpec((D_MODEL, tn), lambda i, j: (0, j))],
        out_specs=pl.BlockSpec((tm, tn), lambda i, j: (i, j)),
        out_shape=jax.ShapeDtypeStruct((T, D_IN_PROJ), F32),
        scratch_shapes=[pltpu.VMEM((tm, D_MODEL), BF16)],
        compiler_params=_params(("parallel", "arbitrary")),
        name="inproj",
    )(x2, ln_w.reshape(1, D_MODEL), w_in)


def _retention_tables():
    C = RET_CHUNK
    h = np.arange(RET_HEADS, dtype=np.float64)
    log_gamma = np.log1p(-np.exp2(-5.0 - h))
    idx = np.arange(C, dtype=np.float64)
    rel = idx[:, None] - idx[None, :]
    intra = np.where(rel >= 0, np.exp(log_gamma[:, None, None] * np.maximum(rel, 0.0)), 0.0)
    q_decay = np.exp(log_gamma[:, None] * (idx + 1.0))
    k_decay = np.exp(log_gamma[:, None] * (C - 1.0 - idx))
    chunk_decay = np.exp(log_gamma * C)
    qd = np.broadcast_to(q_decay[:, :, None], (RET_HEADS, C, RET_HEAD_DIM))
    kd = np.broadcast_to(k_decay[:, :, None], (RET_HEADS, C, RET_HEAD_DIM))
    return (jnp.asarray(intra, F32), jnp.asarray(qd, F32), jnp.asarray(kd, F32),
            [float(c) for c in chunk_decay])


def _retention_kernel(chunk_decay, pos_ref, invf_ref, q_ref, k_ref, v_ref, g_ref,
                      intra_ref, qd_ref, kd_ref, o_ref, state_ref):
    @pl.when(pl.program_id(0) == 0)
    def _():
        state_ref[...] = jnp.zeros_like(state_ref)

    half = RET_HEAD_DIM // 2
    ang = pos_ref[...] * invf_ref[...]
    cos = jnp.cos(ang)
    sin = jnp.sin(ang)

    def rope(t):
        t1, t2 = t[:, :half], t[:, half:]
        return jnp.concatenate([t1 * cos - t2 * sin, t2 * cos + t1 * sin], axis=-1)

    for h in range(RET_HEADS):
        sl = slice(h * RET_HEAD_DIM, (h + 1) * RET_HEAD_DIM)
        q = rope(q_ref[:, sl])
        k = rope(k_ref[:, sl]) * (RET_HEAD_DIM ** -0.5)
        v = v_ref[:, sl].astype(BF16)
        state = state_ref[h]
        scores = _dot_nt(q.astype(BF16), k.astype(BF16)) * intra_ref[h]
        inner = _dot(scores.astype(BF16), v)
        cross = _dot((q * qd_ref[h]).astype(BF16), state.astype(BF16))
        state_ref[h] = chunk_decay[h] * state + _dot_tn((k * kd_ref[h]).astype(BF16), v)
        o = inner + cross
        o = o * lax.rsqrt(jnp.mean(o * o, axis=-1, keepdims=True) + EPS)
        o_ref[:, sl] = (o * _silu(g_ref[:, sl])).astype(o_ref.dtype)


def _retention(proj, pos_col, inv_freq):
    T = proj.shape[0]
    C = RET_CHUNK
    intra, qd, kd, chunk_decay = _retention_tables()
    col = lambda j: pl.BlockSpec((C, RET_WIDTH), lambda c, j=j: (c, j))
    const3 = lambda shape: pl.BlockSpec(shape, lambda c: (0, 0, 0))
    return pl.pallas_call(
        functools.partial(_retention_kernel, chunk_decay),
        grid=(T // C,),
        in_specs=[pl.BlockSpec((C, 1), lambda c: (c, 0)),
                  pl.BlockSpec((1, RET_HEAD_DIM // 2), lambda c: (0, 0)),
                  col(0), col(1), col(2), col(3),
                  const3((RET_HEADS, C, C)),
                  const3((RET_HEADS, C, RET_HEAD_DIM)),
                  const3((RET_HEADS, C, RET_HEAD_DIM))],
        out_specs=pl.BlockSpec((C, RET_WIDTH), lambda c: (c, 0)),
        out_shape=jax.ShapeDtypeStruct((T, RET_WIDTH), BF16),
        scratch_shapes=[pltpu.VMEM((RET_HEADS, RET_HEAD_DIM, RET_HEAD_DIM), F32)],
        compiler_params=_params(("arbitrary",)),
        name="retention",
    )(pos_col, inv_freq, proj, proj, proj, proj, intra, qd, kd)


def _ssd_kernel(xs0_ref, xs1_ref, bc_ref, z_ref, dt_ref, convw_ref, convb_ref, dtb_ref, a_ref,
                dskip_ref, normw_ref, expand_ref, o_ref, ext_ref, state_ref):
    C = SSD_CHUNK
    HW = SSM_WIDTH // SSM_GROUPS
    CARRY = 8

    @pl.when(pl.program_id(0) == 0)
    def _():
        ext_ref[0:CARRY, :] = jnp.zeros((CARRY, XBC_WIDTH), F32)
        state_ref[...] = jnp.zeros_like(state_ref)

    ext_ref[CARRY:CARRY + C, 0:HW] = xs0_ref[...]
    ext_ref[CARRY:CARRY + C, HW:2 * HW] = xs1_ref[...]
    ext_ref[CARRY:CARRY + C, 2 * HW:3 * HW] = bc_ref[...]
    conv = convb_ref[...]
    for k in range(CONV_WIDTH):
        off = CARRY - (CONV_WIDTH - 1) + k
        conv = conv + convw_ref[k:k + 1, :] * ext_ref[off:off + C, :]
    ext_ref[0:CARRY, :] = ext_ref[C:C + CARRY, :]
    xbc = _silu(conv)
    xs = xbc[:, :SSM_WIDTH]

    lane = lax.broadcasted_iota(jnp.int32, (1, LANES), 1)
    dt_raw = jnp.where(lane < SSM_HEADS, dt_ref[...], 0.0) + dtb_ref[...]
    dt = jnp.maximum(dt_raw, 0.0) + jnp.log1p(jnp.exp(-jnp.abs(dt_raw)))
    dta = dt * a_ref[...]

    row = lax.broadcasted_iota(jnp.int32, (C, C), 0)
    colm = lax.broadcasted_iota(jnp.int32, (C, C), 1)
    tril = row >= colm
    a_cum = _dot_exact_lhs01(jnp.where(tril, 1.0, 0.0).astype(BF16), dta)
    a_cum_t = a_cum.T

    expand = expand_ref[...]
    a_exp = _dot_exact_rhs01(a_cum, expand)
    dt_exp = _dot_exact_rhs01(dt, expand)
    a_last = a_exp[C - 1:C, :]
    decay_in = jnp.exp(a_exp)
    decay_out = jnp.exp(a_last - a_exp)
    chunk_decay = jnp.exp(a_last)
    xdt = xs * dt_exp

    lane2 = lax.broadcasted_iota(jnp.int32, (1, LANES), 1)
    lo_head = lane2 < SSM_HEAD_DIM
    ys = []
    for g in range(SSM_GROUPS):
        gs = slice(g * HW, (g + 1) * HW)
        b_g = xbc[:, SSM_WIDTH + g * SSM_STATE:SSM_WIDTH + (g + 1) * SSM_STATE].astype(BF16)
        c0 = SSM_WIDTH + SSM_GROUPS * SSM_STATE
        c_g = xbc[:, c0 + g * SSM_STATE:c0 + (g + 1) * SSM_STATE].astype(BF16)
        cb = _dot_nt(c_g, b_g)
        state = state_ref[g]
        y_off = _dot(c_g, state.astype(BF16)) * decay_in[:, gs]
        xw = (xdt[:, gs] * decay_out[:, gs]).astype(BF16)
        state_ref[g] = chunk_decay[:, gs] * state + _dot_tn(b_g, xw)
        slabs = []
        for s in range(HW // LANES):
            xd = xdt[:, g * HW + s * LANES:g * HW + (s + 1) * LANES]
            acc = None
            for e in range(2):
                hh = g * (SSM_HEADS // SSM_GROUPS) + 2 * s + e
                seg = a_cum[:, hh:hh + 1] - a_cum_t[hh:hh + 1, :]
                m = cb * jnp.exp(jnp.where(tril, seg, -jnp.inf))
                xm = jnp.where(lo_head if e == 0 else jnp.logical_not(lo_head), xd, 0.0)
                part = _dot(m.astype(BF16), xm.astype(BF16))
                acc = part if acc is None else acc + part
            slabs.append(acc)
        ys.append(jnp.concatenate(slabs, axis=-1) + y_off)
    y = jnp.concatenate(ys, axis=-1) + dskip_ref[...] * xs
    y = y * _silu(z_ref[...])
    outs = []
    for g in range(SSM_GROUPS):
        yg = y[:, g * HW:(g + 1) * HW]
        outs.append(yg * lax.rsqrt(jnp.mean(yg * yg, axis=-1, keepdims=True) + EPS))
    o_ref[...] = (jnp.concatenate(outs, axis=-1) * normw_ref[...]).astype(o_ref.dtype)


def _ssd(proj, conv_w, conv_b, dt_bias, a_log, d_skip, ssm_norm_w):
    T = proj.shape[0]
    C = SSD_CHUNK
    HW = SSM_WIDTH // SSM_GROUPS
    xbc0 = (4 * RET_WIDTH + SSM_WIDTH) // HW
    dt0 = (D_IN_PROJ - SSM_HEADS) // LANES
    pad = lambda v: jnp.zeros((1, LANES), F32).at[0, :SSM_HEADS].set(v.astype(F32))
    a_neg = pad(-jnp.exp(a_log.astype(F32)))
    expand_np = np.zeros((LANES, SSM_WIDTH), np.float32)
    for hh in range(SSM_HEADS):
        expand_np[hh, hh * SSM_HEAD_DIM:(hh + 1) * SSM_HEAD_DIM] = 1.0
    expand = jnp.asarray(expand_np, BF16)
    dskip_exp = jnp.repeat(d_skip.astype(F32), SSM_HEAD_DIM).reshape(1, SSM_WIDTH)
    const = lambda shape: pl.BlockSpec(shape, lambda c: (0, 0))
    return pl.pallas_call(
        _ssd_kernel,
        grid=(T // C,),
        in_specs=[pl.BlockSpec((C, HW), lambda c: (c, xbc0)),
                  pl.BlockSpec((C, HW), lambda c: (c, xbc0 + 1)),
                  pl.BlockSpec((C, HW), lambda c: (c, xbc0 + 2)),
                  pl.BlockSpec((C, SSM_WIDTH), lambda c: (c, 4 * RET_WIDTH // SSM_WIDTH)),
                  pl.BlockSpec((C, LANES), lambda c: (c, dt0)),
                  const((CONV_WIDTH, XBC_WIDTH)), const((1, XBC_WIDTH)),
                  const((1, LANES)), const((1, LANES)),
                  const((1, SSM_WIDTH)), const((1, SSM_WIDTH)),
                  const((LANES, SSM_WIDTH))],
        out_specs=pl.BlockSpec((C, SSM_WIDTH), lambda c: (c, 0)),
        out_shape=jax.ShapeDtypeStruct((T, SSM_WIDTH), BF16),
        scratch_shapes=[pltpu.VMEM((C + 8, XBC_WIDTH), F32),
                        pltpu.VMEM((SSM_GROUPS, SSM_STATE, HW), F32)],
        compiler_params=_params(("arbitrary",)),
        name="ssd",
    )(proj, proj, proj, proj, proj, conv_w, conv_b.reshape(1, XBC_WIDTH), pad(dt_bias), a_neg,
      dskip_exp, ssm_norm_w.reshape(1, SSM_WIDTH), expand)


def _outproj_kernel(ret_ref, ssm_ref, w_ref, x_ref, lnw_ref, wr_ref, br_ref, h_ref, u_ref, lg_ref):
    k = pl.program_id(1)
    m = jnp.where(k < pl.num_programs(1) // 2, ret_ref[...], ssm_ref[...])
    part = _dot(m, w_ref[...].astype(BF16))

    @pl.when(k == 0)
    def _():
        h_ref[...] = x_ref[...] + part

    @pl.when(k > 0)
    def _():
        h_ref[...] += part

    @pl.when(k == pl.num_programs(1) - 1)
    def _():
        h = h_ref[...]
        u = h * lax.rsqrt(jnp.mean(h * h, axis=-1, keepdims=True) + EPS) * lnw_ref[...]
        u_ref[...] = u
        uh, um, ul = _split3(u)
        wh, wm, wl = _split3(wr_ref[...])
        lg = (_dot(uh, wh) + (_dot(uh, wm) + _dot(um, wh))
              + (_dot(uh, wl) + _dot(um, wm) + _dot(ul, wh)))
        lg_ref[...] = lg + br_ref[...]


def _outproj(ret, ssm, w_out, x2, ln_w, w_router, b_router):
    T = x2.shape[0]
    tm, tk = 512, 512
    kh = RET_WIDTH // tk
    return pl.pallas_call(
        _outproj_kernel,
        grid=(T // tm, D_MODEL // tk),
        in_specs=[pl.BlockSpec((tm, tk), lambda i, k: (i, jnp.minimum(k, kh - 1))),
                  pl.BlockSpec((tm, tk), lambda i, k: (i, jnp.maximum(k - kh, 0))),
                  pl.BlockSpec((tk, D_MODEL), lambda i, k: (k, 0)),
                  pl.BlockSpec((tm, D_MODEL), lambda i, k: (i, 0)),
                  pl.BlockSpec((1, D_MODEL), lambda i, k: (0, 0)),
                  pl.BlockSpec((D_MODEL, N_EXPERTS), lambda i, k: (0, 0)),
                  pl.BlockSpec((1, N_EXPERTS), lambda i, k: (0, 0))],
        out_specs=[pl.BlockSpec((tm, D_MODEL), lambda i, k: (i, 0)),
                   pl.BlockSpec((tm, D_MODEL), lambda i, k: (i, 0)),
                   pl.BlockSpec((tm, N_EXPERTS), lambda i, k: (i, 0))],
        out_shape=[jax.ShapeDtypeStruct((T, D_MODEL), F32),
                   jax.ShapeDtypeStruct((T, D_MODEL), F32),
                   jax.ShapeDtypeStruct((T, N_EXPERTS), F32)],
        compiler_params=_params(("parallel", "arbitrary")),
        name="outproj",
    )(ret, ssm, w_out, x2, ln_w.reshape(1, D_MODEL), w_router, b_router.reshape(1, N_EXPERTS))


def _dispatch_kernel(tokens_per_step, n_rows, dest_ref, wflat_ref, u_hbm, rows_in_hbm,
                     rows_hbm, rw_ref, sem):
    del rows_in_hbm
    i = pl.program_id(0)

    @pl.when(i == 0)
    def _():
        def zero(r, c):
            rw_ref[r] = jnp.float32(0.0)
            return c
        lax.fori_loop(0, n_rows, zero, 0)

    def copy(tok, d):
        return pltpu.make_async_copy(u_hbm.at[tok], rows_hbm.at[d], sem)

    def issue(t, c):
        tok = i * tokens_per_step + t
        for k in range(TOP_K):
            d = dest_ref[tok * TOP_K + k]
            rw_ref[d] = wflat_ref[tok * TOP_K + k]
            copy(tok, d).start()
        return c

    lax.fori_loop(0, tokens_per_step, issue, 0)

    def drain(t, c):
        for k in range(TOP_K):
            copy(0, 0).wait()
        return c

    lax.fori_loop(0, tokens_per_step, drain, 0)


def _dispatch(dest, wflat, u3, n_rows):
    T = u3.shape[0]
    tps = 512
    rows0 = jnp.zeros((n_rows, ROW_SLABS, LANES), F32)
    grid_spec = pltpu.PrefetchScalarGridSpec(
        num_scalar_prefetch=2,
        grid=(T // tps,),
        in_specs=[pl.BlockSpec(memory_space=pl.ANY), pl.BlockSpec(memory_space=pl.ANY)],
        out_specs=[pl.BlockSpec(memory_space=pl.ANY), pl.BlockSpec(memory_space=pltpu.SMEM)],
        scratch_shapes=[pltpu.SemaphoreType.DMA(())],
    )
    return pl.pallas_call(
        functools.partial(_dispatch_kernel, tps, n_rows),
        grid_spec=grid_spec,
        out_shape=[jax.ShapeDtypeStruct((n_rows, ROW_SLABS, LANES), F32),
                   jax.ShapeDtypeStruct((n_rows,), F32)],
        input_output_aliases={3: 0},
        compiler_params=_params(("arbitrary",)),
        name="dispatch",
    )(dest, wflat, u3, rows0)


def _expert_kernel(n_ff, item_e, item_row, item_nsub, item_nzero, item_live,
                   x_hbm, rw_hbm, wga_ref, wgb_ref, bga_ref, bgb_ref, wda_ref, wdb_ref, bd_ref,
                   y_hbm, xbuf, rwbuf, acc, wdm, sem_in, sem_out):
    del item_e, item_live
    i = pl.program_id(0)
    f = pl.program_id(1)
    nsub = item_nsub[i]
    nzero = item_nzero[i]
    row0 = item_row[i]
    SUB = ROW_BLK

    def hbm_rows(j):
        return pl.ds(pl.multiple_of(row0 + j * SUB, SUB), SUB)

    def vmem_rows(j):
        return pl.ds(pl.multiple_of(j * SUB, SUB), SUB)

    def x_copy(j):
        return pltpu.make_async_copy(x_hbm.at[hbm_rows(j)], xbuf.at[vmem_rows(j)], sem_in)

    def rw_copy(j):
        return pltpu.make_async_copy(rw_hbm.at[hbm_rows(j)], rwbuf.at[vmem_rows(j)], sem_in)

    def y_copy(j):
        return pltpu.make_async_copy(acc.at[vmem_rows(j)], y_hbm.at[hbm_rows(j)], sem_out)

    @pl.when(f == 0)
    def _():
        def start(j, c):
            x_copy(j).start()
            rw_copy(j).start()
            return c
        lax.fori_loop(0, nsub, start, 0)

        def wait(j, c):
            x_copy(j).wait()
            rw_copy(j).wait()
            return c
        lax.fori_loop(0, nsub, wait, 0)

    wa = wga_ref[...].astype(BF16)
    wb = wgb_ref[...].astype(BF16)
    for s in range(ROW_SLABS):
        wdm[s, pl.ds(0, FF_TILE, stride=2), :] = wda_ref[:, s * LANES:(s + 1) * LANES]
        wdm[s, pl.ds(1, FF_TILE, stride=2), :] = wdb_ref[:, s * LANES:(s + 1) * LANES]
    wd = jnp.concatenate([wdm[s] for s in range(ROW_SLABS)], axis=-1).astype(BF16)
    lane = lax.broadcasted_iota(jnp.int32, (1, 2 * FF_TILE), 1)
    even = (lane % 2) == 0
    bga = bga_ref[...]
    bgb = bgb_ref[...]
    W2 = 2 * FF_TILE

    def sub(j, c):
        r = pl.multiple_of(j * SUB, SUB)
        xs = xbuf[pl.ds(r, SUB), :]
        ga = _dot(xs, wa) + bga
        gb = _dot(xs, wb) + bgb
        gate = jnp.where(even, ga, pltpu.roll(gb, 1, 1))
        up = jnp.where(even, pltpu.roll(ga, W2 - 1, 1), gb)
        gate = jnp.minimum(gate, SWIGLU_LIMIT)
        up = jnp.clip(up, -SWIGLU_LIMIT, SWIGLU_LIMIT)
        act = (up + 1.0) * (gate * jax.nn.sigmoid(gate * SWIGLU_ALPHA))
        part = _dot(act.astype(BF16), wd)

        @pl.when(f == 0)
        def _():
            acc[pl.ds(r, SUB), :] = part

        @pl.when(f > 0)
        def _():
            acc[pl.ds(r, SUB), :] += part
        return c

    lax.fori_loop(0, nsub, sub, 0)

    @pl.when(f == n_ff - 1)
    def _():
        def fin(j, c):
            r = pl.multiple_of(j * SUB, SUB)
            acc[pl.ds(r, SUB), :] = (acc[pl.ds(r, SUB), :] + bd_ref[...]) * rwbuf[pl.ds(r, SUB), :]
            y_copy(j).start()
            return c
        lax.fori_loop(0, nsub, fin, 0)

        def wait(j, c):
            y_copy(j).wait()
            return c
        lax.fori_loop(0, nsub, wait, 0)

    @pl.when(jnp.logical_and(f == n_ff - 1, nzero > 0))
    def _():
        acc[0:SUB, :] = jnp.zeros((SUB, D_MODEL), F32)

        def zero_copy(j):
            return pltpu.make_async_copy(acc.at[pl.ds(0, SUB)], y_hbm.at[hbm_rows(j)], sem_out)

        def start(j, c):
            zero_copy(j).start()
            return c
        lax.fori_loop(0, nzero, start, 0)

        def wait(j, c):
            zero_copy(j).wait()
            return c
        lax.fori_loop(0, nzero, wait, 0)


def _experts(x_rows, row_w, items, w_gate_up, b_gate_up, w_down, b_down, n_items):
    n_rows = x_rows.shape[0]
    n_ff = (D_FF // 2) // FF_TILE
    W2 = 2 * FF_TILE
    item_e, item_row, item_nsub, item_nzero, item_live = items

    def ff(i, f, live):
        return jnp.where(live[i] > 0, f, n_ff - 1)

    grid_spec = pltpu.PrefetchScalarGridSpec(
        num_scalar_prefetch=5,
        grid=(n_items, n_ff),
        in_specs=[
            pl.BlockSpec(memory_space=pl.ANY),
            pl.BlockSpec(memory_space=pl.ANY),
            pl.BlockSpec((None, D_MODEL, W2), lambda i, f, e, r, n, z, lv: (e[i], 0, ff(i, f, lv))),
            pl.BlockSpec((None, D_MODEL, W2), lambda i, f, e, r, n, z, lv: (e[i], 0, n_ff + ff(i, f, lv))),
            pl.BlockSpec((None, 1, W2), lambda i, f, e, r, n, z, lv: (e[i], 0, ff(i, f, lv))),
            pl.BlockSpec((None, 1, W2), lambda i, f, e, r, n, z, lv: (e[i], 0, n_ff + ff(i, f, lv))),
            pl.BlockSpec((None, FF_TILE, D_MODEL), lambda i, f, e, r, n, z, lv: (e[i], ff(i, f, lv), 0)),
            pl.BlockSpec((None, FF_TILE, D_MODEL), lambda i, f, e, r, n, z, lv: (e[i], n_ff + ff(i, f, lv), 0)),
            pl.BlockSpec((None, 1, D_MODEL), lambda i, f, e, r, n, z, lv: (e[i], 0, 0)),
        ],
        out_specs=pl.BlockSpec(memory_space=pl.ANY),
        scratch_shapes=[pltpu.VMEM((ITEM_ROWS, D_MODEL), BF16),
                        pltpu.VMEM((ITEM_ROWS, 1), F32),
                        pltpu.VMEM((ITEM_ROWS, D_MODEL), F32),
                        pltpu.VMEM((ROW_SLABS, W2, LANES), F32),
                        pltpu.SemaphoreType.DMA(()),
                        pltpu.SemaphoreType.DMA(())],
    )
    return pl.pallas_call(
        functools.partial(_expert_kernel, n_ff),
        grid_spec=grid_spec,
        out_shape=jax.ShapeDtypeStruct((n_rows, D_MODEL), F32),
        compiler_params=_params(("arbitrary", "arbitrary")),
        name="experts",
    )(item_e, item_row, item_nsub, item_nzero, item_live, x_rows, row_w,
      w_gate_up, w_gate_up, b_gate_up.reshape(N_EXPERTS, 1, 2 * D_FF), b_gate_up.reshape(N_EXPERTS, 1, 2 * D_FF),
      w_down, w_down, b_down.reshape(N_EXPERTS, 1, D_MODEL))


def _combine_kernel(tm, n_steps, dest_ref, y_hbm, h_ref, lnw_ref, o_ref, buf, sem):
    i = pl.program_id(0)

    def copy(step, slot, t, k):
        d = dest_ref[(step * tm + t) * TOP_K + k]
        return pltpu.make_async_copy(y_hbm.at[d], buf.at[slot, k, t], sem.at[slot])

    def issue(step, slot):
        def body(t, c):
            for k in range(TOP_K):
                copy(step, slot, t, k).start()
            return c
        lax.fori_loop(0, tm, body, 0)

    @pl.when(i == 0)
    def _():
        issue(0, 0)

    @pl.when(i + 1 < n_steps)
    def _():
        issue(i + 1, (i + 1) % 2)

    slot = i % 2

    def drain(t, c):
        for k in range(TOP_K):
            pltpu.make_async_copy(y_hbm.at[0], buf.at[slot, k, 0], sem.at[slot]).wait()
        return c
    lax.fori_loop(0, tm, drain, 0)

    h = h_ref[...]
    for k in range(TOP_K):
        h = h + buf[slot, k]
    ms = jnp.mean(jnp.mean(h * h, axis=2, keepdims=True), axis=1, keepdims=True)
    o_ref[...] = h * lax.rsqrt(ms + EPS) * lnw_ref[...]


def _combine(dest, y3, h3, ln_w):
    T = h3.shape[0]
    tm = 128
    n_steps = T // tm
    grid_spec = pltpu.PrefetchScalarGridSpec(
        num_scalar_prefetch=1,
        grid=(n_steps,),
        in_specs=[pl.BlockSpec(memory_space=pl.ANY),
                  pl.BlockSpec((tm, ROW_SLABS, LANES), lambda i, d: (i, 0, 0)),
                  pl.BlockSpec((1, ROW_SLABS, LANES), lambda i, d: (0, 0, 0))],
        out_specs=pl.BlockSpec((tm, ROW_SLABS, LANES), lambda i, d: (i, 0, 0)),
        scratch_shapes=[pltpu.VMEM((2, TOP_K, tm, ROW_SLABS, LANES), F32),
                        pltpu.SemaphoreType.DMA((2,))],
    )
    return pl.pallas_call(
        functools.partial(_combine_kernel, tm, n_steps),
        grid_spec=grid_spec,
        out_shape=jax.ShapeDtypeStruct((T, ROW_SLABS, LANES), F32),
        compiler_params=_params(("arbitrary",)),
        name="combine",
    )(dest, y3, h3, ln_w.reshape(1, ROW_SLABS, LANES))


def _route(logits, n_rows, n_items):
    T = logits.shape[0]
    top_logits, top_idx = lax.top_k(logits, TOP_K)
    top_w = jax.nn.softmax(top_logits, axis=-1)
    e_flat = top_idx.reshape(-1).astype(jnp.int32)
    onehot = (e_flat[:, None] == jnp.arange(N_EXPERTS, dtype=jnp.int32)[None, :]).astype(jnp.int32)
    csum = jnp.cumsum(onehot, axis=0)
    rank = jnp.sum(onehot * (csum - onehot), axis=1)
    counts = csum[-1]
    padded = (counts + ROW_BLK - 1) // ROW_BLK * ROW_BLK
    pend = jnp.cumsum(padded)
    pstart = pend - padded
    dest = (pstart[e_flat] + rank).astype(jnp.int32)

    per_e = (padded + ITEM_ROWS - 1) // ITEM_ROWS
    iend = jnp.cumsum(per_e)
    istart = iend - per_e
    ii = jnp.arange(n_items, dtype=jnp.int32)
    total = iend[-1]
    live = (ii < total).astype(jnp.int32)
    ic = jnp.minimum(ii, total - 1)
    ie = jnp.minimum(jnp.searchsorted(iend, ic, side='right'), N_EXPERTS - 1).astype(jnp.int32)
    within = ic - istart[ie]
    irow = (pstart[ie] + within * ITEM_ROWS).astype(jnp.int32)
    insub = jnp.minimum((padded[ie] - within * ITEM_ROWS) // ROW_BLK, ITEM_ROWS // ROW_BLK).astype(jnp.int32)
    insub = insub * live
    tail_rows = n_rows - pend[-1]
    tt = ii - total
    tail = jnp.logical_and(tt >= 0, tt * ITEM_ROWS < tail_rows)
    inzero = jnp.where(tail, jnp.minimum(tail_rows - tt * ITEM_ROWS, ITEM_ROWS) // ROW_BLK, 0).astype(jnp.int32)
    irow = jnp.where(tail, pend[-1] + tt * ITEM_ROWS, irow).astype(jnp.int32)
    return dest, top_w.reshape(-1).astype(F32), (ie, irow, insub, inzero, live)


def kernel(x, positions, ln_mix_w, w_in, conv_w, conv_b, dt_bias, a_log, d_skip, ssm_norm_w, w_out,
           ln_ffn_w, w_router, b_router, w_gate_up, b_gate_up, w_down, b_down, ln_final_w):
    B, L, _ = x.shape
    T = B * L
    assert B == 1 and T % 1024 == 0
    x2 = x.reshape(T, D_MODEL)
    half = RET_HEAD_DIM // 2
    inv_freq = (ROPE_BASE ** (-jnp.arange(half, dtype=F32) / half)).reshape(1, half)
    pos_col = positions.reshape(T, 1).astype(F32)

    proj = _inproj(x2, ln_mix_w[0], w_in[0])
    ret = _retention(proj, pos_col, inv_freq)
    ssm = _ssd(proj, conv_w[0], conv_b[0], dt_bias[0], a_log[0], d_skip[0], ssm_norm_w[0])
    h, u, logits = _outproj(ret, ssm, w_out[0], x2, ln_ffn_w[0], w_router[0], b_router[0])

    n_rows = -(-(T * TOP_K + N_EXPERTS * (ROW_BLK - 1)) // ROW_BLK) * ROW_BLK
    n_items = N_EXPERTS + 1 + n_rows // ITEM_ROWS
    dest, wflat, items = _route(logits, n_rows, n_items)

    x_rows3, row_w = _dispatch(dest, wflat, u.reshape(T, ROW_SLABS, LANES), n_rows)
    x_rows = x_rows3.reshape(n_rows, D_MODEL).astype(BF16)
    y_rows = _experts(x_rows, row_w.reshape(n_rows, 1), items, w_gate_up[0], b_gate_up[0],
                      w_down[0], b_down[0], n_items)
    out3 = _combine(dest, y_rows.reshape(n_rows, ROW_SLABS, LANES), h.reshape(T, ROW_SLABS, LANES),
                    ln_final_w)
    return out3.reshape(B, L, D_MODEL)
```

```python
import functools

import numpy as np
import jax
import jax.numpy as jnp
from jax import lax
from jax.experimental import pallas as pl
from jax.experimental.pallas import tpu as pltpu

F32 = jnp.float32
BF16 = jnp.bfloat16

D_MODEL = 2048
RET_HEADS = 4
RET_HEAD_DIM = 256
RET_WIDTH = RET_HEADS * RET_HEAD_DIM
SSM_WIDTH = D_MODEL - RET_WIDTH
SSM_HEAD_DIM = 64
SSM_HEADS = SSM_WIDTH // SSM_HEAD_DIM
SSM_GROUPS = 2
SSM_STATE = 128
CONV_WIDTH = 4
XBC_WIDTH = SSM_WIDTH + 2 * SSM_GROUPS * SSM_STATE
D_IN_PROJ = 4 * RET_WIDTH + SSM_WIDTH + XBC_WIDTH + SSM_HEADS
ROPE_BASE = 10000.0
N_EXPERTS = 32
TOP_K = 4
D_FF = D_MODEL
SWIGLU_LIMIT = 7.0
SWIGLU_ALPHA = 1.702
EPS = 1e-6

LANES = 128
VMEM_LIMIT = 56 * 1024 * 1024

RET_CHUNK = 256
SSD_CHUNK = 128
ROW_BLK = 256
ITEM_ROWS = 2048
FF_TILE = 128
ROW_SLABS = D_MODEL // LANES


def _params(sem, **kw):
    return pltpu.CompilerParams(dimension_semantics=sem, vmem_limit_bytes=VMEM_LIMIT, **kw)


def _dot(a, b):
    return jnp.dot(a, b, preferred_element_type=F32)


def _dot_nt(a, b):
    return lax.dot_general(a, b, (((1,), (1,)), ((), ())), preferred_element_type=F32)


def _dot_tn(a, b):
    return lax.dot_general(a, b, (((0,), (0,)), ((), ())), preferred_element_type=F32)


def _split3(x):
    hi = x.astype(BF16)
    r = x - hi.astype(F32)
    mid = r.astype(BF16)
    lo = (r - mid.astype(F32)).astype(BF16)
    return hi, mid, lo


def _dot_exact_rhs01(x, m01):
    hi, mid, lo = _split3(x)
    return _dot(hi, m01) + _dot(mid, m01) + _dot(lo, m01)


def _dot_exact_lhs01(m01, x):
    hi, mid, lo = _split3(x)
    return _dot(m01, hi) + _dot(m01, mid) + _dot(m01, lo)


def _silu(x):
    return x * jax.nn.sigmoid(x)


def _inproj_kernel(x_ref, lnw_ref, w_ref, o_ref, u_ref):
    @pl.when(pl.program_id(1) == 0)
    def _():
        x = x_ref[...]
        ms = jnp.mean(x * x, axis=-1, keepdims=True)
        u_ref[...] = (x * lax.rsqrt(ms + EPS) * lnw_ref[...]).astype(BF16)

    o_ref[...] = _dot(u_ref[...], w_ref[...].astype(BF16))


def _inproj(x2, ln_w, w_in):
    T = x2.shape[0]
    tm, tn = 1024, 512
    return pl.pallas_call(
        _inproj_kernel,
        grid=(T // tm, pl.cdiv(D_IN_PROJ, tn)),
        in_specs=[pl.BlockSpec((tm, D_MODEL), lambda i, j: (i, 0)),
                  pl.BlockSpec((1, D_MODEL), lambda i, j: (0, 0)),
                  pl.BlockSpec((D_MODEL, tn), lambda i, j: (0, j))],
        out_specs=pl.BlockSpec((tm, tn), lambda i, j: (i, j)),
        out_shape=jax.ShapeDtypeStruct((T, D_IN_PROJ), F32),
        scratch_shapes=[pltpu.VMEM((tm, D_MODEL), BF16)],
        compiler_params=_params(("parallel", "arbitrary")),
        name="inproj",
    )(x2, ln_w.reshape(1, D_MODEL), w_in)


def _retention_tables():
    C = RET_CHUNK
    h = np.arange(RET_HEADS, dtype=np.float64)
    log_gamma = np.log1p(-np.exp2(-5.0 - h))
    idx = np.arange(C, dtype=np.float64)
    rel = idx[:, None] - idx[None, :]
    intra = np.where(rel >= 0, np.exp(log_gamma[:, None, None] * np.maximum(rel, 0.0)), 0.0)
    q_decay = np.exp(log_gamma[:, None] * (idx + 1.0))
    k_decay = np.exp(log_gamma[:, None] * (C - 1.0 - idx))
    chunk_decay = np.exp(log_gamma * C)
    qd = np.broadcast_to(q_decay[:, :, None], (RET_HEADS, C, RET_HEAD_DIM))
    kd = np.broadcast_to(k_decay[:, :, None], (RET_HEADS, C, RET_HEAD_DIM))
    return (jnp.asarray(intra, F32), jnp.asarray(qd, F32), jnp.asarray(kd, F32),
            [float(c) for c in chunk_decay])


def _retention_kernel(chunk_decay, pos_ref, invf_ref, q_ref, k_ref, v_ref, g_ref,
                      intra_ref, qd_ref, kd_ref, o_ref, state_ref):
    @pl.when(pl.program_id(0) == 0)
    def _():
        state_ref[...] = jnp.zeros_like(state_ref)

    half = RET_HEAD_DIM // 2
    ang = pos_ref[...] * invf_ref[...]
    cos = jnp.cos(ang)
    sin = jnp.sin(ang)

    def rope(t):
        t1, t2 = t[:, :half], t[:, half:]
        return jnp.concatenate([t1 * cos - t2 * sin, t2 * cos + t1 * sin], axis=-1)

    for h in range(RET_HEADS):
        sl = slice(h * RET_HEAD_DIM, (h + 1) * RET_HEAD_DIM)
        q = rope(q_ref[:, sl])
        k = rope(k_ref[:, sl]) * (RET_HEAD_DIM ** -0.5)
        v = v_ref[:, sl].astype(BF16)
        state = state_ref[h]
        scores = _dot_nt(q.astype(BF16), k.astype(BF16)) * intra_ref[h]
        inner = _dot(scores.astype(BF16), v)
        cross = _dot((q * qd_ref[h]).astype(BF16), state.astype(BF16))
        state_ref[h] = chunk_decay[h] * state + _dot_tn((k * kd_ref[h]).astype(BF16), v)
        o = inner + cross
        o = o * lax.rsqrt(jnp.mean(o * o, axis=-1, keepdims=True) + EPS)
        o_ref[:, sl] = (o * _silu(g_ref[:, sl])).astype(o_ref.dtype)


def _retention(proj, pos_col, inv_freq):
    T = proj.shape[0]
    C = RET_CHUNK
    intra, qd, kd, chunk_decay = _retention_tables()
    col = lambda j: pl.BlockSpec((C, RET_WIDTH), lambda c, j=j: (c, j))
    const3 = lambda shape: pl.BlockSpec(shape, lambda c: (0, 0, 0))
    return pl.pallas_call(
        functools.partial(_retention_kernel, chunk_decay),
        grid=(T // C,),
        in_specs=[pl.BlockSpec((C, 1), lambda c: (c, 0)),
                  pl.BlockSpec((1, RET_HEAD_DIM // 2), lambda c: (0, 0)),
                  col(0), col(1), col(2), col(3),
                  const3((RET_HEADS, C, C)),
                  const3((RET_HEADS, C, RET_HEAD_DIM)),
                  const3((RET_HEADS, C, RET_HEAD_DIM))],
        out_specs=pl.BlockSpec((C, RET_WIDTH), lambda c: (c, 0)),
        out_shape=jax.ShapeDtypeStruct((T, RET_WIDTH), BF16),
        scratch_shapes=[pltpu.VMEM((RET_HEADS, RET_HEAD_DIM, RET_HEAD_DIM), F32)],
        compiler_params=_params(("arbitrary",)),
        name="retention",
    )(pos_col, inv_freq, proj, proj, proj, proj, intra, qd, kd)


def _ssd_kernel(xs0_ref, xs1_ref, bc_ref, z_ref, dt_ref, convw_ref, convb_ref, dtb_ref, a_ref,
                dskip_ref, normw_ref, expand_ref, o_ref, ext_ref, state_ref):
    C = SSD_CHUNK
    HW = SSM_WIDTH // SSM_GROUPS
    CARRY = 8

    @pl.when(pl.program_id(0) == 0)
    def _():
        ext_ref[0:CARRY, :] = jnp.zeros((CARRY, XBC_WIDTH), F32)
        state_ref[...] = jnp.zeros_like(state_ref)

    ext_ref[CARRY:CARRY + C, 0:HW] = xs0_ref[...]
    ext_ref[CARRY:CARRY + C, HW:2 * HW] = xs1_ref[...]
    ext_ref[CARRY:CARRY + C, 2 * HW:3 * HW] = bc_ref[...]
    conv = convb_ref[...]
    for k in range(CONV_WIDTH):
        off = CARRY - (CONV_WIDTH - 1) + k
        conv = conv + convw_ref[k:k + 1, :] * ext_ref[off:off + C, :]
    ext_ref[0:CARRY, :] = ext_ref[C:C + CARRY, :]
    xbc = _silu(conv)
    xs = xbc[:, :SSM_WIDTH]

    lane = lax.broadcasted_iota(jnp.int32, (1, LANES), 1)
    dt_raw = jnp.where(lane < SSM_HEADS, dt_ref[...], 0.0) + dtb_ref[...]
    dt = jnp.maximum(dt_raw, 0.0) + jnp.log1p(jnp.exp(-jnp.abs(dt_raw)))
    dta = dt * a_ref[...]

    row = lax.broadcasted_iota(jnp.int32, (C, C), 0)
    colm = lax.broadcasted_iota(jnp.int32, (C, C), 1)
    tril = row >= colm
    a_cum = _dot_exact_lhs01(jnp.where(tril, 1.0, 0.0).astype(BF16), dta)
    a_cum_t = a_cum.T

    expand = expand_ref[...]
    a_exp = _dot_exact_rhs01(a_cum, expand)
    dt_exp = _dot_exact_rhs01(dt, expand)
    a_last = a_exp[C - 1:C, :]
    decay_in = jnp.exp(a_exp)
    decay_out = jnp.exp(a_last - a_exp)
    chunk_decay = jnp.exp(a_last)
    xdt = xs * dt_exp

    lane2 = lax.broadcasted_iota(jnp.int32, (1, LANES), 1)
    lo_head = lane2 < SSM_HEAD_DIM
    ys = []
    for g in range(SSM_GROUPS):
        gs = slice(g * HW, (g + 1) * HW)
        b_g = xbc[:, SSM_WIDTH + g * SSM_STATE:SSM_WIDTH + (g + 1) * SSM_STATE].astype(BF16)
        c0 = SSM_WIDTH + SSM_GROUPS * SSM_STATE
        c_g = xbc[:, c0 + g * SSM_STATE:c0 + (g + 1) * SSM_STATE].astype(BF16)
        cb = _dot_nt(c_g, b_g)
        state = state_ref[g]
        y_off = _dot(c_g, state.astype(BF16)) * decay_in[:, gs]
        xw = (xdt[:, gs] * decay_out[:, gs]).astype(BF16)
        state_ref[g] = chunk_decay[:, gs] * state + _dot_tn(b_g, xw)
        slabs = []
        for s in range(HW // LANES):
            xd = xdt[:, g * HW + s * LANES:g * HW + (s + 1) * LANES]
            acc = None
            for e in range(2):
                hh = g * (SSM_HEADS // SSM_GROUPS) + 2 * s + e
                seg = a_cum[:, hh:hh + 1] - a_cum_t[hh:hh + 1, :]
                m = cb * jnp.exp(jnp.where(tril, seg, -jnp.inf))
                xm = jnp.where(lo_head if e == 0 else jnp.logical_not(lo_head), xd, 0.0)
                part = _dot(m.astype(BF16), xm.astype(BF16))
                acc = part if acc is None else acc + part
            slabs.append(acc)
        ys.append(jnp.concatenate(slabs, axis=-1) + y_off)
    y = jnp.concatenate(ys, axis=-1) + dskip_ref[...] * xs
    y = y * _silu(z_ref[...])
    outs = []
    for g in range(SSM_GROUPS):
        yg = y[:, g * HW:(g + 1) * HW]
        outs.append(yg * lax.rsqrt(jnp.mean(yg * yg, axis=-1, keepdims=True) + EPS))
    o_ref[...] = (jnp.concatenate(outs, axis=-1) * normw_ref[...]).astype(o_ref.dtype)


def _ssd(proj, conv_w, conv_b, dt_bias, a_log, d_skip, ssm_norm_w):
    T = proj.shape[0]
    C = SSD_CHUNK
    HW = SSM_WIDTH // SSM_GROUPS
    xbc0 = (4 * RET_WIDTH + SSM_WIDTH) // HW
    dt0 = (D_IN_PROJ - SSM_HEADS) // LANES
    pad = lambda v: jnp.zeros((1, LANES), F32).at[0, :SSM_HEADS].set(v.astype(F32))
    a_neg = pad(-jnp.exp(a_log.astype(F32)))
    expand_np = np.zeros((LANES, SSM_WIDTH), np.float32)
    for hh in range(SSM_HEADS):
        expand_np[hh, hh * SSM_HEAD_DIM:(hh + 1) * SSM_HEAD_DIM] = 1.0
    expand = jnp.asarray(expand_np, BF16)
    dskip_exp = jnp.repeat(d_skip.astype(F32), SSM_HEAD_DIM).reshape(1, SSM_WIDTH)
    const = lambda shape: pl.BlockSpec(shape, lambda c: (0, 0))
    return pl.pallas_call(
        _ssd_kernel,
        grid=(T // C,),
        in_specs=[pl.BlockSpec((C, HW), lambda c: (c, xbc0)),
                  pl.BlockSpec((C, HW), lambda c: (c, xbc0 + 1)),
                  pl.BlockSpec((C, HW), lambda c: (c, xbc0 + 2)),
                  pl.BlockSpec((C, SSM_WIDTH), lambda c: (c, 4 * RET_WIDTH // SSM_WIDTH)),
                  pl.BlockSpec((C, LANES), lambda c: (c, dt0)),
                  const((CONV_WIDTH, XBC_WIDTH)), const((1, XBC_WIDTH)),
                  const((1, LANES)), const((1, LANES)),
                  const((1, SSM_WIDTH)), const((1, SSM_WIDTH)),
                  const((LANES, SSM_WIDTH))],
        out_specs=pl.BlockSpec((C, SSM_WIDTH), lambda c: (c, 0)),
        out_shape=jax.ShapeDtypeStruct((T, SSM_WIDTH), BF16),
        scratch_shapes=[pltpu.VMEM((C + 8, XBC_WIDTH), F32),
                        pltpu.VMEM((SSM_GROUPS, SSM_STATE, HW), F32)],
        compiler_params=_params(("arbitrary",)),
        name="ssd",
    )(proj, proj, proj, proj, proj, conv_w, conv_b.reshape(1, XBC_WIDTH), pad(dt_bias), a_neg,
      dskip_exp, ssm_norm_w.reshape(1, SSM_WIDTH), expand)


def _outproj_kernel(ret_ref, ssm_ref, w_ref, x_ref, lnw_ref, wr_ref, br_ref, h_ref, u_ref, lg_ref):
    k = pl.program_id(1)
    m = jnp.where(k < pl.num_programs(1) // 2, ret_ref[...], ssm_ref[...])
    part = _dot(m, w_ref[...].astype(BF16))

    @pl.when(k == 0)
    def _():
        h_ref[...] = x_ref[...] + part

    @pl.when(k > 0)
    def _():
        h_ref[...] += part

    @pl.when(k == pl.num_programs(1) - 1)
    def _():
        h = h_ref[...]
        u = h * lax.rsqrt(jnp.mean(h * h, axis=-1, keepdims=True) + EPS) * lnw_ref[...]
        u_ref[...] = u
        uh, um, ul = _split3(u)
        wh, wm, wl = _split3(wr_ref[...])
        lg = (_dot(uh, wh) + (_dot(uh, wm) + _dot(um, wh))
              + (_dot(uh, wl) + _dot(um, wm) + _dot(ul, wh)))
        lg_ref[...] = lg + br_ref[...]


def _outproj(ret, ssm, w_out, x2, ln_w, w_router, b_router):
    T = x2.shape[0]
    tm, tk = 512, 512
    kh = RET_WIDTH // tk
    return pl.pallas_call(
        _outproj_kernel,
        grid=(T // tm, D_MODEL // tk),
        in_specs=[pl.BlockSpec((tm, tk), lambda i, k: (i, jnp.minimum(k, kh - 1))),
                  pl.BlockSpec((tm, tk), lambda i, k: (i, jnp.maximum(k - kh, 0))),
                  pl.BlockSpec((tk, D_MODEL), lambda i, k: (k, 0)),
                  pl.BlockSpec((tm, D_MODEL), lambda i, k: (i, 0)),
                  pl.BlockSpec((1, D_MODEL), lambda i, k: (0, 0)),
                  pl.BlockSpec((D_MODEL, N_EXPERTS), lambda i, k: (0, 0)),
                  pl.BlockSpec((1, N_EXPERTS), lambda i, k: (0, 0))],
        out_specs=[pl.BlockSpec((tm, D_MODEL), lambda i, k: (i, 0)),
                   pl.BlockSpec((tm, D_MODEL), lambda i, k: (i, 0)),
                   pl.BlockSpec((tm, N_EXPERTS), lambda i, k: (i, 0))],
        out_shape=[jax.ShapeDtypeStruct((T, D_MODEL), F32),
                   jax.ShapeDtypeStruct((T, D_MODEL), F32),
                   jax.ShapeDtypeStruct((T, N_EXPERTS), F32)],
        compiler_params=_params(("parallel", "arbitrary")),
        name="outproj",
    )(ret, ssm, w_out, x2, ln_w.reshape(1, D_MODEL), w_router, b_router.reshape(1, N_EXPERTS))


def _dispatch_kernel(rb, n_steps, n_assign, dest_ref, pad_lo_ref, pad_hi_ref, u_hbm,
                     x_ref, row_tok, gbuf, sem):
    b = pl.program_id(0)

    @pl.when(b == 0)
    def _():
        for g in range(N_EXPERTS + 1):
            def fill(r, c):
                row_tok[r] = 0
                return c
            lax.fori_loop(pad_lo_ref[g], pad_hi_ref[g], fill, 0)

        def scatter(n, c):
            row_tok[dest_ref[n]] = n // TOP_K
            return c
        lax.fori_loop(0, n_assign, scatter, 0)

    def copy(tok, slot, r):
        src = u_hbm.at[pl.ds(pl.multiple_of(tok * ROW_SLABS, ROW_SLABS), ROW_SLABS)]
        dst = gbuf.at[pl.ds(pl.multiple_of((slot * rb + r) * ROW_SLABS, ROW_SLABS), ROW_SLABS)]
        return pltpu.make_async_copy(src, dst, sem.at[slot])

    def issue(step, slot):
        def body(r, c):
            copy(row_tok[step * rb + r], slot, r).start()
            return c
        lax.fori_loop(0, rb, body, 0)

    @pl.when(b == 0)
    def _():
        issue(0, 0)

    @pl.when(b + 1 < n_steps)
    def _():
        issue(b + 1, (b + 1) % 2)

    slot = b % 2

    base = pl.multiple_of(slot * rb * ROW_SLABS, rb * ROW_SLABS)
    pltpu.make_async_copy(u_hbm.at[pl.ds(0, rb * ROW_SLABS)], gbuf.at[pl.ds(base, rb * ROW_SLABS)],
                          sem.at[slot]).wait()
    for s in range(ROW_SLABS):
        x_ref[:, s * LANES:(s + 1) * LANES] = gbuf[pl.ds(base + s, rb, stride=ROW_SLABS), :].astype(x_ref.dtype)


def _dispatch(dest, pad_lo, pad_hi, u_slabs, n_rows):
    rb = ROW_BLK
    n_steps = n_rows // rb
    grid_spec = pltpu.PrefetchScalarGridSpec(
        num_scalar_prefetch=3,
        grid=(n_steps,),
        in_specs=[pl.BlockSpec(memory_space=pl.ANY)],
        out_specs=pl.BlockSpec((rb, D_MODEL), lambda b, d, lo, hi: (b, 0)),
        scratch_shapes=[pltpu.SMEM((n_rows,), jnp.int32),
                        pltpu.VMEM((2 * rb * ROW_SLABS, LANES), F32),
                        pltpu.SemaphoreType.DMA((2,))],
    )
    return pl.pallas_call(
        functools.partial(_dispatch_kernel, rb, n_steps, dest.shape[0]),
        grid_spec=grid_spec,
        out_shape=jax.ShapeDtypeStruct((n_rows, D_MODEL), BF16),
        compiler_params=_params(("arbitrary",)),
        name="dispatch",
    )(dest, pad_lo, pad_hi, u_slabs)


def _expert_kernel(n_ff, item_e, item_row, item_nsub, item_nzero, item_live,
                   x_hbm, wga_ref, wgb_ref, bga_ref, bgb_ref, wda_ref, wdb_ref, bd_ref,
                   y_hbm, xbuf, acc, wdm, sem_in, sem_out):
    del item_e, item_live
    i = pl.program_id(0)
    f = pl.program_id(1)
    nsub = item_nsub[i]
    nzero = item_nzero[i]
    row0 = item_row[i]
    SUB = ROW_BLK

    def hbm_rows(j):
        return pl.ds(pl.multiple_of(row0 + j * SUB, SUB), SUB)

    def vmem_rows(j):
        return pl.ds(pl.multiple_of(j * SUB, SUB), SUB)

    def x_copy(j):
        return pltpu.make_async_copy(x_hbm.at[hbm_rows(j)], xbuf.at[vmem_rows(j)], sem_in)

    def y_copy(j):
        return pltpu.make_async_copy(acc.at[vmem_rows(j)], y_hbm.at[hbm_rows(j)], sem_out)

    @pl.when(f == 0)
    def _():
        def start(j, c):
            x_copy(j).start()
            return c
        lax.fori_loop(0, nsub, start, 0)

        def wait(j, c):
            x_copy(j).wait()
            return c
        lax.fori_loop(0, nsub, wait, 0)

    wa = wga_ref[...].astype(BF16)
    wb = wgb_ref[...].astype(BF16)
    for s in range(ROW_SLABS):
        wdm[s, pl.ds(0, FF_TILE, stride=2), :] = wda_ref[:, s * LANES:(s + 1) * LANES]
        wdm[s, pl.ds(1, FF_TILE, stride=2), :] = wdb_ref[:, s * LANES:(s + 1) * LANES]
    wd = jnp.concatenate([wdm[s] for s in range(ROW_SLABS)], axis=-1).astype(BF16)
    lane = lax.broadcasted_iota(jnp.int32, (1, 2 * FF_TILE), 1)
    even = (lane % 2) == 0
    bga = bga_ref[...]
    bgb = bgb_ref[...]
    W2 = 2 * FF_TILE

    def sub(j, c):
        r = pl.multiple_of(j * SUB, SUB)
        xs = xbuf[pl.ds(r, SUB), :]
        ga = _dot(xs, wa) + bga
        gb = _dot(xs, wb) + bgb
        gate = jnp.where(even, ga, pltpu.roll(gb, 1, 1))
        up = jnp.where(even, pltpu.roll(ga, W2 - 1, 1), gb)
        gate = jnp.minimum(gate, SWIGLU_LIMIT)
        up = jnp.clip(up, -SWIGLU_LIMIT, SWIGLU_LIMIT)
        act = (up + 1.0) * (gate * jax.nn.sigmoid(gate * SWIGLU_ALPHA))
        part = _dot(act.astype(BF16), wd)

        @pl.when(f == 0)
        def _():
            acc[pl.ds(r, SUB), :] = part

        @pl.when(f > 0)
        def _():
            acc[pl.ds(r, SUB), :] += part
        return c

    lax.fori_loop(0, nsub, sub, 0)

    @pl.when(f == n_ff - 1)
    def _():
        def fin(j, c):
            r = pl.multiple_of(j * SUB, SUB)
            acc[pl.ds(r, SUB), :] = acc[pl.ds(r, SUB), :] + bd_ref[...]
            y_copy(j).start()
            return c
        lax.fori_loop(0, nsub, fin, 0)

        def wait(j, c):
            y_copy(j).wait()
            return c
        lax.fori_loop(0, nsub, wait, 0)

    @pl.when(jnp.logical_and(f == n_ff - 1, nzero > 0))
    def _():
        acc[0:SUB, :] = jnp.zeros((SUB, D_MODEL), F32)

        def zero_copy(j):
            return pltpu.make_async_copy(acc.at[pl.ds(0, SUB)], y_hbm.at[hbm_rows(j)], sem_out)

        def start(j, c):
            zero_copy(j).start()
            return c
        lax.fori_loop(0, nzero, start, 0)

        def wait(j, c):
            zero_copy(j).wait()
            return c
        lax.fori_loop(0, nzero, wait, 0)


def _experts(x_rows, items, w_gate_up, b_gate_up, w_down, b_down, n_items):
    n_rows = x_rows.shape[0]
    n_ff = (D_FF // 2) // FF_TILE
    W2 = 2 * FF_TILE
    item_e, item_row, item_nsub, item_nzero, item_live = items

    def ff(i, f, live):
        return jnp.where(live[i] > 0, f, n_ff - 1)

    grid_spec = pltpu.PrefetchScalarGridSpec(
        num_scalar_prefetch=5,
        grid=(n_items, n_ff),
        in_specs=[
            pl.BlockSpec(memory_space=pl.ANY),
            pl.BlockSpec((None, D_MODEL, W2), lambda i, f, e, r, n, z, lv: (e[i], 0, ff(i, f, lv))),
            pl.BlockSpec((None, D_MODEL, W2), lambda i, f, e, r, n, z, lv: (e[i], 0, n_ff + ff(i, f, lv))),
            pl.BlockSpec((None, 1, W2), lambda i, f, e, r, n, z, lv: (e[i], 0, ff(i, f, lv))),
            pl.BlockSpec((None, 1, W2), lambda i, f, e, r, n, z, lv: (e[i], 0, n_ff + ff(i, f, lv))),
            pl.BlockSpec((None, FF_TILE, D_MODEL), lambda i, f, e, r, n, z, lv: (e[i], ff(i, f, lv), 0)),
            pl.BlockSpec((None, FF_TILE, D_MODEL), lambda i, f, e, r, n, z, lv: (e[i], n_ff + ff(i, f, lv), 0)),
            pl.BlockSpec((None, 1, D_MODEL), lambda i, f, e, r, n, z, lv: (e[i], 0, 0)),
        ],
        out_specs=pl.BlockSpec(memory_space=pl.ANY),
        scratch_shapes=[pltpu.VMEM((ITEM_ROWS, D_MODEL), BF16),
                        pltpu.VMEM((ITEM_ROWS, D_MODEL), F32),
                        pltpu.VMEM((ROW_SLABS, W2, LANES), F32),
                        pltpu.SemaphoreType.DMA(()),
                        pltpu.SemaphoreType.DMA(())],
    )
    return pl.pallas_call(
        functools.partial(_expert_kernel, n_ff),
        grid_spec=grid_spec,
        out_shape=jax.ShapeDtypeStruct((n_rows, D_MODEL), F32),
        compiler_params=_params(("arbitrary", "arbitrary")),
        name="experts",
    )(item_e, item_row, item_nsub, item_nzero, item_live, x_rows,
      w_gate_up, w_gate_up, b_gate_up.reshape(N_EXPERTS, 1, 2 * D_FF), b_gate_up.reshape(N_EXPERTS, 1, 2 * D_FF),
      w_down, w_down, b_down.reshape(N_EXPERTS, 1, D_MODEL))


def _combine_kernel(tm, n_steps, dest_ref, w_ref, y_hbm, h_ref, lnw_ref, o_ref, buf, sem):
    i = pl.program_id(0)

    def copy(step, slot, t, k):
        d = dest_ref[(step * tm + t) * TOP_K + k]
        return pltpu.make_async_copy(y_hbm.at[d], buf.at[slot, k * tm + t], sem.at[slot])

    def issue(step, slot):
        def body(t, c):
            for k in range(TOP_K):
                copy(step, slot, t, k).start()
            return c
        lax.fori_loop(0, tm, body, 0)

    @pl.when(i == 0)
    def _():
        issue(0, 0)

    @pl.when(i + 1 < n_steps)
    def _():
        issue(i + 1, (i + 1) % 2)

    slot = i % 2

    pltpu.make_async_copy(y_hbm.at[pl.ds(0, TOP_K * tm)], buf.at[slot], sem.at[slot]).wait()

    def token(t, c):
        a = h_ref[t]
        for k in range(TOP_K):
            a = a + w_ref[(i * tm + t) * TOP_K + k] * buf[slot, k * tm + t]
        o_ref[t] = a
        return c
    lax.fori_loop(0, tm, token, 0)

    h = o_ref[...]
    ms = jnp.mean(jnp.mean(h * h, axis=2, keepdims=True), axis=1, keepdims=True)
    o_ref[...] = h * lax.rsqrt(ms + EPS) * lnw_ref[...]


def _combine(dest, wflat, y3, h3, ln_w):
    T = h3.shape[0]
    tm = 128
    n_steps = T // tm
    grid_spec = pltpu.PrefetchScalarGridSpec(
        num_scalar_prefetch=2,
        grid=(n_steps,),
        in_specs=[pl.BlockSpec(memory_space=pl.ANY),
                  pl.BlockSpec((tm, ROW_SLABS, LANES), lambda i, d, w: (i, 0, 0)),
                  pl.BlockSpec((1, ROW_SLABS, LANES), lambda i, d, w: (0, 0, 0))],
        out_specs=pl.BlockSpec((tm, ROW_SLABS, LANES), lambda i, d, w: (i, 0, 0)),
        scratch_shapes=[pltpu.VMEM((2, TOP_K * tm, ROW_SLABS, LANES), F32),
                        pltpu.SemaphoreType.DMA((2,))],
    )
    return pl.pallas_call(
        functools.partial(_combine_kernel, tm, n_steps),
        grid_spec=grid_spec,
        out_shape=jax.ShapeDtypeStruct((T, ROW_SLABS, LANES), F32),
        compiler_params=_params(("arbitrary",)),
        name="combine",
    )(dest, wflat, y3, h3, ln_w.reshape(1, ROW_SLABS, LANES))


def _route(logits, n_rows, n_items):
    T = logits.shape[0]
    top_logits, top_idx = lax.top_k(logits, TOP_K)
    top_w = jax.nn.softmax(top_logits, axis=-1)
    e_flat = top_idx.reshape(-1).astype(jnp.int32)
    onehot = (e_flat[:, None] == jnp.arange(N_EXPERTS, dtype=jnp.int32)[None, :]).astype(jnp.int32)
    csum = jnp.cumsum(onehot, axis=0)
    rank = jnp.sum(onehot * (csum - onehot), axis=1)
    counts = csum[-1]
    padded = (counts + ROW_BLK - 1) // ROW_BLK * ROW_BLK
    pend = jnp.cumsum(padded)
    pstart = pend - padded
    dest = (pstart[e_flat] + rank).astype(jnp.int32)

    per_e = (padded + ITEM_ROWS - 1) // ITEM_ROWS
    iend = jnp.cumsum(per_e)
    istart = iend - per_e
    ii = jnp.arange(n_items, dtype=jnp.int32)
    total = iend[-1]
    live = (ii < total).astype(jnp.int32)
    ic = jnp.minimum(ii, total - 1)
    ie = jnp.minimum(jnp.searchsorted(iend, ic, side='right'), N_EXPERTS - 1).astype(jnp.int32)
    within = ic - istart[ie]
    irow = (pstart[ie] + within * ITEM_ROWS).astype(jnp.int32)
    insub = jnp.minimum((padded[ie] - within * ITEM_ROWS) // ROW_BLK, ITEM_ROWS // ROW_BLK).astype(jnp.int32)
    insub = insub * live
    tail_rows = n_rows - pend[-1]
    tt = ii - total
    tail = jnp.logical_and(tt >= 0, tt * ITEM_ROWS < tail_rows)
    inzero = jnp.where(tail, jnp.minimum(tail_rows - tt * ITEM_ROWS, ITEM_ROWS) // ROW_BLK, 0).astype(jnp.int32)
    irow = jnp.where(tail, pend[-1] + tt * ITEM_ROWS, irow).astype(jnp.int32)
    pad_lo = jnp.concatenate([pstart + counts, pend[-1:]]).astype(jnp.int32)
    pad_hi = jnp.concatenate([pend, jnp.full((1,), n_rows)]).astype(jnp.int32)
    return dest, top_w.reshape(-1).astype(F32), (ie, irow, insub, inzero, live), (pad_lo, pad_hi)


def kernel(x, positions, ln_mix_w, w_in, conv_w, conv_b, dt_bias, a_log, d_skip, ssm_norm_w, w_out,
           ln_ffn_w, w_router, b_router, w_gate_up, b_gate_up, w_down, b_down, ln_final_w):
    B, L, _ = x.shape
    T = B * L
    assert B == 1 and T % 1024 == 0
    x2 = x.reshape(T, D_MODEL)
    half = RET_HEAD_DIM // 2
    inv_freq = (ROPE_BASE ** (-jnp.arange(half, dtype=F32) / half)).reshape(1, half)
    pos_col = positions.reshape(T, 1).astype(F32)

    proj = _inproj(x2, ln_mix_w[0], w_in[0])
    ret = _retention(proj, pos_col, inv_freq)
    ssm = _ssd(proj, conv_w[0], conv_b[0], dt_bias[0], a_log[0], d_skip[0], ssm_norm_w[0])
    h, u, logits = _outproj(ret, ssm, w_out[0], x2, ln_ffn_w[0], w_router[0], b_router[0])

    n_rows = -(-(T * TOP_K + N_EXPERTS * (ROW_BLK - 1)) // ROW_BLK) * ROW_BLK
    n_items = N_EXPERTS + 1 + n_rows // ITEM_ROWS
    dest, wflat, items, (pad_lo, pad_hi) = _route(logits, n_rows, n_items)

    x_rows = _dispatch(dest, pad_lo, pad_hi, u.reshape(T * ROW_SLABS, LANES), n_rows)
    y_rows = _experts(x_rows, items, w_gate_up[0], b_gate_up[0], w_down[0], b_down[0], n_items)
    out3 = _combine(dest, wflat, y_rows.reshape(n_rows, ROW_SLABS, LANES), h.reshape(T, ROW_SLABS, LANES),
                    ln_final_w)
    return out3.reshape(B, L, D_MODEL)
```

```python
import functools

import numpy as np
import jax
import jax.numpy as jnp
from jax import lax
from jax.experimental import pallas as pl
from jax.experimental.pallas import tpu as pltpu

F32 = jnp.float32
BF16 = jnp.bfloat16

D_MODEL = 2048
RET_HEADS = 4
RET_HEAD_DIM = 256
RET_WIDTH = RET_HEADS * RET_HEAD_DIM
SSM_WIDTH = D_MODEL - RET_WIDTH
SSM_HEAD_DIM = 64
SSM_HEADS = SSM_WIDTH // SSM_HEAD_DIM
SSM_GROUPS = 2
SSM_STATE = 128
CONV_WIDTH = 4
XBC_WIDTH = SSM_WIDTH + 2 * SSM_GROUPS * SSM_STATE
D_IN_PROJ = 4 * RET_WIDTH + SSM_WIDTH + XBC_WIDTH + SSM_HEADS
ROPE_BASE = 10000.0
N_EXPERTS = 32
TOP_K = 4
D_FF = D_MODEL
SWIGLU_LIMIT = 7.0
SWIGLU_ALPHA = 1.702
EPS = 1e-6

LANES = 128
VMEM_LIMIT = 56 * 1024 * 1024

RET_CHUNK = 256
SSD_CHUNK = 128
ROW_BLK = 256
ITEM_ROWS = 2048
FF_TILE = 128
ROW_SLABS = D_MODEL // LANES
PACK_SLABS = ROW_SLABS // 2


def _params(sem, **kw):
    return pltpu.CompilerParams(dimension_semantics=sem, vmem_limit_bytes=VMEM_LIMIT, **kw)


def _dot(a, b):
    return jnp.dot(a, b, preferred_element_type=F32)


def _dot_nt(a, b):
    return lax.dot_general(a, b, (((1,), (1,)), ((), ())), preferred_element_type=F32)


def _dot_tn(a, b):
    return lax.dot_general(a, b, (((0,), (0,)), ((), ())), preferred_element_type=F32)


def _split3(x):
    hi = x.astype(BF16)
    r = x - hi.astype(F32)
    mid = r.astype(BF16)
    lo = (r - mid.astype(F32)).astype(BF16)
    return hi, mid, lo


def _dot_exact_rhs01(x, m01):
    hi, mid, lo = _split3(x)
    return _dot(hi, m01) + _dot(mid, m01) + _dot(lo, m01)


def _dot_exact_lhs01(m01, x):
    hi, mid, lo = _split3(x)
    return _dot(m01, hi) + _dot(m01, mid) + _dot(m01, lo)


def _silu(x):
    return x * jax.nn.sigmoid(x)


def _inproj_kernel(x_ref, lnw_ref, w_ref, o_ref, u_ref):
    @pl.when(pl.program_id(1) == 0)
    def _():
        x = x_ref[...]
        ms = jnp.mean(x * x, axis=-1, keepdims=True)
        u_ref[...] = (x * lax.rsqrt(ms + EPS) * lnw_ref[...]).astype(BF16)

    o_ref[...] = _dot_nt(u_ref[...], w_ref[...].astype(BF16))


def _inproj(x2, ln_w, w_in_t):
    T = x2.shape[0]
    tm, tn = 1024, 512
    return pl.pallas_call(
        _inproj_kernel,
        grid=(T // tm, pl.cdiv(D_IN_PROJ, tn)),
        in_specs=[pl.BlockSpec((tm, D_MODEL), lambda i, j: (i, 0)),
                  pl.BlockSpec((1, D_MODEL), lambda i, j: (0, 0)),
                  pl.BlockSpec((tn, D_MODEL), lambda i, j: (j, 0))],
        out_specs=pl.BlockSpec((tm, tn), lambda i, j: (i, j)),
        out_shape=jax.ShapeDtypeStruct((T, D_IN_PROJ), F32),
        scratch_shapes=[pltpu.VMEM((tm, D_MODEL), BF16)],
        compiler_params=_params(("parallel", "arbitrary")),
        name="inproj",
    )(x2, ln_w.reshape(1, D_MODEL), w_in_t)


def _retention_tables():
    C = RET_CHUNK
    h = np.arange(RET_HEADS, dtype=np.float64)
    log_gamma = np.log1p(-np.exp2(-5.0 - h))
    idx = np.arange(C, dtype=np.float64)
    rel = idx[:, None] - idx[None, :]
    intra = np.where(rel >= 0, np.exp(log_gamma[:, None, None] * np.maximum(rel, 0.0)), 0.0)
    q_decay = np.exp(log_gamma[:, None] * (idx + 1.0))
    k_decay = np.exp(log_gamma[:, None] * (C - 1.0 - idx))
    chunk_decay = np.exp(log_gamma * C)
    qd = np.broadcast_to(q_decay[:, :, None], (RET_HEADS, C, RET_HEAD_DIM))
    kd = np.broadcast_to(k_decay[:, :, None], (RET_HEADS, C, RET_HEAD_DIM))
    return (jnp.asarray(intra, F32), jnp.asarray(qd, F32), jnp.asarray(kd, F32),
            [float(c) for c in chunk_decay])


def _retention_kernel(chunk_decay, pos_ref, invf_ref, q_ref, k_ref, v_ref, g_ref,
                      intra_ref, qd_ref, kd_ref, o_ref, state_ref):
    @pl.when(pl.program_id(0) == 0)
    def _():
        state_ref[...] = jnp.zeros_like(state_ref)

    half = RET_HEAD_DIM // 2
    ang = pos_ref[...] * invf_ref[...]
    cos = jnp.cos(ang)
    sin = jnp.sin(ang)

    def rope(t):
        t1, t2 = t[:, :half], t[:, half:]
        return jnp.concatenate([t1 * cos - t2 * sin, t2 * cos + t1 * sin], axis=-1)

    for h in range(RET_HEADS):
        sl = slice(h * RET_HEAD_DIM, (h + 1) * RET_HEAD_DIM)
        q = rope(q_ref[:, sl])
        k = rope(k_ref[:, sl]) * (RET_HEAD_DIM ** -0.5)
        v = v_ref[:, sl].astype(BF16)
        state = state_ref[h]
        scores = _dot_nt(q.astype(BF16), k.astype(BF16)) * intra_ref[h]
        inner = _dot(scores.astype(BF16), v)
        cross = _dot((q * qd_ref[h]).astype(BF16), state.astype(BF16))
        state_ref[h] = chunk_decay[h] * state + _dot_tn((k * kd_ref[h]).astype(BF16), v)
        o = inner + cross
        o = o * lax.rsqrt(jnp.mean(o * o, axis=-1, keepdims=True) + EPS)
        o_ref[:, sl] = (o * _silu(g_ref[:, sl])).astype(o_ref.dtype)


def _retention(proj, pos_col, inv_freq):
    T = proj.shape[0]
    C = RET_CHUNK
    intra, qd, kd, chunk_decay = _retention_tables()
    col = lambda j: pl.BlockSpec((C, RET_WIDTH), lambda c, j=j: (c, j))
    const3 = lambda shape: pl.BlockSpec(shape, lambda c: (0, 0, 0))
    return pl.pallas_call(
        functools.partial(_retention_kernel, chunk_decay),
        grid=(T // C,),
        in_specs=[pl.BlockSpec((C, 1), lambda c: (c, 0)),
                  pl.BlockSpec((1, RET_HEAD_DIM // 2), lambda c: (0, 0)),
                  col(0), col(1), col(2), col(3),
                  const3((RET_HEADS, C, C)),
                  const3((RET_HEADS, C, RET_HEAD_DIM)),
                  const3((RET_HEADS, C, RET_HEAD_DIM))],
        out_specs=pl.BlockSpec((C, RET_WIDTH), lambda c: (c, 0)),
        out_shape=jax.ShapeDtypeStruct((T, RET_WIDTH), BF16),
        scratch_shapes=[pltpu.VMEM((RET_HEADS, RET_HEAD_DIM, RET_HEAD_DIM), F32)],
        compiler_params=_params(("arbitrary",)),
        name="retention",
    )(pos_col, inv_freq, proj, proj, proj, proj, intra, qd, kd)


def _ssd_kernel(xs0_ref, xs1_ref, bc_ref, z_ref, dt_ref, convw_ref, convb_ref, dtb_ref, a_ref,
                dskip_ref, normw_ref, expand_ref, o_ref, ext_ref, state_ref):
    C = SSD_CHUNK
    HW = SSM_WIDTH // SSM_GROUPS
    CARRY = 8

    @pl.when(pl.program_id(0) == 0)
    def _():
        ext_ref[0:CARRY, :] = jnp.zeros((CARRY, XBC_WIDTH), F32)
        state_ref[...] = jnp.zeros_like(state_ref)

    ext_ref[CARRY:CARRY + C, 0:HW] = xs0_ref[...]
    ext_ref[CARRY:CARRY + C, HW:2 * HW] = xs1_ref[...]
    ext_ref[CARRY:CARRY + C, 2 * HW:3 * HW] = bc_ref[...]
    conv = convb_ref[...]
    for k in range(CONV_WIDTH):
        off = CARRY - (CONV_WIDTH - 1) + k
        conv = conv + convw_ref[k:k + 1, :] * ext_ref[off:off + C, :]
    ext_ref[0:CARRY, :] = ext_ref[C:C + CARRY, :]
    xbc = _silu(conv)
    xs = xbc[:, :SSM_WIDTH]

    lane = lax.broadcasted_iota(jnp.int32, (1, LANES), 1)
    dt_raw = jnp.where(lane < SSM_HEADS, dt_ref[...], 0.0) + dtb_ref[...]
    dt = jnp.maximum(dt_raw, 0.0) + jnp.log1p(jnp.exp(-jnp.abs(dt_raw)))
    dta = dt * a_ref[...]

    row = lax.broadcasted_iota(jnp.int32, (C, C), 0)
    colm = lax.broadcasted_iota(jnp.int32, (C, C), 1)
    tril = row >= colm
    a_cum = _dot_exact_lhs01(jnp.where(tril, 1.0, 0.0).astype(BF16), dta)
    a_cum_t = a_cum.T

    expand = expand_ref[...]
    a_exp = _dot_exact_rhs01(a_cum, expand)
    dt_exp = _dot_exact_rhs01(dt, expand)
    a_last = a_exp[C - 1:C, :]
    decay_in = jnp.exp(a_exp)
    decay_out = jnp.exp(a_last - a_exp)
    chunk_decay = jnp.exp(a_last)
    xdt = xs * dt_exp

    lane2 = lax.broadcasted_iota(jnp.int32, (1, LANES), 1)
    lo_head = lane2 < SSM_HEAD_DIM
    ys = []
    for g in range(SSM_GROUPS):
        gs = slice(g * HW, (g + 1) * HW)
        b_g = xbc[:, SSM_WIDTH + g * SSM_STATE:SSM_WIDTH + (g + 1) * SSM_STATE].astype(BF16)
        c0 = SSM_WIDTH + SSM_GROUPS * SSM_STATE
        c_g = xbc[:, c0 + g * SSM_STATE:c0 + (g + 1) * SSM_STATE].astype(BF16)
        cb = _dot_nt(c_g, b_g)
        state = state_ref[g]
        y_off = _dot(c_g, state.astype(BF16)) * decay_in[:, gs]
        xw = (xdt[:, gs] * decay_out[:, gs]).astype(BF16)
        state_ref[g] = chunk_decay[:, gs] * state + _dot_tn(b_g, xw)
        slabs = []
        for s in range(HW // LANES):
            xd = xdt[:, g * HW + s * LANES:g * HW + (s + 1) * LANES]
            acc = None
            for e in range(2):
                hh = g * (SSM_HEADS // SSM_GROUPS) + 2 * s + e
                seg = a_cum[:, hh:hh + 1] - a_cum_t[hh:hh + 1, :]
                m = cb * jnp.exp(jnp.where(tril, seg, -jnp.inf))
                xm = jnp.where(lo_head if e == 0 else jnp.logical_not(lo_head), xd, 0.0)
                part = _dot(m.astype(BF16), xm.astype(BF16))
                acc = part if acc is None else acc + part
            slabs.append(acc)
        ys.append(jnp.concatenate(slabs, axis=-1) + y_off)
    y = jnp.concatenate(ys, axis=-1) + dskip_ref[...] * xs
    y = y * _silu(z_ref[...])
    outs = []
    for g in range(SSM_GROUPS):
        yg = y[:, g * HW:(g + 1) * HW]
        outs.append(yg * lax.rsqrt(jnp.mean(yg * yg, axis=-1, keepdims=True) + EPS))
    o_ref[...] = (jnp.concatenate(outs, axis=-1) * normw_ref[...]).astype(o_ref.dtype)


def _ssd(proj, conv_w, conv_b, dt_bias, a_log, d_skip, ssm_norm_w):
    T = proj.shape[0]
    C = SSD_CHUNK
    HW = SSM_WIDTH // SSM_GROUPS
    xbc0 = (4 * RET_WIDTH + SSM_WIDTH) // HW
    dt0 = (D_IN_PROJ - SSM_HEADS) // LANES
    pad = lambda v: jnp.zeros((1, LANES), F32).at[0, :SSM_HEADS].set(v.astype(F32))
    a_neg = pad(-jnp.exp(a_log.astype(F32)))
    expand_np = np.zeros((LANES, SSM_WIDTH), np.float32)
    for hh in range(SSM_HEADS):
        expand_np[hh, hh * SSM_HEAD_DIM:(hh + 1) * SSM_HEAD_DIM] = 1.0
    expand = jnp.asarray(expand_np, BF16)
    dskip_exp = jnp.repeat(d_skip.astype(F32), SSM_HEAD_DIM).reshape(1, SSM_WIDTH)
    const = lambda shape: pl.BlockSpec(shape, lambda c: (0, 0))
    return pl.pallas_call(
        _ssd_kernel,
        grid=(T // C,),
        in_specs=[pl.BlockSpec((C, HW), lambda c: (c, xbc0)),
                  pl.BlockSpec((C, HW), lambda c: (c, xbc0 + 1)),
                  pl.BlockSpec((C, HW), lambda c: (c, xbc0 + 2)),
                  pl.BlockSpec((C, SSM_WIDTH), lambda c: (c, 4 * RET_WIDTH // SSM_WIDTH)),
                  pl.BlockSpec((C, LANES), lambda c: (c, dt0)),
                  const((CONV_WIDTH, XBC_WIDTH)), const((1, XBC_WIDTH)),
                  const((1, LANES)), const((1, LANES)),
                  const((1, SSM_WIDTH)), const((1, SSM_WIDTH)),
                  const((LANES, SSM_WIDTH))],
        out_specs=pl.BlockSpec((C, SSM_WIDTH), lambda c: (c, 0)),
        out_shape=jax.ShapeDtypeStruct((T, SSM_WIDTH), BF16),
        scratch_shapes=[pltpu.VMEM((C + 8, XBC_WIDTH), F32),
                        pltpu.VMEM((SSM_GROUPS, SSM_STATE, HW), F32)],
        compiler_params=_params(("arbitrary",)),
        name="ssd",
    )(proj, proj, proj, proj, proj, conv_w, conv_b.reshape(1, XBC_WIDTH), pad(dt_bias), a_neg,
      dskip_exp, ssm_norm_w.reshape(1, SSM_WIDTH), expand)


def _outproj_kernel(ret_ref, ssm_ref, w_ref, x_ref, lnw_ref, wr_ref, br_ref,
                    hs_ref, up_ref, lg_ref, acc_ref):
    k = pl.program_id(1)
    tm = acc_ref.shape[0]
    m = jnp.where(k < pl.num_programs(1) // 2, ret_ref[...], ssm_ref[...])
    part = _dot(m, w_ref[...].astype(BF16))

    @pl.when(k == 0)
    def _():
        acc_ref[...] = x_ref[...] + part

    @pl.when(k > 0)
    def _():
        acc_ref[...] += part

    @pl.when(k == pl.num_programs(1) - 1)
    def _():
        h = acc_ref[...]
        for s in range(ROW_SLABS):
            hs_ref[pl.ds(s, tm, stride=ROW_SLABS), :] = h[:, s * LANES:(s + 1) * LANES]
        u = h * lax.rsqrt(jnp.mean(h * h, axis=-1, keepdims=True) + EPS) * lnw_ref[...]
        bits = lax.bitcast_convert_type(u.astype(BF16).astype(F32), jnp.uint32)
        packed = bits[:, D_MODEL // 2:] | (bits[:, :D_MODEL // 2] >> 16)
        for s in range(PACK_SLABS):
            up_ref[pl.ds(s, tm, stride=PACK_SLABS), :] = packed[:, s * LANES:(s + 1) * LANES]
        uh, um, ul = _split3(u)
        wh, wm, wl = _split3(wr_ref[...])
        lg = (_dot(uh, wh) + (_dot(uh, wm) + _dot(um, wh))
              + (_dot(uh, wl) + _dot(um, wm) + _dot(ul, wh)))
        lg_ref[...] = lg + br_ref[...]


def _outproj(ret, ssm, w_out, x2, ln_w, w_router, b_router):
    T = x2.shape[0]
    tm, tk = 512, 512
    kh = RET_WIDTH // tk
    return pl.pallas_call(
        _outproj_kernel,
        grid=(T // tm, D_MODEL // tk),
        in_specs=[pl.BlockSpec((tm, tk), lambda i, k: (i, jnp.minimum(k, kh - 1))),
                  pl.BlockSpec((tm, tk), lambda i, k: (i, jnp.maximum(k - kh, 0))),
                  pl.BlockSpec((tk, D_MODEL), lambda i, k: (k, 0)),
                  pl.BlockSpec((tm, D_MODEL), lambda i, k: (i, 0)),
                  pl.BlockSpec((1, D_MODEL), lambda i, k: (0, 0)),
                  pl.BlockSpec((D_MODEL, N_EXPERTS), lambda i, k: (0, 0)),
                  pl.BlockSpec((1, N_EXPERTS), lambda i, k: (0, 0))],
        out_specs=[pl.BlockSpec((tm * ROW_SLABS, LANES), lambda i, k: (i, 0)),
                   pl.BlockSpec((tm * PACK_SLABS, LANES), lambda i, k: (i, 0)),
                   pl.BlockSpec((tm, N_EXPERTS), lambda i, k: (i, 0))],
        out_shape=[jax.ShapeDtypeStruct((T * ROW_SLABS, LANES), F32),
                   jax.ShapeDtypeStruct((T * PACK_SLABS, LANES), jnp.uint32),
                   jax.ShapeDtypeStruct((T, N_EXPERTS), F32)],
        scratch_shapes=[pltpu.VMEM((tm, D_MODEL), F32)],
        compiler_params=_params(("parallel", "arbitrary")),
        name="outproj",
    )(ret, ssm, w_out, x2, ln_w.reshape(1, D_MODEL), w_router, b_router.reshape(1, N_EXPERTS))


def _dispatch_kernel(rb, n_steps, n_assign, dest_ref, pad_lo_ref, pad_hi_ref, u_hbm,
                     x_ref, row_tok, gbuf, sem):
    b = pl.program_id(0)

    @pl.when(b == 0)
    def _():
        for g in range(N_EXPERTS + 1):
            def fill(r, c):
                row_tok[r] = 0
                return c
            lax.fori_loop(pad_lo_ref[g], pad_hi_ref[g], fill, 0)

        def scatter(n, c):
            row_tok[dest_ref[n]] = n // TOP_K
            return c
        lax.fori_loop(0, n_assign, scatter, 0)

    PS = PACK_SLABS

    def copy(tok, slot, r):
        src = u_hbm.at[pl.ds(pl.multiple_of(tok * PS, PS), PS)]
        dst = gbuf.at[pl.ds(pl.multiple_of((slot * rb + r) * PS, PS), PS)]
        return pltpu.make_async_copy(src, dst, sem.at[slot])

    def issue(step, slot):
        def body(r, c):
            copy(row_tok[step * rb + r], slot, r).start()
            return c
        lax.fori_loop(0, rb, body, 0)

    @pl.when(b == 0)
    def _():
        issue(0, 0)

    @pl.when(b + 1 < n_steps)
    def _():
        issue(b + 1, (b + 1) % 2)

    slot = b % 2

    base = pl.multiple_of(slot * rb * PS, rb * PS)
    pltpu.make_async_copy(u_hbm.at[pl.ds(0, rb * PS)], gbuf.at[pl.ds(base, rb * PS)], sem.at[slot]).wait()
    half = D_MODEL // 2
    for s in range(PS):
        p = gbuf[pl.ds(base + s, rb, stride=PS), :]
        lo = lax.bitcast_convert_type(p << 16, F32)
        hi = lax.bitcast_convert_type(p & jnp.uint32(0xFFFF0000), F32)
        x_ref[:, s * LANES:(s + 1) * LANES] = lo.astype(x_ref.dtype)
        x_ref[:, half + s * LANES:half + (s + 1) * LANES] = hi.astype(x_ref.dtype)


def _dispatch(dest, pad_lo, pad_hi, u_slabs, n_rows):
    rb = ROW_BLK
    n_steps = n_rows // rb
    grid_spec = pltpu.PrefetchScalarGridSpec(
        num_scalar_prefetch=3,
        grid=(n_steps,),
        in_specs=[pl.BlockSpec(memory_space=pl.ANY)],
        out_specs=pl.BlockSpec((rb, D_MODEL), lambda b, d, lo, hi: (b, 0)),
        scratch_shapes=[pltpu.SMEM((n_rows,), jnp.int32),
                        pltpu.VMEM((2 * rb * PACK_SLABS, LANES), jnp.uint32),
                        pltpu.SemaphoreType.DMA((2,))],
    )
    return pl.pallas_call(
        functools.partial(_dispatch_kernel, rb, n_steps, dest.shape[0]),
        grid_spec=grid_spec,
        out_shape=jax.ShapeDtypeStruct((n_rows, D_MODEL), BF16),
        compiler_params=_params(("arbitrary",)),
        name="dispatch",
    )(dest, pad_lo, pad_hi, u_slabs)


def _expert_kernel(n_ff, item_e, item_row, item_nsub, item_nzero, item_live,
                   x_hbm, wga_ref, wgb_ref, bga_ref, bgb_ref, wda_ref, wdb_ref, bd_ref,
                   y_hbm, xbuf, acc, wdm, wa_s, wb_s, wd_s, sem_in, sem_out):
    del item_e, item_live
    i = pl.program_id(0)
    f = pl.program_id(1)
    nsub = item_nsub[i]
    nzero = item_nzero[i]
    row0 = item_row[i]
    SUB = ROW_BLK

    def hbm_rows(j):
        return pl.ds(pl.multiple_of(row0 + j * SUB, SUB), SUB)

    def vmem_rows(j):
        return pl.ds(pl.multiple_of(j * SUB, SUB), SUB)

    def x_copy(j):
        return pltpu.make_async_copy(x_hbm.at[hbm_rows(j)], xbuf.at[vmem_rows(j)], sem_in)

    def y_copy(j):
        return pltpu.make_async_copy(acc.at[vmem_rows(j)], y_hbm.at[hbm_rows(j)], sem_out)

    @pl.when(f == 0)
    def _():
        def start(j, c):
            x_copy(j).start()
            return c
        lax.fori_loop(0, nsub, start, 0)

        def zero(j, c):
            acc[vmem_rows(j), :] = jnp.zeros((SUB, D_MODEL), F32)
            return c
        lax.fori_loop(0, nsub, zero, 0)

        def wait(j, c):
            x_copy(j).wait()
            return c
        lax.fori_loop(0, nsub, wait, 0)

    wa_s[...] = wga_ref[...].astype(BF16)
    wb_s[...] = wgb_ref[...].astype(BF16)
    for s in range(ROW_SLABS):
        wdm[s, pl.ds(0, FF_TILE, stride=2), :] = wda_ref[:, s * LANES:(s + 1) * LANES]
        wdm[s, pl.ds(1, FF_TILE, stride=2), :] = wdb_ref[:, s * LANES:(s + 1) * LANES]
    for s in range(ROW_SLABS):
        wd_s[:, s * LANES:(s + 1) * LANES] = wdm[s].astype(BF16)
    lane = lax.broadcasted_iota(jnp.int32, (1, 2 * FF_TILE), 1)
    even = (lane % 2) == 0
    bga = bga_ref[...]
    bgb = bgb_ref[...]
    W2 = 2 * FF_TILE

    def block(r, rows):
        xs = xbuf[pl.ds(r, rows), :]
        ga = _dot(xs, wa_s[...]) + bga
        gb = _dot(xs, wb_s[...]) + bgb
        gate = jnp.where(even, ga, pltpu.roll(gb, 1, 1))
        up = jnp.where(even, pltpu.roll(ga, W2 - 1, 1), gb)
        gate = jnp.minimum(gate, SWIGLU_LIMIT)
        up = jnp.clip(up, -SWIGLU_LIMIT, SWIGLU_LIMIT)
        act = (up + 1.0) * (gate * jax.nn.sigmoid(gate * SWIGLU_ALPHA))
        acc[pl.ds(r, rows), :] += _dot(act.astype(BF16), wd_s[...])

    def pair(j, c):
        block(pl.multiple_of(j * (2 * SUB), 2 * SUB), 2 * SUB)
        return c
    lax.fori_loop(0, nsub // 2, pair, 0)

    @pl.when(nsub % 2 == 1)
    def _():
        block(pl.multiple_of((nsub - 1) * SUB, SUB), SUB)

    @pl.when(f == n_ff - 1)
    def _():
        def fin(j, c):
            r = pl.multiple_of(j * SUB, SUB)
            acc[pl.ds(r, SUB), :] = acc[pl.ds(r, SUB), :] + bd_ref[...]
            y_copy(j).start()
            return c
        lax.fori_loop(0, nsub, fin, 0)

        def wait(j, c):
            y_copy(j).wait()
            return c
        lax.fori_loop(0, nsub, wait, 0)

    @pl.when(jnp.logical_and(f == n_ff - 1, nzero > 0))
    def _():
        acc[0:SUB, :] = jnp.zeros((SUB, D_MODEL), F32)

        def zero_copy(j):
            return pltpu.make_async_copy(acc.at[pl.ds(0, SUB)], y_hbm.at[hbm_rows(j)], sem_out)

        def start(j, c):
            zero_copy(j).start()
            return c
        lax.fori_loop(0, nzero, start, 0)

        def wait(j, c):
            zero_copy(j).wait()
            return c
        lax.fori_loop(0, nzero, wait, 0)


def _experts(x_rows, items, w_gate_up, b_gate_up, w_down, b_down, n_items):
    n_rows = x_rows.shape[0]
    n_ff = (D_FF // 2) // FF_TILE
    W2 = 2 * FF_TILE
    item_e, item_row, item_nsub, item_nzero, item_live = items

    def ff(i, f, live):
        return jnp.where(live[i] > 0, f, n_ff - 1)

    grid_spec = pltpu.PrefetchScalarGridSpec(
        num_scalar_prefetch=5,
        grid=(n_items, n_ff),
        in_specs=[
            pl.BlockSpec(memory_space=pl.ANY),
            pl.BlockSpec((None, D_MODEL, W2), lambda i, f, e, r, n, z, lv: (e[i], 0, ff(i, f, lv))),
            pl.BlockSpec((None, D_MODEL, W2), lambda i, f, e, r, n, z, lv: (e[i], 0, n_ff + ff(i, f, lv))),
            pl.BlockSpec((None, 1, W2), lambda i, f, e, r, n, z, lv: (e[i], 0, ff(i, f, lv))),
            pl.BlockSpec((None, 1, W2), lambda i, f, e, r, n, z, lv: (e[i], 0, n_ff + ff(i, f, lv))),
            pl.BlockSpec((None, FF_TILE, D_MODEL), lambda i, f, e, r, n, z, lv: (e[i], ff(i, f, lv), 0)),
            pl.BlockSpec((None, FF_TILE, D_MODEL), lambda i, f, e, r, n, z, lv: (e[i], n_ff + ff(i, f, lv), 0)),
            pl.BlockSpec((None, 1, D_MODEL), lambda i, f, e, r, n, z, lv: (e[i], 0, 0)),
        ],
        out_specs=pl.BlockSpec(memory_space=pl.ANY),
        scratch_shapes=[pltpu.VMEM((ITEM_ROWS, D_MODEL), BF16),
                        pltpu.VMEM((ITEM_ROWS, D_MODEL), F32),
                        pltpu.VMEM((ROW_SLABS, W2, LANES), F32),
                        pltpu.VMEM((D_MODEL, W2), BF16),
                        pltpu.VMEM((D_MODEL, W2), BF16),
                        pltpu.VMEM((W2, D_MODEL), BF16),
                        pltpu.SemaphoreType.DMA(()),
                        pltpu.SemaphoreType.DMA(())],
    )
    return pl.pallas_call(
        functools.partial(_expert_kernel, n_ff),
        grid_spec=grid_spec,
        out_shape=jax.ShapeDtypeStruct((n_rows, D_MODEL), F32),
        compiler_params=_params(("arbitrary", "arbitrary")),
        name="experts",
    )(item_e, item_row, item_nsub, item_nzero, item_live, x_rows,
      w_gate_up, w_gate_up, b_gate_up.reshape(N_EXPERTS, 1, 2 * D_FF), b_gate_up.reshape(N_EXPERTS, 1, 2 * D_FF),
      w_down, w_down, b_down.reshape(N_EXPERTS, 1, D_MODEL))


def _combine_kernel(tm, n_steps, dest_ref, w_ref, y_hbm, h_ref, lnw_ref, o_ref, buf, sem):
    i = pl.program_id(0)

    def copy(step, slot, t, k):
        d = dest_ref[(step * tm + t) * TOP_K + k]
        return pltpu.make_async_copy(y_hbm.at[d], buf.at[slot, k * tm + t], sem.at[slot])

    def issue(step, slot):
        def body(t, c):
            for k in range(TOP_K):
                copy(step, slot, t, k).start()
            return c
        lax.fori_loop(0, tm, body, 0)

    @pl.when(i == 0)
    def _():
        issue(0, 0)

    @pl.when(i + 1 < n_steps)
    def _():
        issue(i + 1, (i + 1) % 2)

    slot = i % 2

    pltpu.make_async_copy(y_hbm.at[pl.ds(0, TOP_K * tm)], buf.at[slot], sem.at[slot]).wait()

    def token(t, c):
        a = h_ref[t]
        for k in range(TOP_K):
            a = a + w_ref[(i * tm + t) * TOP_K + k] * buf[slot, k * tm + t]
        o_ref[t] = a
        return c
    lax.fori_loop(0, tm, token, 0)

    h = o_ref[...]
    ms = jnp.mean(jnp.mean(h * h, axis=2, keepdims=True), axis=1, keepdims=True)
    o_ref[...] = h * lax.rsqrt(ms + EPS) * lnw_ref[...]


def _combine(dest, wflat, y3, h3, ln_w):
    T = h3.shape[0]
    tm = 128
    n_steps = T // tm
    grid_spec = pltpu.PrefetchScalarGridSpec(
        num_scalar_prefetch=2,
        grid=(n_steps,),
        in_specs=[pl.BlockSpec(memory_space=pl.ANY),
                  pl.BlockSpec((tm, ROW_SLABS, LANES), lambda i, d, w: (i, 0, 0)),
                  pl.BlockSpec((1, ROW_SLABS, LANES), lambda i, d, w: (0, 0, 0))],
        out_specs=pl.BlockSpec((tm, ROW_SLABS, LANES), lambda i, d, w: (i, 0, 0)),
        scratch_shapes=[pltpu.VMEM((2, TOP_K * tm, ROW_SLABS, LANES), F32),
                        pltpu.SemaphoreType.DMA((2,))],
    )
    return pl.pallas_call(
        functools.partial(_combine_kernel, tm, n_steps),
        grid_spec=grid_spec,
        out_shape=jax.ShapeDtypeStruct((T, ROW_SLABS, LANES), F32),
        compiler_params=_params(("arbitrary",)),
        name="combine",
    )(dest, wflat, y3, h3, ln_w.reshape(1, ROW_SLABS, LANES))


def _route(logits, n_rows, n_items):
    T = logits.shape[0]
    top_logits, top_idx = lax.top_k(logits, TOP_K)
    top_w = jax.nn.softmax(top_logits, axis=-1)
    e_flat = top_idx.reshape(-1).astype(jnp.int32)
    onehot = (e_flat[:, None] == jnp.arange(N_EXPERTS, dtype=jnp.int32)[None, :]).astype(jnp.int32)
    csum = jnp.cumsum(onehot, axis=0)
    rank = jnp.sum(onehot * (csum - onehot), axis=1)
    counts = csum[-1]
    padded = (counts + ROW_BLK - 1) // ROW_BLK * ROW_BLK
    pend = jnp.cumsum(padded)
    pstart = pend - padded
    dest = (pstart[e_flat] + rank).astype(jnp.int32)

    per_e = (padded + ITEM_ROWS - 1) // ITEM_ROWS
    iend = jnp.cumsum(per_e)
    istart = iend - per_e
    ii = jnp.arange(n_items, dtype=jnp.int32)
    total = iend[-1]
    live = (ii < total).astype(jnp.int32)
    ic = jnp.minimum(ii, total - 1)
    ie = jnp.minimum(jnp.searchsorted(iend, ic, side='right'), N_EXPERTS - 1).astype(jnp.int32)
    within = ic - istart[ie]
    irow = (pstart[ie] + within * ITEM_ROWS).astype(jnp.int32)
    insub = jnp.minimum((padded[ie] - within * ITEM_ROWS) // ROW_BLK, ITEM_ROWS // ROW_BLK).astype(jnp.int32)
    insub = insub * live
    tail_rows = n_rows - pend[-1]
    tt = ii - total
    tail = jnp.logical_and(tt >= 0, tt * ITEM_ROWS < tail_rows)
    inzero = jnp.where(tail, jnp.minimum(tail_rows - tt * ITEM_ROWS, ITEM_ROWS) // ROW_BLK, 0).astype(jnp.int32)
    irow = jnp.where(tail, pend[-1] + tt * ITEM_ROWS, irow).astype(jnp.int32)
    pad_lo = jnp.concatenate([pstart + counts, pend[-1:]]).astype(jnp.int32)
    pad_hi = jnp.concatenate([pend, jnp.full((1,), n_rows)]).astype(jnp.int32)
    return dest, top_w.reshape(-1).astype(F32), (ie, irow, insub, inzero, live), (pad_lo, pad_hi)


def kernel(x, positions, ln_mix_w, w_in, conv_w, conv_b, dt_bias, a_log, d_skip, ssm_norm_w, w_out,
           ln_ffn_w, w_router, b_router, w_gate_up, b_gate_up, w_down, b_down, ln_final_w):
    B, L, _ = x.shape
    T = B * L
    assert B == 1 and T % 1024 == 0
    x2 = x.reshape(T, D_MODEL)
    half = RET_HEAD_DIM // 2
    inv_freq = (ROPE_BASE ** (-jnp.arange(half, dtype=F32) / half)).reshape(1, half)
    pos_col = positions.reshape(T, 1).astype(F32)

    proj = _inproj(x2, ln_mix_w[0], jnp.swapaxes(w_in[0], 0, 1))
    ret = _retention(proj, pos_col, inv_freq)
    ssm = _ssd(proj, conv_w[0], conv_b[0], dt_bias[0], a_log[0], d_skip[0], ssm_norm_w[0])
    h_slabs, u_packed, logits = _outproj(ret, ssm, w_out[0], x2, ln_ffn_w[0], w_router[0], b_router[0])

    n_rows = -(-(T * TOP_K + N_EXPERTS * (ROW_BLK - 1)) // ROW_BLK) * ROW_BLK
    n_items = N_EXPERTS + 1 + n_rows // ITEM_ROWS
    dest, wflat, items, (pad_lo, pad_hi) = _route(logits, n_rows, n_items)

    x_rows = _dispatch(dest, pad_lo, pad_hi, u_packed, n_rows)
    y_rows = _experts(x_rows, items, w_gate_up[0], b_gate_up[0], w_down[0], b_down[0], n_items)
    out3 = _combine(dest, wflat, y_rows.reshape(n_rows, ROW_SLABS, LANES),
                    h_slabs.reshape(T, ROW_SLABS, LANES), ln_final_w)
    return out3.reshape(B, L, D_MODEL)
```

```python
import functools

import numpy as np
import jax
import jax.numpy as jnp
from jax import lax
from jax.experimental import pallas as pl
from jax.experimental.pallas import tpu as pltpu

F32 = jnp.float32
BF16 = jnp.bfloat16

D_MODEL = 2048
RET_HEADS = 4
RET_HEAD_DIM = 256
RET_WIDTH = RET_HEADS * RET_HEAD_DIM
SSM_WIDTH = D_MODEL - RET_WIDTH
SSM_HEAD_DIM = 64
SSM_HEADS = SSM_WIDTH // SSM_HEAD_DIM
SSM_GROUPS = 2
SSM_STATE = 128
CONV_WIDTH = 4
XBC_WIDTH = SSM_WIDTH + 2 * SSM_GROUPS * SSM_STATE
D_IN_PROJ = 4 * RET_WIDTH + SSM_WIDTH + XBC_WIDTH + SSM_HEADS
ROPE_BASE = 10000.0
N_EXPERTS = 32
TOP_K = 4
D_FF = D_MODEL
SWIGLU_LIMIT = 7.0
SWIGLU_ALPHA = 1.702
EPS = 1e-6

LANES = 128
VMEM_LIMIT = 56 * 1024 * 1024

RET_CHUNK = 256
SSD_CHUNK = 128
ROW_BLK = 256
DISPATCH_ROWS = 512
ITEM_ROWS = 2048
FF_TILE = 128
ROW_SLABS = D_MODEL // LANES
PACK_SLABS = ROW_SLABS // 2


def _params(sem, **kw):
    return pltpu.CompilerParams(dimension_semantics=sem, vmem_limit_bytes=VMEM_LIMIT, **kw)


def _dot(a, b):
    return jnp.dot(a, b, preferred_element_type=F32)


def _dot_nt(a, b):
    return lax.dot_general(a, b, (((1,), (1,)), ((), ())), preferred_element_type=F32)


def _dot_tn(a, b):
    return lax.dot_general(a, b, (((0,), (0,)), ((), ())), preferred_element_type=F32)


def _split3(x):
    hi = x.astype(BF16)
    r = x - hi.astype(F32)
    mid = r.astype(BF16)
    lo = (r - mid.astype(F32)).astype(BF16)
    return hi, mid, lo


def _dot_exact_rhs01(x, m01):
    hi, mid, lo = _split3(x)
    return _dot(hi, m01) + _dot(mid, m01) + _dot(lo, m01)


def _dot_exact_lhs01(m01, x):
    hi, mid, lo = _split3(x)
    return _dot(m01, hi) + _dot(m01, mid) + _dot(m01, lo)


def _silu(x):
    return x * jax.nn.sigmoid(x)


def _pack_bf16_pairs(lo, hi):
    lo_bits = lax.bitcast_convert_type(lo.astype(BF16).astype(F32), jnp.uint32)
    hi_bits = lax.bitcast_convert_type(hi.astype(BF16).astype(F32), jnp.uint32)
    return hi_bits | (lo_bits >> 16)


def _inproj_kernel(x_ref, lnw_ref, w_ref, o_ref, u_ref):
    @pl.when(pl.program_id(1) == 0)
    def _():
        x = x_ref[...]
        ms = jnp.mean(x * x, axis=-1, keepdims=True)
        u_ref[...] = (x * lax.rsqrt(ms + EPS) * lnw_ref[...]).astype(BF16)

    o_ref[...] = _dot_nt(u_ref[...], w_ref[...].astype(BF16))


def _inproj(x2, ln_w, w_in_t):
    T = x2.shape[0]
    tm, tn = 1024, 512
    return pl.pallas_call(
        _inproj_kernel,
        grid=(T // tm, pl.cdiv(D_IN_PROJ, tn)),
        in_specs=[pl.BlockSpec((tm, D_MODEL), lambda i, j: (i, 0)),
                  pl.BlockSpec((1, D_MODEL), lambda i, j: (0, 0)),
                  pl.BlockSpec((tn, D_MODEL), lambda i, j: (j, 0))],
        out_specs=pl.BlockSpec((tm, tn), lambda i, j: (i, j)),
        out_shape=jax.ShapeDtypeStruct((T, D_IN_PROJ), F32),
        scratch_shapes=[pltpu.VMEM((tm, D_MODEL), BF16)],
        compiler_params=_params(("parallel", "arbitrary")),
        name="inproj",
    )(x2, ln_w.reshape(1, D_MODEL), w_in_t)


def _retention_tables():
    C = RET_CHUNK
    h = np.arange(RET_HEADS, dtype=np.float64)
    log_gamma = np.log1p(-np.exp2(-5.0 - h))
    idx = np.arange(C, dtype=np.float64)
    rel = idx[:, None] - idx[None, :]
    intra = np.where(rel >= 0, np.exp(log_gamma[:, None, None] * np.maximum(rel, 0.0)), 0.0)
    q_decay = np.exp(log_gamma[:, None] * (idx + 1.0))
    k_decay = np.exp(log_gamma[:, None] * (C - 1.0 - idx))
    chunk_decay = np.exp(log_gamma * C)
    qd = np.broadcast_to(q_decay[:, :, None], (RET_HEADS, C, RET_HEAD_DIM))
    kd = np.broadcast_to(k_decay[:, :, None], (RET_HEADS, C, RET_HEAD_DIM))
    return (jnp.asarray(intra, F32), jnp.asarray(qd, F32), jnp.asarray(kd, F32),
            [float(c) for c in chunk_decay])


def _retention_kernel(chunk_decay, pos_ref, invf_ref, q_ref, k_ref, v_ref, g_ref,
                      intra_ref, qd_ref, kd_ref, o_ref, state_ref):
    @pl.when(pl.program_id(0) == 0)
    def _():
        state_ref[...] = jnp.zeros_like(state_ref)

    half = RET_HEAD_DIM // 2
    ang = pos_ref[...] * invf_ref[...]
    cos = jnp.cos(ang)
    sin = jnp.sin(ang)

    def rope(t):
        t1, t2 = t[:, :half], t[:, half:]
        return jnp.concatenate([t1 * cos - t2 * sin, t2 * cos + t1 * sin], axis=-1)

    for h in range(RET_HEADS):
        sl = slice(h * RET_HEAD_DIM, (h + 1) * RET_HEAD_DIM)
        q = rope(q_ref[:, sl])
        k = rope(k_ref[:, sl]) * (RET_HEAD_DIM ** -0.5)
        v = v_ref[:, sl].astype(BF16)
        state = state_ref[h]
        scores = _dot_nt(q.astype(BF16), k.astype(BF16)) * intra_ref[h]
        inner = _dot(scores.astype(BF16), v)
        cross = _dot((q * qd_ref[h]).astype(BF16), state.astype(BF16))
        state_ref[h] = chunk_decay[h] * state + _dot_tn((k * kd_ref[h]).astype(BF16), v)
        o = inner + cross
        o = o * lax.rsqrt(jnp.mean(o * o, axis=-1, keepdims=True) + EPS)
        o_ref[:, sl] = (o * _silu(g_ref[:, sl])).astype(o_ref.dtype)


def _retention(proj, pos_col, inv_freq):
    T = proj.shape[0]
    C = RET_CHUNK
    intra, qd, kd, chunk_decay = _retention_tables()
    col = lambda j: pl.BlockSpec((C, RET_WIDTH), lambda c, j=j: (c, j))
    const3 = lambda shape: pl.BlockSpec(shape, lambda c: (0, 0, 0))
    return pl.pallas_call(
        functools.partial(_retention_kernel, chunk_decay),
        grid=(T // C,),
        in_specs=[pl.BlockSpec((C, 1), lambda c: (c, 0)),
                  pl.BlockSpec((1, RET_HEAD_DIM // 2), lambda c: (0, 0)),
                  col(0), col(1), col(2), col(3),
                  const3((RET_HEADS, C, C)),
                  const3((RET_HEADS, C, RET_HEAD_DIM)),
                  const3((RET_HEADS, C, RET_HEAD_DIM))],
        out_specs=pl.BlockSpec((C, RET_WIDTH), lambda c: (c, 0)),
        out_shape=jax.ShapeDtypeStruct((T, RET_WIDTH), BF16),
        scratch_shapes=[pltpu.VMEM((RET_HEADS, RET_HEAD_DIM, RET_HEAD_DIM), F32)],
        compiler_params=_params(("arbitrary",)),
        name="retention",
    )(pos_col, inv_freq, proj, proj, proj, proj, intra, qd, kd)


def _ssd_kernel(xs0_ref, xs1_ref, bc_ref, z_ref, dt_ref, convw_ref, convb_ref, dtb_ref, a_ref,
                dskip_ref, normw_ref, expand_ref, o_ref, ext_ref, state_ref):
    C = SSD_CHUNK
    HW = SSM_WIDTH // SSM_GROUPS
    CARRY = 8

    @pl.when(pl.program_id(0) == 0)
    def _():
        ext_ref[0:CARRY, :] = jnp.zeros((CARRY, XBC_WIDTH), F32)
        state_ref[...] = jnp.zeros_like(state_ref)

    ext_ref[CARRY:CARRY + C, 0:HW] = xs0_ref[...]
    ext_ref[CARRY:CARRY + C, HW:2 * HW] = xs1_ref[...]
    ext_ref[CARRY:CARRY + C, 2 * HW:3 * HW] = bc_ref[...]
    conv = convb_ref[...]
    for k in range(CONV_WIDTH):
        off = CARRY - (CONV_WIDTH - 1) + k
        conv = conv + convw_ref[k:k + 1, :] * ext_ref[off:off + C, :]
    ext_ref[0:CARRY, :] = ext_ref[C:C + CARRY, :]
    xbc = _silu(conv)
    xs = xbc[:, :SSM_WIDTH]

    lane = lax.broadcasted_iota(jnp.int32, (1, LANES), 1)
    dt_raw = jnp.where(lane < SSM_HEADS, dt_ref[...], 0.0) + dtb_ref[...]
    dt = jnp.maximum(dt_raw, 0.0) + jnp.log1p(jnp.exp(-jnp.abs(dt_raw)))
    dta = dt * a_ref[...]

    row = lax.broadcasted_iota(jnp.int32, (C, C), 0)
    colm = lax.broadcasted_iota(jnp.int32, (C, C), 1)
    tril = row >= colm
    a_cum = _dot_exact_lhs01(jnp.where(tril, 1.0, 0.0).astype(BF16), dta)
    a_cum_t = a_cum.T

    expand = expand_ref[...]
    a_exp = _dot_exact_rhs01(a_cum, expand)
    dt_exp = _dot_exact_rhs01(dt, expand)
    a_last = a_exp[C - 1:C, :]
    decay_in = jnp.exp(a_exp)
    decay_out = jnp.exp(a_last - a_exp)
    chunk_decay = jnp.exp(a_last)
    xdt = xs * dt_exp

    lane2 = lax.broadcasted_iota(jnp.int32, (1, LANES), 1)
    lo_head = lane2 < SSM_HEAD_DIM
    ys = []
    for g in range(SSM_GROUPS):
        gs = slice(g * HW, (g + 1) * HW)
        b_g = xbc[:, SSM_WIDTH + g * SSM_STATE:SSM_WIDTH + (g + 1) * SSM_STATE].astype(BF16)
        c0 = SSM_WIDTH + SSM_GROUPS * SSM_STATE
        c_g = xbc[:, c0 + g * SSM_STATE:c0 + (g + 1) * SSM_STATE].astype(BF16)
        cb = _dot_nt(c_g, b_g)
        state = state_ref[g]
        y_off = _dot(c_g, state.astype(BF16)) * decay_in[:, gs]
        xw = (xdt[:, gs] * decay_out[:, gs]).astype(BF16)
        state_ref[g] = chunk_decay[:, gs] * state + _dot_tn(b_g, xw)
        slabs = []
        for s in range(HW // LANES):
            xd = xdt[:, g * HW + s * LANES:g * HW + (s + 1) * LANES]
            acc = None
            for e in range(2):
                hh = g * (SSM_HEADS // SSM_GROUPS) + 2 * s + e
                seg = a_cum[:, hh:hh + 1] - a_cum_t[hh:hh + 1, :]
                m = cb * jnp.exp(jnp.where(tril, seg, -jnp.inf))
                xm = jnp.where(lo_head if e == 0 else jnp.logical_not(lo_head), xd, 0.0)
                part = _dot(m.astype(BF16), xm.astype(BF16))
                acc = part if acc is None else acc + part
            slabs.append(acc)
        ys.append(jnp.concatenate(slabs, axis=-1) + y_off)
    y = jnp.concatenate(ys, axis=-1) + dskip_ref[...] * xs
    y = y * _silu(z_ref[...])
    outs = []
    for g in range(SSM_GROUPS):
        yg = y[:, g * HW:(g + 1) * HW]
        outs.append(yg * lax.rsqrt(jnp.mean(yg * yg, axis=-1, keepdims=True) + EPS))
    o_ref[...] = (jnp.concatenate(outs, axis=-1) * normw_ref[...]).astype(o_ref.dtype)


def _ssd(proj, conv_w, conv_b, dt_bias, a_log, d_skip, ssm_norm_w):
    T = proj.shape[0]
    C = SSD_CHUNK
    HW = SSM_WIDTH // SSM_GROUPS
    xbc0 = (4 * RET_WIDTH + SSM_WIDTH) // HW
    dt0 = (D_IN_PROJ - SSM_HEADS) // LANES
    pad = lambda v: jnp.zeros((1, LANES), F32).at[0, :SSM_HEADS].set(v.astype(F32))
    a_neg = pad(-jnp.exp(a_log.astype(F32)))
    expand_np = np.zeros((LANES, SSM_WIDTH), np.float32)
    for hh in range(SSM_HEADS):
        expand_np[hh, hh * SSM_HEAD_DIM:(hh + 1) * SSM_HEAD_DIM] = 1.0
    expand = jnp.asarray(expand_np, BF16)
    dskip_exp = jnp.repeat(d_skip.astype(F32), SSM_HEAD_DIM).reshape(1, SSM_WIDTH)
    const = lambda shape: pl.BlockSpec(shape, lambda c: (0, 0))
    return pl.pallas_call(
        _ssd_kernel,
        grid=(T // C,),
        in_specs=[pl.BlockSpec((C, HW), lambda c: (c, xbc0)),
                  pl.BlockSpec((C, HW), lambda c: (c, xbc0 + 1)),
                  pl.BlockSpec((C, HW), lambda c: (c, xbc0 + 2)),
                  pl.BlockSpec((C, SSM_WIDTH), lambda c: (c, 4 * RET_WIDTH // SSM_WIDTH)),
                  pl.BlockSpec((C, LANES), lambda c: (c, dt0)),
                  const((CONV_WIDTH, XBC_WIDTH)), const((1, XBC_WIDTH)),
                  const((1, LANES)), const((1, LANES)),
                  const((1, SSM_WIDTH)), const((1, SSM_WIDTH)),
                  const((LANES, SSM_WIDTH))],
        out_specs=pl.BlockSpec((C, SSM_WIDTH), lambda c: (c, 0)),
        out_shape=jax.ShapeDtypeStruct((T, SSM_WIDTH), BF16),
        scratch_shapes=[pltpu.VMEM((C + 8, XBC_WIDTH), F32),
                        pltpu.VMEM((SSM_GROUPS, SSM_STATE, HW), F32)],
        compiler_params=_params(("arbitrary",)),
        name="ssd",
    )(proj, proj, proj, proj, proj, conv_w, conv_b.reshape(1, XBC_WIDTH), pad(dt_bias), a_neg,
      dskip_exp, ssm_norm_w.reshape(1, SSM_WIDTH), expand)


def _outproj_kernel(ret_ref, ssm_ref, w_ref, x_ref, lnw_ref, wr_ref, br_ref,
                    hs_ref, up_ref, lg_ref, acc_ref):
    k = pl.program_id(1)
    tm = acc_ref.shape[0]
    m = jnp.where(k < pl.num_programs(1) // 2, ret_ref[...], ssm_ref[...])
    part = _dot(m, w_ref[...].astype(BF16))

    @pl.when(k == 0)
    def _():
        acc_ref[...] = x_ref[...] + part

    @pl.when(k > 0)
    def _():
        acc_ref[...] += part

    @pl.when(k == pl.num_programs(1) - 1)
    def _():
        h = acc_ref[...]
        for s in range(ROW_SLABS):
            hs_ref[pl.ds(s, tm, stride=ROW_SLABS), :] = h[:, s * LANES:(s + 1) * LANES]
        u = h * lax.rsqrt(jnp.mean(h * h, axis=-1, keepdims=True) + EPS) * lnw_ref[...]
        packed = _pack_bf16_pairs(u[:, :D_MODEL // 2], u[:, D_MODEL // 2:])
        for s in range(PACK_SLABS):
            up_ref[pl.ds(s, tm, stride=PACK_SLABS), :] = packed[:, s * LANES:(s + 1) * LANES]
        uh, um, ul = _split3(u)
        wh, wm, wl = _split3(wr_ref[...])
        lg = (_dot(uh, wh) + (_dot(uh, wm) + _dot(um, wh))
              + (_dot(uh, wl) + _dot(um, wm) + _dot(ul, wh)))
        lg_ref[...] = lg + br_ref[...]


def _outproj(ret, ssm, w_out, x2, ln_w, w_router, b_router):
    T = x2.shape[0]
    tm, tk = 512, 512
    kh = RET_WIDTH // tk
    return pl.pallas_call(
        _outproj_kernel,
        grid=(T // tm, D_MODEL // tk),
        in_specs=[pl.BlockSpec((tm, tk), lambda i, k: (i, jnp.minimum(k, kh - 1))),
                  pl.BlockSpec((tm, tk), lambda i, k: (i, jnp.maximum(k - kh, 0))),
                  pl.BlockSpec((tk, D_MODEL), lambda i, k: (k, 0)),
                  pl.BlockSpec((tm, D_MODEL), lambda i, k: (i, 0)),
                  pl.BlockSpec((1, D_MODEL), lambda i, k: (0, 0)),
                  pl.BlockSpec((D_MODEL, N_EXPERTS), lambda i, k: (0, 0)),
                  pl.BlockSpec((1, N_EXPERTS), lambda i, k: (0, 0))],
        out_specs=[pl.BlockSpec((tm * ROW_SLABS, LANES), lambda i, k: (i, 0)),
                   pl.BlockSpec((tm * PACK_SLABS, LANES), lambda i, k: (i, 0)),
                   pl.BlockSpec((tm, N_EXPERTS), lambda i, k: (i, 0))],
        out_shape=[jax.ShapeDtypeStruct((T * ROW_SLABS, LANES), F32),
                   jax.ShapeDtypeStruct((T * PACK_SLABS, LANES), jnp.uint32),
                   jax.ShapeDtypeStruct((T, N_EXPERTS), F32)],
        scratch_shapes=[pltpu.VMEM((tm, D_MODEL), F32)],
        compiler_params=_params(("parallel", "arbitrary")),
        name="outproj",
    )(ret, ssm, w_out, x2, ln_w.reshape(1, D_MODEL), w_router, b_router.reshape(1, N_EXPERTS))


def _dispatch_kernel(rb, n_steps, row_tok, u_hbm, x_ref, gbuf, sem):
    b = pl.program_id(0)
    PS = PACK_SLABS

    def copy(tok, slot, r):
        src = u_hbm.at[pl.ds(pl.multiple_of(tok * PS, PS), PS)]
        dst = gbuf.at[pl.ds(pl.multiple_of((slot * rb + r) * PS, PS), PS)]
        return pltpu.make_async_copy(src, dst, sem.at[slot])

    def issue(step, slot):
        def body(q, c):
            for p in range(2):
                r = 2 * q + p
                copy(row_tok[step * rb + r], slot, r).start(priority=p)
            return c
        lax.fori_loop(0, rb // 2, body, 0)

    @pl.when(b == 0)
    def _():
        issue(0, 0)

    @pl.when(b + 1 < n_steps)
    def _():
        issue(b + 1, (b + 1) % 2)

    slot = b % 2

    base = pl.multiple_of(slot * rb * PS, rb * PS)
    pltpu.make_async_copy(u_hbm.at[pl.ds(0, rb * PS)], gbuf.at[pl.ds(base, rb * PS)], sem.at[slot]).wait()
    half = D_MODEL // 2
    for s in range(PS):
        p = gbuf[pl.ds(base + s, rb, stride=PS), :]
        lo = lax.bitcast_convert_type(p << 16, F32)
        hi = lax.bitcast_convert_type(p & jnp.uint32(0xFFFF0000), F32)
        x_ref[:, s * LANES:(s + 1) * LANES] = lo.astype(x_ref.dtype)
        x_ref[:, half + s * LANES:half + (s + 1) * LANES] = hi.astype(x_ref.dtype)


def _dispatch(row_tok, u_slabs):
    n_rows = row_tok.shape[0]
    rb = DISPATCH_ROWS
    n_steps = n_rows // rb
    grid_spec = pltpu.PrefetchScalarGridSpec(
        num_scalar_prefetch=1,
        grid=(n_steps,),
        in_specs=[pl.BlockSpec(memory_space=pl.ANY)],
        out_specs=pl.BlockSpec((rb, D_MODEL), lambda b, rt: (b, 0)),
        scratch_shapes=[pltpu.VMEM((2 * rb * PACK_SLABS, LANES), jnp.uint32),
                        pltpu.SemaphoreType.DMA((2,))],
    )
    return pl.pallas_call(
        functools.partial(_dispatch_kernel, rb, n_steps),
        grid_spec=grid_spec,
        out_shape=jax.ShapeDtypeStruct((n_rows, D_MODEL), BF16),
        compiler_params=_params(("arbitrary",)),
        name="dispatch",
    )(row_tok, u_slabs)


def _expert_kernel(n_ff, item_e, item_row, item_nsub, item_nzero, item_live,
                   x_hbm, wga_ref, wgb_ref, bga_ref, bgb_ref, wda_ref, wdb_ref, bd_ref,
                   y_hbm, xbuf, acc, ybuf, wa_s, wb_s, wd_s, sem_in, sem_out):
    del item_e, item_live
    i = pl.program_id(0)
    f = pl.program_id(1)
    nsub = item_nsub[i]
    nzero = item_nzero[i]
    row0 = item_row[i]
    SUB = ROW_BLK
    YS = SUB * ROW_SLABS

    def hbm_rows(j):
        return pl.ds(pl.multiple_of(row0 + j * SUB, SUB), SUB)

    def vmem_rows(j):
        return pl.ds(pl.multiple_of(j * SUB, SUB), SUB)

    def x_copy(j):
        return pltpu.make_async_copy(x_hbm.at[hbm_rows(j)], xbuf.at[vmem_rows(j)], sem_in)

    def y_copy(slot, j):
        dst = y_hbm.at[pl.ds(pl.multiple_of((row0 + j * SUB) * ROW_SLABS, YS), YS)]
        return pltpu.make_async_copy(ybuf.at[pl.ds(slot * YS, YS)], dst, sem_out.at[slot])

    @pl.when(f == 0)
    def _():
        def start(j, c):
            x_copy(j).start()
            return c
        lax.fori_loop(0, nsub, start, 0)

    wa_s[...] = wga_ref[...].astype(BF16)
    wb_s[...] = wgb_ref[...].astype(BF16)
    wd_s[...] = pltpu.bitcast(_pack_bf16_pairs(wda_ref[...], wdb_ref[...]), BF16)
    lane = lax.broadcasted_iota(jnp.int32, (1, 2 * FF_TILE), 1)
    even = (lane % 2) == 0
    bga = bga_ref[...]
    bgb = bgb_ref[...]
    W2 = 2 * FF_TILE

    @pl.when(f == 0)
    def _():
        def wait(j, c):
            x_copy(j).wait()
            return c
        lax.fori_loop(0, nsub, wait, 0)

    def stage_y(slot, j, val, pending):
        @pl.when(pending)
        def _():
            y_copy(slot, 0).wait()
        for s in range(ROW_SLABS):
            ybuf[pl.ds(slot * YS + s, SUB, stride=ROW_SLABS), :] = val[:, s * LANES:(s + 1) * LANES]
        y_copy(slot, j).start()

    def block(r, rows, mode, j0, pending):
        xs = xbuf[pl.ds(r, rows), :]
        ga = _dot(xs, wa_s[...]) + bga
        gb = _dot(xs, wb_s[...]) + bgb
        gate = jnp.where(even, ga, pltpu.roll(gb, 1, 1))
        up = jnp.where(even, pltpu.roll(ga, W2 - 1, 1), gb)
        gate = jnp.minimum(gate, SWIGLU_LIMIT)
        up = jnp.clip(up, -SWIGLU_LIMIT, SWIGLU_LIMIT)
        act = (up + 1.0) * (gate * jax.nn.sigmoid(gate * SWIGLU_ALPHA))
        part = _dot(act.astype(BF16), wd_s[...])
        if mode == "first":
            acc[pl.ds(r, rows), :] = part
        elif mode == "mid":
            acc[pl.ds(r, rows), :] += part
        else:
            val = acc[pl.ds(r, rows), :] + part + bd_ref[...]
            for q in range(rows // SUB):
                stage_y(q, j0 + q, val[q * SUB:(q + 1) * SUB, :], pending)

    def run(mode):
        def pair(j, c):
            block(pl.multiple_of(j * (2 * SUB), 2 * SUB), 2 * SUB, mode, 2 * j, j > 0)
            return c
        lax.fori_loop(0, nsub // 2, pair, 0)

        @pl.when(nsub % 2 == 1)
        def _():
            block(pl.multiple_of((nsub - 1) * SUB, SUB), SUB, mode, nsub - 1, nsub > 1)

    @pl.when(f == 0)
    def _():
        run("first")

    @pl.when(jnp.logical_and(f > 0, f < n_ff - 1))
    def _():
        run("mid")

    @pl.when(f == n_ff - 1)
    def _():
        run("last")

        @pl.when(nsub > 0)
        def _():
            y_copy(0, 0).wait()

        @pl.when(nsub > 1)
        def _():
            y_copy(1, 0).wait()

    @pl.when(jnp.logical_and(f == n_ff - 1, nzero > 0))
    def _():
        ybuf[0:YS, :] = jnp.zeros((YS, LANES), F32)

        def start(j, c):
            y_copy(0, j).start()
            return c
        lax.fori_loop(0, nzero, start, 0)

        def wait(j, c):
            y_copy(0, 0).wait()
            return c
        lax.fori_loop(0, nzero, wait, 0)


def _experts(x_rows, items, w_gate_up, b_gate_up, w_down, b_down, n_items):
    n_rows = x_rows.shape[0]
    n_ff = (D_FF // 2) // FF_TILE
    W2 = 2 * FF_TILE
    item_e, item_row, item_nsub, item_nzero, item_live = items

    def ff(i, f, live):
        return jnp.where(live[i] > 0, f, n_ff - 1)

    grid_spec = pltpu.PrefetchScalarGridSpec(
        num_scalar_prefetch=5,
        grid=(n_items, n_ff),
        in_specs=[
            pl.BlockSpec(memory_space=pl.ANY),
            pl.BlockSpec((None, D_MODEL, W2), lambda i, f, e, r, n, z, lv: (e[i], 0, ff(i, f, lv))),
            pl.BlockSpec((None, D_MODEL, W2), lambda i, f, e, r, n, z, lv: (e[i], 0, n_ff + ff(i, f, lv))),
            pl.BlockSpec((None, 1, W2), lambda i, f, e, r, n, z, lv: (e[i], 0, ff(i, f, lv))),
            pl.BlockSpec((None, 1, W2), lambda i, f, e, r, n, z, lv: (e[i], 0, n_ff + ff(i, f, lv))),
            pl.BlockSpec((None, FF_TILE, D_MODEL), lambda i, f, e, r, n, z, lv: (e[i], ff(i, f, lv), 0)),
            pl.BlockSpec((None, FF_TILE, D_MODEL), lambda i, f, e, r, n, z, lv: (e[i], n_ff + ff(i, f, lv), 0)),
            pl.BlockSpec((None, 1, D_MODEL), lambda i, f, e, r, n, z, lv: (e[i], 0, 0)),
        ],
        out_specs=pl.BlockSpec(memory_space=pl.ANY),
        scratch_shapes=[pltpu.VMEM((ITEM_ROWS, D_MODEL), BF16),
                        pltpu.VMEM((ITEM_ROWS, D_MODEL), F32),
                        pltpu.VMEM((2 * ROW_BLK * ROW_SLABS, LANES), F32),
                        pltpu.VMEM((D_MODEL, W2), BF16),
                        pltpu.VMEM((D_MODEL, W2), BF16),
                        pltpu.VMEM((W2, D_MODEL), BF16),
                        pltpu.SemaphoreType.DMA(()),
                        pltpu.SemaphoreType.DMA((2,))],
    )
    return pl.pallas_call(
        functools.partial(_expert_kernel, n_ff),
        grid_spec=grid_spec,
        out_shape=jax.ShapeDtypeStruct((n_rows * ROW_SLABS, LANES), F32),
        compiler_params=_params(("arbitrary", "arbitrary")),
        name="experts",
    )(item_e, item_row, item_nsub, item_nzero, item_live, x_rows,
      w_gate_up, w_gate_up, b_gate_up.reshape(N_EXPERTS, 1, 2 * D_FF), b_gate_up.reshape(N_EXPERTS, 1, 2 * D_FF),
      w_down, w_down, b_down.reshape(N_EXPERTS, 1, D_MODEL))


def _combine_kernel(tm, n_steps, dest_ref, w_ref, y_hbm, h_ref, lnw_ref, o_ref, buf, sem):
    i = pl.program_id(0)

    def copy(step, slot, t, k):
        d = dest_ref[(step * tm + t) * TOP_K + k]
        return pltpu.make_async_copy(y_hbm.at[d], buf.at[slot, k * tm + t], sem.at[slot])

    def issue(step, slot):
        def body(t, c):
            for k in range(TOP_K):
                copy(step, slot, t, k).start()
            return c
        lax.fori_loop(0, tm, body, 0)

    @pl.when(i == 0)
    def _():
        issue(0, 0)

    @pl.when(i + 1 < n_steps)
    def _():
        issue(i + 1, (i + 1) % 2)

    slot = i % 2

    pltpu.make_async_copy(y_hbm.at[pl.ds(0, TOP_K * tm)], buf.at[slot], sem.at[slot]).wait()

    def token(t, c):
        a = h_ref[t]
        for k in range(TOP_K):
            a = a + w_ref[(i * tm + t) * TOP_K + k] * buf[slot, k * tm + t]
        o_ref[t] = a
        return c
    lax.fori_loop(0, tm, token, 0)

    h = o_ref[...]
    ms = jnp.mean(jnp.mean(h * h, axis=2, keepdims=True), axis=1, keepdims=True)
    o_ref[...] = h * lax.rsqrt(ms + EPS) * lnw_ref[...]


def _combine(dest, wflat, y3, h3, ln_w):
    T = h3.shape[0]
    tm = 128
    n_steps = T // tm
    grid_spec = pltpu.PrefetchScalarGridSpec(
        num_scalar_prefetch=2,
        grid=(n_steps,),
        in_specs=[pl.BlockSpec(memory_space=pl.ANY),
                  pl.BlockSpec((tm, ROW_SLABS, LANES), lambda i, d, w: (i, 0, 0)),
                  pl.BlockSpec((1, ROW_SLABS, LANES), lambda i, d, w: (0, 0, 0))],
        out_specs=pl.BlockSpec((tm, ROW_SLABS, LANES), lambda i, d, w: (i, 0, 0)),
        scratch_shapes=[pltpu.VMEM((2, TOP_K * tm, ROW_SLABS, LANES), F32),
                        pltpu.SemaphoreType.DMA((2,))],
    )
    return pl.pallas_call(
        functools.partial(_combine_kernel, tm, n_steps),
        grid_spec=grid_spec,
        out_shape=jax.ShapeDtypeStruct((T, ROW_SLABS, LANES), F32),
        compiler_params=_params(("arbitrary",)),
        name="combine",
    )(dest, wflat, y3, h3, ln_w.reshape(1, ROW_SLABS, LANES))


def _route(logits, n_rows, n_items):
    T = logits.shape[0]
    top_logits, top_idx = lax.top_k(logits, TOP_K)
    top_w = jax.nn.softmax(top_logits, axis=-1)
    e_flat = top_idx.reshape(-1).astype(jnp.int32)
    onehot = (e_flat[:, None] == jnp.arange(N_EXPERTS, dtype=jnp.int32)[None, :]).astype(jnp.int32)
    csum = jnp.cumsum(onehot, axis=0)
    rank = jnp.sum(onehot * (csum - onehot), axis=1)
    counts = csum[-1]
    padded = (counts + ROW_BLK - 1) // ROW_BLK * ROW_BLK
    pend = jnp.cumsum(padded)
    pstart = pend - padded
    dest = (pstart[e_flat] + rank).astype(jnp.int32)

    per_e = (padded + ITEM_ROWS - 1) // ITEM_ROWS
    iend = jnp.cumsum(per_e)
    istart = iend - per_e
    ii = jnp.arange(n_items, dtype=jnp.int32)
    total = iend[-1]
    live = (ii < total).astype(jnp.int32)
    ic = jnp.minimum(ii, total - 1)
    ie = jnp.minimum(jnp.searchsorted(iend, ic, side='right'), N_EXPERTS - 1).astype(jnp.int32)
    within = ic - istart[ie]
    irow = (pstart[ie] + within * ITEM_ROWS).astype(jnp.int32)
    insub = jnp.minimum((padded[ie] - within * ITEM_ROWS) // ROW_BLK, ITEM_ROWS // ROW_BLK).astype(jnp.int32)
    insub = insub * live
    tail_rows = n_rows - pend[-1]
    tt = ii - total
    tail = jnp.logical_and(tt >= 0, tt * ITEM_ROWS < tail_rows)
    inzero = jnp.where(tail, jnp.minimum(tail_rows - tt * ITEM_ROWS, ITEM_ROWS) // ROW_BLK, 0).astype(jnp.int32)
    irow = jnp.where(tail, pend[-1] + tt * ITEM_ROWS, irow).astype(jnp.int32)
    n_assign = T * TOP_K
    order = jnp.sort(e_flat * n_assign + jnp.arange(n_assign, dtype=jnp.int32)) % n_assign
    rr = jnp.arange(n_rows, dtype=jnp.int32)
    re = jnp.minimum(jnp.searchsorted(pend, rr, side='right'), N_EXPERTS - 1).astype(jnp.int32)
    within_e = rr - pstart[re]
    owned = jnp.logical_and(within_e < counts[re], rr < pend[-1])
    src = jnp.clip((jnp.cumsum(counts) - counts)[re] + within_e, 0, n_assign - 1)
    row_tok = jnp.where(owned, order[src] // TOP_K, 0).astype(jnp.int32)
    return dest, top_w.reshape(-1).astype(F32), (ie, irow, insub, inzero, live), row_tok


def kernel(x, positions, ln_mix_w, w_in, conv_w, conv_b, dt_bias, a_log, d_skip, ssm_norm_w, w_out,
           ln_ffn_w, w_router, b_router, w_gate_up, b_gate_up, w_down, b_down, ln_final_w):
    B, L, _ = x.shape
    T = B * L
    assert B == 1 and T % 1024 == 0
    x2 = x.reshape(T, D_MODEL)
    half = RET_HEAD_DIM // 2
    inv_freq = (ROPE_BASE ** (-jnp.arange(half, dtype=F32) / half)).reshape(1, half)
    pos_col = positions.reshape(T, 1).astype(F32)

    proj = _inproj(x2, ln_mix_w[0], jnp.swapaxes(w_in[0], 0, 1))
    ret = _retention(proj, pos_col, inv_freq)
    ssm = _ssd(proj, conv_w[0], conv_b[0], dt_bias[0], a_log[0], d_skip[0], ssm_norm_w[0])
    h_slabs, u_packed, logits = _outproj(ret, ssm, w_out[0], x2, ln_ffn_w[0], w_router[0], b_router[0])

    n_rows = -(-(T * TOP_K + N_EXPERTS * (ROW_BLK - 1)) // DISPATCH_ROWS) * DISPATCH_ROWS
    n_items = N_EXPERTS + 1 + n_rows // ITEM_ROWS
    dest, wflat, items, row_tok = _route(logits, n_rows, n_items)

    x_rows = _dispatch(row_tok, u_packed)
    y_rows = _experts(x_rows, items, w_gate_up[0], b_gate_up[0], w_down[0], b_down[0], n_items)
    out3 = _combine(dest, wflat, y_rows.reshape(n_rows, ROW_SLABS, LANES),
                    h_slabs.reshape(T, ROW_SLABS, LANES), ln_final_w)
    return out3.reshape(B, L, D_MODEL)
```

```python
import functools

import numpy as np
import jax
import jax.numpy as jnp
from jax import lax
from jax.experimental import pallas as pl
from jax.experimental.pallas import tpu as pltpu

F32 = jnp.float32
BF16 = jnp.bfloat16

D_MODEL = 2048
RET_HEADS = 4
RET_HEAD_DIM = 256
RET_WIDTH = RET_HEADS * RET_HEAD_DIM
SSM_WIDTH = D_MODEL - RET_WIDTH
SSM_HEAD_DIM = 64
SSM_HEADS = SSM_WIDTH // SSM_HEAD_DIM
SSM_GROUPS = 2
SSM_STATE = 128
CONV_WIDTH = 4
XBC_WIDTH = SSM_WIDTH + 2 * SSM_GROUPS * SSM_STATE
D_IN_PROJ = 4 * RET_WIDTH + SSM_WIDTH + XBC_WIDTH + SSM_HEADS
ROPE_BASE = 10000.0
N_EXPERTS = 32
TOP_K = 4
D_FF = D_MODEL
SWIGLU_LIMIT = 7.0
SWIGLU_ALPHA = 1.702
EPS = 1e-6

LANES = 128
VMEM_LIMIT = 56 * 1024 * 1024

RET_CHUNK = 256
SSD_CHUNK = 128
ROW_BLK = 256
ITEM_ROWS = 1536
FF_TILE = 128
ROW_SLABS = D_MODEL // LANES
PACK_SLABS = ROW_SLABS // 2


def _params(sem, **kw):
    return pltpu.CompilerParams(dimension_semantics=sem, vmem_limit_bytes=VMEM_LIMIT, **kw)


def _dot(a, b):
    return jnp.dot(a, b, preferred_element_type=F32)


def _dot_nt(a, b):
    return lax.dot_general(a, b, (((1,), (1,)), ((), ())), preferred_element_type=F32)


def _dot_tn(a, b):
    return lax.dot_general(a, b, (((0,), (0,)), ((), ())), preferred_element_type=F32)


def _split3(x):
    hi = x.astype(BF16)
    r = x - hi.astype(F32)
    mid = r.astype(BF16)
    lo = (r - mid.astype(F32)).astype(BF16)
    return hi, mid, lo


def _dot_exact_rhs01(x, m01):
    hi, mid, lo = _split3(x)
    return _dot(hi, m01) + _dot(mid, m01) + _dot(lo, m01)


def _dot_exact_lhs01(m01, x):
    hi, mid, lo = _split3(x)
    return _dot(m01, hi) + _dot(m01, mid) + _dot(m01, lo)


def _silu(x):
    return x * jax.nn.sigmoid(x)


def _pack_bf16_pairs(lo, hi):
    lo_bits = lax.bitcast_convert_type(lo.astype(BF16).astype(F32), jnp.uint32)
    hi_bits = lax.bitcast_convert_type(hi.astype(BF16).astype(F32), jnp.uint32)
    return hi_bits | (lo_bits >> 16)


def _inproj_kernel(x_ref, lnw_ref, w_ref, o_ref, u_ref):
    @pl.when(pl.program_id(1) == 0)
    def _():
        x = x_ref[...]
        ms = jnp.mean(x * x, axis=-1, keepdims=True)
        u_ref[...] = (x * lax.rsqrt(ms + EPS) * lnw_ref[...]).astype(BF16)

    o_ref[...] = _dot_nt(u_ref[...], w_ref[...].astype(BF16))


def _inproj(x2, ln_w, w_in_t):
    T = x2.shape[0]
    tm, tn = 1024, 512
    return pl.pallas_call(
        _inproj_kernel,
        grid=(T // tm, pl.cdiv(D_IN_PROJ, tn)),
        in_specs=[pl.BlockSpec((tm, D_MODEL), lambda i, j: (i, 0)),
                  pl.BlockSpec((1, D_MODEL), lambda i, j: (0, 0)),
                  pl.BlockSpec((tn, D_MODEL), lambda i, j: (j, 0))],
        out_specs=pl.BlockSpec((tm, tn), lambda i, j: (i, j)),
        out_shape=jax.ShapeDtypeStruct((T, D_IN_PROJ), F32),
        scratch_shapes=[pltpu.VMEM((tm, D_MODEL), BF16)],
        compiler_params=_params(("parallel", "arbitrary")),
        name="inproj",
    )(x2, ln_w.reshape(1, D_MODEL), w_in_t)


def _retention_tables():
    C = RET_CHUNK
    h = np.arange(RET_HEADS, dtype=np.float64)
    log_gamma = np.log1p(-np.exp2(-5.0 - h))
    idx = np.arange(C, dtype=np.float64)
    rel = idx[:, None] - idx[None, :]
    intra = np.where(rel >= 0, np.exp(log_gamma[:, None, None] * np.maximum(rel, 0.0)), 0.0)
    q_decay = np.exp(log_gamma[:, None] * (idx + 1.0))
    k_decay = np.exp(log_gamma[:, None] * (C - 1.0 - idx))
    chunk_decay = np.exp(log_gamma * C)
    qd = np.broadcast_to(q_decay[:, :, None], (RET_HEADS, C, RET_HEAD_DIM))
    kd = np.broadcast_to(k_decay[:, :, None], (RET_HEADS, C, RET_HEAD_DIM))
    return (jnp.asarray(intra, F32), jnp.asarray(qd, F32), jnp.asarray(kd, F32),
            [float(c) for c in chunk_decay])


def _retention_kernel(chunk_decay, pos_ref, invf_ref, q_ref, k_ref, v_ref, g_ref,
                      intra_ref, qd_ref, kd_ref, o_ref, state_ref):
    @pl.when(pl.program_id(0) == 0)
    def _():
        state_ref[...] = jnp.zeros_like(state_ref)

    half = RET_HEAD_DIM // 2
    ang = pos_ref[...] * invf_ref[...]
    cos = jnp.cos(ang)
    sin = jnp.sin(ang)

    def rope(t):
        t1, t2 = t[:, :half], t[:, half:]
        return jnp.concatenate([t1 * cos - t2 * sin, t2 * cos + t1 * sin], axis=-1)

    for h in range(RET_HEADS):
        sl = slice(h * RET_HEAD_DIM, (h + 1) * RET_HEAD_DIM)
        q = rope(q_ref[:, sl])
        k = rope(k_ref[:, sl]) * (RET_HEAD_DIM ** -0.5)
        v = v_ref[:, sl].astype(BF16)
        state = state_ref[h]
        scores = _dot_nt(q.astype(BF16), k.astype(BF16)) * intra_ref[h]
        inner = _dot(scores.astype(BF16), v)
        cross = _dot((q * qd_ref[h]).astype(BF16), state.astype(BF16))
        state_ref[h] = chunk_decay[h] * state + _dot_tn((k * kd_ref[h]).astype(BF16), v)
        o = inner + cross
        o = o * lax.rsqrt(jnp.mean(o * o, axis=-1, keepdims=True) + EPS)
        o_ref[:, sl] = (o * _silu(g_ref[:, sl])).astype(o_ref.dtype)


def _retention(proj, pos_col, inv_freq):
    T = proj.shape[0]
    C = RET_CHUNK
    intra, qd, kd, chunk_decay = _retention_tables()
    col = lambda j: pl.BlockSpec((C, RET_WIDTH), lambda c, j=j: (c, j))
    const3 = lambda shape: pl.BlockSpec(shape, lambda c: (0, 0, 0))
    return pl.pallas_call(
        functools.partial(_retention_kernel, chunk_decay),
        grid=(T // C,),
        in_specs=[pl.BlockSpec((C, 1), lambda c: (c, 0)),
                  pl.BlockSpec((1, RET_HEAD_DIM // 2), lambda c: (0, 0)),
                  col(0), col(1), col(2), col(3),
                  const3((RET_HEADS, C, C)),
                  const3((RET_HEADS, C, RET_HEAD_DIM)),
                  const3((RET_HEADS, C, RET_HEAD_DIM))],
        out_specs=pl.BlockSpec((C, RET_WIDTH), lambda c: (c, 0)),
        out_shape=jax.ShapeDtypeStruct((T, RET_WIDTH), BF16),
        scratch_shapes=[pltpu.VMEM((RET_HEADS, RET_HEAD_DIM, RET_HEAD_DIM), F32)],
        compiler_params=_params(("arbitrary",)),
        name="retention",
    )(pos_col, inv_freq, proj, proj, proj, proj, intra, qd, kd)


def _ssd_kernel(xs0_ref, xs1_ref, bc_ref, z_ref, dt_ref, convw_ref, convb_ref, dtb_ref, a_ref,
                dskip_ref, normw_ref, expand_ref, o_ref, ext_ref, state_ref):
    C = SSD_CHUNK
    HW = SSM_WIDTH // SSM_GROUPS
    CARRY = 8

    @pl.when(pl.program_id(0) == 0)
    def _():
        ext_ref[0:CARRY, :] = jnp.zeros((CARRY, XBC_WIDTH), F32)
        state_ref[...] = jnp.zeros_like(state_ref)

    ext_ref[CARRY:CARRY + C, 0:HW] = xs0_ref[...]
    ext_ref[CARRY:CARRY + C, HW:2 * HW] = xs1_ref[...]
    ext_ref[CARRY:CARRY + C, 2 * HW:3 * HW] = bc_ref[...]
    conv = convb_ref[...]
    for k in range(CONV_WIDTH):
        off = CARRY - (CONV_WIDTH - 1) + k
        conv = conv + convw_ref[k:k + 1, :] * ext_ref[off:off + C, :]
    ext_ref[0:CARRY, :] = ext_ref[C:C + CARRY, :]
    xbc = _silu(conv)
    xs = xbc[:, :SSM_WIDTH]

    lane = lax.broadcasted_iota(jnp.int32, (1, LANES), 1)
    dt_raw = jnp.where(lane < SSM_HEADS, dt_ref[...], 0.0) + dtb_ref[...]
    dt = jnp.maximum(dt_raw, 0.0) + jnp.log1p(jnp.exp(-jnp.abs(dt_raw)))
    dta = dt * a_ref[...]

    row = lax.broadcasted_iota(jnp.int32, (C, C), 0)
    colm = lax.broadcasted_iota(jnp.int32, (C, C), 1)
    tril = row >= colm
    a_cum = _dot_exact_lhs01(jnp.where(tril, 1.0, 0.0).astype(BF16), dta)
    a_cum_t = a_cum.T

    expand = expand_ref[...]
    a_exp = _dot_exact_rhs01(a_cum, expand)
    dt_exp = _dot_exact_rhs01(dt, expand)
    a_last = a_exp[C - 1:C, :]
    decay_in = jnp.exp(a_exp)
    decay_out = jnp.exp(a_last - a_exp)
    chunk_decay = jnp.exp(a_last)
    xdt = xs * dt_exp

    lane2 = lax.broadcasted_iota(jnp.int32, (1, LANES), 1)
    lo_head = lane2 < SSM_HEAD_DIM
    ys = []
    for g in range(SSM_GROUPS):
        gs = slice(g * HW, (g + 1) * HW)
        b_g = xbc[:, SSM_WIDTH + g * SSM_STATE:SSM_WIDTH + (g + 1) * SSM_STATE].astype(BF16)
        c0 = SSM_WIDTH + SSM_GROUPS * SSM_STATE
        c_g = xbc[:, c0 + g * SSM_STATE:c0 + (g + 1) * SSM_STATE].astype(BF16)
        cb = _dot_nt(c_g, b_g)
        state = state_ref[g]
        y_off = _dot(c_g, state.astype(BF16)) * decay_in[:, gs]
        xw = (xdt[:, gs] * decay_out[:, gs]).astype(BF16)
        state_ref[g] = chunk_decay[:, gs] * state + _dot_tn(b_g, xw)
        slabs = []
        for s in range(HW // LANES):
            xd = xdt[:, g * HW + s * LANES:g * HW + (s + 1) * LANES]
            acc = None
            for e in range(2):
                hh = g * (SSM_HEADS // SSM_GROUPS) + 2 * s + e
                seg = a_cum[:, hh:hh + 1] - a_cum_t[hh:hh + 1, :]
                m = cb * jnp.exp(jnp.where(tril, seg, -jnp.inf))
                xm = jnp.where(lo_head if e == 0 else jnp.logical_not(lo_head), xd, 0.0)
                part = _dot(m.astype(BF16), xm.astype(BF16))
                acc = part if acc is None else acc + part
            slabs.append(acc)
        ys.append(jnp.concatenate(slabs, axis=-1) + y_off)
    y = jnp.concatenate(ys, axis=-1) + dskip_ref[...] * xs
    y = y * _silu(z_ref[...])
    outs = []
    for g in range(SSM_GROUPS):
        yg = y[:, g * HW:(g + 1) * HW]
        outs.append(yg * lax.rsqrt(jnp.mean(yg * yg, axis=-1, keepdims=True) + EPS))
    o_ref[...] = (jnp.concatenate(outs, axis=-1) * normw_ref[...]).astype(o_ref.dtype)


def _ssd(proj, conv_w, conv_b, dt_bias, a_log, d_skip, ssm_norm_w):
    T = proj.shape[0]
    C = SSD_CHUNK
    HW = SSM_WIDTH // SSM_GROUPS
    xbc0 = (4 * RET_WIDTH + SSM_WIDTH) // HW
    dt0 = (D_IN_PROJ - SSM_HEADS) // LANES
    pad = lambda v: jnp.zeros((1, LANES), F32).at[0, :SSM_HEADS].set(v.astype(F32))
    a_neg = pad(-jnp.exp(a_log.astype(F32)))
    expand_np = np.zeros((LANES, SSM_WIDTH), np.float32)
    for hh in range(SSM_HEADS):
        expand_np[hh, hh * SSM_HEAD_DIM:(hh + 1) * SSM_HEAD_DIM] = 1.0
    expand = jnp.asarray(expand_np, BF16)
    dskip_exp = jnp.repeat(d_skip.astype(F32), SSM_HEAD_DIM).reshape(1, SSM_WIDTH)
    const = lambda shape: pl.BlockSpec(shape, lambda c: (0, 0))
    return pl.pallas_call(
        _ssd_kernel,
        grid=(T // C,),
        in_specs=[pl.BlockSpec((C, HW), lambda c: (c, xbc0)),
                  pl.BlockSpec((C, HW), lambda c: (c, xbc0 + 1)),
                  pl.BlockSpec((C, HW), lambda c: (c, xbc0 + 2)),
                  pl.BlockSpec((C, SSM_WIDTH), lambda c: (c, 4 * RET_WIDTH // SSM_WIDTH)),
                  pl.BlockSpec((C, LANES), lambda c: (c, dt0)),
                  const((CONV_WIDTH, XBC_WIDTH)), const((1, XBC_WIDTH)),
                  const((1, LANES)), const((1, LANES)),
                  const((1, SSM_WIDTH)), const((1, SSM_WIDTH)),
                  const((LANES, SSM_WIDTH))],
        out_specs=pl.BlockSpec((C, SSM_WIDTH), lambda c: (c, 0)),
        out_shape=jax.ShapeDtypeStruct((T, SSM_WIDTH), BF16),
        scratch_shapes=[pltpu.VMEM((C + 8, XBC_WIDTH), F32),
                        pltpu.VMEM((SSM_GROUPS, SSM_STATE, HW), F32)],
        compiler_params=_params(("arbitrary",)),
        name="ssd",
    )(proj, proj, proj, proj, proj, conv_w, conv_b.reshape(1, XBC_WIDTH), pad(dt_bias), a_neg,
      dskip_exp, ssm_norm_w.reshape(1, SSM_WIDTH), expand)


def _outproj_kernel(ret_ref, ssm_ref, w_ref, x_ref, lnw_ref, wr_ref, br_ref, hs_ref, up_ref, lg_ref):
    tm = x_ref.shape[0]
    h = (x_ref[...] + _dot(ret_ref[...], w_ref[0:RET_WIDTH, :])
         + _dot(ssm_ref[...], w_ref[RET_WIDTH:D_MODEL, :]))
    for s in range(ROW_SLABS):
        hs_ref[pl.ds(s, tm, stride=ROW_SLABS), :] = h[:, s * LANES:(s + 1) * LANES]
    u = h * lax.rsqrt(jnp.mean(h * h, axis=-1, keepdims=True) + EPS) * lnw_ref[...]
    packed = _pack_bf16_pairs(u[:, :D_MODEL // 2], u[:, D_MODEL // 2:])
    for s in range(PACK_SLABS):
        up_ref[pl.ds(s, tm, stride=PACK_SLABS), :] = packed[:, s * LANES:(s + 1) * LANES]
    uh, um, ul = _split3(u)
    wh, wm, wl = _split3(wr_ref[...])
    lg = (_dot(uh, wh) + (_dot(uh, wm) + _dot(um, wh))
          + (_dot(uh, wl) + _dot(um, wm) + _dot(ul, wh)))
    lg_ref[...] = lg + br_ref[...]


def _outproj(ret, ssm, w_out_bf16, x2, ln_w, w_router, b_router):
    T = x2.shape[0]
    tm = 512
    return pl.pallas_call(
        _outproj_kernel,
        grid=(T // tm,),
        in_specs=[pl.BlockSpec((tm, RET_WIDTH), lambda i: (i, 0)),
                  pl.BlockSpec((tm, SSM_WIDTH), lambda i: (i, 0)),
                  pl.BlockSpec((D_MODEL, D_MODEL), lambda i: (0, 0)),
                  pl.BlockSpec((tm, D_MODEL), lambda i: (i, 0)),
                  pl.BlockSpec((1, D_MODEL), lambda i: (0, 0)),
                  pl.BlockSpec((D_MODEL, N_EXPERTS), lambda i: (0, 0)),
                  pl.BlockSpec((1, N_EXPERTS), lambda i: (0, 0))],
        out_specs=[pl.BlockSpec((tm * ROW_SLABS, LANES), lambda i: (i, 0)),
                   pl.BlockSpec((tm * PACK_SLABS, LANES), lambda i: (i, 0)),
                   pl.BlockSpec((tm, N_EXPERTS), lambda i: (i, 0))],
        out_shape=[jax.ShapeDtypeStruct((T * ROW_SLABS, LANES), F32),
                   jax.ShapeDtypeStruct((T * PACK_SLABS, LANES), jnp.uint32),
                   jax.ShapeDtypeStruct((T, N_EXPERTS), F32)],
        compiler_params=_params(("parallel",)),
        name="outproj",
    )(ret, ssm, w_out_bf16, x2, ln_w.reshape(1, D_MODEL), w_router, b_router.reshape(1, N_EXPERTS))


def _expert_kernel(n_ff, n_items, item_e, item_row, item_nsub, item_nzero, item_live, row_tok,
                   u_hbm, wga_ref, wgb_ref, bga_ref, bgb_ref, wda_ref, wdb_ref, bd_ref,
                   y_hbm, stage, xbuf, acc, ybuf, wa_s, wb_s, wd_s, sem_in, sem_out):
    del item_e, item_live
    i = pl.program_id(0)
    f = pl.program_id(1)
    nsub = item_nsub[i]
    nzero = item_nzero[i]
    row0 = item_row[i]
    SUB = ROW_BLK
    YS = SUB * ROW_SLABS
    PS = PACK_SLABS
    XS = SUB * PS

    def y_copy(slot, j):
        dst = y_hbm.at[pl.ds(pl.multiple_of((row0 + j * SUB) * ROW_SLABS, YS), YS)]
        return pltpu.make_async_copy(ybuf.at[pl.ds(slot * YS, YS)], dst, sem_out.at[slot])

    def gather_copy(tok, r):
        src = u_hbm.at[pl.ds(pl.multiple_of(tok * PS, PS), PS)]
        return pltpu.make_async_copy(src, stage.at[pl.ds(pl.multiple_of(r * PS, PS), PS)], sem_in)

    def issue_gathers(item):
        base = item_row[item]

        def body(q, c):
            for p in range(2):
                r = 2 * q + p
                gather_copy(row_tok[base + r], r).start(priority=p)
            return c
        lax.fori_loop(0, item_nsub[item] * (SUB // 2), body, 0)

    @pl.when(jnp.logical_and(i == 0, f == 0))
    def _():
        issue_gathers(0)

    wa_s[...] = wga_ref[...].astype(BF16)
    wb_s[...] = wgb_ref[...].astype(BF16)
    wd_s[...] = pltpu.bitcast(_pack_bf16_pairs(wda_ref[...], wdb_ref[...]), BF16)
    lane = lax.broadcasted_iota(jnp.int32, (1, 2 * FF_TILE), 1)
    even = (lane % 2) == 0
    bga = bga_ref[...]
    bgb = bgb_ref[...]
    W2 = 2 * FF_TILE

    @pl.when(f == 0)
    def _():
        def wait(j, c):
            pltpu.make_async_copy(u_hbm.at[pl.ds(0, XS)], stage.at[pl.ds(0, XS)], sem_in).wait()
            return c
        lax.fori_loop(0, nsub, wait, 0)

        half = D_MODEL // 2

        def unpack(j, c):
            rows = pl.ds(pl.multiple_of(j * SUB, SUB), SUB)
            for s in range(PS):
                p = stage[pl.ds(j * XS + s, SUB, stride=PS), :]
                lo = lax.bitcast_convert_type(p << 16, F32)
                hi = lax.bitcast_convert_type(p & jnp.uint32(0xFFFF0000), F32)
                xbuf[rows, s * LANES:(s + 1) * LANES] = lo.astype(BF16)
                xbuf[rows, half + s * LANES:half + (s + 1) * LANES] = hi.astype(BF16)
            return c
        lax.fori_loop(0, nsub, unpack, 0)

    @pl.when(jnp.logical_and(f == 1, i + 1 < n_items))
    def _():
        issue_gathers(i + 1)

    def stage_y(slot, j, val, pending):
        @pl.when(pending)
        def _():
            y_copy(slot, 0).wait()
        for s in range(ROW_SLABS):
            ybuf[pl.ds(slot * YS + s, SUB, stride=ROW_SLABS), :] = val[:, s * LANES:(s + 1) * LANES]
        y_copy(slot, j).start()

    def block(r, rows, mode, j0, pending):
        xs = xbuf[pl.ds(r, rows), :]
        ga = _dot(xs, wa_s[...]) + bga
        gb = _dot(xs, wb_s[...]) + bgb
        gate = jnp.where(even, ga, pltpu.roll(gb, 1, 1))
        up = jnp.where(even, pltpu.roll(ga, W2 - 1, 1), gb)
        gate = jnp.minimum(gate, SWIGLU_LIMIT)
        up = jnp.clip(up, -SWIGLU_LIMIT, SWIGLU_LIMIT)
        act = (up + 1.0) * (gate * jax.nn.sigmoid(gate * SWIGLU_ALPHA))
        part = _dot(act.astype(BF16), wd_s[...])
        if mode == "first":
            acc[pl.ds(r, rows), :] = part
        elif mode == "mid":
            acc[pl.ds(r, rows), :] += part
        else:
            val = acc[pl.ds(r, rows), :] + part + bd_ref[...]
            for q in range(rows // SUB):
                stage_y(q, j0 + q, val[q * SUB:(q + 1) * SUB, :], pending)

    def run(mode):
        def pair(j, c):
            block(pl.multiple_of(j * (2 * SUB), 2 * SUB), 2 * SUB, mode, 2 * j, j > 0)
            return c
        lax.fori_loop(0, nsub // 2, pair, 0)

        @pl.when(nsub % 2 == 1)
        def _():
            block(pl.multiple_of((nsub - 1) * SUB, SUB), SUB, mode, nsub - 1, nsub > 1)

    @pl.when(f == 0)
    def _():
        run("first")

    @pl.when(jnp.logical_and(f > 0, f < n_ff - 1))
    def _():
        run("mid")

    @pl.when(f == n_ff - 1)
    def _():
        run("last")

        @pl.when(nsub > 0)
        def _():
            y_copy(0, 0).wait()

        @pl.when(nsub > 1)
        def _():
            y_copy(1, 0).wait()

    @pl.when(jnp.logical_and(f == n_ff - 1, nzero > 0))
    def _():
        ybuf[0:YS, :] = jnp.zeros((YS, LANES), F32)

        def start(j, c):
            y_copy(0, j).start()
            return c
        lax.fori_loop(0, nzero, start, 0)

        def wait(j, c):
            y_copy(0, 0).wait()
            return c
        lax.fori_loop(0, nzero, wait, 0)


def _experts(u_packed, row_tok, items, w_gate_up, b_gate_up, w_down, b_down, n_items):
    n_rows = row_tok.shape[0]
    n_ff = (D_FF // 2) // FF_TILE
    W2 = 2 * FF_TILE
    item_e, item_row, item_nsub, item_nzero, item_live = items

    def ff(i, f, live):
        return jnp.where(live[i] > 0, f, n_ff - 1)

    grid_spec = pltpu.PrefetchScalarGridSpec(
        num_scalar_prefetch=6,
        grid=(n_items, n_ff),
        in_specs=[
            pl.BlockSpec(memory_space=pl.ANY),
            pl.BlockSpec((None, D_MODEL, W2), lambda i, f, e, r, n, z, lv, rt: (e[i], 0, ff(i, f, lv))),
            pl.BlockSpec((None, D_MODEL, W2), lambda i, f, e, r, n, z, lv, rt: (e[i], 0, n_ff + ff(i, f, lv))),
            pl.BlockSpec((None, 1, W2), lambda i, f, e, r, n, z, lv, rt: (e[i], 0, ff(i, f, lv))),
            pl.BlockSpec((None, 1, W2), lambda i, f, e, r, n, z, lv, rt: (e[i], 0, n_ff + ff(i, f, lv))),
            pl.BlockSpec((None, FF_TILE, D_MODEL), lambda i, f, e, r, n, z, lv, rt: (e[i], ff(i, f, lv), 0)),
            pl.BlockSpec((None, FF_TILE, D_MODEL), lambda i, f, e, r, n, z, lv, rt: (e[i], n_ff + ff(i, f, lv), 0)),
            pl.BlockSpec((None, 1, D_MODEL), lambda i, f, e, r, n, z, lv, rt: (e[i], 0, 0)),
        ],
        out_specs=pl.BlockSpec(memory_space=pl.ANY),
        scratch_shapes=[pltpu.VMEM((ITEM_ROWS * PACK_SLABS, LANES), jnp.uint32),
                        pltpu.VMEM((ITEM_ROWS, D_MODEL), BF16),
                        pltpu.VMEM((ITEM_ROWS, D_MODEL), F32),
                        pltpu.VMEM((2 * ROW_BLK * ROW_SLABS, LANES), F32),
                        pltpu.VMEM((D_MODEL, W2), BF16),
                        pltpu.VMEM((D_MODEL, W2), BF16),
                        pltpu.VMEM((W2, D_MODEL), BF16),
                        pltpu.SemaphoreType.DMA(()),
                        pltpu.SemaphoreType.DMA((2,))],
    )
    return pl.pallas_call(
        functools.partial(_expert_kernel, n_ff, n_items),
        grid_spec=grid_spec,
        out_shape=jax.ShapeDtypeStruct((n_rows * ROW_SLABS, LANES), F32),
        compiler_params=_params(("arbitrary", "arbitrary")),
        name="experts",
    )(item_e, item_row, item_nsub, item_nzero, item_live, row_tok, u_packed,
      w_gate_up, w_gate_up, b_gate_up.reshape(N_EXPERTS, 1, 2 * D_FF), b_gate_up.reshape(N_EXPERTS, 1, 2 * D_FF),
      w_down, w_down, b_down.reshape(N_EXPERTS, 1, D_MODEL))


def _combine_kernel(tm, n_steps, dest_ref, w_ref, y_hbm, h_ref, lnw_ref, o_ref, buf, sem):
    i = pl.program_id(0)

    def copy(step, slot, t, k):
        d = dest_ref[(step * tm + t) * TOP_K + k]
        return pltpu.make_async_copy(y_hbm.at[d], buf.at[slot, k * tm + t], sem.at[slot])

    def issue(step, slot):
        def body(t, c):
            for k in range(TOP_K):
                copy(step, slot, t, k).start()
            return c
        lax.fori_loop(0, tm, body, 0)

    @pl.when(i == 0)
    def _():
        issue(0, 0)

    @pl.when(i + 1 < n_steps)
    def _():
        issue(i + 1, (i + 1) % 2)

    slot = i % 2

    pltpu.make_async_copy(y_hbm.at[pl.ds(0, TOP_K * tm)], buf.at[slot], sem.at[slot]).wait()

    def token(t, c):
        a = h_ref[t]
        for k in range(TOP_K):
            a = a + w_ref[(i * tm + t) * TOP_K + k] * buf[slot, k * tm + t]
        o_ref[t] = a
        return c
    lax.fori_loop(0, tm, token, 0)

    h = o_ref[...]
    ms = jnp.mean(jnp.mean(h * h, axis=2, keepdims=True), axis=1, keepdims=True)
    o_ref[...] = h * lax.rsqrt(ms + EPS) * lnw_ref[...]


def _combine(dest, wflat, y3, h3, ln_w):
    T = h3.shape[0]
    tm = 128
    n_steps = T // tm
    grid_spec = pltpu.PrefetchScalarGridSpec(
        num_scalar_prefetch=2,
        grid=(n_steps,),
        in_specs=[pl.BlockSpec(memory_space=pl.ANY),
                  pl.BlockSpec((tm, ROW_SLABS, LANES), lambda i, d, w: (i, 0, 0)),
                  pl.BlockSpec((1, ROW_SLABS, LANES), lambda i, d, w: (0, 0, 0))],
        out_specs=pl.BlockSpec((tm, ROW_SLABS, LANES), lambda i, d, w: (i, 0, 0)),
        scratch_shapes=[pltpu.VMEM((2, TOP_K * tm, ROW_SLABS, LANES), F32),
                        pltpu.SemaphoreType.DMA((2,))],
    )
    return pl.pallas_call(
        functools.partial(_combine_kernel, tm, n_steps),
        grid_spec=grid_spec,
        out_shape=jax.ShapeDtypeStruct((T, ROW_SLABS, LANES), F32),
        compiler_params=_params(("arbitrary",)),
        name="combine",
    )(dest, wflat, y3, h3, ln_w.reshape(1, ROW_SLABS, LANES))


def _route(logits, n_rows, n_items):
    T = logits.shape[0]
    top_logits, top_idx = lax.top_k(logits, TOP_K)
    top_w = jax.nn.softmax(top_logits, axis=-1)
    e_flat = top_idx.reshape(-1).astype(jnp.int32)
    onehot = (e_flat[:, None] == jnp.arange(N_EXPERTS, dtype=jnp.int32)[None, :]).astype(jnp.int32)
    csum = jnp.cumsum(onehot, axis=0)
    counts = csum[-1]
    padded = (counts + ROW_BLK - 1) // ROW_BLK * ROW_BLK
    pend = jnp.cumsum(padded)
    pstart = pend - padded
    dest = jnp.sum(onehot * (pstart[None, :] + csum - onehot), axis=1).astype(jnp.int32)

    per_e = (padded + ITEM_ROWS - 1) // ITEM_ROWS
    iend = jnp.cumsum(per_e)
    istart = iend - per_e
    ii = jnp.arange(n_items, dtype=jnp.int32)
    total = iend[-1]
    live = (ii < total).astype(jnp.int32)
    ic = jnp.minimum(ii, total - 1)
    ie = jnp.minimum(jnp.sum(ic[:, None] >= iend[None, :], axis=1), N_EXPERTS - 1).astype(jnp.int32)
    within = ic - istart[ie]
    irow = (pstart[ie] + within * ITEM_ROWS).astype(jnp.int32)
    insub = jnp.minimum((padded[ie] - within * ITEM_ROWS) // ROW_BLK, ITEM_ROWS // ROW_BLK).astype(jnp.int32)
    insub = insub * live
    tail_rows = n_rows - pend[-1]
    tt = ii - total
    tail = jnp.logical_and(tt >= 0, tt * ITEM_ROWS < tail_rows)
    inzero = jnp.where(tail, jnp.minimum(tail_rows - tt * ITEM_ROWS, ITEM_ROWS) // ROW_BLK, 0).astype(jnp.int32)
    irow = jnp.where(tail, pend[-1] + tt * ITEM_ROWS, irow).astype(jnp.int32)
    n_assign = T * TOP_K
    order = jnp.sort(e_flat * n_assign + jnp.arange(n_assign, dtype=jnp.int32)) % n_assign
    rr = jnp.arange(n_rows, dtype=jnp.int32)[:, None]
    cstart = jnp.cumsum(counts) - counts
    owner = jnp.logical_and(rr >= pstart[None, :], rr < (pstart + counts)[None, :])
    owned = jnp.any(owner, axis=1)
    src = jnp.sum(jnp.where(owner, rr - pstart[None, :] + cstart[None, :], 0), axis=1)
    row_tok = jnp.where(owned, order[jnp.clip(src, 0, n_assign - 1)] // TOP_K, 0).astype(jnp.int32)
    return dest, top_w.reshape(-1).astype(F32), (ie, irow, insub, inzero, live), row_tok


def kernel(x, positions, ln_mix_w, w_in, conv_w, conv_b, dt_bias, a_log, d_skip, ssm_norm_w, w_out,
           ln_ffn_w, w_router, b_router, w_gate_up, b_gate_up, w_down, b_down, ln_final_w):
    B, L, _ = x.shape
    T = B * L
    assert B == 1 and T % 1024 == 0
    x2 = x.reshape(T, D_MODEL)
    half = RET_HEAD_DIM // 2
    inv_freq = (ROPE_BASE ** (-jnp.arange(half, dtype=F32) / half)).reshape(1, half)
    pos_col = positions.reshape(T, 1).astype(F32)

    proj = _inproj(x2, ln_mix_w[0], jnp.swapaxes(w_in[0], 0, 1))
    ret = _retention(proj, pos_col, inv_freq)
    ssm = _ssd(proj, conv_w[0], conv_b[0], dt_bias[0], a_log[0], d_skip[0], ssm_norm_w[0])
    h_slabs, u_packed, logits = _outproj(ret, ssm, w_out[0].astype(BF16), x2, ln_ffn_w[0], w_router[0],
                                         b_router[0])

    n_rows = -(-(T * TOP_K + N_EXPERTS * (ROW_BLK - 1)) // ROW_BLK) * ROW_BLK
    n_items = N_EXPERTS + 1 + n_rows // ITEM_ROWS
    dest, wflat, items, row_tok = _route(logits, n_rows, n_items)

    y_rows = _experts(u_packed, row_tok, items, w_gate_up[0], b_gate_up[0], w_down[0], b_down[0], n_items)
    out3 = _combine(dest, wflat, y_rows.reshape(n_rows, ROW_SLABS, LANES),
                    h_slabs.reshape(T, ROW_SLABS, LANES), ln_final_w)
    return out3.reshape(B, L, D_MODEL)
```

```python
import functools

import numpy as np
import jax
import jax.numpy as jnp
from jax import lax
from jax.experimental import pallas as pl
from jax.experimental.pallas import tpu as pltpu

F32 = jnp.float32
BF16 = jnp.bfloat16

D_MODEL = 2048
RET_HEADS = 4
RET_HEAD_DIM = 256
RET_WIDTH = RET_HEADS * RET_HEAD_DIM
SSM_WIDTH = D_MODEL - RET_WIDTH
SSM_HEAD_DIM = 64
SSM_HEADS = SSM_WIDTH // SSM_HEAD_DIM
SSM_GROUPS = 2
SSM_STATE = 128
CONV_WIDTH = 4
XBC_WIDTH = SSM_WIDTH + 2 * SSM_GROUPS * SSM_STATE
D_IN_PROJ = 4 * RET_WIDTH + SSM_WIDTH + XBC_WIDTH + SSM_HEADS
ROPE_BASE = 10000.0
N_EXPERTS = 32
TOP_K = 4
D_FF = D_MODEL
SWIGLU_LIMIT = 7.0
SWIGLU_ALPHA = 1.702
EPS = 1e-6

LANES = 128
VMEM_LIMIT = 56 * 1024 * 1024

RET_CHUNK = 256
SSD_CHUNK = 128
ROW_BLK = 256
ITEM_ROWS = 1536
FF_TILE = 128
ROW_SLABS = D_MODEL // LANES
PACK_SLABS = ROW_SLABS // 2


def _params(sem, **kw):
    return pltpu.CompilerParams(dimension_semantics=sem, vmem_limit_bytes=VMEM_LIMIT, **kw)


def _dot(a, b):
    return jnp.dot(a, b, preferred_element_type=F32)


def _dot_nt(a, b):
    return lax.dot_general(a, b, (((1,), (1,)), ((), ())), preferred_element_type=F32)


def _dot_tn(a, b):
    return lax.dot_general(a, b, (((0,), (0,)), ((), ())), preferred_element_type=F32)


def _split3(x):
    hi = x.astype(BF16)
    r = x - hi.astype(F32)
    mid = r.astype(BF16)
    lo = (r - mid.astype(F32)).astype(BF16)
    return hi, mid, lo


def _dot_exact_rhs01(x, m01):
    hi, mid, lo = _split3(x)
    return _dot(hi, m01) + _dot(mid, m01) + _dot(lo, m01)


def _dot_exact_lhs01(m01, x):
    hi, mid, lo = _split3(x)
    return _dot(m01, hi) + _dot(m01, mid) + _dot(m01, lo)


def _silu(x):
    return x * jax.nn.sigmoid(x)


def _pack_bf16_pairs(lo, hi):
    lo_bits = lax.bitcast_convert_type(lo.astype(BF16).astype(F32), jnp.uint32)
    hi_bits = lax.bitcast_convert_type(hi.astype(BF16).astype(F32), jnp.uint32)
    return hi_bits | (lo_bits >> 16)


INPROJ_CHUNKS = 4


def _inproj_kernel(x_hbm, lnw_ref, w_ref, o_ref, xs_ref, u_ref, sem):
    tm = u_ref.shape[0]
    rows = tm // INPROJ_CHUNKS
    i = pl.program_id(0)

    @pl.when(pl.program_id(1) == 0)
    def _():
        def copy(c):
            return pltpu.make_async_copy(x_hbm.at[pl.ds(i * tm + c * rows, rows)], xs_ref.at[c % 2], sem.at[c % 2])

        copy(0).start()
        copy(1).start()
        for c in range(INPROJ_CHUNKS):
            copy(c).wait()
            x = xs_ref[c % 2]
            ms = jnp.mean(x * x, axis=-1, keepdims=True)
            u_ref[c * rows:(c + 1) * rows, :] = (x * lax.rsqrt(ms + EPS) * lnw_ref[...]).astype(BF16)
            if c + 2 < INPROJ_CHUNKS:
                copy(c + 2).start()

    o_ref[...] = _dot_nt(u_ref[...], w_ref[...].astype(BF16))


def _inproj(x2, ln_w, w_in_t):
    T = x2.shape[0]
    tm, tn = (2048 if T % 2048 == 0 else 1024), 512
    return pl.pallas_call(
        _inproj_kernel,
        grid=(T // tm, pl.cdiv(D_IN_PROJ, tn)),
        in_specs=[pl.BlockSpec(memory_space=pl.ANY),
                  pl.BlockSpec((1, D_MODEL), lambda i, j: (0, 0)),
                  pl.BlockSpec((tn, D_MODEL), lambda i, j: (j, 0))],
        out_specs=pl.BlockSpec((tm, tn), lambda i, j: (i, j)),
        out_shape=jax.ShapeDtypeStruct((T, D_IN_PROJ), F32),
        scratch_shapes=[pltpu.VMEM((2, tm // INPROJ_CHUNKS, D_MODEL), F32),
                        pltpu.VMEM((tm, D_MODEL), BF16),
                        pltpu.SemaphoreType.DMA((2,))],
        compiler_params=_params(("arbitrary", "arbitrary")),
        name="inproj",
    )(x2, ln_w.reshape(1, D_MODEL), w_in_t)


def _retention_tables():
    C = RET_CHUNK
    h = np.arange(RET_HEADS, dtype=np.float64)
    log_gamma = np.log1p(-np.exp2(-5.0 - h))
    idx = np.arange(C, dtype=np.float64)
    rel = idx[:, None] - idx[None, :]
    intra = np.where(rel >= 0, np.exp(log_gamma[:, None, None] * np.maximum(rel, 0.0)), 0.0)
    q_decay = np.exp(log_gamma[:, None] * (idx + 1.0))
    k_decay = np.exp(log_gamma[:, None] * (C - 1.0 - idx))
    chunk_decay = np.exp(log_gamma * C)
    qd = np.broadcast_to(q_decay[:, :, None], (RET_HEADS, C, RET_HEAD_DIM))
    kd = np.broadcast_to(k_decay[:, :, None], (RET_HEADS, C, RET_HEAD_DIM))
    return (jnp.asarray(intra, F32), jnp.asarray(qd, F32), jnp.asarray(kd, F32),
            [float(c) for c in chunk_decay])


def _retention_kernel(chunk_decay, pos_ref, invf_ref, q_ref, k_ref, v_ref, g_ref,
                      intra_ref, qd_ref, kd_ref, o_ref, state_ref):
    @pl.when(pl.program_id(0) == 0)
    def _():
        state_ref[...] = jnp.zeros_like(state_ref)

    half = RET_HEAD_DIM // 2
    ang = pos_ref[...] * invf_ref[...]
    cos = jnp.cos(ang)
    sin = jnp.sin(ang)

    def rope(t):
        t1, t2 = t[:, :half], t[:, half:]
        return jnp.concatenate([t1 * cos - t2 * sin, t2 * cos + t1 * sin], axis=-1)

    for h in range(RET_HEADS):
        sl = slice(h * RET_HEAD_DIM, (h + 1) * RET_HEAD_DIM)
        q = rope(q_ref[:, sl])
        k = rope(k_ref[:, sl]) * (RET_HEAD_DIM ** -0.5)
        v = v_ref[:, sl].astype(BF16)
        state = state_ref[h]
        scores = _dot_nt(q.astype(BF16), k.astype(BF16)) * intra_ref[h]
        inner = _dot(scores.astype(BF16), v)
        cross = _dot((q * qd_ref[h]).astype(BF16), state.astype(BF16))
        state_ref[h] = chunk_decay[h] * state + _dot_tn((k * kd_ref[h]).astype(BF16), v)
        o = inner + cross
        o = o * lax.rsqrt(jnp.mean(o * o, axis=-1, keepdims=True) + EPS)
        o_ref[:, sl] = (o * _silu(g_ref[:, sl])).astype(o_ref.dtype)


def _retention(proj, pos_col, inv_freq):
    T = proj.shape[0]
    C = RET_CHUNK
    intra, qd, kd, chunk_decay = _retention_tables()
    col = lambda j: pl.BlockSpec((C, RET_WIDTH), lambda c, j=j: (c, j))
    const3 = lambda shape: pl.BlockSpec(shape, lambda c: (0, 0, 0))
    return pl.pallas_call(
        functools.partial(_retention_kernel, chunk_decay),
        grid=(T // C,),
        in_specs=[pl.BlockSpec((C, 1), lambda c: (c, 0)),
                  pl.BlockSpec((1, RET_HEAD_DIM // 2), lambda c: (0, 0)),
                  col(0), col(1), col(2), col(3),
                  const3((RET_HEADS, C, C)),
                  const3((RET_HEADS, C, RET_HEAD_DIM)),
                  const3((RET_HEADS, C, RET_HEAD_DIM))],
        out_specs=pl.BlockSpec((C, RET_WIDTH), lambda c: (c, 0)),
        out_shape=jax.ShapeDtypeStruct((T, RET_WIDTH), BF16),
        scratch_shapes=[pltpu.VMEM((RET_HEADS, RET_HEAD_DIM, RET_HEAD_DIM), F32)],
        compiler_params=_params(("arbitrary",)),
        name="retention",
    )(pos_col, inv_freq, proj, proj, proj, proj, intra, qd, kd)


def _ssd_kernel(xs0_ref, xs1_ref, bc_ref, z_ref, dt_ref, convw_ref, convb_ref, dtb_ref, a_ref,
                dskip_ref, normw_ref, expand_ref, o_ref, ext_ref, state_ref):
    C = SSD_CHUNK
    HW = SSM_WIDTH // SSM_GROUPS
    CARRY = 8

    @pl.when(pl.program_id(0) == 0)
    def _():
        ext_ref[0:CARRY, :] = jnp.zeros((CARRY, XBC_WIDTH), F32)
        state_ref[...] = jnp.zeros_like(state_ref)

    ext_ref[CARRY:CARRY + C, 0:HW] = xs0_ref[...]
    ext_ref[CARRY:CARRY + C, HW:2 * HW] = xs1_ref[...]
    ext_ref[CARRY:CARRY + C, 2 * HW:3 * HW] = bc_ref[...]
    conv = convb_ref[...]
    for k in range(CONV_WIDTH):
        off = CARRY - (CONV_WIDTH - 1) + k
        conv = conv + convw_ref[k:k + 1, :] * ext_ref[off:off + C, :]
    ext_ref[0:CARRY, :] = ext_ref[C:C + CARRY, :]
    xbc = _silu(conv)
    xs = xbc[:, :SSM_WIDTH]

    lane = lax.broadcasted_iota(jnp.int32, (1, LANES), 1)
    dt_raw = jnp.where(lane < SSM_HEADS, dt_ref[...], 0.0) + dtb_ref[...]
    dt = jnp.maximum(dt_raw, 0.0) + jnp.log1p(jnp.exp(-jnp.abs(dt_raw)))
    dta = dt * a_ref[...]

    row = lax.broadcasted_iota(jnp.int32, (C, C), 0)
    colm = lax.broadcasted_iota(jnp.int32, (C, C), 1)
    tril = row >= colm
    a_cum = _dot_exact_lhs01(jnp.where(tril, 1.0, 0.0).astype(BF16), dta)
    a_cum_t = a_cum.T

    expand = expand_ref[...]
    a_exp = _dot_exact_rhs01(a_cum, expand)
    dt_exp = _dot_exact_rhs01(dt, expand)
    a_last = a_exp[C - 1:C, :]
    decay_in = jnp.exp(a_exp)
    decay_out = jnp.exp(a_last - a_exp)
    chunk_decay = jnp.exp(a_last)
    xdt = xs * dt_exp

    lane2 = lax.broadcasted_iota(jnp.int32, (1, LANES), 1)
    lo_head = lane2 < SSM_HEAD_DIM
    ys = []
    for g in range(SSM_GROUPS):
        gs = slice(g * HW, (g + 1) * HW)
        b_g = xbc[:, SSM_WIDTH + g * SSM_STATE:SSM_WIDTH + (g + 1) * SSM_STATE].astype(BF16)
        c0 = SSM_WIDTH + SSM_GROUPS * SSM_STATE
        c_g = xbc[:, c0 + g * SSM_STATE:c0 + (g + 1) * SSM_STATE].astype(BF16)
        cb = _dot_nt(c_g, b_g)
        state = state_ref[g]
        y_off = _dot(c_g, state.astype(BF16)) * decay_in[:, gs]
        xw = (xdt[:, gs] * decay_out[:, gs]).astype(BF16)
        state_ref[g] = chunk_decay[:, gs] * state + _dot_tn(b_g, xw)
        slabs = []
        for s in range(HW // LANES):
            xd = xdt[:, g * HW + s * LANES:g * HW + (s + 1) * LANES]
            acc = None
            for e in range(2):
                hh = g * (SSM_HEADS // SSM_GROUPS) + 2 * s + e
                seg = a_cum[:, hh:hh + 1] - a_cum_t[hh:hh + 1, :]
                m = cb * jnp.exp(jnp.where(tril, seg, -jnp.inf))
                xm = jnp.where(lo_head if e == 0 else jnp.logical_not(lo_head), xd, 0.0)
                part = _dot(m.astype(BF16), xm.astype(BF16))
                acc = part if acc is None else acc + part
            slabs.append(acc)
        ys.append(jnp.concatenate(slabs, axis=-1) + y_off)
    y = jnp.concatenate(ys, axis=-1) + dskip_ref[...] * xs
    y = y * _silu(z_ref[...])
    outs = []
    for g in range(SSM_GROUPS):
        yg = y[:, g * HW:(g + 1) * HW]
        outs.append(yg * lax.rsqrt(jnp.mean(yg * yg, axis=-1, keepdims=True) + EPS))
    o_ref[...] = (jnp.concatenate(outs, axis=-1) * normw_ref[...]).astype(o_ref.dtype)


def _ssd(proj, conv_w, conv_b, dt_bias, a_log, d_skip, ssm_norm_w):
    T = proj.shape[0]
    C = SSD_CHUNK
    HW = SSM_WIDTH // SSM_GROUPS
    xbc0 = (4 * RET_WIDTH + SSM_WIDTH) // HW
    dt0 = (D_IN_PROJ - SSM_HEADS) // LANES
    pad = lambda v: jnp.zeros((1, LANES), F32).at[0, :SSM_HEADS].set(v.astype(F32))
    a_neg = pad(-jnp.exp(a_log.astype(F32)))
    expand_np = np.zeros((LANES, SSM_WIDTH), np.float32)
    for hh in range(SSM_HEADS):
        expand_np[hh, hh * SSM_HEAD_DIM:(hh + 1) * SSM_HEAD_DIM] = 1.0
    expand = jnp.asarray(expand_np, BF16)
    dskip_exp = jnp.repeat(d_skip.astype(F32), SSM_HEAD_DIM).reshape(1, SSM_WIDTH)
    const = lambda shape: pl.BlockSpec(shape, lambda c: (0, 0))
    return pl.pallas_call(
        _ssd_kernel,
        grid=(T // C,),
        in_specs=[pl.BlockSpec((C, HW), lambda c: (c, xbc0)),
                  pl.BlockSpec((C, HW), lambda c: (c, xbc0 + 1)),
                  pl.BlockSpec((C, HW), lambda c: (c, xbc0 + 2)),
                  pl.BlockSpec((C, SSM_WIDTH), lambda c: (c, 4 * RET_WIDTH // SSM_WIDTH)),
                  pl.BlockSpec((C, LANES), lambda c: (c, dt0)),
                  const((CONV_WIDTH, XBC_WIDTH)), const((1, XBC_WIDTH)),
                  const((1, LANES)), const((1, LANES)),
                  const((1, SSM_WIDTH)), const((1, SSM_WIDTH)),
                  const((LANES, SSM_WIDTH))],
        out_specs=pl.BlockSpec((C, SSM_WIDTH), lambda c: (c, 0)),
        out_shape=jax.ShapeDtypeStruct((T, SSM_WIDTH), BF16),
        scratch_shapes=[pltpu.VMEM((C + 8, XBC_WIDTH), F32),
                        pltpu.VMEM((SSM_GROUPS, SSM_STATE, HW), F32)],
        compiler_params=_params(("arbitrary",)),
        name="ssd",
    )(proj, proj, proj, proj, proj, conv_w, conv_b.reshape(1, XBC_WIDTH), pad(dt_bias), a_neg,
      dskip_exp, ssm_norm_w.reshape(1, SSM_WIDTH), expand)


def _outproj_kernel(ret_ref, ssm_ref, w_ref, x_ref, lnw_ref, wr_ref, br_ref, hs_ref, up_ref, lg_ref):
    tm = x_ref.shape[0]
    h = (x_ref[...] + _dot(ret_ref[...], w_ref[0:RET_WIDTH, :])
         + _dot(ssm_ref[...], w_ref[RET_WIDTH:D_MODEL, :]))
    for s in range(ROW_SLABS):
        hs_ref[pl.ds(s, tm, stride=ROW_SLABS), :] = h[:, s * LANES:(s + 1) * LANES]
    u = h * lax.rsqrt(jnp.mean(h * h, axis=-1, keepdims=True) + EPS) * lnw_ref[...]
    packed = _pack_bf16_pairs(u[:, :D_MODEL // 2], u[:, D_MODEL // 2:])
    for s in range(PACK_SLABS):
        up_ref[pl.ds(s, tm, stride=PACK_SLABS), :] = packed[:, s * LANES:(s + 1) * LANES]
    uh, um, ul = _split3(u)
    wh, wm, wl = _split3(wr_ref[...])
    lg = (_dot(uh, wh) + (_dot(uh, wm) + _dot(um, wh))
          + (_dot(uh, wl) + _dot(um, wm) + _dot(ul, wh)))
    lg_ref[...] = lg + br_ref[...]


def _outproj(ret, ssm, w_out_bf16, x2, ln_w, w_router, b_router):
    T = x2.shape[0]
    tm = 512
    return pl.pallas_call(
        _outproj_kernel,
        grid=(T // tm,),
        in_specs=[pl.BlockSpec((tm, RET_WIDTH), lambda i: (i, 0)),
                  pl.BlockSpec((tm, SSM_WIDTH), lambda i: (i, 0)),
                  pl.BlockSpec((D_MODEL, D_MODEL), lambda i: (0, 0)),
                  pl.BlockSpec((tm, D_MODEL), lambda i: (i, 0)),
                  pl.BlockSpec((1, D_MODEL), lambda i: (0, 0)),
                  pl.BlockSpec((D_MODEL, N_EXPERTS), lambda i: (0, 0)),
                  pl.BlockSpec((1, N_EXPERTS), lambda i: (0, 0))],
        out_specs=[pl.BlockSpec((tm * ROW_SLABS, LANES), lambda i: (i, 0)),
                   pl.BlockSpec((tm * PACK_SLABS, LANES), lambda i: (i, 0)),
                   pl.BlockSpec((tm, N_EXPERTS), lambda i: (i, 0))],
        out_shape=[jax.ShapeDtypeStruct((T * ROW_SLABS, LANES), F32),
                   jax.ShapeDtypeStruct((T * PACK_SLABS, LANES), jnp.uint32),
                   jax.ShapeDtypeStruct((T, N_EXPERTS), F32)],
        compiler_params=_params(("parallel",)),
        name="outproj",
    )(ret, ssm, w_out_bf16, x2, ln_w.reshape(1, D_MODEL), w_router, b_router.reshape(1, N_EXPERTS))


def _expert_kernel(n_ff, n_items, aliased, *refs):
    (item_e, item_row, item_nsub, item_nzero, item_live, row_tok,
     u_hbm, wga_ref, wgb_ref, bga_ref, bgb_ref, wda_ref, wdb_ref, bd_ref) = refs[:14]
    (y_hbm, stage, xbuf, acc, ybuf, wa_s, wb_s, wd_s, sem_in, sem_out) = refs[15:] if aliased else refs[14:]
    del item_e, item_live
    i = pl.program_id(0)
    f = pl.program_id(1)
    nsub = item_nsub[i]
    nzero = item_nzero[i]
    row0 = item_row[i]
    SUB = ROW_BLK
    YS = SUB * ROW_SLABS
    PS = PACK_SLABS
    XS = SUB * PS

    def y_copy(slot, j):
        dst = y_hbm.at[pl.ds(pl.multiple_of((row0 + j * SUB) * ROW_SLABS, YS), YS)]
        return pltpu.make_async_copy(ybuf.at[pl.ds(slot * YS, YS)], dst, sem_out.at[slot])

    def gather_copy(tok, r):
        src = u_hbm.at[pl.ds(pl.multiple_of(tok * PS, PS), PS)]
        return pltpu.make_async_copy(src, stage.at[pl.ds(pl.multiple_of(r * PS, PS), PS)], sem_in)

    def issue_gathers(item):
        base = item_row[item]

        def body(q, c):
            for p in range(2):
                r = 2 * q + p
                gather_copy(row_tok[base + r], r).start(priority=p)
            return c
        lax.fori_loop(0, item_nsub[item] * (SUB // 2), body, 0)

    @pl.when(jnp.logical_and(i == 0, f == 0))
    def _():
        issue_gathers(0)

    @pl.when(nsub > 0)
    def _():
        wa_s[...] = wga_ref[...].astype(BF16)
        wb_s[...] = wgb_ref[...].astype(BF16)
        wd_s[...] = pltpu.bitcast(_pack_bf16_pairs(wda_ref[...], wdb_ref[...]), BF16)

    lane = lax.broadcasted_iota(jnp.int32, (1, 2 * FF_TILE), 1)
    even = (lane % 2) == 0
    bga = bga_ref[...]
    bgb = bgb_ref[...]
    W2 = 2 * FF_TILE

    @pl.when(f == 0)
    def _():
        def wait(j, c):
            pltpu.make_async_copy(u_hbm.at[pl.ds(0, XS)], stage.at[pl.ds(0, XS)], sem_in).wait()
            return c
        lax.fori_loop(0, nsub, wait, 0)

        half = D_MODEL // 2

        def unpack(j, c):
            rows = pl.ds(pl.multiple_of(j * SUB, SUB), SUB)
            for s in range(PS):
                p = stage[pl.ds(j * XS + s, SUB, stride=PS), :]
                lo = lax.bitcast_convert_type(p << 16, F32)
                hi = lax.bitcast_convert_type(p & jnp.uint32(0xFFFF0000), F32)
                xbuf[rows, s * LANES:(s + 1) * LANES] = lo.astype(BF16)
                xbuf[rows, half + s * LANES:half + (s + 1) * LANES] = hi.astype(BF16)
            return c
        lax.fori_loop(0, nsub, unpack, 0)

    @pl.when(jnp.logical_and(f == 1, i + 1 < n_items))
    def _():
        issue_gathers(i + 1)

    def stage_y(slot, j, val, pending):
        @pl.when(pending)
        def _():
            y_copy(slot, 0).wait()
        for s in range(ROW_SLABS):
            ybuf[pl.ds(slot * YS + s, SUB, stride=ROW_SLABS), :] = val[:, s * LANES:(s + 1) * LANES]
        y_copy(slot, j).start()

    def block(r, rows, mode, j0, pending):
        xs = xbuf[pl.ds(r, rows), :]
        ga = _dot(xs, wa_s[...]) + bga
        gb = _dot(xs, wb_s[...]) + bgb
        gate = jnp.where(even, ga, pltpu.roll(gb, 1, 1))
        up = jnp.where(even, pltpu.roll(ga, W2 - 1, 1), gb)
        gate = jnp.minimum(gate, SWIGLU_LIMIT)
        up = jnp.clip(up, -SWIGLU_LIMIT, SWIGLU_LIMIT)
        act = (up + 1.0) * (gate * jax.nn.sigmoid(gate * SWIGLU_ALPHA))
        part = _dot(act.astype(BF16), wd_s[...])
        if mode == "first":
            acc[pl.ds(r, rows), :] = part
        elif mode == "mid":
            acc[pl.ds(r, rows), :] += part
        else:
            val = acc[pl.ds(r, rows), :] + part + bd_ref[...]
            for q in range(rows // SUB):
                stage_y(q, j0 + q, val[q * SUB:(q + 1) * SUB, :], pending)

    def run(mode):
        def pair(j, c):
            block(pl.multiple_of(j * (2 * SUB), 2 * SUB), 2 * SUB, mode, 2 * j, j > 0)
            return c
        lax.fori_loop(0, nsub // 2, pair, 0)

        @pl.when(nsub % 2 == 1)
        def _():
            block(pl.multiple_of((nsub - 1) * SUB, SUB), SUB, mode, nsub - 1, nsub > 1)

    @pl.when(f == 0)
    def _():
        run("first")

    @pl.when(jnp.logical_and(f > 0, f < n_ff - 1))
    def _():
        run("mid")

    @pl.when(f == n_ff - 1)
    def _():
        run("last")

        @pl.when(nsub > 0)
        def _():
            y_copy(0, 0).wait()

        @pl.when(nsub > 1)
        def _():
            y_copy(1, 0).wait()

    @pl.when(jnp.logical_and(f == n_ff - 1, nzero > 0))
    def _():
        ybuf[0:YS, :] = jnp.zeros((YS, LANES), F32)

        def start(j, c):
            y_copy(0, j).start()
            return c
        lax.fori_loop(0, nzero, start, 0)

        def wait(j, c):
            y_copy(0, 0).wait()
            return c
        lax.fori_loop(0, nzero, wait, 0)


def _experts_call(u_packed, row_tok, items, w_gate_up, b_gate_up, w_down, b_down, y_in=None):
    n_rows = row_tok.shape[0]
    n_ff = (D_FF // 2) // FF_TILE
    W2 = 2 * FF_TILE
    item_e, item_row, item_nsub, item_nzero, item_live = items
    n_items = item_e.shape[0]
    aliased = y_in is not None

    def ff(i, f, live):
        return jnp.where(live[i] > 0, f, n_ff - 1)

    grid_spec = pltpu.PrefetchScalarGridSpec(
        num_scalar_prefetch=6,
        grid=(n_items, n_ff),
        in_specs=[
            pl.BlockSpec(memory_space=pl.ANY),
            pl.BlockSpec((None, D_MODEL, W2), lambda i, f, e, r, n, z, lv, rt: (e[i], 0, ff(i, f, lv))),
            pl.BlockSpec((None, D_MODEL, W2), lambda i, f, e, r, n, z, lv, rt: (e[i], 0, n_ff + ff(i, f, lv))),
            pl.BlockSpec((None, 1, W2), lambda i, f, e, r, n, z, lv, rt: (e[i], 0, ff(i, f, lv))),
            pl.BlockSpec((None, 1, W2), lambda i, f, e, r, n, z, lv, rt: (e[i], 0, n_ff + ff(i, f, lv))),
            pl.BlockSpec((None, FF_TILE, D_MODEL), lambda i, f, e, r, n, z, lv, rt: (e[i], ff(i, f, lv), 0)),
            pl.BlockSpec((None, FF_TILE, D_MODEL), lambda i, f, e, r, n, z, lv, rt: (e[i], n_ff + ff(i, f, lv), 0)),
            pl.BlockSpec((None, 1, D_MODEL), lambda i, f, e, r, n, z, lv, rt: (e[i], 0, 0)),
        ] + ([pl.BlockSpec(memory_space=pl.ANY)] if aliased else []),
        out_specs=pl.BlockSpec(memory_space=pl.ANY),
        scratch_shapes=[pltpu.VMEM((ITEM_ROWS * PACK_SLABS, LANES), jnp.uint32),
                        pltpu.VMEM((ITEM_ROWS, D_MODEL), BF16),
                        pltpu.VMEM((ITEM_ROWS, D_MODEL), F32),
                        pltpu.VMEM((2 * ROW_BLK * ROW_SLABS, LANES), F32),
                        pltpu.VMEM((D_MODEL, W2), BF16),
                        pltpu.VMEM((D_MODEL, W2), BF16),
                        pltpu.VMEM((W2, D_MODEL), BF16),
                        pltpu.SemaphoreType.DMA(()),
                        pltpu.SemaphoreType.DMA((2,))],
    )
    args = (item_e, item_row, item_nsub, item_nzero, item_live, row_tok, u_packed,
            w_gate_up, w_gate_up, b_gate_up.reshape(N_EXPERTS, 1, 2 * D_FF),
            b_gate_up.reshape(N_EXPERTS, 1, 2 * D_FF), w_down, w_down, b_down.reshape(N_EXPERTS, 1, D_MODEL))
    return pl.pallas_call(
        functools.partial(_expert_kernel, n_ff, n_items, aliased),
        grid_spec=grid_spec,
        out_shape=jax.ShapeDtypeStruct((n_rows * ROW_SLABS, LANES), F32),
        input_output_aliases={len(args): 0} if aliased else {},
        compiler_params=_params(("arbitrary", "arbitrary")),
        name="experts_overflow" if aliased else "experts",
    )(*args, *((y_in,) if aliased else ()))


def _experts(u_packed, row_tok, items, n_used, w_gate_up, b_gate_up, w_down, b_down):
    n_main = N_EXPERTS + 1
    weights = (w_gate_up, b_gate_up, w_down, b_down)
    y = _experts_call(u_packed, row_tok, tuple(a[:n_main] for a in items), *weights)
    rest = tuple(a[n_main:] for a in items)
    return lax.cond(n_used > n_main,
                    lambda y_: _experts_call(u_packed, row_tok, rest, *weights, y_in=y_),
                    lambda y_: y_, y)


def _combine_kernel(tm, n_steps, dest_ref, w_ref, y_hbm, h_ref, lnw_ref, o_ref, buf, osum, sem):
    i = pl.program_id(0)

    def copy(step, slot, t, k):
        d = dest_ref[(step * tm + t) * TOP_K + k]
        return pltpu.make_async_copy(y_hbm.at[d], buf.at[slot, k * tm + t], sem.at[slot])

    def issue(step, slot):
        def body(t, c):
            for k in range(TOP_K):
                copy(step, slot, t, k).start(priority=k % 2)
            return c
        lax.fori_loop(0, tm, body, 0)

    @pl.when(i == 0)
    def _():
        issue(0, 0)

    @pl.when(i + 1 < n_steps)
    def _():
        issue(i + 1, (i + 1) % 2)

    slot = i % 2

    pltpu.make_async_copy(y_hbm.at[pl.ds(0, TOP_K * tm)], buf.at[slot], sem.at[slot]).wait()

    def token(t, c):
        a = h_ref[t]
        for k in range(TOP_K):
            a = a + w_ref[(i * tm + t) * TOP_K + k] * buf[slot, k * tm + t]
        osum[pl.ds(pl.multiple_of(t * ROW_SLABS, ROW_SLABS), ROW_SLABS), :] = a
        return c
    lax.fori_loop(0, tm, token, 0)

    h = osum[...].reshape(tm, ROW_SLABS, LANES)
    ms = jnp.mean(jnp.mean(h * h, axis=2, keepdims=True), axis=1, keepdims=True)
    osum[...] = (h * lax.rsqrt(ms + EPS) * lnw_ref[...]).reshape(tm * ROW_SLABS, LANES)
    for s in range(ROW_SLABS):
        o_ref[:, s * LANES:(s + 1) * LANES] = osum[pl.ds(s, tm, stride=ROW_SLABS), :]


def _combine(dest, wflat, y3, h3, ln_w):
    T = h3.shape[0]
    tm = 128
    n_steps = T // tm
    grid_spec = pltpu.PrefetchScalarGridSpec(
        num_scalar_prefetch=2,
        grid=(n_steps,),
        in_specs=[pl.BlockSpec(memory_space=pl.ANY),
                  pl.BlockSpec((tm, ROW_SLABS, LANES), lambda i, d, w: (i, 0, 0)),
                  pl.BlockSpec((1, ROW_SLABS, LANES), lambda i, d, w: (0, 0, 0))],
        out_specs=pl.BlockSpec((tm, D_MODEL), lambda i, d, w: (i, 0)),
        scratch_shapes=[pltpu.VMEM((2, TOP_K * tm, ROW_SLABS, LANES), F32),
                        pltpu.VMEM((tm * ROW_SLABS, LANES), F32),
                        pltpu.SemaphoreType.DMA((2,))],
    )
    return pl.pallas_call(
        functools.partial(_combine_kernel, tm, n_steps),
        grid_spec=grid_spec,
        out_shape=jax.ShapeDtypeStruct((T, D_MODEL), F32),
        compiler_params=_params(("arbitrary",)),
        name="combine",
    )(dest, wflat, y3, h3, ln_w.reshape(1, ROW_SLABS, LANES))


def _route(logits, n_rows, n_items):
    T = logits.shape[0]
    top_logits, top_idx = lax.top_k(logits, TOP_K)
    top_w = jax.nn.softmax(top_logits, axis=-1)
    e_flat = top_idx.reshape(-1).astype(jnp.int32)
    onehot = (e_flat[:, None] == jnp.arange(N_EXPERTS, dtype=jnp.int32)[None, :]).astype(jnp.int32)
    csum = jnp.cumsum(onehot, axis=0)
    counts = csum[-1]
    padded = (counts + ROW_BLK - 1) // ROW_BLK * ROW_BLK
    pend = jnp.cumsum(padded)
    pstart = pend - padded
    dest = jnp.sum(onehot * (pstart[None, :] + csum - onehot), axis=1).astype(jnp.int32)

    per_e = (padded + ITEM_ROWS - 1) // ITEM_ROWS
    iend = jnp.cumsum(per_e)
    istart = iend - per_e
    ii = jnp.arange(n_items, dtype=jnp.int32)
    total = iend[-1]
    live = (ii < total).astype(jnp.int32)
    ic = jnp.minimum(ii, total - 1)
    ie = jnp.minimum(jnp.sum(ic[:, None] >= iend[None, :], axis=1), N_EXPERTS - 1).astype(jnp.int32)
    within = ic - istart[ie]
    irow = (pstart[ie] + within * ITEM_ROWS).astype(jnp.int32)
    insub = jnp.minimum((padded[ie] - within * ITEM_ROWS) // ROW_BLK, ITEM_ROWS // ROW_BLK).astype(jnp.int32)
    insub = insub * live
    tail_rows = n_rows - pend[-1]
    tail = jnp.logical_and(ii == total, tail_rows > 0)
    inzero = jnp.where(tail, tail_rows // ROW_BLK, 0).astype(jnp.int32)
    irow = jnp.where(tail, pend[-1], irow).astype(jnp.int32)
    n_used = total + (tail_rows > 0).astype(jnp.int32)
    n_assign = T * TOP_K
    order = jnp.sort(e_flat * n_assign + jnp.arange(n_assign, dtype=jnp.int32)) % n_assign
    rr = jnp.arange(n_rows, dtype=jnp.int32)[:, None]
    cstart = jnp.cumsum(counts) - counts
    owner = jnp.logical_and(rr >= pstart[None, :], rr < (pstart + counts)[None, :])
    owned = jnp.any(owner, axis=1)
    src = jnp.sum(jnp.where(owner, rr - pstart[None, :] + cstart[None, :], 0), axis=1)
    row_tok = jnp.where(owned, order[jnp.clip(src, 0, n_assign - 1)] // TOP_K, 0).astype(jnp.int32)
    return dest, top_w.reshape(-1).astype(F32), (ie, irow, insub, inzero, live), n_used, row_tok


def kernel(x, positions, ln_mix_w, w_in, conv_w, conv_b, dt_bias, a_log, d_skip, ssm_norm_w, w_out,
           ln_ffn_w, w_router, b_router, w_gate_up, b_gate_up, w_down, b_down, ln_final_w):
    B, L, _ = x.shape
    T = B * L
    assert B == 1 and T % 1024 == 0
    x2 = x.reshape(T, D_MODEL)
    half = RET_HEAD_DIM // 2
    inv_freq = (ROPE_BASE ** (-jnp.arange(half, dtype=F32) / half)).reshape(1, half)
    pos_col = positions.reshape(T, 1).astype(F32)

    proj = _inproj(x2, ln_mix_w[0], jnp.swapaxes(w_in[0], 0, 1))
    ret = _retention(proj, pos_col, inv_freq)
    ssm = _ssd(proj, conv_w[0], conv_b[0], dt_bias[0], a_log[0], d_skip[0], ssm_norm_w[0])
    h_slabs, u_packed, logits = _outproj(ret, ssm, w_out[0].astype(BF16), x2, ln_ffn_w[0], w_router[0],
                                         b_router[0])

    n_rows = -(-(T * TOP_K + N_EXPERTS * (ROW_BLK - 1)) // ROW_BLK) * ROW_BLK
    n_items = N_EXPERTS + 1 + n_rows // ITEM_ROWS
    dest, wflat, items, n_used, row_tok = _route(logits, n_rows, n_items)

    y_rows = _experts(u_packed, row_tok, items, n_used, w_gate_up[0], b_gate_up[0], w_down[0], b_down[0])
    out = _combine(dest, wflat, y_rows.reshape(n_rows, ROW_SLABS, LANES),
                   h_slabs.reshape(T, ROW_SLABS, LANES), ln_final_w)
    return out.reshape(B, L, D_MODEL)
```

```python
import functools

import numpy as np
import jax
import jax.numpy as jnp
from jax import lax
from jax.experimental import pallas as pl
from jax.experimental.pallas import tpu as pltpu

F32 = jnp.float32
BF16 = jnp.bfloat16

D_MODEL = 2048
RET_HEADS = 4
RET_HEAD_DIM = 256
RET_WIDTH = RET_HEADS * RET_HEAD_DIM
SSM_WIDTH = D_MODEL - RET_WIDTH
SSM_HEAD_DIM = 64
SSM_HEADS = SSM_WIDTH // SSM_HEAD_DIM
SSM_GROUPS = 2
SSM_STATE = 128
CONV_WIDTH = 4
XBC_WIDTH = SSM_WIDTH + 2 * SSM_GROUPS * SSM_STATE
D_IN_PROJ = 4 * RET_WIDTH + SSM_WIDTH + XBC_WIDTH + SSM_HEADS
ROPE_BASE = 10000.0
N_EXPERTS = 32
TOP_K = 4
D_FF = D_MODEL
SWIGLU_LIMIT = 7.0
SWIGLU_ALPHA = 1.702
EPS = 1e-6

LANES = 128
VMEM_LIMIT = 56 * 1024 * 1024

RET_CHUNK = 256
SSD_CHUNK = 128
ROW_BLK = 256
ITEM_ROWS = 1536
ITEM_VARIANTS = (4, 5, 6)
FF_TILE = 128
DOWN_TILE = 512
ROW_SLABS = D_MODEL // LANES
PACK_SLABS = ROW_SLABS // 2


def _params(sem, **kw):
    return pltpu.CompilerParams(dimension_semantics=sem, vmem_limit_bytes=VMEM_LIMIT, **kw)


def _dot(a, b):
    return jnp.dot(a, b, preferred_element_type=F32)


def _dot_nt(a, b):
    return lax.dot_general(a, b, (((1,), (1,)), ((), ())), preferred_element_type=F32)


def _dot_tn(a, b):
    return lax.dot_general(a, b, (((0,), (0,)), ((), ())), preferred_element_type=F32)


def _split3(x):
    hi = x.astype(BF16)
    r = x - hi.astype(F32)
    mid = r.astype(BF16)
    lo = (r - mid.astype(F32)).astype(BF16)
    return hi, mid, lo


def _dot_exact_rhs01(x, m01):
    hi, mid, lo = _split3(x)
    return _dot(hi, m01) + _dot(mid, m01) + _dot(lo, m01)


def _dot_exact_lhs01(m01, x):
    hi, mid, lo = _split3(x)
    return _dot(m01, hi) + _dot(m01, mid) + _dot(m01, lo)


def _silu(x):
    return x * jax.nn.sigmoid(x)


def _pack_bf16_pairs(lo, hi):
    lo_bits = lax.bitcast_convert_type(lo.astype(BF16).astype(F32), jnp.uint32)
    hi_bits = lax.bitcast_convert_type(hi.astype(BF16).astype(F32), jnp.uint32)
    return hi_bits | (lo_bits >> 16)


INPROJ_CHUNKS = 4


def _inproj_kernel(x_hbm, lnw_ref, w_ref, o_ref, xs_ref, u_ref, sem):
    tm = u_ref.shape[0]
    rows = tm // INPROJ_CHUNKS
    i = pl.program_id(0)

    @pl.when(pl.program_id(1) == 0)
    def _():
        def copy(c):
            return pltpu.make_async_copy(x_hbm.at[pl.ds(i * tm + c * rows, rows)], xs_ref.at[c % 2], sem.at[c % 2])

        copy(0).start()
        copy(1).start()
        for c in range(INPROJ_CHUNKS):
            copy(c).wait()
            x = xs_ref[c % 2]
            ms = jnp.mean(x * x, axis=-1, keepdims=True)
            u_ref[c * rows:(c + 1) * rows, :] = (x * lax.rsqrt(ms + EPS) * lnw_ref[...]).astype(BF16)
            if c + 2 < INPROJ_CHUNKS:
                copy(c + 2).start()

    o_ref[...] = _dot_nt(u_ref[...], w_ref[...].astype(BF16))


def _inproj(x2, ln_w, w_in_t):
    T = x2.shape[0]
    tm, tn = (2048 if T % 2048 == 0 else 1024), 512
    return pl.pallas_call(
        _inproj_kernel,
        grid=(T // tm, pl.cdiv(D_IN_PROJ, tn)),
        in_specs=[pl.BlockSpec(memory_space=pl.ANY),
                  pl.BlockSpec((1, D_MODEL), lambda i, j: (0, 0)),
                  pl.BlockSpec((tn, D_MODEL), lambda i, j: (j, 0))],
        out_specs=pl.BlockSpec((tm, tn), lambda i, j: (i, j)),
        out_shape=jax.ShapeDtypeStruct((T, D_IN_PROJ), F32),
        scratch_shapes=[pltpu.VMEM((2, tm // INPROJ_CHUNKS, D_MODEL), F32),
                        pltpu.VMEM((tm, D_MODEL), BF16),
                        pltpu.SemaphoreType.DMA((2,))],
        compiler_params=_params(("arbitrary", "arbitrary")),
        name="inproj",
    )(x2, ln_w.reshape(1, D_MODEL), w_in_t)


def _retention_tables():
    C = RET_CHUNK
    h = np.arange(RET_HEADS, dtype=np.float64)
    log_gamma = np.log1p(-np.exp2(-5.0 - h))
    idx = np.arange(C, dtype=np.float64)
    rel = idx[:, None] - idx[None, :]
    intra = np.where(rel >= 0, np.exp(log_gamma[:, None, None] * np.maximum(rel, 0.0)), 0.0)
    q_decay = np.exp(log_gamma[:, None] * (idx + 1.0))
    k_decay = np.exp(log_gamma[:, None] * (C - 1.0 - idx))
    chunk_decay = np.exp(log_gamma * C)
    qd = np.broadcast_to(q_decay[:, :, None], (RET_HEADS, C, RET_HEAD_DIM))
    kd = np.broadcast_to(k_decay[:, :, None], (RET_HEADS, C, RET_HEAD_DIM))
    return (jnp.asarray(intra, F32), jnp.asarray(qd, F32), jnp.asarray(kd, F32),
            [float(c) for c in chunk_decay])


def _retention_kernel(chunk_decay, pos_ref, invf_ref, q_ref, k_ref, v_ref, g_ref,
                      intra_ref, qd_ref, kd_ref, o_ref, state_ref):
    @pl.when(pl.program_id(0) == 0)
    def _():
        state_ref[...] = jnp.zeros_like(state_ref)

    half = RET_HEAD_DIM // 2
    ang = pos_ref[...] * invf_ref[...]
    cos = jnp.cos(ang)
    sin = jnp.sin(ang)

    def rope(t):
        t1, t2 = t[:, :half], t[:, half:]
        return jnp.concatenate([t1 * cos - t2 * sin, t2 * cos + t1 * sin], axis=-1)

    for h in range(RET_HEADS):
        sl = slice(h * RET_HEAD_DIM, (h + 1) * RET_HEAD_DIM)
        q = rope(q_ref[:, sl])
        k = rope(k_ref[:, sl]) * (RET_HEAD_DIM ** -0.5)
        v = v_ref[:, sl].astype(BF16)
        state = state_ref[h]
        scores = _dot_nt(q.astype(BF16), k.astype(BF16)) * intra_ref[h]
        inner = _dot(scores.astype(BF16), v)
        cross = _dot((q * qd_ref[h]).astype(BF16), state.astype(BF16))
        state_ref[h] = chunk_decay[h] * state + _dot_tn((k * kd_ref[h]).astype(BF16), v)
        o = inner + cross
        o = o * lax.rsqrt(jnp.mean(o * o, axis=-1, keepdims=True) + EPS)
        o_ref[:, sl] = (o * _silu(g_ref[:, sl])).astype(o_ref.dtype)


def _retention(proj, pos_col, inv_freq):
    T = proj.shape[0]
    C = RET_CHUNK
    intra, qd, kd, chunk_decay = _retention_tables()
    col = lambda j: pl.BlockSpec((C, RET_WIDTH), lambda c, j=j: (c, j))
    const3 = lambda shape: pl.BlockSpec(shape, lambda c: (0, 0, 0))
    return pl.pallas_call(
        functools.partial(_retention_kernel, chunk_decay),
        grid=(T // C,),
        in_specs=[pl.BlockSpec((C, 1), lambda c: (c, 0)),
                  pl.BlockSpec((1, RET_HEAD_DIM // 2), lambda c: (0, 0)),
                  col(0), col(1), col(2), col(3),
                  const3((RET_HEADS, C, C)),
                  const3((RET_HEADS, C, RET_HEAD_DIM)),
                  const3((RET_HEADS, C, RET_HEAD_DIM))],
        out_specs=pl.BlockSpec((C, RET_WIDTH), lambda c: (c, 0)),
        out_shape=jax.ShapeDtypeStruct((T, RET_WIDTH), BF16),
        scratch_shapes=[pltpu.VMEM((RET_HEADS, RET_HEAD_DIM, RET_HEAD_DIM), F32)],
        compiler_params=_params(("arbitrary",)),
        name="retention",
    )(pos_col, inv_freq, proj, proj, proj, proj, intra, qd, kd)


def _ssd_kernel(xs0_ref, xs1_ref, bc_ref, z_ref, dt_ref, convw_ref, convb_ref, dtb_ref, a_ref,
                dskip_ref, normw_ref, expand_ref, o_ref, ext_ref, state_ref):
    C = SSD_CHUNK
    HW = SSM_WIDTH // SSM_GROUPS
    CARRY = 8

    @pl.when(pl.program_id(0) == 0)
    def _():
        ext_ref[0:CARRY, :] = jnp.zeros((CARRY, XBC_WIDTH), F32)
        state_ref[...] = jnp.zeros_like(state_ref)

    ext_ref[CARRY:CARRY + C, 0:HW] = xs0_ref[...]
    ext_ref[CARRY:CARRY + C, HW:2 * HW] = xs1_ref[...]
    ext_ref[CARRY:CARRY + C, 2 * HW:3 * HW] = bc_ref[...]
    conv = convb_ref[...]
    for k in range(CONV_WIDTH):
        off = CARRY - (CONV_WIDTH - 1) + k
        conv = conv + convw_ref[k:k + 1, :] * ext_ref[off:off + C, :]
    ext_ref[0:CARRY, :] = ext_ref[C:C + CARRY, :]
    xbc = _silu(conv)
    xs = xbc[:, :SSM_WIDTH]

    lane = lax.broadcasted_iota(jnp.int32, (1, LANES), 1)
    dt_raw = jnp.where(lane < SSM_HEADS, dt_ref[...], 0.0) + dtb_ref[...]
    dt = jnp.maximum(dt_raw, 0.0) + jnp.log1p(jnp.exp(-jnp.abs(dt_raw)))
    dta = dt * a_ref[...]

    row = lax.broadcasted_iota(jnp.int32, (C, C), 0)
    colm = lax.broadcasted_iota(jnp.int32, (C, C), 1)
    tril = row >= colm
    a_cum = _dot_exact_lhs01(jnp.where(tril, 1.0, 0.0).astype(BF16), dta)
    a_cum_t = a_cum.T

    expand = expand_ref[...]
    a_exp = _dot_exact_rhs01(a_cum, expand)
    dt_exp = _dot_exact_rhs01(dt, expand)
    a_last = a_exp[C - 1:C, :]
    decay_in = jnp.exp(a_exp)
    decay_out = jnp.exp(a_last - a_exp)
    chunk_decay = jnp.exp(a_last)
    xdt = xs * dt_exp

    lane2 = lax.broadcasted_iota(jnp.int32, (1, LANES), 1)
    lo_head = lane2 < SSM_HEAD_DIM
    ys = []
    for g in range(SSM_GROUPS):
        gs = slice(g * HW, (g + 1) * HW)
        b_g = xbc[:, SSM_WIDTH + g * SSM_STATE:SSM_WIDTH + (g + 1) * SSM_STATE].astype(BF16)
        c0 = SSM_WIDTH + SSM_GROUPS * SSM_STATE
        c_g = xbc[:, c0 + g * SSM_STATE:c0 + (g + 1) * SSM_STATE].astype(BF16)
        cb = _dot_nt(c_g, b_g)
        state = state_ref[g]
        y_off = _dot(c_g, state.astype(BF16)) * decay_in[:, gs]
        xw = (xdt[:, gs] * decay_out[:, gs]).astype(BF16)
        state_ref[g] = chunk_decay[:, gs] * state + _dot_tn(b_g, xw)
        slabs = []
        for s in range(HW // LANES):
            xd = xdt[:, g * HW + s * LANES:g * HW + (s + 1) * LANES]
            acc = None
            for e in range(2):
                hh = g * (SSM_HEADS // SSM_GROUPS) + 2 * s + e
                seg = a_cum[:, hh:hh + 1] - a_cum_t[hh:hh + 1, :]
                m = cb * jnp.exp(jnp.where(tril, seg, -jnp.inf))
                xm = jnp.where(lo_head if e == 0 else jnp.logical_not(lo_head), xd, 0.0)
                part = _dot(m.astype(BF16), xm.astype(BF16))
                acc = part if acc is None else acc + part
            slabs.append(acc)
        ys.append(jnp.concatenate(slabs, axis=-1) + y_off)
    y = jnp.concatenate(ys, axis=-1) + dskip_ref[...] * xs
    y = y * _silu(z_ref[...])
    outs = []
    for g in range(SSM_GROUPS):
        yg = y[:, g * HW:(g + 1) * HW]
        outs.append(yg * lax.rsqrt(jnp.mean(yg * yg, axis=-1, keepdims=True) + EPS))
    o_ref[...] = (jnp.concatenate(outs, axis=-1) * normw_ref[...]).astype(o_ref.dtype)


def _ssd(proj, conv_w, conv_b, dt_bias, a_log, d_skip, ssm_norm_w):
    T = proj.shape[0]
    C = SSD_CHUNK
    HW = SSM_WIDTH // SSM_GROUPS
    xbc0 = (4 * RET_WIDTH + SSM_WIDTH) // HW
    dt0 = (D_IN_PROJ - SSM_HEADS) // LANES
    pad = lambda v: jnp.zeros((1, LANES), F32).at[0, :SSM_HEADS].set(v.astype(F32))
    a_neg = pad(-jnp.exp(a_log.astype(F32)))
    expand_np = np.zeros((LANES, SSM_WIDTH), np.float32)
    for hh in range(SSM_HEADS):
        expand_np[hh, hh * SSM_HEAD_DIM:(hh + 1) * SSM_HEAD_DIM] = 1.0
    expand = jnp.asarray(expand_np, BF16)
    dskip_exp = jnp.repeat(d_skip.astype(F32), SSM_HEAD_DIM).reshape(1, SSM_WIDTH)
    const = lambda shape: pl.BlockSpec(shape, lambda c: (0, 0))
    return pl.pallas_call(
        _ssd_kernel,
        grid=(T // C,),
        in_specs=[pl.BlockSpec((C, HW), lambda c: (c, xbc0)),
                  pl.BlockSpec((C, HW), lambda c: (c, xbc0 + 1)),
                  pl.BlockSpec((C, HW), lambda c: (c, xbc0 + 2)),
                  pl.BlockSpec((C, SSM_WIDTH), lambda c: (c, 4 * RET_WIDTH // SSM_WIDTH)),
                  pl.BlockSpec((C, LANES), lambda c: (c, dt0)),
                  const((CONV_WIDTH, XBC_WIDTH)), const((1, XBC_WIDTH)),
                  const((1, LANES)), const((1, LANES)),
                  const((1, SSM_WIDTH)), const((1, SSM_WIDTH)),
                  const((LANES, SSM_WIDTH))],
        out_specs=pl.BlockSpec((C, SSM_WIDTH), lambda c: (c, 0)),
        out_shape=jax.ShapeDtypeStruct((T, SSM_WIDTH), BF16),
        scratch_shapes=[pltpu.VMEM((C + 8, XBC_WIDTH), F32),
                        pltpu.VMEM((SSM_GROUPS, SSM_STATE, HW), F32)],
        compiler_params=_params(("arbitrary",)),
        name="ssd",
    )(proj, proj, proj, proj, proj, conv_w, conv_b.reshape(1, XBC_WIDTH), pad(dt_bias), a_neg,
      dskip_exp, ssm_norm_w.reshape(1, SSM_WIDTH), expand)


def _outproj_kernel(ret_ref, ssm_ref, w_ref, x_ref, lnw_ref, wr_ref, br_ref, hs_ref, up_ref, lg_ref):
    tm = x_ref.shape[0]
    h = (x_ref[...] + _dot(ret_ref[...], w_ref[0:RET_WIDTH, :])
         + _dot(ssm_ref[...], w_ref[RET_WIDTH:D_MODEL, :]))
    for s in range(ROW_SLABS):
        hs_ref[pl.ds(s, tm, stride=ROW_SLABS), :] = h[:, s * LANES:(s + 1) * LANES]
    u = h * lax.rsqrt(jnp.mean(h * h, axis=-1, keepdims=True) + EPS) * lnw_ref[...]
    packed = _pack_bf16_pairs(u[:, :D_MODEL // 2], u[:, D_MODEL // 2:])
    for s in range(PACK_SLABS):
        up_ref[pl.ds(s, tm, stride=PACK_SLABS), :] = packed[:, s * LANES:(s + 1) * LANES]
    E = N_EXPERTS
    uh, um, ul = _split3(u)
    ph = _dot(uh, wr_ref[...])
    pm = _dot(um, wr_ref[...])
    pw = _dot(ul, wr_ref[...])
    lg = (ph[:, 0:E] + (ph[:, E:2 * E] + pm[:, 0:E])
          + (ph[:, 2 * E:3 * E] + pm[:, E:2 * E] + pw[:, 0:E]))
    lg_ref[...] = lg + br_ref[...]


def _outproj(ret, ssm, w_out_bf16, x2, ln_w, w_router, b_router):
    T = x2.shape[0]
    tm = 512
    return pl.pallas_call(
        _outproj_kernel,
        grid=(T // tm,),
        in_specs=[pl.BlockSpec((tm, RET_WIDTH), lambda i: (i, 0)),
                  pl.BlockSpec((tm, SSM_WIDTH), lambda i: (i, 0)),
                  pl.BlockSpec((D_MODEL, D_MODEL), lambda i: (0, 0)),
                  pl.BlockSpec((tm, D_MODEL), lambda i: (i, 0)),
                  pl.BlockSpec((1, D_MODEL), lambda i: (0, 0)),
                  pl.BlockSpec((D_MODEL, 3 * N_EXPERTS), lambda i: (0, 0)),
                  pl.BlockSpec((1, N_EXPERTS), lambda i: (0, 0))],
        out_specs=[pl.BlockSpec((tm * ROW_SLABS, LANES), lambda i: (i, 0)),
                   pl.BlockSpec((tm * PACK_SLABS, LANES), lambda i: (i, 0)),
                   pl.BlockSpec((tm, N_EXPERTS), lambda i: (i, 0))],
        out_shape=[jax.ShapeDtypeStruct((T * ROW_SLABS, LANES), F32),
                   jax.ShapeDtypeStruct((T * PACK_SLABS, LANES), jnp.uint32),
                   jax.ShapeDtypeStruct((T, N_EXPERTS), F32)],
        compiler_params=_params(("parallel",)),
        name="outproj",
    )(ret, ssm, w_out_bf16, x2, ln_w.reshape(1, D_MODEL), jnp.concatenate(_split3(w_router), axis=1),
      b_router.reshape(1, N_EXPERTS))


def _expert_kernel(n_ff, n_items, aliased, *refs):
    (item_e, item_row, item_nsub, item_nzero, item_live, item_src, sorted_tok,
     u_hbm, wga_ref, wgb_ref, bga_ref, bgb_ref, wda_ref, wdb_ref, bd_ref) = refs[:15]
    (y_hbm, stage, xbuf, acc, ybuf, wa_s, wb_s, wd_s, sem_in, sem_out) = refs[16:] if aliased else refs[15:]
    del item_e, item_live
    i = pl.program_id(0)
    f = pl.program_id(1)
    nsub = item_nsub[i]
    nzero = item_nzero[i]
    row0 = item_row[i]
    SUB = ROW_BLK
    YS = SUB * ROW_SLABS
    PS = PACK_SLABS
    XS = SUB * PS

    def y_copy(slot, j):
        dst = y_hbm.at[pl.ds(pl.multiple_of((row0 + j * SUB) * ROW_SLABS, YS), YS)]
        return pltpu.make_async_copy(ybuf.at[pl.ds(slot * YS, YS)], dst, sem_out.at[slot])

    def gather_copy(tok, r):
        src = u_hbm.at[pl.ds(pl.multiple_of(tok * PS, PS), PS)]
        return pltpu.make_async_copy(src, stage.at[pl.ds(pl.multiple_of(r * PS, PS), PS)], sem_in)

    n_assign = sorted_tok.shape[0]
    n_sub_max = ITEM_ROWS // SUB
    CH = ITEM_ROWS // (n_ff - 2)
    nxt = jnp.minimum(i + 1, n_items - 1)
    mid_step = jnp.logical_and(f > 0, f < n_ff - 1)

    def issue_rows(item, lo, count, unrolled):
        src0 = item_src[item]

        def one(r, p):
            gather_copy(sorted_tok[jnp.minimum(src0 + r, n_assign - 1)], r).start(priority=p)

        if unrolled:
            for k in range(count):
                one(lo + k, k % 2)
        else:
            def body(q, c):
                for p in range(2):
                    one(lo + 2 * q + p, p)
                return c
            lax.fori_loop(0, count // 2, body, 0)

    def wait_rows():
        for _ in range(n_sub_max):
            pltpu.make_async_copy(u_hbm.at[pl.ds(0, XS)], stage.at[pl.ds(0, XS)], sem_in).wait()

    @pl.when(jnp.logical_and(i == 0, f == 0))
    def _():
        issue_rows(0, 0, ITEM_ROWS, False)

    @pl.when(nsub > 0)
    def _():
        wa_s[...] = wga_ref[...].astype(BF16)
        wb_s[...] = wgb_ref[...].astype(BF16)
        wd_s[...] = pltpu.bitcast(_pack_bf16_pairs(wda_ref[...], wdb_ref[...]), BF16)

    lane = lax.broadcasted_iota(jnp.int32, (1, 2 * FF_TILE), 1)
    even = (lane % 2) == 0
    bga = bga_ref[...]
    bgb = bgb_ref[...]
    W2 = 2 * FF_TILE

    @pl.when(f == 0)
    def _():
        wait_rows()
        half = D_MODEL // 2

        def unpack(j, c):
            rows = pl.ds(pl.multiple_of(j * SUB, SUB), SUB)
            for s in range(PS):
                p = stage[pl.ds(j * XS + s, SUB, stride=PS), :]
                lo = lax.bitcast_convert_type(p << 16, F32)
                hi = lax.bitcast_convert_type(p & jnp.uint32(0xFFFF0000), F32)
                xbuf[rows, s * LANES:(s + 1) * LANES] = lo.astype(BF16)
                xbuf[rows, half + s * LANES:half + (s + 1) * LANES] = hi.astype(BF16)
            return c
        lax.fori_loop(0, nsub, unpack, 0)

    @pl.when(jnp.logical_and(mid_step, nsub == 0))
    def _():
        issue_rows(nxt, (f - 1) * CH, CH, False)

    def block(rows, first, prefetch):
        xs = xbuf[0:rows, :]
        ga = _dot(xs, wa_s[...]) + bga
        if prefetch:
            issue_rows(nxt, (f - 1) * CH, CH, True)
        gb = _dot(xs, wb_s[...]) + bgb
        gate = jnp.where(even, ga, pltpu.roll(gb, 1, 1))
        up = jnp.where(even, pltpu.roll(ga, W2 - 1, 1), gb)
        gate = jnp.minimum(gate, SWIGLU_LIMIT)
        up = jnp.clip(up, -SWIGLU_LIMIT, SWIGLU_LIMIT)
        act = ((up + 1.0) * (gate * jax.nn.sigmoid(gate * SWIGLU_ALPHA))).astype(BF16)
        for n in range(0, D_MODEL, DOWN_TILE):
            part = _dot(act, wd_s[:, n:n + DOWN_TILE])
            if first:
                acc[0:rows, n:n + DOWN_TILE] = part
            else:
                acc[0:rows, n:n + DOWN_TILE] += part

    def run(first, prefetch):
        for m in ITEM_VARIANTS:
            cover = (nsub == m) if m > ITEM_VARIANTS[0] else jnp.logical_and(nsub > 0, nsub <= m)

            @pl.when(cover)
            def _(m=m):
                block(m * SUB, first, prefetch)

    @pl.when(f == 0)
    def _():
        run(True, False)

    @pl.when(mid_step)
    def _():
        run(False, True)

    @pl.when(f == n_ff - 1)
    def _():
        run(False, False)

        @pl.when(i == n_items - 1)
        def _():
            wait_rows()

        for q in range(ITEM_ROWS // SUB):
            @pl.when(q < nsub)
            def _(q=q):
                slot = q % 2
                if q >= 2:
                    y_copy(slot, 0).wait()
                val = acc[q * SUB:(q + 1) * SUB, :] + bd_ref[...]
                for s in range(ROW_SLABS):
                    ybuf[pl.ds(slot * YS + s, SUB, stride=ROW_SLABS), :] = val[:, s * LANES:(s + 1) * LANES]
                y_copy(slot, q).start()

        @pl.when(nsub > 0)
        def _():
            y_copy(0, 0).wait()

        @pl.when(nsub > 1)
        def _():
            y_copy(1, 0).wait()

    @pl.when(jnp.logical_and(f == n_ff - 1, nzero > 0))
    def _():
        ybuf[0:YS, :] = jnp.zeros((YS, LANES), F32)

        def start(j, c):
            y_copy(0, j).start()
            return c
        lax.fori_loop(0, nzero, start, 0)

        def wait(j, c):
            y_copy(0, 0).wait()
            return c
        lax.fori_loop(0, nzero, wait, 0)


def _experts_call(u_packed, sorted_tok, n_rows, items, w_gate_up, b_gate_up, w_down, b_down, y_in=None):
    n_ff = (D_FF // 2) // FF_TILE
    assert ITEM_ROWS % (n_ff - 2) == 0 and ITEM_VARIANTS[-1] * ROW_BLK == ITEM_ROWS
    W2 = 2 * FF_TILE
    item_e, item_row, item_nsub, item_nzero, item_live, item_src = items
    n_items = item_e.shape[0]
    aliased = y_in is not None

    def ff(i, f, live):
        return jnp.where(live[i] > 0, f, n_ff - 1)

    grid_spec = pltpu.PrefetchScalarGridSpec(
        num_scalar_prefetch=7,
        grid=(n_items, n_ff),
        in_specs=[
            pl.BlockSpec(memory_space=pl.ANY),
            pl.BlockSpec((None, D_MODEL, W2), lambda i, f, e, r, n, z, lv, sr, st: (e[i], 0, ff(i, f, lv))),
            pl.BlockSpec((None, D_MODEL, W2), lambda i, f, e, r, n, z, lv, sr, st: (e[i], 0, n_ff + ff(i, f, lv))),
            pl.BlockSpec((None, 1, W2), lambda i, f, e, r, n, z, lv, sr, st: (e[i], 0, ff(i, f, lv))),
            pl.BlockSpec((None, 1, W2), lambda i, f, e, r, n, z, lv, sr, st: (e[i], 0, n_ff + ff(i, f, lv))),
            pl.BlockSpec((None, FF_TILE, D_MODEL), lambda i, f, e, r, n, z, lv, sr, st: (e[i], ff(i, f, lv), 0)),
            pl.BlockSpec((None, FF_TILE, D_MODEL), lambda i, f, e, r, n, z, lv, sr, st: (e[i], n_ff + ff(i, f, lv), 0)),
            pl.BlockSpec((None, 1, D_MODEL), lambda i, f, e, r, n, z, lv, sr, st: (e[i], 0, 0)),
        ] + ([pl.BlockSpec(memory_space=pl.ANY)] if aliased else []),
        out_specs=pl.BlockSpec(memory_space=pl.ANY),
        scratch_shapes=[pltpu.VMEM((ITEM_ROWS * PACK_SLABS, LANES), jnp.uint32),
                        pltpu.VMEM((ITEM_ROWS, D_MODEL), BF16),
                        pltpu.VMEM((ITEM_ROWS, D_MODEL), F32),
                        pltpu.VMEM((2 * ROW_BLK * ROW_SLABS, LANES), F32),
                        pltpu.VMEM((D_MODEL, W2), BF16),
                        pltpu.VMEM((D_MODEL, W2), BF16),
                        pltpu.VMEM((W2, D_MODEL), BF16),
                        pltpu.SemaphoreType.DMA(()),
                        pltpu.SemaphoreType.DMA((2,))],
    )
    args = (item_e, item_row, item_nsub, item_nzero, item_live, item_src, sorted_tok, u_packed,
            w_gate_up, w_gate_up, b_gate_up.reshape(N_EXPERTS, 1, 2 * D_FF),
            b_gate_up.reshape(N_EXPERTS, 1, 2 * D_FF), w_down, w_down, b_down.reshape(N_EXPERTS, 1, D_MODEL))
    return pl.pallas_call(
        functools.partial(_expert_kernel, n_ff, n_items, aliased),
        grid_spec=grid_spec,
        out_shape=jax.ShapeDtypeStruct((n_rows * ROW_SLABS, LANES), F32),
        input_output_aliases={len(args): 0} if aliased else {},
        compiler_params=_params(("arbitrary", "arbitrary")),
        name="experts_overflow" if aliased else "experts",
    )(*args, *((y_in,) if aliased else ()))


def _experts(u_packed, sorted_tok, n_rows, items, n_used, w_gate_up, b_gate_up, w_down, b_down):
    n_main = N_EXPERTS + 1
    weights = (w_gate_up, b_gate_up, w_down, b_down)
    y = _experts_call(u_packed, sorted_tok, n_rows, tuple(a[:n_main] for a in items), *weights)
    rest = tuple(a[n_main:] for a in items)
    return lax.cond(n_used > n_main,
                    lambda y_: _experts_call(u_packed, sorted_tok, n_rows, rest, *weights, y_in=y_),
                    lambda y_: y_, y)


def _combine_kernel(tm, n_steps, dest_ref, w_ref, y_hbm, h_ref, lnw_ref, o_ref, buf, osum, sem):
    i = pl.program_id(0)

    def copy(step, slot, t, k):
        d = dest_ref[(step * tm + t) * TOP_K + k]
        return pltpu.make_async_copy(y_hbm.at[d], buf.at[slot, k * tm + t], sem.at[slot])

    def issue(step, slot):
        def body(t, c):
            for k in range(TOP_K):
                copy(step, slot, t, k).start(priority=k % 2)
            return c
        lax.fori_loop(0, tm, body, 0)

    @pl.when(i == 0)
    def _():
        issue(0, 0)

    @pl.when(i + 1 < n_steps)
    def _():
        issue(i + 1, (i + 1) % 2)

    slot = i % 2

    pltpu.make_async_copy(y_hbm.at[pl.ds(0, TOP_K * tm)], buf.at[slot], sem.at[slot]).wait()

    def token(t, c):
        a = h_ref[t]
        for k in range(TOP_K):
            a = a + w_ref[(i * tm + t) * TOP_K + k] * buf[slot, k * tm + t]
        osum[pl.ds(pl.multiple_of(t * ROW_SLABS, ROW_SLABS), ROW_SLABS), :] = a
        return c
    lax.fori_loop(0, tm, token, 0)

    h = osum[...].reshape(tm, ROW_SLABS, LANES)
    ms = jnp.mean(jnp.mean(h * h, axis=2, keepdims=True), axis=1, keepdims=True)
    osum[...] = (h * lax.rsqrt(ms + EPS) * lnw_ref[...]).reshape(tm * ROW_SLABS, LANES)
    for s in range(ROW_SLABS):
        o_ref[:, s * LANES:(s + 1) * LANES] = osum[pl.ds(s, tm, stride=ROW_SLABS), :]


def _combine(dest, wflat, y3, h3, ln_w):
    T = h3.shape[0]
    tm = 128
    n_steps = T // tm
    grid_spec = pltpu.PrefetchScalarGridSpec(
        num_scalar_prefetch=2,
        grid=(n_steps,),
        in_specs=[pl.BlockSpec(memory_space=pl.ANY),
                  pl.BlockSpec((tm, ROW_SLABS, LANES), lambda i, d, w: (i, 0, 0)),
                  pl.BlockSpec((1, ROW_SLABS, LANES), lambda i, d, w: (0, 0, 0))],
        out_specs=pl.BlockSpec((tm, D_MODEL), lambda i, d, w: (i, 0)),
        scratch_shapes=[pltpu.VMEM((2, TOP_K * tm, ROW_SLABS, LANES), F32),
                        pltpu.VMEM((tm * ROW_SLABS, LANES), F32),
                        pltpu.SemaphoreType.DMA((2,))],
    )
    return pl.pallas_call(
        functools.partial(_combine_kernel, tm, n_steps),
        grid_spec=grid_spec,
        out_shape=jax.ShapeDtypeStruct((T, D_MODEL), F32),
        compiler_params=_params(("arbitrary",)),
        name="combine",
    )(dest, wflat, y3, h3, ln_w.reshape(1, ROW_SLABS, LANES))


def _route(logits, n_rows, n_items):
    T = logits.shape[0]
    top_logits, top_idx = lax.top_k(logits, TOP_K)
    top_w = jax.nn.softmax(top_logits, axis=-1)
    e_flat = top_idx.reshape(-1).astype(jnp.int32)
    onehot = (e_flat[:, None] == jnp.arange(N_EXPERTS, dtype=jnp.int32)[None, :]).astype(jnp.int32)
    csum = jnp.cumsum(onehot, axis=0)
    counts = csum[-1]
    padded = (counts + ROW_BLK - 1) // ROW_BLK * ROW_BLK
    pend = jnp.cumsum(padded)
    pstart = pend - padded
    dest = jnp.sum(onehot * (pstart[None, :] + csum - onehot), axis=1).astype(jnp.int32)

    per_e = (padded + ITEM_ROWS - 1) // ITEM_ROWS
    iend = jnp.cumsum(per_e)
    istart = iend - per_e
    ii = jnp.arange(n_items, dtype=jnp.int32)
    total = iend[-1]
    live = (ii < total).astype(jnp.int32)
    ic = jnp.minimum(ii, total - 1)
    ie = jnp.minimum(jnp.sum(ic[:, None] >= iend[None, :], axis=1), N_EXPERTS - 1).astype(jnp.int32)
    within = ic - istart[ie]
    irow = (pstart[ie] + within * ITEM_ROWS).astype(jnp.int32)
    insub = jnp.minimum((padded[ie] - within * ITEM_ROWS) // ROW_BLK, ITEM_ROWS // ROW_BLK).astype(jnp.int32)
    insub = insub * live
    tail_rows = n_rows - pend[-1]
    tail = jnp.logical_and(ii == total, tail_rows > 0)
    inzero = jnp.where(tail, tail_rows // ROW_BLK, 0).astype(jnp.int32)
    irow = jnp.where(tail, pend[-1], irow).astype(jnp.int32)
    n_used = total + (tail_rows > 0).astype(jnp.int32)
    n_assign = T * TOP_K
    assert N_EXPERTS * n_assign < 2 ** 31
    order = jnp.sort(e_flat * n_assign + jnp.arange(n_assign, dtype=jnp.int32)) % n_assign
    sorted_tok = (order // TOP_K).astype(jnp.int32)
    cstart = jnp.cumsum(counts) - counts
    isrc = ((cstart[ie] + within * ITEM_ROWS) * live).astype(jnp.int32)
    return dest, top_w.reshape(-1).astype(F32), (ie, irow, insub, inzero, live, isrc), n_used, sorted_tok


def kernel(x, positions, ln_mix_w, w_in, conv_w, conv_b, dt_bias, a_log, d_skip, ssm_norm_w, w_out,
           ln_ffn_w, w_router, b_router, w_gate_up, b_gate_up, w_down, b_down, ln_final_w):
    B, L, _ = x.shape
    T = B * L
    assert B == 1 and T % 1024 == 0
    x2 = x.reshape(T, D_MODEL)
    half = RET_HEAD_DIM // 2
    inv_freq = (ROPE_BASE ** (-jnp.arange(half, dtype=F32) / half)).reshape(1, half)
    pos_col = positions.reshape(T, 1).astype(F32)

    proj = _inproj(x2, ln_mix_w[0], jnp.swapaxes(w_in[0], 0, 1))
    ret = _retention(proj, pos_col, inv_freq)
    ssm = _ssd(proj, conv_w[0], conv_b[0], dt_bias[0], a_log[0], d_skip[0], ssm_norm_w[0])
    h_slabs, u_packed, logits = _outproj(ret, ssm, w_out[0].astype(BF16), x2, ln_ffn_w[0], w_router[0],
                                         b_router[0])

    n_rows = -(-(T * TOP_K + N_EXPERTS * (ROW_BLK - 1)) // ROW_BLK) * ROW_BLK
    n_items = N_EXPERTS + 1 + n_rows // ITEM_ROWS
    dest, wflat, items, n_used, sorted_tok = _route(logits, n_rows, n_items)

    y_rows = _experts(u_packed, sorted_tok, n_rows, items, n_used, w_gate_up[0], b_gate_up[0], w_down[0],
                      b_down[0])
    out = _combine(dest, wflat, y_rows.reshape(n_rows, ROW_SLABS, LANES),
                   h_slabs.reshape(T, ROW_SLABS, LANES), ln_final_w)
    return out.reshape(B, L, D_MODEL)
```

```python
import functools

import numpy as np
import jax
import jax.numpy as jnp
from jax import lax
from jax.experimental import pallas as pl
from jax.experimental.pallas import tpu as pltpu

F32 = jnp.float32
BF16 = jnp.bfloat16

D_MODEL = 2048
RET_HEADS = 4
RET_HEAD_DIM = 256
RET_WIDTH = RET_HEADS * RET_HEAD_DIM
SSM_WIDTH = D_MODEL - RET_WIDTH
SSM_HEAD_DIM = 64
SSM_HEADS = SSM_WIDTH // SSM_HEAD_DIM
SSM_GROUPS = 2
SSM_STATE = 128
CONV_WIDTH = 4
XBC_WIDTH = SSM_WIDTH + 2 * SSM_GROUPS * SSM_STATE
D_IN_PROJ = 4 * RET_WIDTH + SSM_WIDTH + XBC_WIDTH + SSM_HEADS
ROPE_BASE = 10000.0
N_EXPERTS = 32
TOP_K = 4
D_FF = D_MODEL
SWIGLU_LIMIT = 7.0
SWIGLU_ALPHA = 1.702
EPS = 1e-6

LANES = 128
VMEM_LIMIT = 56 * 1024 * 1024

RET_CHUNK = 256
SSD_CHUNK = 128
ROW_BLK = 256
ITEM_ROWS = 1536
ITEM_VARIANTS = (4, 5, 6)
FF_TILE = 128
DOWN_TILE = 512
ROW_SLABS = D_MODEL // LANES
PACK_SLABS = ROW_SLABS // 2


def _params(sem, **kw):
    return pltpu.CompilerParams(dimension_semantics=sem, vmem_limit_bytes=VMEM_LIMIT, **kw)


def _dot(a, b):
    return jnp.dot(a, b, preferred_element_type=F32)


def _dot_nt(a, b):
    return lax.dot_general(a, b, (((1,), (1,)), ((), ())), preferred_element_type=F32)


def _dot_tn(a, b):
    return lax.dot_general(a, b, (((0,), (0,)), ((), ())), preferred_element_type=F32)


def _split3(x):
    hi = x.astype(BF16)
    r = x - hi.astype(F32)
    mid = r.astype(BF16)
    lo = (r - mid.astype(F32)).astype(BF16)
    return hi, mid, lo


def _dot_exact_rhs01(x, m01):
    hi, mid, lo = _split3(x)
    return _dot(hi, m01) + _dot(mid, m01) + _dot(lo, m01)


def _dot_exact_lhs01(m01, x):
    hi, mid, lo = _split3(x)
    return _dot(m01, hi) + _dot(m01, mid) + _dot(m01, lo)


def _silu(x):
    return x * jax.nn.sigmoid(x)


def _pack_bf16_pairs(lo, hi):
    lo_bits = lax.bitcast_convert_type(lo.astype(BF16).astype(F32), jnp.uint32)
    hi_bits = lax.bitcast_convert_type(hi.astype(BF16).astype(F32), jnp.uint32)
    return hi_bits | (lo_bits >> 16)


INPROJ_CHUNKS = 4


def _inproj_kernel(x_hbm, lnw_ref, w_ref, o_ref, xs_ref, u_ref, sem):
    tm = u_ref.shape[0]
    rows = tm // INPROJ_CHUNKS
    i = pl.program_id(0)

    @pl.when(pl.program_id(1) == 0)
    def _():
        def copy(c):
            return pltpu.make_async_copy(x_hbm.at[pl.ds(i * tm + c * rows, rows)], xs_ref.at[c % 2], sem.at[c % 2])

        copy(0).start()
        copy(1).start()
        for c in range(INPROJ_CHUNKS):
            copy(c).wait()
            x = xs_ref[c % 2]
            ms = jnp.mean(x * x, axis=-1, keepdims=True)
            u_ref[c * rows:(c + 1) * rows, :] = (x * lax.rsqrt(ms + EPS) * lnw_ref[...]).astype(BF16)
            if c + 2 < INPROJ_CHUNKS:
                copy(c + 2).start()

    o_ref[...] = _dot_nt(u_ref[...], w_ref[...].astype(BF16))


def _inproj(x2, ln_w, w_in_t):
    T = x2.shape[0]
    tm, tn = (2048 if T % 2048 == 0 else 1024), 512
    return pl.pallas_call(
        _inproj_kernel,
        grid=(T // tm, pl.cdiv(D_IN_PROJ, tn)),
        in_specs=[pl.BlockSpec(memory_space=pl.ANY),
                  pl.BlockSpec((1, D_MODEL), lambda i, j: (0, 0)),
                  pl.BlockSpec((tn, D_MODEL), lambda i, j: (j, 0))],
        out_specs=pl.BlockSpec((tm, tn), lambda i, j: (i, j)),
        out_shape=jax.ShapeDtypeStruct((T, D_IN_PROJ), F32),
        scratch_shapes=[pltpu.VMEM((2, tm // INPROJ_CHUNKS, D_MODEL), F32),
                        pltpu.VMEM((tm, D_MODEL), BF16),
                        pltpu.SemaphoreType.DMA((2,))],
        compiler_params=_params(("arbitrary", "arbitrary")),
        name="inproj",
    )(x2, ln_w.reshape(1, D_MODEL), w_in_t)


def _retention_tables():
    C = RET_CHUNK
    h = np.arange(RET_HEADS, dtype=np.float64)
    log_gamma = np.log1p(-np.exp2(-5.0 - h))
    idx = np.arange(C, dtype=np.float64)
    rel = idx[:, None] - idx[None, :]
    intra = np.where(rel >= 0, np.exp(log_gamma[:, None, None] * np.maximum(rel, 0.0)), 0.0)
    q_decay = np.exp(log_gamma[:, None] * (idx + 1.0))
    k_decay = np.exp(log_gamma[:, None] * (C - 1.0 - idx))
    chunk_decay = np.exp(log_gamma * C)
    qd = np.broadcast_to(q_decay[:, :, None], (RET_HEADS, C, RET_HEAD_DIM))
    kd = np.broadcast_to(k_decay[:, :, None], (RET_HEADS, C, RET_HEAD_DIM))
    return (jnp.asarray(intra, F32), jnp.asarray(qd, F32), jnp.asarray(kd, F32),
            [float(c) for c in chunk_decay])


def _retention_kernel(chunk_decay, pos_ref, invf_ref, q_ref, k_ref, v_ref, g_ref,
                      intra_ref, qd_ref, kd_ref, o_ref, state_ref):
    @pl.when(pl.program_id(0) == 0)
    def _():
        state_ref[...] = jnp.zeros_like(state_ref)

    half = RET_HEAD_DIM // 2
    ang = pos_ref[...] * invf_ref[...]
    cos = jnp.cos(ang)
    sin = jnp.sin(ang)

    def rope(t):
        t1, t2 = t[:, :half], t[:, half:]
        return jnp.concatenate([t1 * cos - t2 * sin, t2 * cos + t1 * sin], axis=-1)

    for h in range(RET_HEADS):
        sl = slice(h * RET_HEAD_DIM, (h + 1) * RET_HEAD_DIM)
        q = rope(q_ref[:, sl])
        k = rope(k_ref[:, sl]) * (RET_HEAD_DIM ** -0.5)
        v = v_ref[:, sl].astype(BF16)
        state = state_ref[h]
        scores = _dot_nt(q.astype(BF16), k.astype(BF16)) * intra_ref[h]
        inner = _dot(scores.astype(BF16), v)
        cross = _dot((q * qd_ref[h]).astype(BF16), state.astype(BF16))
        state_ref[h] = chunk_decay[h] * state + _dot_tn((k * kd_ref[h]).astype(BF16), v)
        o = inner + cross
        o = o * lax.rsqrt(jnp.mean(o * o, axis=-1, keepdims=True) + EPS)
        o_ref[:, sl] = (o * _silu(g_ref[:, sl])).astype(o_ref.dtype)


def _retention(proj, pos_col, inv_freq):
    T = proj.shape[0]
    C = RET_CHUNK
    intra, qd, kd, chunk_decay = _retention_tables()
    col = lambda j: pl.BlockSpec((C, RET_WIDTH), lambda c, j=j: (c, j))
    const3 = lambda shape: pl.BlockSpec(shape, lambda c: (0, 0, 0))
    return pl.pallas_call(
        functools.partial(_retention_kernel, chunk_decay),
        grid=(T // C,),
        in_specs=[pl.BlockSpec((C, 1), lambda c: (c, 0)),
                  pl.BlockSpec((1, RET_HEAD_DIM // 2), lambda c: (0, 0)),
                  col(0), col(1), col(2), col(3),
                  const3((RET_HEADS, C, C)),
                  const3((RET_HEADS, C, RET_HEAD_DIM)),
                  const3((RET_HEADS, C, RET_HEAD_DIM))],
        out_specs=pl.BlockSpec((C, RET_WIDTH), lambda c: (c, 0)),
        out_shape=jax.ShapeDtypeStruct((T, RET_WIDTH), BF16),
        scratch_shapes=[pltpu.VMEM((RET_HEADS, RET_HEAD_DIM, RET_HEAD_DIM), F32)],
        compiler_params=_params(("arbitrary",)),
        name="retention",
    )(pos_col, inv_freq, proj, proj, proj, proj, intra, qd, kd)


def _ssd_kernel(xs0_ref, xs1_ref, bc_ref, z_ref, dt_ref, convw_ref, convb_ref, dtb_ref, a_ref,
                dskip_ref, normw_ref, expand_ref, o_ref, ext_ref, state_ref):
    C = SSD_CHUNK
    HW = SSM_WIDTH // SSM_GROUPS
    CARRY = 8

    @pl.when(pl.program_id(0) == 0)
    def _():
        ext_ref[0:CARRY, :] = jnp.zeros((CARRY, XBC_WIDTH), F32)
        state_ref[...] = jnp.zeros_like(state_ref)

    ext_ref[CARRY:CARRY + C, 0:HW] = xs0_ref[...]
    ext_ref[CARRY:CARRY + C, HW:2 * HW] = xs1_ref[...]
    ext_ref[CARRY:CARRY + C, 2 * HW:3 * HW] = bc_ref[...]
    conv = convb_ref[...]
    for k in range(CONV_WIDTH):
        off = CARRY - (CONV_WIDTH - 1) + k
        conv = conv + convw_ref[k:k + 1, :] * ext_ref[off:off + C, :]
    ext_ref[0:CARRY, :] = ext_ref[C:C + CARRY, :]
    xbc = _silu(conv)
    xs = xbc[:, :SSM_WIDTH]

    lane = lax.broadcasted_iota(jnp.int32, (1, LANES), 1)
    dt_raw = jnp.where(lane < SSM_HEADS, dt_ref[...], 0.0) + dtb_ref[...]
    dt = jnp.maximum(dt_raw, 0.0) + jnp.log1p(jnp.exp(-jnp.abs(dt_raw)))
    dta = dt * a_ref[...]

    row = lax.broadcasted_iota(jnp.int32, (C, C), 0)
    colm = lax.broadcasted_iota(jnp.int32, (C, C), 1)
    tril = row >= colm
    a_cum = _dot_exact_lhs01(jnp.where(tril, 1.0, 0.0).astype(BF16), dta)
    a_cum_t = a_cum.T

    expand = expand_ref[...]
    a_exp = _dot_exact_rhs01(a_cum, expand)
    dt_exp = _dot_exact_rhs01(dt, expand)
    a_last = a_exp[C - 1:C, :]
    decay_in = jnp.exp(a_exp)
    decay_out = jnp.exp(a_last - a_exp)
    chunk_decay = jnp.exp(a_last)
    xdt = xs * dt_exp

    lane2 = lax.broadcasted_iota(jnp.int32, (1, LANES), 1)
    lo_head = lane2 < SSM_HEAD_DIM
    ys = []
    for g in range(SSM_GROUPS):
        gs = slice(g * HW, (g + 1) * HW)
        b_g = xbc[:, SSM_WIDTH + g * SSM_STATE:SSM_WIDTH + (g + 1) * SSM_STATE].astype(BF16)
        c0 = SSM_WIDTH + SSM_GROUPS * SSM_STATE
        c_g = xbc[:, c0 + g * SSM_STATE:c0 + (g + 1) * SSM_STATE].astype(BF16)
        cb = _dot_nt(c_g, b_g)
        state = state_ref[g]
        y_off = _dot(c_g, state.astype(BF16)) * decay_in[:, gs]
        xw = (xdt[:, gs] * decay_out[:, gs]).astype(BF16)
        state_ref[g] = chunk_decay[:, gs] * state + _dot_tn(b_g, xw)
        slabs = []
        for s in range(HW // LANES):
            xd = xdt[:, g * HW + s * LANES:g * HW + (s + 1) * LANES]
            acc = None
            for e in range(2):
                hh = g * (SSM_HEADS // SSM_GROUPS) + 2 * s + e
                seg = a_cum[:, hh:hh + 1] - a_cum_t[hh:hh + 1, :]
                m = cb * jnp.exp(jnp.where(tril, seg, -jnp.inf))
                xm = jnp.where(lo_head if e == 0 else jnp.logical_not(lo_head), xd, 0.0)
                part = _dot(m.astype(BF16), xm.astype(BF16))
                acc = part if acc is None else acc + part
            slabs.append(acc)
        ys.append(jnp.concatenate(slabs, axis=-1) + y_off)
    y = jnp.concatenate(ys, axis=-1) + dskip_ref[...] * xs
    y = y * _silu(z_ref[...])
    outs = []
    for g in range(SSM_GROUPS):
        yg = y[:, g * HW:(g + 1) * HW]
        outs.append(yg * lax.rsqrt(jnp.mean(yg * yg, axis=-1, keepdims=True) + EPS))
    o_ref[...] = (jnp.concatenate(outs, axis=-1) * normw_ref[...]).astype(o_ref.dtype)


def _ssd(proj, conv_w, conv_b, dt_bias, a_log, d_skip, ssm_norm_w):
    T = proj.shape[0]
    C = SSD_CHUNK
    HW = SSM_WIDTH // SSM_GROUPS
    xbc0 = (4 * RET_WIDTH + SSM_WIDTH) // HW
    dt0 = (D_IN_PROJ - SSM_HEADS) // LANES
    pad = lambda v: jnp.zeros((1, LANES), F32).at[0, :SSM_HEADS].set(v.astype(F32))
    a_neg = pad(-jnp.exp(a_log.astype(F32)))
    expand_np = np.zeros((LANES, SSM_WIDTH), np.float32)
    for hh in range(SSM_HEADS):
        expand_np[hh, hh * SSM_HEAD_DIM:(hh + 1) * SSM_HEAD_DIM] = 1.0
    expand = jnp.asarray(expand_np, BF16)
    dskip_exp = jnp.repeat(d_skip.astype(F32), SSM_HEAD_DIM).reshape(1, SSM_WIDTH)
    const = lambda shape: pl.BlockSpec(shape, lambda c: (0, 0))
    return pl.pallas_call(
        _ssd_kernel,
        grid=(T // C,),
        in_specs=[pl.BlockSpec((C, HW), lambda c: (c, xbc0)),
                  pl.BlockSpec((C, HW), lambda c: (c, xbc0 + 1)),
                  pl.BlockSpec((C, HW), lambda c: (c, xbc0 + 2)),
                  pl.BlockSpec((C, SSM_WIDTH), lambda c: (c, 4 * RET_WIDTH // SSM_WIDTH)),
                  pl.BlockSpec((C, LANES), lambda c: (c, dt0)),
                  const((CONV_WIDTH, XBC_WIDTH)), const((1, XBC_WIDTH)),
                  const((1, LANES)), const((1, LANES)),
                  const((1, SSM_WIDTH)), const((1, SSM_WIDTH)),
                  const((LANES, SSM_WIDTH))],
        out_specs=pl.BlockSpec((C, SSM_WIDTH), lambda c: (c, 0)),
        out_shape=jax.ShapeDtypeStruct((T, SSM_WIDTH), BF16),
        scratch_shapes=[pltpu.VMEM((C + 8, XBC_WIDTH), F32),
                        pltpu.VMEM((SSM_GROUPS, SSM_STATE, HW), F32)],
        compiler_params=_params(("arbitrary",)),
        name="ssd",
    )(proj, proj, proj, proj, proj, conv_w, conv_b.reshape(1, XBC_WIDTH), pad(dt_bias), a_neg,
      dskip_exp, ssm_norm_w.reshape(1, SSM_WIDTH), expand)


def _outproj_kernel(ret_ref, ssm_ref, w_ref, x_ref, lnw_ref, wr_ref, br_ref, hs_ref, up_ref, lg_ref):
    tm = x_ref.shape[0]
    h = (x_ref[...] + _dot(ret_ref[...], w_ref[0:RET_WIDTH, :])
         + _dot(ssm_ref[...], w_ref[RET_WIDTH:D_MODEL, :]))
    for s in range(ROW_SLABS):
        hs_ref[pl.ds(s, tm, stride=ROW_SLABS), :] = h[:, s * LANES:(s + 1) * LANES]
    u = h * lax.rsqrt(jnp.mean(h * h, axis=-1, keepdims=True) + EPS) * lnw_ref[...]
    packed = _pack_bf16_pairs(u[:, :D_MODEL // 2], u[:, D_MODEL // 2:])
    for s in range(PACK_SLABS):
        up_ref[pl.ds(s, tm, stride=PACK_SLABS), :] = packed[:, s * LANES:(s + 1) * LANES]
    E = N_EXPERTS
    uh, um, ul = _split3(u)
    ph = _dot(uh, wr_ref[...])
    pm = _dot(um, wr_ref[...])
    pw = _dot(ul, wr_ref[...])
    lg = (ph[:, 0:E] + (ph[:, E:2 * E] + pm[:, 0:E])
          + (ph[:, 2 * E:3 * E] + pm[:, E:2 * E] + pw[:, 0:E]))
    lg_ref[...] = lg + br_ref[...]


def _outproj(ret, ssm, w_out_bf16, x2, ln_w, w_router, b_router):
    T = x2.shape[0]
    tm = 512
    return pl.pallas_call(
        _outproj_kernel,
        grid=(T // tm,),
        in_specs=[pl.BlockSpec((tm, RET_WIDTH), lambda i: (i, 0)),
                  pl.BlockSpec((tm, SSM_WIDTH), lambda i: (i, 0)),
                  pl.BlockSpec((D_MODEL, D_MODEL), lambda i: (0, 0)),
                  pl.BlockSpec((tm, D_MODEL), lambda i: (i, 0)),
                  pl.BlockSpec((1, D_MODEL), lambda i: (0, 0)),
                  pl.BlockSpec((D_MODEL, 3 * N_EXPERTS), lambda i: (0, 0)),
                  pl.BlockSpec((1, N_EXPERTS), lambda i: (0, 0))],
        out_specs=[pl.BlockSpec((tm * ROW_SLABS, LANES), lambda i: (i, 0)),
                   pl.BlockSpec((tm * PACK_SLABS, LANES), lambda i: (i, 0)),
                   pl.BlockSpec((tm, N_EXPERTS), lambda i: (i, 0))],
        out_shape=[jax.ShapeDtypeStruct((T * ROW_SLABS, LANES), F32),
                   jax.ShapeDtypeStruct((T * PACK_SLABS, LANES), jnp.uint32),
                   jax.ShapeDtypeStruct((T, N_EXPERTS), F32)],
        compiler_params=_params(("parallel",)),
        name="outproj",
    )(ret, ssm, w_out_bf16, x2, ln_w.reshape(1, D_MODEL), jnp.concatenate(_split3(w_router), axis=1),
      b_router.reshape(1, N_EXPERTS))


def _expert_kernel(n_ff, n_items, aliased, *refs):
    (item_e, item_row, item_nsub, item_nzero, item_live, item_src, sorted_tok,
     u_hbm, wga_ref, wgb_ref, bga_ref, bgb_ref, wda_ref, wdb_ref, bd_ref) = refs[:15]
    (y_hbm, stage, xbuf, acc, ybuf, wa_s, wb_s, wd_s, sem_in, sem_out) = refs[16:] if aliased else refs[15:]
    del item_e, item_live
    i = pl.program_id(0)
    f = pl.program_id(1)
    nsub = item_nsub[i]
    nzero = item_nzero[i]
    row0 = item_row[i]
    SUB = ROW_BLK
    YS = SUB * ROW_SLABS
    PS = PACK_SLABS
    XS = SUB * PS

    def y_copy(slot, j):
        dst = y_hbm.at[pl.ds(pl.multiple_of((row0 + j * SUB) * ROW_SLABS, YS), YS)]
        return pltpu.make_async_copy(ybuf.at[pl.ds(slot * YS, YS)], dst, sem_out.at[slot])

    def gather_copy(tok, r):
        src = u_hbm.at[pl.ds(pl.multiple_of(tok * PS, PS), PS)]
        return pltpu.make_async_copy(src, stage.at[pl.ds(pl.multiple_of(r * PS, PS), PS)], sem_in)

    n_assign = sorted_tok.shape[0]
    n_sub_max = ITEM_ROWS // SUB
    CH = ITEM_ROWS // n_ff
    nxt = jnp.minimum(i + 1, n_items - 1)

    def issue_rows(item, lo, count, unrolled):
        src0 = item_src[item]

        def one(r, p):
            gather_copy(sorted_tok[jnp.minimum(src0 + r, n_assign - 1)], r).start(priority=p)

        if unrolled:
            for k in range(count):
                one(lo + k, k % 2)
        else:
            def body(q, c):
                for p in range(2):
                    one(lo + 2 * q + p, p)
                return c
            lax.fori_loop(0, count // 2, body, 0)

    def wait_rows():
        for _ in range(n_sub_max):
            pltpu.make_async_copy(u_hbm.at[pl.ds(0, XS)], stage.at[pl.ds(0, XS)], sem_in).wait()

    @pl.when(jnp.logical_and(i == 0, f == 0))
    def _():
        issue_rows(0, 0, ITEM_ROWS, False)

    @pl.when(nsub > 0)
    def _():
        wa_s[...] = wga_ref[...].astype(BF16)
        wb_s[...] = wgb_ref[...].astype(BF16)
        wd_s[...] = pltpu.bitcast(_pack_bf16_pairs(wda_ref[...], wdb_ref[...]), BF16)

    lane = lax.broadcasted_iota(jnp.int32, (1, 2 * FF_TILE), 1)
    even = (lane % 2) == 0
    bga = bga_ref[...]
    bgb = bgb_ref[...]
    W2 = 2 * FF_TILE

    @pl.when(f == 0)
    def _():
        wait_rows()
        half = D_MODEL // 2

        def unpack(j, c):
            rows = pl.ds(pl.multiple_of(j * SUB, SUB), SUB)
            for s in range(PS):
                p = stage[pl.ds(j * XS + s, SUB, stride=PS), :]
                lo = lax.bitcast_convert_type(p << 16, F32)
                hi = lax.bitcast_convert_type(p & jnp.uint32(0xFFFF0000), F32)
                xbuf[rows, s * LANES:(s + 1) * LANES] = lo.astype(BF16)
                xbuf[rows, half + s * LANES:half + (s + 1) * LANES] = hi.astype(BF16)
            acc[rows, :] = jnp.zeros((SUB, D_MODEL), F32)
            return c
        lax.fori_loop(0, jnp.where(nsub > 0, jnp.maximum(nsub, ITEM_VARIANTS[0]), 0), unpack, 0)

    @pl.when(nsub == 0)
    def _():
        issue_rows(nxt, f * CH, CH, False)

    def block(rows):
        xs = xbuf[0:rows, :]
        ga = _dot(xs, wa_s[...]) + bga
        issue_rows(nxt, f * CH, CH, True)
        gb = _dot(xs, wb_s[...]) + bgb
        gate = jnp.where(even, ga, pltpu.roll(gb, 1, 1))
        up = jnp.where(even, pltpu.roll(ga, W2 - 1, 1), gb)
        gate = jnp.minimum(gate, SWIGLU_LIMIT)
        up = jnp.clip(up, -SWIGLU_LIMIT, SWIGLU_LIMIT)
        act = ((up + 1.0) * (gate * jax.nn.sigmoid(gate * SWIGLU_ALPHA))).astype(BF16)
        for n in range(0, D_MODEL, DOWN_TILE):
            acc[0:rows, n:n + DOWN_TILE] += _dot(act, wd_s[:, n:n + DOWN_TILE])

    for m in ITEM_VARIANTS:
        cover = (nsub == m) if m > ITEM_VARIANTS[0] else jnp.logical_and(nsub > 0, nsub <= m)

        @pl.when(cover)
        def _(m=m):
            block(m * SUB)

    @pl.when(f == n_ff - 1)
    def _():
        @pl.when(i == n_items - 1)
        def _():
            wait_rows()

        for q in range(ITEM_ROWS // SUB):
            @pl.when(q < nsub)
            def _(q=q):
                slot = q % 2
                if q >= 2:
                    y_copy(slot, 0).wait()
                val = acc[q * SUB:(q + 1) * SUB, :] + bd_ref[...]
                for s in range(ROW_SLABS):
                    ybuf[pl.ds(slot * YS + s, SUB, stride=ROW_SLABS), :] = val[:, s * LANES:(s + 1) * LANES]
                y_copy(slot, q).start()

        @pl.when(nsub > 0)
        def _():
            y_copy(0, 0).wait()

        @pl.when(nsub > 1)
        def _():
            y_copy(1, 0).wait()

    @pl.when(jnp.logical_and(f == n_ff - 1, nzero > 0))
    def _():
        ybuf[0:YS, :] = jnp.zeros((YS, LANES), F32)

        def start(j, c):
            y_copy(0, j).start()
            return c
        lax.fori_loop(0, nzero, start, 0)

        def wait(j, c):
            y_copy(0, 0).wait()
            return c
        lax.fori_loop(0, nzero, wait, 0)


def _experts_call(u_packed, sorted_tok, n_rows, items, w_gate_up, b_gate_up, w_down, b_down, y_in=None):
    n_ff = (D_FF // 2) // FF_TILE
    assert ITEM_ROWS % (2 * n_ff) == 0 and ITEM_VARIANTS[-1] * ROW_BLK == ITEM_ROWS
    W2 = 2 * FF_TILE
    item_e, item_row, item_nsub, item_nzero, item_live, item_src = items
    n_items = item_e.shape[0]
    aliased = y_in is not None

    def ff(i, f, live):
        return jnp.where(live[i] > 0, f, n_ff - 1)

    grid_spec = pltpu.PrefetchScalarGridSpec(
        num_scalar_prefetch=7,
        grid=(n_items, n_ff),
        in_specs=[
            pl.BlockSpec(memory_space=pl.ANY),
            pl.BlockSpec((None, D_MODEL, W2), lambda i, f, e, r, n, z, lv, sr, st: (e[i], 0, ff(i, f, lv))),
            pl.BlockSpec((None, D_MODEL, W2), lambda i, f, e, r, n, z, lv, sr, st: (e[i], 0, n_ff + ff(i, f, lv))),
            pl.BlockSpec((None, 1, W2), lambda i, f, e, r, n, z, lv, sr, st: (e[i], 0, ff(i, f, lv))),
            pl.BlockSpec((None, 1, W2), lambda i, f, e, r, n, z, lv, sr, st: (e[i], 0, n_ff + ff(i, f, lv))),
            pl.BlockSpec((None, FF_TILE, D_MODEL), lambda i, f, e, r, n, z, lv, sr, st: (e[i], ff(i, f, lv), 0)),
            pl.BlockSpec((None, FF_TILE, D_MODEL), lambda i, f, e, r, n, z, lv, sr, st: (e[i], n_ff + ff(i, f, lv), 0)),
            pl.BlockSpec((None, 1, D_MODEL), lambda i, f, e, r, n, z, lv, sr, st: (e[i], 0, 0)),
        ] + ([pl.BlockSpec(memory_space=pl.ANY)] if aliased else []),
        out_specs=pl.BlockSpec(memory_space=pl.ANY),
        scratch_shapes=[pltpu.VMEM((ITEM_ROWS * PACK_SLABS, LANES), jnp.uint32),
                        pltpu.VMEM((ITEM_ROWS, D_MODEL), BF16),
                        pltpu.VMEM((ITEM_ROWS, D_MODEL), F32),
                        pltpu.VMEM((2 * ROW_BLK * ROW_SLABS, LANES), F32),
                        pltpu.VMEM((D_MODEL, W2), BF16),
                        pltpu.VMEM((D_MODEL, W2), BF16),
                        pltpu.VMEM((W2, D_MODEL), BF16),
                        pltpu.SemaphoreType.DMA(()),
                        pltpu.SemaphoreType.DMA((2,))],
    )
    args = (item_e, item_row, item_nsub, item_nzero, item_live, item_src, sorted_tok, u_packed,
            w_gate_up, w_gate_up, b_gate_up.reshape(N_EXPERTS, 1, 2 * D_FF),
            b_gate_up.reshape(N_EXPERTS, 1, 2 * D_FF), w_down, w_down, b_down.reshape(N_EXPERTS, 1, D_MODEL))
    return pl.pallas_call(
        functools.partial(_expert_kernel, n_ff, n_items, aliased),
        grid_spec=grid_spec,
        out_shape=jax.ShapeDtypeStruct((n_rows * ROW_SLABS, LANES), F32),
        input_output_aliases={len(args): 0} if aliased else {},
        compiler_params=_params(("arbitrary", "arbitrary")),
        name="experts_overflow" if aliased else "experts",
    )(*args, *((y_in,) if aliased else ()))


def _experts(u_packed, sorted_tok, n_rows, items, n_used, w_gate_up, b_gate_up, w_down, b_down):
    n_main = N_EXPERTS + 1
    weights = (w_gate_up, b_gate_up, w_down, b_down)
    y = _experts_call(u_packed, sorted_tok, n_rows, tuple(a[:n_main] for a in items), *weights)
    rest = tuple(a[n_main:] for a in items)
    return lax.cond(n_used > n_main,
                    lambda y_: _experts_call(u_packed, sorted_tok, n_rows, rest, *weights, y_in=y_),
                    lambda y_: y_, y)


def _combine_kernel(tm, n_steps, dest_ref, w_ref, y_hbm, h_ref, lnw_ref, o_ref, buf, osum, sem):
    i = pl.program_id(0)

    def copy(step, slot, t, k):
        d = dest_ref[(step * tm + t) * TOP_K + k]
        return pltpu.make_async_copy(y_hbm.at[d], buf.at[slot, k * tm + t], sem.at[slot])

    def issue(step, slot):
        def body(t, c):
            for k in range(TOP_K):
                copy(step, slot, t, k).start(priority=k % 2)
            return c
        lax.fori_loop(0, tm, body, 0)

    @pl.when(i == 0)
    def _():
        issue(0, 0)

    @pl.when(i + 1 < n_steps)
    def _():
        issue(i + 1, (i + 1) % 2)

    slot = i % 2

    pltpu.make_async_copy(y_hbm.at[pl.ds(0, TOP_K * tm)], buf.at[slot], sem.at[slot]).wait()

    def token(t, c):
        a = h_ref[t]
        for k in range(TOP_K):
            a = a + w_ref[(i * tm + t) * TOP_K + k] * buf[slot, k * tm + t]
        osum[pl.ds(pl.multiple_of(t * ROW_SLABS, ROW_SLABS), ROW_SLABS), :] = a
        return c
    lax.fori_loop(0, tm, token, 0)

    h = osum[...].reshape(tm, ROW_SLABS, LANES)
    ms = jnp.mean(jnp.mean(h * h, axis=2, keepdims=True), axis=1, keepdims=True)
    osum[...] = (h * lax.rsqrt(ms + EPS) * lnw_ref[...]).reshape(tm * ROW_SLABS, LANES)
    for s in range(ROW_SLABS):
        o_ref[:, s * LANES:(s + 1) * LANES] = osum[pl.ds(s, tm, stride=ROW_SLABS), :]


def _combine(dest, wflat, y3, h3, ln_w):
    T = h3.shape[0]
    tm = 128
    n_steps = T // tm
    grid_spec = pltpu.PrefetchScalarGridSpec(
        num_scalar_prefetch=2,
        grid=(n_steps,),
        in_specs=[pl.BlockSpec(memory_space=pl.ANY),
                  pl.BlockSpec((tm, ROW_SLABS, LANES), lambda i, d, w: (i, 0, 0)),
                  pl.BlockSpec((1, ROW_SLABS, LANES), lambda i, d, w: (0, 0, 0))],
        out_specs=pl.BlockSpec((tm, D_MODEL), lambda i, d, w: (i, 0)),
        scratch_shapes=[pltpu.VMEM((2, TOP_K * tm, ROW_SLABS, LANES), F32),
                        pltpu.VMEM((tm * ROW_SLABS, LANES), F32),
                        pltpu.SemaphoreType.DMA((2,))],
    )
    return pl.pallas_call(
        functools.partial(_combine_kernel, tm, n_steps),
        grid_spec=grid_spec,
        out_shape=jax.ShapeDtypeStruct((T, D_MODEL), F32),
        compiler_params=_params(("arbitrary",)),
        name="combine",
    )(dest, wflat, y3, h3, ln_w.reshape(1, ROW_SLABS, LANES))


def _route(logits, n_rows, n_items):
    T = logits.shape[0]
    top_logits, top_idx = lax.top_k(logits, TOP_K)
    top_w = jax.nn.softmax(top_logits, axis=-1)
    e_flat = top_idx.reshape(-1).astype(jnp.int32)
    onehot = (e_flat[:, None] == jnp.arange(N_EXPERTS, dtype=jnp.int32)[None, :]).astype(jnp.int32)
    csum = jnp.cumsum(onehot, axis=0)
    counts = csum[-1]
    padded = (counts + ROW_BLK - 1) // ROW_BLK * ROW_BLK
    pend = jnp.cumsum(padded)
    pstart = pend - padded
    dest = jnp.sum(onehot * (pstart[None, :] + csum - onehot), axis=1).astype(jnp.int32)

    per_e = (padded + ITEM_ROWS - 1) // ITEM_ROWS
    iend = jnp.cumsum(per_e)
    istart = iend - per_e
    ii = jnp.arange(n_items, dtype=jnp.int32)
    total = iend[-1]
    live = (ii < total).astype(jnp.int32)
    ic = jnp.minimum(ii, total - 1)
    ie = jnp.minimum(jnp.sum(ic[:, None] >= iend[None, :], axis=1), N_EXPERTS - 1).astype(jnp.int32)
    within = ic - istart[ie]
    irow = (pstart[ie] + within * ITEM_ROWS).astype(jnp.int32)
    insub = jnp.minimum((padded[ie] - within * ITEM_ROWS) // ROW_BLK, ITEM_ROWS // ROW_BLK).astype(jnp.int32)
    insub = insub * live
    tail_rows = n_rows - pend[-1]
    tail = jnp.logical_and(ii == total, tail_rows > 0)
    inzero = jnp.where(tail, tail_rows // ROW_BLK, 0).astype(jnp.int32)
    irow = jnp.where(tail, pend[-1], irow).astype(jnp.int32)
    n_used = total + (tail_rows > 0).astype(jnp.int32)
    n_assign = T * TOP_K
    assert N_EXPERTS * n_assign < 2 ** 31
    order = jnp.sort(e_flat * n_assign + jnp.arange(n_assign, dtype=jnp.int32)) % n_assign
    sorted_tok = (order // TOP_K).astype(jnp.int32)
    cstart = jnp.cumsum(counts) - counts
    isrc = ((cstart[ie] + within * ITEM_ROWS) * live).astype(jnp.int32)
    return dest, top_w.reshape(-1).astype(F32), (ie, irow, insub, inzero, live, isrc), n_used, sorted_tok


def kernel(x, positions, ln_mix_w, w_in, conv_w, conv_b, dt_bias, a_log, d_skip, ssm_norm_w, w_out,
           ln_ffn_w, w_router, b_router, w_gate_up, b_gate_up, w_down, b_down, ln_final_w):
    B, L, _ = x.shape
    T = B * L
    assert B == 1 and T % 1024 == 0
    x2 = x.reshape(T, D_MODEL)
    half = RET_HEAD_DIM // 2
    inv_freq = (ROPE_BASE ** (-jnp.arange(half, dtype=F32) / half)).reshape(1, half)
    pos_col = positions.reshape(T, 1).astype(F32)

    proj = _inproj(x2, ln_mix_w[0], jnp.swapaxes(w_in[0], 0, 1))
    ret = _retention(proj, pos_col, inv_freq)
    ssm = _ssd(proj, conv_w[0], conv_b[0], dt_bias[0], a_log[0], d_skip[0], ssm_norm_w[0])
    h_slabs, u_packed, logits = _outproj(ret, ssm, w_out[0].astype(BF16), x2, ln_ffn_w[0], w_router[0],
                                         b_router[0])

    n_rows = -(-(T * TOP_K + N_EXPERTS * (ROW_BLK - 1)) // ROW_BLK) * ROW_BLK
    n_items = N_EXPERTS + 1 + n_rows // ITEM_ROWS
    dest, wflat, items, n_used, sorted_tok = _route(logits, n_rows, n_items)

    y_rows = _experts(u_packed, sorted_tok, n_rows, items, n_used, w_gate_up[0], b_gate_up[0], w_down[0],
                      b_down[0])
    out = _combine(dest, wflat, y_rows.reshape(n_rows, ROW_SLABS, LANES),
                   h_slabs.reshape(T, ROW_SLABS, LANES), ln_final_w)
    return out.reshape(B, L, D_MODEL)
```

```python
import functools

import numpy as np
import jax
import jax.numpy as jnp
from jax import lax
from jax.experimental import pallas as pl
from jax.experimental.pallas import tpu as pltpu

F32 = jnp.float32
BF16 = jnp.bfloat16

D_MODEL = 2048
RET_HEADS = 4
RET_HEAD_DIM = 256
RET_WIDTH = RET_HEADS * RET_HEAD_DIM
SSM_WIDTH = D_MODEL - RET_WIDTH
SSM_HEAD_DIM = 64
SSM_HEADS = SSM_WIDTH // SSM_HEAD_DIM
SSM_GROUPS = 2
SSM_STATE = 128
CONV_WIDTH = 4
XBC_WIDTH = SSM_WIDTH + 2 * SSM_GROUPS * SSM_STATE
D_IN_PROJ = 4 * RET_WIDTH + SSM_WIDTH + XBC_WIDTH + SSM_HEADS
ROPE_BASE = 10000.0
N_EXPERTS = 32
TOP_K = 4
D_FF = D_MODEL
SWIGLU_LIMIT = 7.0
SWIGLU_ALPHA = 1.702
EPS = 1e-6

LANES = 128
VMEM_LIMIT = 56 * 1024 * 1024

RET_CHUNK = 256
SSD_CHUNK = 128
ROW_BLK = 256
ITEM_ROWS = 1536
ITEM_VARIANTS = (4, 5, 6)
FF_TILE = 128
DOWN_TILE = 512
ROW_SLABS = D_MODEL // LANES
PACK_SLABS = ROW_SLABS // 2


def _params(sem, **kw):
    return pltpu.CompilerParams(dimension_semantics=sem, vmem_limit_bytes=VMEM_LIMIT, **kw)


def _dot(a, b):
    return jnp.dot(a, b, preferred_element_type=F32)


def _dot_nt(a, b):
    return lax.dot_general(a, b, (((1,), (1,)), ((), ())), preferred_element_type=F32)


def _dot_tn(a, b):
    return lax.dot_general(a, b, (((0,), (0,)), ((), ())), preferred_element_type=F32)


def _split3(x):
    hi = x.astype(BF16)
    r = x - hi.astype(F32)
    mid = r.astype(BF16)
    lo = (r - mid.astype(F32)).astype(BF16)
    return hi, mid, lo


def _dot_exact_rhs01(x, m01):
    hi, mid, lo = _split3(x)
    return _dot(hi, m01) + _dot(mid, m01) + _dot(lo, m01)


def _dot_exact_lhs01(m01, x):
    hi, mid, lo = _split3(x)
    return _dot(m01, hi) + _dot(m01, mid) + _dot(m01, lo)


def _silu(x):
    return x * jax.nn.sigmoid(x)


def _pack_bf16_pairs(lo, hi):
    lo_bits = lax.bitcast_convert_type(lo.astype(BF16).astype(F32), jnp.uint32)
    hi_bits = lax.bitcast_convert_type(hi.astype(BF16).astype(F32), jnp.uint32)
    return hi_bits | (lo_bits >> 16)


INPROJ_CHUNKS = 4


def _inproj_kernel(x_hbm, lnw_ref, w_ref, o_ref, xs_ref, u_ref, sem):
    tm = u_ref.shape[0]
    rows = tm // INPROJ_CHUNKS
    i = pl.program_id(0)

    @pl.when(pl.program_id(1) == 0)
    def _():
        def copy(c):
            return pltpu.make_async_copy(x_hbm.at[pl.ds(i * tm + c * rows, rows)], xs_ref.at[c % 2], sem.at[c % 2])

        copy(0).start()
        copy(1).start()
        for c in range(INPROJ_CHUNKS):
            copy(c).wait()
            x = xs_ref[c % 2]
            ms = jnp.mean(x * x, axis=-1, keepdims=True)
            u_ref[c * rows:(c + 1) * rows, :] = (x * lax.rsqrt(ms + EPS) * lnw_ref[...]).astype(BF16)
            if c + 2 < INPROJ_CHUNKS:
                copy(c + 2).start()

    o_ref[...] = _dot_nt(u_ref[...], w_ref[...].astype(BF16))


def _inproj(x2, ln_w, w_in_t):
    T = x2.shape[0]
    tm, tn = (2048 if T % 2048 == 0 else 1024), 512
    return pl.pallas_call(
        _inproj_kernel,
        grid=(T // tm, pl.cdiv(D_IN_PROJ, tn)),
        in_specs=[pl.BlockSpec(memory_space=pl.ANY),
                  pl.BlockSpec((1, D_MODEL), lambda i, j: (0, 0)),
                  pl.BlockSpec((tn, D_MODEL), lambda i, j: (j, 0))],
        out_specs=pl.BlockSpec((tm, tn), lambda i, j: (i, j)),
        out_shape=jax.ShapeDtypeStruct((T, D_IN_PROJ), F32),
        scratch_shapes=[pltpu.VMEM((2, tm // INPROJ_CHUNKS, D_MODEL), F32),
                        pltpu.VMEM((tm, D_MODEL), BF16),
                        pltpu.SemaphoreType.DMA((2,))],
        compiler_params=_params(("arbitrary", "arbitrary")),
        name="inproj",
    )(x2, ln_w.reshape(1, D_MODEL), w_in_t)


def _retention_tables():
    C = RET_CHUNK
    h = np.arange(RET_HEADS, dtype=np.float64)
    log_gamma = np.log1p(-np.exp2(-5.0 - h))
    idx = np.arange(C, dtype=np.float64)
    rel = idx[:, None] - idx[None, :]
    intra = np.where(rel >= 0, np.exp(log_gamma[:, None, None] * np.maximum(rel, 0.0)), 0.0)
    q_decay = np.exp(log_gamma[:, None] * (idx + 1.0))
    k_decay = np.exp(log_gamma[:, None] * (C - 1.0 - idx))
    chunk_decay = np.exp(log_gamma * C)
    qd = np.broadcast_to(q_decay[:, :, None], (RET_HEADS, C, RET_HEAD_DIM))
    kd = np.broadcast_to(k_decay[:, :, None], (RET_HEADS, C, RET_HEAD_DIM))
    return (jnp.asarray(intra, F32), jnp.asarray(qd, F32), jnp.asarray(kd, F32),
            [float(c) for c in chunk_decay])


def _retention_kernel(chunk_decay, pos_ref, invf_ref, q_ref, k_ref, v_ref, g_ref,
                      intra_ref, qd_ref, kd_ref, o_ref, state_ref):
    @pl.when(pl.program_id(0) == 0)
    def _():
        state_ref[...] = jnp.zeros_like(state_ref)

    half = RET_HEAD_DIM // 2
    ang = pos_ref[...] * invf_ref[...]
    cos = jnp.cos(ang)
    sin = jnp.sin(ang)

    def rope(t):
        t1, t2 = t[:, :half], t[:, half:]
        return jnp.concatenate([t1 * cos - t2 * sin, t2 * cos + t1 * sin], axis=-1)

    for h in range(RET_HEADS):
        sl = slice(h * RET_HEAD_DIM, (h + 1) * RET_HEAD_DIM)
        q = rope(q_ref[:, sl])
        k = rope(k_ref[:, sl]) * (RET_HEAD_DIM ** -0.5)
        v = v_ref[:, sl].astype(BF16)
        state = state_ref[h]
        scores = _dot_nt(q.astype(BF16), k.astype(BF16)) * intra_ref[h]
        inner = _dot(scores.astype(BF16), v)
        cross = _dot((q * qd_ref[h]).astype(BF16), state.astype(BF16))
        state_ref[h] = chunk_decay[h] * state + _dot_tn((k * kd_ref[h]).astype(BF16), v)
        o = inner + cross
        o = o * lax.rsqrt(jnp.mean(o * o, axis=-1, keepdims=True) + EPS)
        o_ref[:, sl] = (o * _silu(g_ref[:, sl])).astype(o_ref.dtype)


def _retention(proj, pos_col, inv_freq):
    T = proj.shape[0]
    C = RET_CHUNK
    intra, qd, kd, chunk_decay = _retention_tables()
    col = lambda j: pl.BlockSpec((C, RET_WIDTH), lambda c, j=j: (c, j))
    const3 = lambda shape: pl.BlockSpec(shape, lambda c: (0, 0, 0))
    return pl.pallas_call(
        functools.partial(_retention_kernel, chunk_decay),
        grid=(T // C,),
        in_specs=[pl.BlockSpec((C, 1), lambda c: (c, 0)),
                  pl.BlockSpec((1, RET_HEAD_DIM // 2), lambda c: (0, 0)),
                  col(0), col(1), col(2), col(3),
                  const3((RET_HEADS, C, C)),
                  const3((RET_HEADS, C, RET_HEAD_DIM)),
                  const3((RET_HEADS, C, RET_HEAD_DIM))],
        out_specs=pl.BlockSpec((C, RET_WIDTH), lambda c: (c, 0)),
        out_shape=jax.ShapeDtypeStruct((T, RET_WIDTH), BF16),
        scratch_shapes=[pltpu.VMEM((RET_HEADS, RET_HEAD_DIM, RET_HEAD_DIM), F32)],
        compiler_params=_params(("arbitrary",)),
        name="retention",
    )(pos_col, inv_freq, proj, proj, proj, proj, intra, qd, kd)


def _ssd_kernel(xs0_ref, xs1_ref, bc_ref, z_ref, dt_ref, convw_ref, convb_ref, dtb_ref, a_ref,
                dskip_ref, normw_ref, expand_ref, o_ref, ext_ref, state_ref):
    C = SSD_CHUNK
    HW = SSM_WIDTH // SSM_GROUPS
    CARRY = 8

    @pl.when(pl.program_id(0) == 0)
    def _():
        ext_ref[0:CARRY, :] = jnp.zeros((CARRY, XBC_WIDTH), F32)
        state_ref[...] = jnp.zeros_like(state_ref)

    ext_ref[CARRY:CARRY + C, 0:HW] = xs0_ref[...]
    ext_ref[CARRY:CARRY + C, HW:2 * HW] = xs1_ref[...]
    ext_ref[CARRY:CARRY + C, 2 * HW:3 * HW] = bc_ref[...]
    conv = convb_ref[...]
    for k in range(CONV_WIDTH):
        off = CARRY - (CONV_WIDTH - 1) + k
        conv = conv + convw_ref[k:k + 1, :] * ext_ref[off:off + C, :]
    ext_ref[0:CARRY, :] = ext_ref[C:C + CARRY, :]
    xbc = _silu(conv)
    xs = xbc[:, :SSM_WIDTH]

    lane = lax.broadcasted_iota(jnp.int32, (1, LANES), 1)
    dt_raw = jnp.where(lane < SSM_HEADS, dt_ref[...], 0.0) + dtb_ref[...]
    dt = jnp.maximum(dt_raw, 0.0) + jnp.log1p(jnp.exp(-jnp.abs(dt_raw)))
    dta = dt * a_ref[...]

    row = lax.broadcasted_iota(jnp.int32, (C, C), 0)
    colm = lax.broadcasted_iota(jnp.int32, (C, C), 1)
    tril = row >= colm
    a_cum = _dot_exact_lhs01(jnp.where(tril, 1.0, 0.0).astype(BF16), dta)
    a_cum_t = a_cum.T

    expand = expand_ref[...]
    a_exp = _dot_exact_rhs01(a_cum, expand)
    dt_exp = _dot_exact_rhs01(dt, expand)
    a_last = a_exp[C - 1:C, :]
    decay_in = jnp.exp(a_exp)
    decay_out = jnp.exp(a_last - a_exp)
    chunk_decay = jnp.exp(a_last)
    xdt = xs * dt_exp

    lane2 = lax.broadcasted_iota(jnp.int32, (1, LANES), 1)
    lo_head = lane2 < SSM_HEAD_DIM
    ys = []
    for g in range(SSM_GROUPS):
        gs = slice(g * HW, (g + 1) * HW)
        b_g = xbc[:, SSM_WIDTH + g * SSM_STATE:SSM_WIDTH + (g + 1) * SSM_STATE].astype(BF16)
        c0 = SSM_WIDTH + SSM_GROUPS * SSM_STATE
        c_g = xbc[:, c0 + g * SSM_STATE:c0 + (g + 1) * SSM_STATE].astype(BF16)
        cb = _dot_nt(c_g, b_g)
        state = state_ref[g]
        y_off = _dot(c_g, state.astype(BF16)) * decay_in[:, gs]
        xw = (xdt[:, gs] * decay_out[:, gs]).astype(BF16)
        state_ref[g] = chunk_decay[:, gs] * state + _dot_tn(b_g, xw)
        slabs = []
        for s in range(HW // LANES):
            xd = xdt[:, g * HW + s * LANES:g * HW + (s + 1) * LANES]
            acc = None
            for e in range(2):
                hh = g * (SSM_HEADS // SSM_GROUPS) + 2 * s + e
                seg = a_cum[:, hh:hh + 1] - a_cum_t[hh:hh + 1, :]
                m = cb * jnp.exp(jnp.where(tril, seg, -jnp.inf))
                xm = jnp.where(lo_head if e == 0 else jnp.logical_not(lo_head), xd, 0.0)
                part = _dot(m.astype(BF16), xm.astype(BF16))
                acc = part if acc is None else acc + part
            slabs.append(acc)
        ys.append(jnp.concatenate(slabs, axis=-1) + y_off)
    y = jnp.concatenate(ys, axis=-1) + dskip_ref[...] * xs
    y = y * _silu(z_ref[...])
    outs = []
    for g in range(SSM_GROUPS):
        yg = y[:, g * HW:(g + 1) * HW]
        outs.append(yg * lax.rsqrt(jnp.mean(yg * yg, axis=-1, keepdims=True) + EPS))
    o_ref[...] = (jnp.concatenate(outs, axis=-1) * normw_ref[...]).astype(o_ref.dtype)


def _ssd(proj, conv_w, conv_b, dt_bias, a_log, d_skip, ssm_norm_w):
    T = proj.shape[0]
    C = SSD_CHUNK
    HW = SSM_WIDTH // SSM_GROUPS
    xbc0 = (4 * RET_WIDTH + SSM_WIDTH) // HW
    dt0 = (D_IN_PROJ - SSM_HEADS) // LANES
    pad = lambda v: jnp.zeros((1, LANES), F32).at[0, :SSM_HEADS].set(v.astype(F32))
    a_neg = pad(-jnp.exp(a_log.astype(F32)))
    expand_np = np.zeros((LANES, SSM_WIDTH), np.float32)
    for hh in range(SSM_HEADS):
        expand_np[hh, hh * SSM_HEAD_DIM:(hh + 1) * SSM_HEAD_DIM] = 1.0
    expand = jnp.asarray(expand_np, BF16)
    dskip_exp = jnp.repeat(d_skip.astype(F32), SSM_HEAD_DIM).reshape(1, SSM_WIDTH)
    const = lambda shape: pl.BlockSpec(shape, lambda c: (0, 0))
    return pl.pallas_call(
        _ssd_kernel,
        grid=(T // C,),
        in_specs=[pl.BlockSpec((C, HW), lambda c: (c, xbc0)),
                  pl.BlockSpec((C, HW), lambda c: (c, xbc0 + 1)),
                  pl.BlockSpec((C, HW), lambda c: (c, xbc0 + 2)),
                  pl.BlockSpec((C, SSM_WIDTH), lambda c: (c, 4 * RET_WIDTH // SSM_WIDTH)),
                  pl.BlockSpec((C, LANES), lambda c: (c, dt0)),
                  const((CONV_WIDTH, XBC_WIDTH)), const((1, XBC_WIDTH)),
                  const((1, LANES)), const((1, LANES)),
                  const((1, SSM_WIDTH)), const((1, SSM_WIDTH)),
                  const((LANES, SSM_WIDTH))],
        out_specs=pl.BlockSpec((C, SSM_WIDTH), lambda c: (c, 0)),
        out_shape=jax.ShapeDtypeStruct((T, SSM_WIDTH), BF16),
        scratch_shapes=[pltpu.VMEM((C + 8, XBC_WIDTH), F32),
                        pltpu.VMEM((SSM_GROUPS, SSM_STATE, HW), F32)],
        compiler_params=_params(("arbitrary",)),
        name="ssd",
    )(proj, proj, proj, proj, proj, conv_w, conv_b.reshape(1, XBC_WIDTH), pad(dt_bias), a_neg,
      dskip_exp, ssm_norm_w.reshape(1, SSM_WIDTH), expand)


def _outproj_kernel(ret_ref, ssm_ref, w_ref, x_ref, lnw_ref, wr_ref, br_ref, hs_ref, up_ref, lg_ref):
    tm = x_ref.shape[0]
    h = (x_ref[...] + _dot(ret_ref[...], w_ref[0:RET_WIDTH, :])
         + _dot(ssm_ref[...], w_ref[RET_WIDTH:D_MODEL, :]))
    for s in range(ROW_SLABS):
        hs_ref[pl.ds(s, tm, stride=ROW_SLABS), :] = h[:, s * LANES:(s + 1) * LANES]
    u = h * lax.rsqrt(jnp.mean(h * h, axis=-1, keepdims=True) + EPS) * lnw_ref[...]
    packed = _pack_bf16_pairs(u[:, :D_MODEL // 2], u[:, D_MODEL // 2:])
    for s in range(PACK_SLABS):
        up_ref[pl.ds(s, tm, stride=PACK_SLABS), :] = packed[:, s * LANES:(s + 1) * LANES]
    E = N_EXPERTS
    uh, um, ul = _split3(u)
    ph = _dot(uh, wr_ref[...])
    pm = _dot(um, wr_ref[...])
    pw = _dot(ul, wr_ref[...])
    lg = (ph[:, 0:E] + (ph[:, E:2 * E] + pm[:, 0:E])
          + (ph[:, 2 * E:3 * E] + pm[:, E:2 * E] + pw[:, 0:E]))
    lg_ref[...] = lg + br_ref[...]


def _outproj(ret, ssm, w_out_bf16, x2, ln_w, w_router, b_router):
    T = x2.shape[0]
    tm = 512
    return pl.pallas_call(
        _outproj_kernel,
        grid=(T // tm,),
        in_specs=[pl.BlockSpec((tm, RET_WIDTH), lambda i: (i, 0)),
                  pl.BlockSpec((tm, SSM_WIDTH), lambda i: (i, 0)),
                  pl.BlockSpec((D_MODEL, D_MODEL), lambda i: (0, 0)),
                  pl.BlockSpec((tm, D_MODEL), lambda i: (i, 0)),
                  pl.BlockSpec((1, D_MODEL), lambda i: (0, 0)),
                  pl.BlockSpec((D_MODEL, 3 * N_EXPERTS), lambda i: (0, 0)),
                  pl.BlockSpec((1, N_EXPERTS), lambda i: (0, 0))],
        out_specs=[pl.BlockSpec((tm * ROW_SLABS, LANES), lambda i: (i, 0)),
                   pl.BlockSpec((tm * PACK_SLABS, LANES), lambda i: (i, 0)),
                   pl.BlockSpec((tm, N_EXPERTS), lambda i: (i, 0))],
        out_shape=[jax.ShapeDtypeStruct((T * ROW_SLABS, LANES), F32),
                   jax.ShapeDtypeStruct((T * PACK_SLABS, LANES), jnp.uint32),
                   jax.ShapeDtypeStruct((T, N_EXPERTS), F32)],
        compiler_params=_params(("parallel",)),
        name="outproj",
    )(ret, ssm, w_out_bf16, x2, ln_w.reshape(1, D_MODEL), jnp.concatenate(_split3(w_router), axis=1),
      b_router.reshape(1, N_EXPERTS))


def _expert_kernel(n_ff, n_items, aliased, *refs):
    (item_e, item_row, item_nsub, item_nzero, item_live, item_src, sorted_tok,
     u_hbm, wga_ref, wgb_ref, bga_ref, bgb_ref, wda_ref, wdb_ref, bd_ref) = refs[:15]
    (y_hbm, stage, xbuf, acc, ybuf, wa_s, wb_s, wd_s, sem_in, sem_out) = refs[16:] if aliased else refs[15:]
    del item_e, item_live
    i = pl.program_id(0)
    f = pl.program_id(1)
    nsub = item_nsub[i]
    nzero = item_nzero[i]
    row0 = item_row[i]
    SUB = ROW_BLK
    YS = SUB * ROW_SLABS
    PS = PACK_SLABS
    XS = SUB * PS

    def y_copy(slot, j):
        dst = y_hbm.at[pl.ds(pl.multiple_of((row0 + j * SUB) * ROW_SLABS, YS), YS)]
        return pltpu.make_async_copy(ybuf.at[pl.ds(slot * YS, YS)], dst, sem_out.at[slot])

    def gather_copy(tok, r):
        src = u_hbm.at[pl.ds(pl.multiple_of(tok * PS, PS), PS)]
        return pltpu.make_async_copy(src, stage.at[pl.ds(pl.multiple_of(r * PS, PS), PS)], sem_in)

    n_assign = sorted_tok.shape[0]
    n_sub_max = ITEM_ROWS // SUB
    CH = ITEM_ROWS // n_ff
    nxt = jnp.minimum(i + 1, n_items - 1)

    def issue_rows(item, lo, count, unrolled):
        src0 = item_src[item]

        def one(r, p):
            gather_copy(sorted_tok[jnp.minimum(src0 + r, n_assign - 1)], r).start(priority=p)

        if unrolled:
            for k in range(count):
                one(lo + k, k % 2)
        else:
            def body(q, c):
                for p in range(2):
                    one(lo + 2 * q + p, p)
                return c
            lax.fori_loop(0, count // 2, body, 0)

    def wait_rows():
        for _ in range(n_sub_max):
            pltpu.make_async_copy(u_hbm.at[pl.ds(0, XS)], stage.at[pl.ds(0, XS)], sem_in).wait()

    @pl.when(jnp.logical_and(i == 0, f == 0))
    def _():
        issue_rows(0, 0, ITEM_ROWS, False)

    lane = lax.broadcasted_iota(jnp.int32, (1, 2 * FF_TILE), 1)
    even = (lane % 2) == 0
    bga = bga_ref[...]
    bgb = bgb_ref[...]
    W2 = 2 * FF_TILE

    @pl.when(f == 0)
    def _():
        wait_rows()
        half = D_MODEL // 2

        def unpack(j, c):
            rows = pl.ds(pl.multiple_of(j * SUB, SUB), SUB)
            for s in range(PS):
                p = stage[pl.ds(j * XS + s, SUB, stride=PS), :]
                lo = lax.bitcast_convert_type(p << 16, F32)
                hi = lax.bitcast_convert_type(p & jnp.uint32(0xFFFF0000), F32)
                xbuf[rows, s * LANES:(s + 1) * LANES] = lo.astype(BF16)
                xbuf[rows, half + s * LANES:half + (s + 1) * LANES] = hi.astype(BF16)
            acc[rows, :] = jnp.zeros((SUB, D_MODEL), F32)
            return c
        lax.fori_loop(0, jnp.where(nsub > 0, jnp.maximum(nsub, ITEM_VARIANTS[0]), 0), unpack, 0)

    @pl.when(nsub == 0)
    def _():
        issue_rows(nxt, f * CH, CH, False)

    def block(rows):
        xs = xbuf[0:rows, :]
        wa_s[...] = wga_ref[...].astype(BF16)
        ga = _dot(xs, wa_s[...]) + bga
        issue_rows(nxt, f * CH, CH, True)
        wb_s[...] = wgb_ref[...].astype(BF16)
        gb = _dot(xs, wb_s[...]) + bgb
        wd_s[...] = pltpu.bitcast(_pack_bf16_pairs(wda_ref[...], wdb_ref[...]), BF16)
        gate = jnp.where(even, ga, pltpu.roll(gb, 1, 1))
        up = jnp.where(even, pltpu.roll(ga, W2 - 1, 1), gb)
        gate = jnp.minimum(gate, SWIGLU_LIMIT)
        up = jnp.clip(up, -SWIGLU_LIMIT, SWIGLU_LIMIT)
        act = ((up + 1.0) * (gate * jax.nn.sigmoid(gate * SWIGLU_ALPHA))).astype(BF16)
        for n in range(0, D_MODEL, DOWN_TILE):
            acc[0:rows, n:n + DOWN_TILE] += _dot(act, wd_s[:, n:n + DOWN_TILE])

    for m in ITEM_VARIANTS:
        cover = (nsub == m) if m > ITEM_VARIANTS[0] else jnp.logical_and(nsub > 0, nsub <= m)

        @pl.when(cover)
        def _(m=m):
            block(m * SUB)

    @pl.when(f == n_ff - 1)
    def _():
        @pl.when(i == n_items - 1)
        def _():
            wait_rows()

        for q in range(ITEM_ROWS // SUB):
            @pl.when(q < nsub)
            def _(q=q):
                slot = q % 2
                if q >= 2:
                    y_copy(slot, 0).wait()
                val = acc[q * SUB:(q + 1) * SUB, :] + bd_ref[...]
                for s in range(ROW_SLABS):
                    ybuf[pl.ds(slot * YS + s, SUB, stride=ROW_SLABS), :] = val[:, s * LANES:(s + 1) * LANES]
                y_copy(slot, q).start()

        @pl.when(nsub > 0)
        def _():
            y_copy(0, 0).wait()

        @pl.when(nsub > 1)
        def _():
            y_copy(1, 0).wait()

    @pl.when(jnp.logical_and(f == n_ff - 1, nzero > 0))
    def _():
        ybuf[0:YS, :] = jnp.zeros((YS, LANES), F32)

        def start(j, c):
            y_copy(0, j).start()
            return c
        lax.fori_loop(0, nzero, start, 0)

        def wait(j, c):
            y_copy(0, 0).wait()
            return c
        lax.fori_loop(0, nzero, wait, 0)


def _experts_call(u_packed, sorted_tok, n_rows, items, w_gate_up, b_gate_up, w_down, b_down, y_in=None):
    n_ff = (D_FF // 2) // FF_TILE
    assert ITEM_ROWS % (2 * n_ff) == 0 and ITEM_VARIANTS[-1] * ROW_BLK == ITEM_ROWS
    W2 = 2 * FF_TILE
    item_e, item_row, item_nsub, item_nzero, item_live, item_src = items
    n_items = item_e.shape[0]
    aliased = y_in is not None

    def ff(i, f, live):
        return jnp.where(live[i] > 0, f, n_ff - 1)

    grid_spec = pltpu.PrefetchScalarGridSpec(
        num_scalar_prefetch=7,
        grid=(n_items, n_ff),
        in_specs=[
            pl.BlockSpec(memory_space=pl.ANY),
            pl.BlockSpec((None, D_MODEL, W2), lambda i, f, e, r, n, z, lv, sr, st: (e[i], 0, ff(i, f, lv))),
            pl.BlockSpec((None, D_MODEL, W2), lambda i, f, e, r, n, z, lv, sr, st: (e[i], 0, n_ff + ff(i, f, lv))),
            pl.BlockSpec((None, 1, W2), lambda i, f, e, r, n, z, lv, sr, st: (e[i], 0, ff(i, f, lv))),
            pl.BlockSpec((None, 1, W2), lambda i, f, e, r, n, z, lv, sr, st: (e[i], 0, n_ff + ff(i, f, lv))),
            pl.BlockSpec((None, FF_TILE, D_MODEL), lambda i, f, e, r, n, z, lv, sr, st: (e[i], ff(i, f, lv), 0)),
            pl.BlockSpec((None, FF_TILE, D_MODEL), lambda i, f, e, r, n, z, lv, sr, st: (e[i], n_ff + ff(i, f, lv), 0)),
            pl.BlockSpec((None, 1, D_MODEL), lambda i, f, e, r, n, z, lv, sr, st: (e[i], 0, 0)),
        ] + ([pl.BlockSpec(memory_space=pl.ANY)] if aliased else []),
        out_specs=pl.BlockSpec(memory_space=pl.ANY),
        scratch_shapes=[pltpu.VMEM((ITEM_ROWS * PACK_SLABS, LANES), jnp.uint32),
                        pltpu.VMEM((ITEM_ROWS, D_MODEL), BF16),
                        pltpu.VMEM((ITEM_ROWS, D_MODEL), F32),
                        pltpu.VMEM((2 * ROW_BLK * ROW_SLABS, LANES), F32),
                        pltpu.VMEM((D_MODEL, W2), BF16),
                        pltpu.VMEM((D_MODEL, W2), BF16),
                        pltpu.VMEM((W2, D_MODEL), BF16),
                        pltpu.SemaphoreType.DMA(()),
                        pltpu.SemaphoreType.DMA((2,))],
    )
    args = (item_e, item_row, item_nsub, item_nzero, item_live, item_src, sorted_tok, u_packed,
            w_gate_up, w_gate_up, b_gate_up.reshape(N_EXPERTS, 1, 2 * D_FF),
            b_gate_up.reshape(N_EXPERTS, 1, 2 * D_FF), w_down, w_down, b_down.reshape(N_EXPERTS, 1, D_MODEL))
    return pl.pallas_call(
        functools.partial(_expert_kernel, n_ff, n_items, aliased),
        grid_spec=grid_spec,
        out_shape=jax.ShapeDtypeStruct((n_rows * ROW_SLABS, LANES), F32),
        input_output_aliases={len(args): 0} if aliased else {},
        compiler_params=_params(("arbitrary", "arbitrary")),
        name="experts_overflow" if aliased else "experts",
    )(*args, *((y_in,) if aliased else ()))


def _experts(u_packed, sorted_tok, n_rows, items, n_used, w_gate_up, b_gate_up, w_down, b_down):
    n_main = N_EXPERTS + 1
    weights = (w_gate_up, b_gate_up, w_down, b_down)
    y = _experts_call(u_packed, sorted_tok, n_rows, tuple(a[:n_main] for a in items), *weights)
    rest = tuple(a[n_main:] for a in items)
    return lax.cond(n_used > n_main,
                    lambda y_: _experts_call(u_packed, sorted_tok, n_rows, rest, *weights, y_in=y_),
                    lambda y_: y_, y)


def _combine_kernel(tm, n_steps, dest_ref, w_ref, y_hbm, h_ref, lnw_ref, o_ref, buf, osum, sem):
    i = pl.program_id(0)

    def copy(step, slot, t, k):
        d = dest_ref[(step * tm + t) * TOP_K + k]
        return pltpu.make_async_copy(y_hbm.at[d], buf.at[slot, k * tm + t], sem.at[slot])

    def issue(step, slot):
        def body(t, c):
            for k in range(TOP_K):
                copy(step, slot, t, k).start(priority=k % 2)
            return c
        lax.fori_loop(0, tm, body, 0, unroll=4)

    @pl.when(i == 0)
    def _():
        issue(0, 0)

    @pl.when(i + 1 < n_steps)
    def _():
        issue(i + 1, (i + 1) % 2)

    slot = i % 2

    pltpu.make_async_copy(y_hbm.at[pl.ds(0, TOP_K * tm)], buf.at[slot], sem.at[slot]).wait()

    def token(t, c):
        a = h_ref[t]
        for k in range(TOP_K):
            a = a + w_ref[(i * tm + t) * TOP_K + k] * buf[slot, k * tm + t]
        osum[pl.ds(pl.multiple_of(t * ROW_SLABS, ROW_SLABS), ROW_SLABS), :] = a
        return c
    lax.fori_loop(0, tm, token, 0, unroll=4)

    h = osum[...].reshape(tm, ROW_SLABS, LANES)
    ms = jnp.mean(jnp.mean(h * h, axis=2, keepdims=True), axis=1, keepdims=True)
    osum[...] = (h * lax.rsqrt(ms + EPS) * lnw_ref[...]).reshape(tm * ROW_SLABS, LANES)
    for s in range(ROW_SLABS):
        o_ref[:, s * LANES:(s + 1) * LANES] = osum[pl.ds(s, tm, stride=ROW_SLABS), :]


def _combine(dest, wflat, y3, h3, ln_w):
    T = h3.shape[0]
    tm = 128
    n_steps = T // tm
    grid_spec = pltpu.PrefetchScalarGridSpec(
        num_scalar_prefetch=2,
        grid=(n_steps,),
        in_specs=[pl.BlockSpec(memory_space=pl.ANY),
                  pl.BlockSpec((tm, ROW_SLABS, LANES), lambda i, d, w: (i, 0, 0)),
                  pl.BlockSpec((1, ROW_SLABS, LANES), lambda i, d, w: (0, 0, 0))],
        out_specs=pl.BlockSpec((tm, D_MODEL), lambda i, d, w: (i, 0)),
        scratch_shapes=[pltpu.VMEM((2, TOP_K * tm, ROW_SLABS, LANES), F32),
                        pltpu.VMEM((tm * ROW_SLABS, LANES), F32),
                        pltpu.SemaphoreType.DMA((2,))],
    )
    return pl.pallas_call(
        functools.partial(_combine_kernel, tm, n_steps),
        grid_spec=grid_spec,
        out_shape=jax.ShapeDtypeStruct((T, D_MODEL), F32),
        compiler_params=_params(("arbitrary",)),
        name="combine",
    )(dest, wflat, y3, h3, ln_w.reshape(1, ROW_SLABS, LANES))


def _route(logits, n_rows, n_items):
    T = logits.shape[0]
    top_logits, top_idx = lax.top_k(logits, TOP_K)
    top_w = jax.nn.softmax(top_logits, axis=-1)
    e_flat = top_idx.reshape(-1).astype(jnp.int32)
    onehot = (e_flat[:, None] == jnp.arange(N_EXPERTS, dtype=jnp.int32)[None, :]).astype(jnp.int32)
    csum = jnp.cumsum(onehot, axis=0)
    counts = csum[-1]
    padded = (counts + ROW_BLK - 1) // ROW_BLK * ROW_BLK
    pend = jnp.cumsum(padded)
    pstart = pend - padded
    dest = jnp.sum(onehot * (pstart[None, :] + csum - onehot), axis=1).astype(jnp.int32)

    per_e = (padded + ITEM_ROWS - 1) // ITEM_ROWS
    iend = jnp.cumsum(per_e)
    istart = iend - per_e
    ii = jnp.arange(n_items, dtype=jnp.int32)
    total = iend[-1]
    live = (ii < total).astype(jnp.int32)
    ic = jnp.minimum(ii, total - 1)
    ie = jnp.minimum(jnp.sum(ic[:, None] >= iend[None, :], axis=1), N_EXPERTS - 1).astype(jnp.int32)
    within = ic - istart[ie]
    irow = (pstart[ie] + within * ITEM_ROWS).astype(jnp.int32)
    insub = jnp.minimum((padded[ie] - within * ITEM_ROWS) // ROW_BLK, ITEM_ROWS // ROW_BLK).astype(jnp.int32)
    insub = insub * live
    tail_rows = n_rows - pend[-1]
    tail = jnp.logical_and(ii == total, tail_rows > 0)
    inzero = jnp.where(tail, tail_rows // ROW_BLK, 0).astype(jnp.int32)
    irow = jnp.where(tail, pend[-1], irow).astype(jnp.int32)
    n_used = total + (tail_rows > 0).astype(jnp.int32)
    n_assign = T * TOP_K
    assert N_EXPERTS * n_assign < 2 ** 31
    order = jnp.sort(e_flat * n_assign + jnp.arange(n_assign, dtype=jnp.int32)) % n_assign
    sorted_tok = (order // TOP_K).astype(jnp.int32)
    cstart = jnp.cumsum(counts) - counts
    isrc = ((cstart[ie] + within * ITEM_ROWS) * live).astype(jnp.int32)
    return dest, top_w.reshape(-1).astype(F32), (ie, irow, insub, inzero, live, isrc), n_used, sorted_tok


def kernel(x, positions, ln_mix_w, w_in, conv_w, conv_b, dt_bias, a_log, d_skip, ssm_norm_w, w_out,
           ln_ffn_w, w_router, b_router, w_gate_up, b_gate_up, w_down, b_down, ln_final_w):
    B, L, _ = x.shape
    T = B * L
    assert B == 1 and T % 1024 == 0
    x2 = x.reshape(T, D_MODEL)
    half = RET_HEAD_DIM // 2
    inv_freq = (ROPE_BASE ** (-jnp.arange(half, dtype=F32) / half)).reshape(1, half)
    pos_col = positions.reshape(T, 1).astype(F32)

    proj = _inproj(x2, ln_mix_w[0], jnp.swapaxes(w_in[0], 0, 1))
    ret = _retention(proj, pos_col, inv_freq)
    ssm = _ssd(proj, conv_w[0], conv_b[0], dt_bias[0], a_log[0], d_skip[0], ssm_norm_w[0])
    h_slabs, u_packed, logits = _outproj(ret, ssm, w_out[0].astype(BF16), x2, ln_ffn_w[0], w_router[0],
                                         b_router[0])

    n_rows = -(-(T * TOP_K + N_EXPERTS * (ROW_BLK - 1)) // ROW_BLK) * ROW_BLK
    n_items = N_EXPERTS + 1 + n_rows // ITEM_ROWS
    dest, wflat, items, n_used, sorted_tok = _route(logits, n_rows, n_items)

    y_rows = _experts(u_packed, sorted_tok, n_rows, items, n_used, w_gate_up[0], b_gate_up[0], w_down[0],
                      b_down[0])
    out = _combine(dest, wflat, y_rows.reshape(n_rows, ROW_SLABS, LANES),
                   h_slabs.reshape(T, ROW_SLABS, LANES), ln_final_w)
    return out.reshape(B, L, D_MODEL)
```

```python
import functools

import numpy as np
import jax
import jax.numpy as jnp
from jax import lax
from jax.experimental import pallas as pl
from jax.experimental.pallas import tpu as pltpu

F32 = jnp.float32
BF16 = jnp.bfloat16

D_MODEL = 2048
RET_HEADS = 4
RET_HEAD_DIM = 256
RET_WIDTH = RET_HEADS * RET_HEAD_DIM
SSM_WIDTH = D_MODEL - RET_WIDTH
SSM_HEAD_DIM = 64
SSM_HEADS = SSM_WIDTH // SSM_HEAD_DIM
SSM_GROUPS = 2
SSM_STATE = 128
CONV_WIDTH = 4
XBC_WIDTH = SSM_WIDTH + 2 * SSM_GROUPS * SSM_STATE
D_IN_PROJ = 4 * RET_WIDTH + SSM_WIDTH + XBC_WIDTH + SSM_HEADS
ROPE_BASE = 10000.0
N_EXPERTS = 32
TOP_K = 4
D_FF = D_MODEL
SWIGLU_LIMIT = 7.0
SWIGLU_ALPHA = 1.702
EPS = 1e-6

LANES = 128
VMEM_LIMIT = 56 * 1024 * 1024

RET_CHUNK = 256
SSD_CHUNK = 128
ROW_BLK = 128
ITEM_ROWS = 1536
ITEM_VARIANTS = (8, 9, 10, 12)
FF_TILE = 128
DOWN_TILE = 512
ROW_SLABS = D_MODEL // LANES
PACK_SLABS = ROW_SLABS // 2


def _params(sem, **kw):
    return pltpu.CompilerParams(dimension_semantics=sem, vmem_limit_bytes=VMEM_LIMIT, **kw)


def _dot(a, b):
    return jnp.dot(a, b, preferred_element_type=F32)


def _dot_nt(a, b):
    return lax.dot_general(a, b, (((1,), (1,)), ((), ())), preferred_element_type=F32)


def _dot_tn(a, b):
    return lax.dot_general(a, b, (((0,), (0,)), ((), ())), preferred_element_type=F32)


def _split3(x):
    hi = x.astype(BF16)
    r = x - hi.astype(F32)
    mid = r.astype(BF16)
    lo = (r - mid.astype(F32)).astype(BF16)
    return hi, mid, lo


def _dot_exact_rhs01(x, m01):
    hi, mid, lo = _split3(x)
    return _dot(hi, m01) + _dot(mid, m01) + _dot(lo, m01)


def _dot_exact_lhs01(m01, x):
    hi, mid, lo = _split3(x)
    return _dot(m01, hi) + _dot(m01, mid) + _dot(m01, lo)


def _silu(x):
    return x * jax.nn.sigmoid(x)


def _pack_bf16_pairs(lo, hi):
    lo_bits = lax.bitcast_convert_type(lo.astype(BF16).astype(F32), jnp.uint32)
    hi_bits = lax.bitcast_convert_type(hi.astype(BF16).astype(F32), jnp.uint32)
    return hi_bits | (lo_bits >> 16)


INPROJ_CHUNKS = 4


def _inproj_kernel(x_hbm, lnw_ref, w_ref, o_ref, xs_ref, u_ref, sem):
    tm = u_ref.shape[0]
    rows = tm // INPROJ_CHUNKS
    i = pl.program_id(0)

    @pl.when(pl.program_id(1) == 0)
    def _():
        def copy(c):
            return pltpu.make_async_copy(x_hbm.at[pl.ds(i * tm + c * rows, rows)], xs_ref.at[c % 2], sem.at[c % 2])

        copy(0).start()
        copy(1).start()
        for c in range(INPROJ_CHUNKS):
            copy(c).wait()
            x = xs_ref[c % 2]
            ms = jnp.mean(x * x, axis=-1, keepdims=True)
            u_ref[c * rows:(c + 1) * rows, :] = (x * lax.rsqrt(ms + EPS) * lnw_ref[...]).astype(BF16)
            if c + 2 < INPROJ_CHUNKS:
                copy(c + 2).start()

    tn = o_ref.shape[1]
    tail = D_IN_PROJ % tn
    if 0 < tail <= LANES:
        last = pl.num_programs(1) - 1

        @pl.when(pl.program_id(1) < last)
        def _():
            o_ref[...] = _dot_nt(u_ref[...], w_ref[...].astype(BF16))

        @pl.when(pl.program_id(1) == last)
        def _():
            o_ref[:, 0:LANES] = _dot_nt(u_ref[...], w_ref[0:LANES, :].astype(BF16))
    else:
        o_ref[...] = _dot_nt(u_ref[...], w_ref[...].astype(BF16))


def _inproj(x2, ln_w, w_in_t):
    T = x2.shape[0]
    tm, tn = (2048 if T % 2048 == 0 else 1024), 512
    return pl.pallas_call(
        _inproj_kernel,
        grid=(T // tm, pl.cdiv(D_IN_PROJ, tn)),
        in_specs=[pl.BlockSpec(memory_space=pl.ANY),
                  pl.BlockSpec((1, D_MODEL), lambda i, j: (0, 0)),
                  pl.BlockSpec((tn, D_MODEL), lambda i, j: (j, 0))],
        out_specs=pl.BlockSpec((tm, tn), lambda i, j: (i, j)),
        out_shape=jax.ShapeDtypeStruct((T, D_IN_PROJ), F32),
        scratch_shapes=[pltpu.VMEM((2, tm // INPROJ_CHUNKS, D_MODEL), F32),
                        pltpu.VMEM((tm, D_MODEL), BF16),
                        pltpu.SemaphoreType.DMA((2,))],
        compiler_params=_params(("arbitrary", "arbitrary")),
        name="inproj",
    )(x2, ln_w.reshape(1, D_MODEL), w_in_t)


def _retention_tables():
    C = RET_CHUNK
    h = np.arange(RET_HEADS, dtype=np.float64)
    log_gamma = np.log1p(-np.exp2(-5.0 - h))
    idx = np.arange(C, dtype=np.float64)
    rel = idx[:, None] - idx[None, :]
    intra = np.where(rel >= 0, np.exp(log_gamma[:, None, None] * np.maximum(rel, 0.0)), 0.0)
    q_decay = np.exp(log_gamma[:, None] * (idx + 1.0))
    k_decay = np.exp(log_gamma[:, None] * (C - 1.0 - idx))
    chunk_decay = np.exp(log_gamma * C)
    qd = np.broadcast_to(q_decay[:, :, None], (RET_HEADS, C, RET_HEAD_DIM))
    kd = np.broadcast_to(k_decay[:, :, None], (RET_HEADS, C, RET_HEAD_DIM))
    return (jnp.asarray(intra, F32), jnp.asarray(qd, F32), jnp.asarray(kd, F32),
            [float(c) for c in chunk_decay])


def _retention_kernel(chunk_decay, pos_ref, invf_ref, q_ref, k_ref, v_ref, g_ref,
                      intra_ref, qd_ref, kd_ref, o_ref, state_ref):
    @pl.when(pl.program_id(0) == 0)
    def _():
        state_ref[...] = jnp.zeros_like(state_ref)

    half = RET_HEAD_DIM // 2
    ang = pos_ref[...] * invf_ref[...]
    cos = jnp.cos(ang)
    sin = jnp.sin(ang)

    def rope(t):
        t1, t2 = t[:, :half], t[:, half:]
        return jnp.concatenate([t1 * cos - t2 * sin, t2 * cos + t1 * sin], axis=-1)

    for h in range(RET_HEADS):
        sl = slice(h * RET_HEAD_DIM, (h + 1) * RET_HEAD_DIM)
        q = rope(q_ref[:, sl])
        k = rope(k_ref[:, sl]) * (RET_HEAD_DIM ** -0.5)
        v = v_ref[:, sl].astype(BF16)
        state = state_ref[h]
        scores = _dot_nt(q.astype(BF16), k.astype(BF16)) * intra_ref[h]
        inner = _dot(scores.astype(BF16), v)
        cross = _dot((q * qd_ref[h]).astype(BF16), state.astype(BF16))
        state_ref[h] = chunk_decay[h] * state + _dot_tn((k * kd_ref[h]).astype(BF16), v)
        o = inner + cross
        o = o * lax.rsqrt(jnp.mean(o * o, axis=-1, keepdims=True) + EPS)
        o_ref[:, sl] = (o * _silu(g_ref[:, sl])).astype(o_ref.dtype)


def _retention(proj, pos_col, inv_freq):
    T = proj.shape[0]
    C = RET_CHUNK
    intra, qd, kd, chunk_decay = _retention_tables()
    col = lambda j: pl.BlockSpec((C, RET_WIDTH), lambda c, j=j: (c, j))
    const3 = lambda shape: pl.BlockSpec(shape, lambda c: (0, 0, 0))
    return pl.pallas_call(
        functools.partial(_retention_kernel, chunk_decay),
        grid=(T // C,),
        in_specs=[pl.BlockSpec((C, 1), lambda c: (c, 0)),
                  pl.BlockSpec((1, RET_HEAD_DIM // 2), lambda c: (0, 0)),
                  col(0), col(1), col(2), col(3),
                  const3((RET_HEADS, C, C)),
                  const3((RET_HEADS, C, RET_HEAD_DIM)),
                  const3((RET_HEADS, C, RET_HEAD_DIM))],
        out_specs=pl.BlockSpec((C, RET_WIDTH), lambda c: (c, 0)),
        out_shape=jax.ShapeDtypeStruct((T, RET_WIDTH), BF16),
        scratch_shapes=[pltpu.VMEM((RET_HEADS, RET_HEAD_DIM, RET_HEAD_DIM), F32)],
        compiler_params=_params(("arbitrary",)),
        name="retention",
    )(pos_col, inv_freq, proj, proj, proj, proj, intra, qd, kd)


def _ssd_kernel(xs0_ref, xs1_ref, bc_ref, z_ref, dt_ref, convw_ref, convb_ref, dtb_ref, a_ref,
                dskip_ref, normw_ref, expand_ref, o_ref, ext_ref, state_ref):
    C = SSD_CHUNK
    HW = SSM_WIDTH // SSM_GROUPS
    CARRY = 8

    @pl.when(pl.program_id(0) == 0)
    def _():
        ext_ref[0:CARRY, :] = jnp.zeros((CARRY, XBC_WIDTH), F32)
        state_ref[...] = jnp.zeros_like(state_ref)

    ext_ref[CARRY:CARRY + C, 0:HW] = xs0_ref[...]
    ext_ref[CARRY:CARRY + C, HW:2 * HW] = xs1_ref[...]
    ext_ref[CARRY:CARRY + C, 2 * HW:3 * HW] = bc_ref[...]
    conv = convb_ref[...]
    for k in range(CONV_WIDTH):
        off = CARRY - (CONV_WIDTH - 1) + k
        conv = conv + convw_ref[k:k + 1, :] * ext_ref[off:off + C, :]
    ext_ref[0:CARRY, :] = ext_ref[C:C + CARRY, :]
    xbc = _silu(conv)
    xs = xbc[:, :SSM_WIDTH]

    lane = lax.broadcasted_iota(jnp.int32, (1, LANES), 1)
    dt_raw = jnp.where(lane < SSM_HEADS, dt_ref[...], 0.0) + dtb_ref[...]
    dt = jnp.maximum(dt_raw, 0.0) + jnp.log1p(jnp.exp(-jnp.abs(dt_raw)))
    dta = dt * a_ref[...]

    row = lax.broadcasted_iota(jnp.int32, (C, C), 0)
    colm = lax.broadcasted_iota(jnp.int32, (C, C), 1)
    tril = row >= colm
    a_cum = _dot_exact_lhs01(jnp.where(tril, 1.0, 0.0).astype(BF16), dta)
    a_cum_t = a_cum.T

    expand = expand_ref[...]
    a_exp = _dot_exact_rhs01(a_cum, expand)
    dt_exp = _dot_exact_rhs01(dt, expand)
    a_last = a_exp[C - 1:C, :]
    decay_in = jnp.exp(a_exp)
    decay_out = jnp.exp(a_last - a_exp)
    chunk_decay = jnp.exp(a_last)
    xdt = xs * dt_exp

    lane2 = lax.broadcasted_iota(jnp.int32, (1, LANES), 1)
    lo_head = lane2 < SSM_HEAD_DIM
    ys = []
    for g in range(SSM_GROUPS):
        gs = slice(g * HW, (g + 1) * HW)
        b_g = xbc[:, SSM_WIDTH + g * SSM_STATE:SSM_WIDTH + (g + 1) * SSM_STATE].astype(BF16)
        c0 = SSM_WIDTH + SSM_GROUPS * SSM_STATE
        c_g = xbc[:, c0 + g * SSM_STATE:c0 + (g + 1) * SSM_STATE].astype(BF16)
        cb = _dot_nt(c_g, b_g)
        state = state_ref[g]
        y_off = _dot(c_g, state.astype(BF16)) * decay_in[:, gs]
        xw = (xdt[:, gs] * decay_out[:, gs]).astype(BF16)
        state_ref[g] = chunk_decay[:, gs] * state + _dot_tn(b_g, xw)
        slabs = []
        for s in range(HW // LANES):
            xd = xdt[:, g * HW + s * LANES:g * HW + (s + 1) * LANES]
            acc = None
            for e in range(2):
                hh = g * (SSM_HEADS // SSM_GROUPS) + 2 * s + e
                seg = a_cum[:, hh:hh + 1] - a_cum_t[hh:hh + 1, :]
                m = cb * jnp.exp(jnp.where(tril, seg, -jnp.inf))
                xm = jnp.where(lo_head if e == 0 else jnp.logical_not(lo_head), xd, 0.0)
                part = _dot(m.astype(BF16), xm.astype(BF16))
                acc = part if acc is None else acc + part
            slabs.append(acc)
        ys.append(jnp.concatenate(slabs, axis=-1) + y_off)
    y = jnp.concatenate(ys, axis=-1) + dskip_ref[...] * xs
    y = y * _silu(z_ref[...])
    outs = []
    for g in range(SSM_GROUPS):
        yg = y[:, g * HW:(g + 1) * HW]
        outs.append(yg * lax.rsqrt(jnp.mean(yg * yg, axis=-1, keepdims=True) + EPS))
    o_ref[...] = (jnp.concatenate(outs, axis=-1) * normw_ref[...]).astype(o_ref.dtype)


def _ssd(proj, conv_w, conv_b, dt_bias, a_log, d_skip, ssm_norm_w):
    T = proj.shape[0]
    C = SSD_CHUNK
    HW = SSM_WIDTH // SSM_GROUPS
    xbc0 = (4 * RET_WIDTH + SSM_WIDTH) // HW
    dt0 = (D_IN_PROJ - SSM_HEADS) // LANES
    pad = lambda v: jnp.zeros((1, LANES), F32).at[0, :SSM_HEADS].set(v.astype(F32))
    a_neg = pad(-jnp.exp(a_log.astype(F32)))
    expand_np = np.zeros((LANES, SSM_WIDTH), np.float32)
    for hh in range(SSM_HEADS):
        expand_np[hh, hh * SSM_HEAD_DIM:(hh + 1) * SSM_HEAD_DIM] = 1.0
    expand = jnp.asarray(expand_np, BF16)
    dskip_exp = jnp.repeat(d_skip.astype(F32), SSM_HEAD_DIM).reshape(1, SSM_WIDTH)
    const = lambda shape: pl.BlockSpec(shape, lambda c: (0, 0))
    return pl.pallas_call(
        _ssd_kernel,
        grid=(T // C,),
        in_specs=[pl.BlockSpec((C, HW), lambda c: (c, xbc0)),
                  pl.BlockSpec((C, HW), lambda c: (c, xbc0 + 1)),
                  pl.BlockSpec((C, HW), lambda c: (c, xbc0 + 2)),
                  pl.BlockSpec((C, SSM_WIDTH), lambda c: (c, 4 * RET_WIDTH // SSM_WIDTH)),
                  pl.BlockSpec((C, LANES), lambda c: (c, dt0)),
                  const((CONV_WIDTH, XBC_WIDTH)), const((1, XBC_WIDTH)),
                  const((1, LANES)), const((1, LANES)),
                  const((1, SSM_WIDTH)), const((1, SSM_WIDTH)),
                  const((LANES, SSM_WIDTH))],
        out_specs=pl.BlockSpec((C, SSM_WIDTH), lambda c: (c, 0)),
        out_shape=jax.ShapeDtypeStruct((T, SSM_WIDTH), BF16),
        scratch_shapes=[pltpu.VMEM((C + 8, XBC_WIDTH), F32),
                        pltpu.VMEM((SSM_GROUPS, SSM_STATE, HW), F32)],
        compiler_params=_params(("arbitrary",)),
        name="ssd",
    )(proj, proj, proj, proj, proj, conv_w, conv_b.reshape(1, XBC_WIDTH), pad(dt_bias), a_neg,
      dskip_exp, ssm_norm_w.reshape(1, SSM_WIDTH), expand)


def _outproj_kernel(ret_ref, ssm_ref, w_ref, x_ref, lnw_ref, wr_ref, br_ref, hs_ref, up_ref, lg_ref):
    tm = x_ref.shape[0]
    h = (x_ref[...] + _dot(ret_ref[...], w_ref[0:RET_WIDTH, :])
         + _dot(ssm_ref[...], w_ref[RET_WIDTH:D_MODEL, :]))
    for s in range(ROW_SLABS):
        hs_ref[pl.ds(s, tm, stride=ROW_SLABS), :] = h[:, s * LANES:(s + 1) * LANES]
    u = h * lax.rsqrt(jnp.mean(h * h, axis=-1, keepdims=True) + EPS) * lnw_ref[...]
    packed = _pack_bf16_pairs(u[:, :D_MODEL // 2], u[:, D_MODEL // 2:])
    for s in range(PACK_SLABS):
        up_ref[pl.ds(s, tm, stride=PACK_SLABS), :] = packed[:, s * LANES:(s + 1) * LANES]
    E = N_EXPERTS
    uh, um, ul = _split3(u)
    ph = _dot(uh, wr_ref[...])
    pm = _dot(um, wr_ref[...])
    pw = _dot(ul, wr_ref[...])
    lg = (ph[:, 0:E] + (ph[:, E:2 * E] + pm[:, 0:E])
          + (ph[:, 2 * E:3 * E] + pm[:, E:2 * E] + pw[:, 0:E]))
    lg_ref[...] = lg + br_ref[...]


def _outproj(ret, ssm, w_out_bf16, x2, ln_w, w_router, b_router):
    T = x2.shape[0]
    tm = 512
    return pl.pallas_call(
        _outproj_kernel,
        grid=(T // tm,),
        in_specs=[pl.BlockSpec((tm, RET_WIDTH), lambda i: (i, 0)),
                  pl.BlockSpec((tm, SSM_WIDTH), lambda i: (i, 0)),
                  pl.BlockSpec((D_MODEL, D_MODEL), lambda i: (0, 0)),
                  pl.BlockSpec((tm, D_MODEL), lambda i: (i, 0)),
                  pl.BlockSpec((1, D_MODEL), lambda i: (0, 0)),
                  pl.BlockSpec((D_MODEL, 3 * N_EXPERTS), lambda i: (0, 0)),
                  pl.BlockSpec((1, N_EXPERTS), lambda i: (0, 0))],
        out_specs=[pl.BlockSpec((tm * ROW_SLABS, LANES), lambda i: (i, 0)),
                   pl.BlockSpec((tm * PACK_SLABS, LANES), lambda i: (i, 0)),
                   pl.BlockSpec((tm, N_EXPERTS), lambda i: (i, 0))],
        out_shape=[jax.ShapeDtypeStruct((T * ROW_SLABS, LANES), F32),
                   jax.ShapeDtypeStruct((T * PACK_SLABS, LANES), jnp.uint32),
                   jax.ShapeDtypeStruct((T, N_EXPERTS), F32)],
        compiler_params=_params(("parallel",)),
        name="outproj",
    )(ret, ssm, w_out_bf16, x2, ln_w.reshape(1, D_MODEL), jnp.concatenate(_split3(w_router), axis=1),
      b_router.reshape(1, N_EXPERTS))


def _expert_kernel(n_ff, n_items, aliased, *refs):
    (item_e, item_row, item_nsub, item_nzero, item_live, item_src, sorted_tok,
     u_hbm, wga_ref, wgb_ref, bga_ref, bgb_ref, wda_ref, wdb_ref, bd_ref) = refs[:15]
    (y_hbm, stage, xbuf, acc, ybuf, wa_s, wb_s, wd_s, sem_in, sem_out) = refs[16:] if aliased else refs[15:]
    del item_e, item_live
    i = pl.program_id(0)
    f = pl.program_id(1)
    nsub = item_nsub[i]
    nzero = item_nzero[i]
    row0 = item_row[i]
    SUB = ROW_BLK
    YS = SUB * ROW_SLABS
    PS = PACK_SLABS
    XS = SUB * PS

    def y_copy(slot, j):
        dst = y_hbm.at[pl.ds(pl.multiple_of((row0 + j * SUB) * ROW_SLABS, YS), YS)]
        return pltpu.make_async_copy(ybuf.at[pl.ds(pl.multiple_of(slot * YS, YS), YS)], dst, sem_out.at[slot])

    def gather_copy(tok, r):
        src = u_hbm.at[pl.ds(pl.multiple_of(tok * PS, PS), PS)]
        return pltpu.make_async_copy(src, stage.at[pl.ds(pl.multiple_of(r * PS, PS), PS)], sem_in)

    n_assign = sorted_tok.shape[0]
    n_sub_max = ITEM_ROWS // SUB
    covers = [jnp.logical_and(nsub > lo, nsub <= m) for lo, m in zip((0,) + ITEM_VARIANTS[:-1], ITEM_VARIANTS)]
    CH = ITEM_ROWS // n_ff
    nxt = jnp.minimum(i + 1, n_items - 1)

    def issue_rows(item, lo, count, unrolled):
        src0 = item_src[item]

        def one(r, p):
            gather_copy(sorted_tok[jnp.minimum(src0 + r, n_assign - 1)], r).start(priority=p)

        if unrolled:
            for k in range(count):
                one(lo + k, k % 2)
        else:
            def body(q, c):
                for p in range(2):
                    one(lo + 2 * q + p, p)
                return c
            lax.fori_loop(0, count // 2, body, 0)

    def wait_rows():
        for _ in range(n_sub_max):
            pltpu.make_async_copy(u_hbm.at[pl.ds(0, XS)], stage.at[pl.ds(0, XS)], sem_in).wait()

    @pl.when(jnp.logical_and(i == 0, f == 0))
    def _():
        issue_rows(0, 0, ITEM_ROWS, False)

    lane = lax.broadcasted_iota(jnp.int32, (1, 2 * FF_TILE), 1)
    even = (lane % 2) == 0
    bga = bga_ref[...]
    bgb = bgb_ref[...]
    W2 = 2 * FF_TILE

    @pl.when(f == 0)
    def _():
        wait_rows()
        half = D_MODEL // 2

        def unpack(j, c):
            rows = pl.ds(pl.multiple_of(j * SUB, SUB), SUB)
            for s in range(PS):
                p = stage[pl.ds(j * XS + s, SUB, stride=PS), :]
                lo = lax.bitcast_convert_type(p << 16, F32)
                hi = lax.bitcast_convert_type(p & jnp.uint32(0xFFFF0000), F32)
                xbuf[rows, s * LANES:(s + 1) * LANES] = lo.astype(BF16)
                xbuf[rows, half + s * LANES:half + (s + 1) * LANES] = hi.astype(BF16)
            acc[rows, :] = jnp.zeros((SUB, D_MODEL), F32)
            return c
        lax.fori_loop(0, sum(jnp.where(c, m, 0) for m, c in zip(ITEM_VARIANTS, covers)), unpack, 0)

    @pl.when(nsub == 0)
    def _():
        issue_rows(nxt, f * CH, CH, False)

    def block(rows):
        xs = xbuf[0:rows, :]
        wa_s[...] = wga_ref[...].astype(BF16)
        ga = _dot(xs, wa_s[...]) + bga
        issue_rows(nxt, f * CH, CH, True)
        wb_s[...] = wgb_ref[...].astype(BF16)
        gb = _dot(xs, wb_s[...]) + bgb
        wd_s[...] = pltpu.bitcast(_pack_bf16_pairs(wda_ref[...], wdb_ref[...]), BF16)
        gate = jnp.where(even, ga, pltpu.roll(gb, 1, 1))
        up = jnp.where(even, pltpu.roll(ga, W2 - 1, 1), gb)
        gate = jnp.minimum(gate, SWIGLU_LIMIT)
        up = jnp.clip(up, -SWIGLU_LIMIT, SWIGLU_LIMIT)
        act = ((up + 1.0) * (gate * jax.nn.sigmoid(gate * SWIGLU_ALPHA))).astype(BF16)
        for n in range(0, D_MODEL, DOWN_TILE):
            acc[0:rows, n:n + DOWN_TILE] += _dot(act, wd_s[:, n:n + DOWN_TILE])

    for m, cover in zip(ITEM_VARIANTS, covers):
        @pl.when(cover)
        def _(m=m):
            block(m * SUB)

    @pl.when(f == n_ff - 1)
    def _():
        @pl.when(i == n_items - 1)
        def _():
            wait_rows()

        def stage_out(q, c):
            slot = q % 2

            @pl.when(q >= 2)
            def _():
                y_copy(slot, 0).wait()
            val = acc[pl.ds(pl.multiple_of(q * SUB, SUB), SUB), :] + bd_ref[...]
            for s in range(ROW_SLABS):
                ybuf[pl.ds(slot * YS + s, SUB, stride=ROW_SLABS), :] = val[:, s * LANES:(s + 1) * LANES]
            y_copy(slot, q).start()
            return c
        lax.fori_loop(0, nsub, stage_out, 0)

        @pl.when(nsub > 0)
        def _():
            y_copy(0, 0).wait()

        @pl.when(nsub > 1)
        def _():
            y_copy(1, 0).wait()

    @pl.when(jnp.logical_and(f == n_ff - 1, nzero > 0))
    def _():
        ybuf[0:YS, :] = jnp.zeros((YS, LANES), F32)

        def start(j, c):
            y_copy(0, j).start()
            return c
        lax.fori_loop(0, nzero, start, 0)

        def wait(j, c):
            y_copy(0, 0).wait()
            return c
        lax.fori_loop(0, nzero, wait, 0)


def _experts_call(u_packed, sorted_tok, n_rows, items, w_gate_up, b_gate_up, w_down, b_down, y_in=None):
    n_ff = (D_FF // 2) // FF_TILE
    assert ITEM_ROWS % (2 * n_ff) == 0 and ITEM_VARIANTS[-1] * ROW_BLK == ITEM_ROWS
    W2 = 2 * FF_TILE
    item_e, item_row, item_nsub, item_nzero, item_live, item_src = items
    n_items = item_e.shape[0]
    aliased = y_in is not None

    def ff(i, f, live):
        return jnp.where(live[i] > 0, f, n_ff - 1)

    grid_spec = pltpu.PrefetchScalarGridSpec(
        num_scalar_prefetch=7,
        grid=(n_items, n_ff),
        in_specs=[
            pl.BlockSpec(memory_space=pl.ANY),
            pl.BlockSpec((None, D_MODEL, W2), lambda i, f, e, r, n, z, lv, sr, st: (e[i], 0, ff(i, f, lv))),
            pl.BlockSpec((None, D_MODEL, W2), lambda i, f, e, r, n, z, lv, sr, st: (e[i], 0, n_ff + ff(i, f, lv))),
            pl.BlockSpec((None, 1, W2), lambda i, f, e, r, n, z, lv, sr, st: (e[i], 0, ff(i, f, lv))),
            pl.BlockSpec((None, 1, W2), lambda i, f, e, r, n, z, lv, sr, st: (e[i], 0, n_ff + ff(i, f, lv))),
            pl.BlockSpec((None, FF_TILE, D_MODEL), lambda i, f, e, r, n, z, lv, sr, st: (e[i], ff(i, f, lv), 0)),
            pl.BlockSpec((None, FF_TILE, D_MODEL), lambda i, f, e, r, n, z, lv, sr, st: (e[i], n_ff + ff(i, f, lv), 0)),
            pl.BlockSpec((None, 1, D_MODEL), lambda i, f, e, r, n, z, lv, sr, st: (e[i], 0, 0)),
        ] + ([pl.BlockSpec(memory_space=pl.ANY)] if aliased else []),
        out_specs=pl.BlockSpec(memory_space=pl.ANY),
        scratch_shapes=[pltpu.VMEM((ITEM_ROWS * PACK_SLABS, LANES), jnp.uint32),
                        pltpu.VMEM((ITEM_ROWS, D_MODEL), BF16),
                        pltpu.VMEM((ITEM_ROWS, D_MODEL), F32),
                        pltpu.VMEM((2 * ROW_BLK * ROW_SLABS, LANES), F32),
                        pltpu.VMEM((D_MODEL, W2), BF16),
                        pltpu.VMEM((D_MODEL, W2), BF16),
                        pltpu.VMEM((W2, D_MODEL), BF16),
                        pltpu.SemaphoreType.DMA(()),
                        pltpu.SemaphoreType.DMA((2,))],
    )
    args = (item_e, item_row, item_nsub, item_nzero, item_live, item_src, sorted_tok, u_packed,
            w_gate_up, w_gate_up, b_gate_up.reshape(N_EXPERTS, 1, 2 * D_FF),
            b_gate_up.reshape(N_EXPERTS, 1, 2 * D_FF), w_down, w_down, b_down.reshape(N_EXPERTS, 1, D_MODEL))
    return pl.pallas_call(
        functools.partial(_expert_kernel, n_ff, n_items, aliased),
        grid_spec=grid_spec,
        out_shape=jax.ShapeDtypeStruct((n_rows * ROW_SLABS, LANES), F32),
        input_output_aliases={len(args): 0} if aliased else {},
        compiler_params=_params(("arbitrary", "arbitrary")),
        name="experts_overflow" if aliased else "experts",
    )(*args, *((y_in,) if aliased else ()))


def _experts(u_packed, sorted_tok, n_rows, items, n_used, w_gate_up, b_gate_up, w_down, b_down):
    n_main = N_EXPERTS + 1
    weights = (w_gate_up, b_gate_up, w_down, b_down)
    y = _experts_call(u_packed, sorted_tok, n_rows, tuple(a[:n_main] for a in items), *weights)
    rest = tuple(a[n_main:] for a in items)
    return lax.cond(n_used > n_main,
                    lambda y_: _experts_call(u_packed, sorted_tok, n_rows, rest, *weights, y_in=y_),
                    lambda y_: y_, y)


def _combine_kernel(tm, n_steps, dest_ref, w_ref, y_hbm, h_ref, lnw_ref, o_ref, buf, osum, sem):
    i = pl.program_id(0)

    def copy(step, slot, t, k):
        d = dest_ref[(step * tm + t) * TOP_K + k]
        return pltpu.make_async_copy(y_hbm.at[d], buf.at[slot, k * tm + t], sem.at[slot])

    def issue(step, slot):
        def body(t, c):
            for k in range(TOP_K):
                copy(step, slot, t, k).start(priority=k % 2)
            return c
        lax.fori_loop(0, tm, body, 0, unroll=4)

    @pl.when(i == 0)
    def _():
        issue(0, 0)

    @pl.when(i + 1 < n_steps)
    def _():
        issue(i + 1, (i + 1) % 2)

    slot = i % 2

    pltpu.make_async_copy(y_hbm.at[pl.ds(0, TOP_K * tm)], buf.at[slot], sem.at[slot]).wait()

    def token(t, c):
        a = h_ref[t]
        for k in range(TOP_K):
            a = a + w_ref[(i * tm + t) * TOP_K + k] * buf[slot, k * tm + t]
        osum[pl.ds(pl.multiple_of(t * ROW_SLABS, ROW_SLABS), ROW_SLABS), :] = a
        return c
    lax.fori_loop(0, tm, token, 0, unroll=4)

    h = osum[...].reshape(tm, ROW_SLABS, LANES)
    ms = jnp.mean(jnp.mean(h * h, axis=2, keepdims=True), axis=1, keepdims=True)
    osum[...] = (h * lax.rsqrt(ms + EPS) * lnw_ref[...]).reshape(tm * ROW_SLABS, LANES)
    for s in range(ROW_SLABS):
        o_ref[:, s * LANES:(s + 1) * LANES] = osum[pl.ds(s, tm, stride=ROW_SLABS), :]


def _combine(dest, wflat, y3, h3, ln_w):
    T = h3.shape[0]
    tm = 128
    n_steps = T // tm
    grid_spec = pltpu.PrefetchScalarGridSpec(
        num_scalar_prefetch=2,
        grid=(n_steps,),
        in_specs=[pl.BlockSpec(memory_space=pl.ANY),
                  pl.BlockSpec((tm, ROW_SLABS, LANES), lambda i, d, w: (i, 0, 0)),
                  pl.BlockSpec((1, ROW_SLABS, LANES), lambda i, d, w: (0, 0, 0))],
        out_specs=pl.BlockSpec((tm, D_MODEL), lambda i, d, w: (i, 0)),
        scratch_shapes=[pltpu.VMEM((2, TOP_K * tm, ROW_SLABS, LANES), F32),
                        pltpu.VMEM((tm * ROW_SLABS, LANES), F32),
                        pltpu.SemaphoreType.DMA((2,))],
    )
    return pl.pallas_call(
        functools.partial(_combine_kernel, tm, n_steps),
        grid_spec=grid_spec,
        out_shape=jax.ShapeDtypeStruct((T, D_MODEL), F32),
        compiler_params=_params(("arbitrary",)),
        name="combine",
    )(dest, wflat, y3, h3, ln_w.reshape(1, ROW_SLABS, LANES))


def _route(logits, n_rows, n_items):
    T = logits.shape[0]
    top_logits, top_idx = lax.top_k(logits, TOP_K)
    top_w = jax.nn.softmax(top_logits, axis=-1)
    e_flat = top_idx.reshape(-1).astype(jnp.int32)
    onehot = (e_flat[:, None] == jnp.arange(N_EXPERTS, dtype=jnp.int32)[None, :]).astype(jnp.int32)
    csum = jnp.cumsum(onehot, axis=0)
    counts = csum[-1]
    padded = (counts + ROW_BLK - 1) // ROW_BLK * ROW_BLK
    pend = jnp.cumsum(padded)
    pstart = pend - padded
    dest = jnp.sum(onehot * (pstart[None, :] + csum - onehot), axis=1).astype(jnp.int32)

    per_e = (padded + ITEM_ROWS - 1) // ITEM_ROWS
    iend = jnp.cumsum(per_e)
    istart = iend - per_e
    ii = jnp.arange(n_items, dtype=jnp.int32)
    total = iend[-1]
    live = (ii < total).astype(jnp.int32)
    ic = jnp.minimum(ii, total - 1)
    ie = jnp.minimum(jnp.sum(ic[:, None] >= iend[None, :], axis=1), N_EXPERTS - 1).astype(jnp.int32)
    within = ic - istart[ie]
    irow = (pstart[ie] + within * ITEM_ROWS).astype(jnp.int32)
    insub = jnp.minimum((padded[ie] - within * ITEM_ROWS) // ROW_BLK, ITEM_ROWS // ROW_BLK).astype(jnp.int32)
    insub = insub * live
    tail_rows = n_rows - pend[-1]
    tail = jnp.logical_and(ii == total, tail_rows > 0)
    inzero = jnp.where(tail, tail_rows // ROW_BLK, 0).astype(jnp.int32)
    irow = jnp.where(tail, pend[-1], irow).astype(jnp.int32)
    n_used = total + (tail_rows > 0).astype(jnp.int32)
    n_assign = T * TOP_K
    assert N_EXPERTS * n_assign < 2 ** 31
    order = jnp.sort(e_flat * n_assign + jnp.arange(n_assign, dtype=jnp.int32)) % n_assign
    sorted_tok = (order // TOP_K).astype(jnp.int32)
    cstart = jnp.cumsum(counts) - counts
    isrc = ((cstart[ie] + within * ITEM_ROWS) * live).astype(jnp.int32)
    return dest, top_w.reshape(-1).astype(F32), (ie, irow, insub, inzero, live, isrc), n_used, sorted_tok


def kernel(x, positions, ln_mix_w, w_in, conv_w, conv_b, dt_bias, a_log, d_skip, ssm_norm_w, w_out,
           ln_ffn_w, w_router, b_router, w_gate_up, b_gate_up, w_down, b_down, ln_final_w):
    B, L, _ = x.shape
    T = B * L
    assert B == 1 and T % 1024 == 0
    x2 = x.reshape(T, D_MODEL)
    half = RET_HEAD_DIM // 2
    inv_freq = (ROPE_BASE ** (-jnp.arange(half, dtype=F32) / half)).reshape(1, half)
    pos_col = positions.reshape(T, 1).astype(F32)

    proj = _inproj(x2, ln_mix_w[0], jnp.swapaxes(w_in[0], 0, 1))
    ret = _retention(proj, pos_col, inv_freq)
    ssm = _ssd(proj, conv_w[0], conv_b[0], dt_bias[0], a_log[0], d_skip[0], ssm_norm_w[0])
    h_slabs, u_packed, logits = _outproj(ret, ssm, w_out[0].astype(BF16), x2, ln_ffn_w[0], w_router[0],
                                         b_router[0])

    n_rows = -(-(T * TOP_K + N_EXPERTS * (ROW_BLK - 1)) // ROW_BLK) * ROW_BLK
    n_items = N_EXPERTS + 1 + n_rows // ITEM_ROWS
    dest, wflat, items, n_used, sorted_tok = _route(logits, n_rows, n_items)

    y_rows = _experts(u_packed, sorted_tok, n_rows, items, n_used, w_gate_up[0], b_gate_up[0], w_down[0],
                      b_down[0])
    out = _combine(dest, wflat, y_rows.reshape(n_rows, ROW_SLABS, LANES),
                   h_slabs.reshape(T, ROW_SLABS, LANES), ln_final_w)
    return out.reshape(B, L, D_MODEL)
```

```python
import functools

import numpy as np
import jax
import jax.numpy as jnp
from jax import lax
from jax.experimental import pallas as pl
from jax.experimental.pallas import tpu as pltpu

F32 = jnp.float32
BF16 = jnp.bfloat16

D_MODEL = 2048
RET_HEADS = 4
RET_HEAD_DIM = 256
RET_WIDTH = RET_HEADS * RET_HEAD_DIM
SSM_WIDTH = D_MODEL - RET_WIDTH
SSM_HEAD_DIM = 64
SSM_HEADS = SSM_WIDTH // SSM_HEAD_DIM
SSM_GROUPS = 2
SSM_STATE = 128
CONV_WIDTH = 4
XBC_WIDTH = SSM_WIDTH + 2 * SSM_GROUPS * SSM_STATE
D_IN_PROJ = 4 * RET_WIDTH + SSM_WIDTH + XBC_WIDTH + SSM_HEADS
ROPE_BASE = 10000.0
N_EXPERTS = 32
TOP_K = 4
D_FF = D_MODEL
SWIGLU_LIMIT = 7.0
SWIGLU_ALPHA = 1.702
EPS = 1e-6

LANES = 128
VMEM_LIMIT = 56 * 1024 * 1024

RET_CHUNK = 256
SSD_CHUNK = 128
ROW_BLK = 128
ITEM_ROWS = 1536
ITEM_VARIANTS = (8, 9, 10, 12)
FF_TILE = 128
DOWN_TILE = 512
WEIGHT_SLOTS = 3
ROW_SLABS = D_MODEL // LANES
PACK_SLABS = ROW_SLABS // 2


def _params(sem, **kw):
    return pltpu.CompilerParams(dimension_semantics=sem, vmem_limit_bytes=VMEM_LIMIT, **kw)


def _dot(a, b):
    return jnp.dot(a, b, preferred_element_type=F32)


def _dot_nt(a, b):
    return lax.dot_general(a, b, (((1,), (1,)), ((), ())), preferred_element_type=F32)


def _dot_tn(a, b):
    return lax.dot_general(a, b, (((0,), (0,)), ((), ())), preferred_element_type=F32)


def _split3(x):
    hi = x.astype(BF16)
    r = x - hi.astype(F32)
    mid = r.astype(BF16)
    lo = (r - mid.astype(F32)).astype(BF16)
    return hi, mid, lo


def _dot_exact_rhs01(x, m01):
    hi, mid, lo = _split3(x)
    return _dot(hi, m01) + _dot(mid, m01) + _dot(lo, m01)


def _dot_exact_lhs01(m01, x):
    hi, mid, lo = _split3(x)
    return _dot(m01, hi) + _dot(m01, mid) + _dot(m01, lo)


def _silu(x):
    return x * jax.nn.sigmoid(x)


def _pack_bf16_pairs(lo, hi):
    lo_bits = lax.bitcast_convert_type(lo.astype(BF16).astype(F32), jnp.uint32)
    hi_bits = lax.bitcast_convert_type(hi.astype(BF16).astype(F32), jnp.uint32)
    return hi_bits | (lo_bits >> 16)


INPROJ_CHUNKS = 4


def _inproj_kernel(x_hbm, lnw_ref, w_ref, o_ref, xs_ref, u_ref, sem):
    tm = u_ref.shape[0]
    rows = tm // INPROJ_CHUNKS
    i = pl.program_id(0)

    @pl.when(pl.program_id(1) == 0)
    def _():
        def copy(c):
            return pltpu.make_async_copy(x_hbm.at[pl.ds(i * tm + c * rows, rows)], xs_ref.at[c % 2], sem.at[c % 2])

        copy(0).start()
        copy(1).start()
        for c in range(INPROJ_CHUNKS):
            copy(c).wait()
            x = xs_ref[c % 2]
            ms = jnp.mean(x * x, axis=-1, keepdims=True)
            u_ref[c * rows:(c + 1) * rows, :] = (x * lax.rsqrt(ms + EPS) * lnw_ref[...]).astype(BF16)
            if c + 2 < INPROJ_CHUNKS:
                copy(c + 2).start()

    tn = o_ref.shape[1]
    tail = D_IN_PROJ % tn
    if 0 < tail <= LANES:
        last = pl.num_programs(1) - 1

        @pl.when(pl.program_id(1) < last)
        def _():
            o_ref[...] = _dot_nt(u_ref[...], w_ref[...].astype(BF16))

        @pl.when(pl.program_id(1) == last)
        def _():
            o_ref[:, 0:LANES] = _dot_nt(u_ref[...], w_ref[0:LANES, :].astype(BF16))
    else:
        o_ref[...] = _dot_nt(u_ref[...], w_ref[...].astype(BF16))


def _inproj(x2, ln_w, w_in_t):
    T = x2.shape[0]
    tm, tn = (2048 if T % 2048 == 0 else 1024), 512
    return pl.pallas_call(
        _inproj_kernel,
        grid=(T // tm, pl.cdiv(D_IN_PROJ, tn)),
        in_specs=[pl.BlockSpec(memory_space=pl.ANY),
                  pl.BlockSpec((1, D_MODEL), lambda i, j: (0, 0)),
                  pl.BlockSpec((tn, D_MODEL), lambda i, j: (j, 0))],
        out_specs=pl.BlockSpec((tm, tn), lambda i, j: (i, j)),
        out_shape=jax.ShapeDtypeStruct((T, D_IN_PROJ), F32),
        scratch_shapes=[pltpu.VMEM((2, tm // INPROJ_CHUNKS, D_MODEL), F32),
                        pltpu.VMEM((tm, D_MODEL), BF16),
                        pltpu.SemaphoreType.DMA((2,))],
        compiler_params=_params(("arbitrary", "arbitrary")),
        name="inproj",
    )(x2, ln_w.reshape(1, D_MODEL), w_in_t)


def _retention_tables():
    C = RET_CHUNK
    h = np.arange(RET_HEADS, dtype=np.float64)
    log_gamma = np.log1p(-np.exp2(-5.0 - h))
    idx = np.arange(C, dtype=np.float64)
    rel = idx[:, None] - idx[None, :]
    intra = np.where(rel >= 0, np.exp(log_gamma[:, None, None] * np.maximum(rel, 0.0)), 0.0)
    q_decay = np.exp(log_gamma[:, None] * (idx + 1.0))
    k_decay = np.exp(log_gamma[:, None] * (C - 1.0 - idx))
    chunk_decay = np.exp(log_gamma * C)
    qd = np.broadcast_to(q_decay[:, :, None], (RET_HEADS, C, RET_HEAD_DIM))
    kd = np.broadcast_to(k_decay[:, :, None], (RET_HEADS, C, RET_HEAD_DIM))
    return (jnp.asarray(intra, F32), jnp.asarray(qd, F32), jnp.asarray(kd, F32),
            [float(c) for c in chunk_decay])


def _retention_kernel(chunk_decay, pos_ref, invf_ref, q_ref, k_ref, v_ref, g_ref,
                      intra_ref, qd_ref, kd_ref, o_ref, state_ref):
    @pl.when(pl.program_id(0) == 0)
    def _():
        state_ref[...] = jnp.zeros_like(state_ref)

    half = RET_HEAD_DIM // 2
    ang = pos_ref[...] * invf_ref[...]
    cos = jnp.cos(ang)
    sin = jnp.sin(ang)

    def rope(t):
        t1, t2 = t[:, :half], t[:, half:]
        return jnp.concatenate([t1 * cos - t2 * sin, t2 * cos + t1 * sin], axis=-1)

    for h in range(RET_HEADS):
        sl = slice(h * RET_HEAD_DIM, (h + 1) * RET_HEAD_DIM)
        q = rope(q_ref[:, sl])
        k = rope(k_ref[:, sl]) * (RET_HEAD_DIM ** -0.5)
        v = v_ref[:, sl].astype(BF16)
        state = state_ref[h]
        scores = _dot_nt(q.astype(BF16), k.astype(BF16)) * intra_ref[h]
        inner = _dot(scores.astype(BF16), v)
        cross = _dot((q * qd_ref[h]).astype(BF16), state.astype(BF16))
        state_ref[h] = chunk_decay[h] * state + _dot_tn((k * kd_ref[h]).astype(BF16), v)
        o = inner + cross
        o = o * lax.rsqrt(jnp.mean(o * o, axis=-1, keepdims=True) + EPS)
        o_ref[:, sl] = (o * _silu(g_ref[:, sl])).astype(o_ref.dtype)


def _retention(proj, pos_col, inv_freq):
    T = proj.shape[0]
    C = RET_CHUNK
    intra, qd, kd, chunk_decay = _retention_tables()
    col = lambda j: pl.BlockSpec((C, RET_WIDTH), lambda c, j=j: (c, j))
    const3 = lambda shape: pl.BlockSpec(shape, lambda c: (0, 0, 0))
    return pl.pallas_call(
        functools.partial(_retention_kernel, chunk_decay),
        grid=(T // C,),
        in_specs=[pl.BlockSpec((C, 1), lambda c: (c, 0)),
                  pl.BlockSpec((1, RET_HEAD_DIM // 2), lambda c: (0, 0)),
                  col(0), col(1), col(2), col(3),
                  const3((RET_HEADS, C, C)),
                  const3((RET_HEADS, C, RET_HEAD_DIM)),
                  const3((RET_HEADS, C, RET_HEAD_DIM))],
        out_specs=pl.BlockSpec((C, RET_WIDTH), lambda c: (c, 0)),
        out_shape=jax.ShapeDtypeStruct((T, RET_WIDTH), BF16),
        scratch_shapes=[pltpu.VMEM((RET_HEADS, RET_HEAD_DIM, RET_HEAD_DIM), F32)],
        compiler_params=_params(("arbitrary",)),
        name="retention",
    )(pos_col, inv_freq, proj, proj, proj, proj, intra, qd, kd)


def _ssd_kernel(xs0_ref, xs1_ref, bc_ref, z_ref, dt_ref, convw_ref, convb_ref, dtb_ref, a_ref,
                dskip_ref, normw_ref, expand_ref, o_ref, ext_ref, state_ref):
    C = SSD_CHUNK
    HW = SSM_WIDTH // SSM_GROUPS
    CARRY = 8

    @pl.when(pl.program_id(0) == 0)
    def _():
        ext_ref[0:CARRY, :] = jnp.zeros((CARRY, XBC_WIDTH), F32)
        state_ref[...] = jnp.zeros_like(state_ref)

    ext_ref[CARRY:CARRY + C, 0:HW] = xs0_ref[...]
    ext_ref[CARRY:CARRY + C, HW:2 * HW] = xs1_ref[...]
    ext_ref[CARRY:CARRY + C, 2 * HW:3 * HW] = bc_ref[...]
    conv = convb_ref[...]
    for k in range(CONV_WIDTH):
        off = CARRY - (CONV_WIDTH - 1) + k
        conv = conv + convw_ref[k:k + 1, :] * ext_ref[off:off + C, :]
    ext_ref[0:CARRY, :] = ext_ref[C:C + CARRY, :]
    xbc = _silu(conv)
    xs = xbc[:, :SSM_WIDTH]

    lane = lax.broadcasted_iota(jnp.int32, (1, LANES), 1)
    dt_raw = jnp.where(lane < SSM_HEADS, dt_ref[...], 0.0) + dtb_ref[...]
    dt = jnp.maximum(dt_raw, 0.0) + jnp.log1p(jnp.exp(-jnp.abs(dt_raw)))
    dta = dt * a_ref[...]

    row = lax.broadcasted_iota(jnp.int32, (C, C), 0)
    colm = lax.broadcasted_iota(jnp.int32, (C, C), 1)
    tril = row >= colm
    a_cum = _dot_exact_lhs01(jnp.where(tril, 1.0, 0.0).astype(BF16), dta)
    a_cum_t = a_cum.T

    expand = expand_ref[...]
    a_exp = _dot_exact_rhs01(a_cum, expand)
    dt_exp = _dot_exact_rhs01(dt, expand)
    a_last = a_exp[C - 1:C, :]
    decay_in = jnp.exp(a_exp)
    decay_out = jnp.exp(a_last - a_exp)
    chunk_decay = jnp.exp(a_last)
    xdt = xs * dt_exp

    lane2 = lax.broadcasted_iota(jnp.int32, (1, LANES), 1)
    lo_head = lane2 < SSM_HEAD_DIM
    ys = []
    for g in range(SSM_GROUPS):
        gs = slice(g * HW, (g + 1) * HW)
        b_g = xbc[:, SSM_WIDTH + g * SSM_STATE:SSM_WIDTH + (g + 1) * SSM_STATE].astype(BF16)
        c0 = SSM_WIDTH + SSM_GROUPS * SSM_STATE
        c_g = xbc[:, c0 + g * SSM_STATE:c0 + (g + 1) * SSM_STATE].astype(BF16)
        cb = _dot_nt(c_g, b_g)
        state = state_ref[g]
        y_off = _dot(c_g, state.astype(BF16)) * decay_in[:, gs]
        xw = (xdt[:, gs] * decay_out[:, gs]).astype(BF16)
        state_ref[g] = chunk_decay[:, gs] * state + _dot_tn(b_g, xw)
        slabs = []
        for s in range(HW // LANES):
            xd = xdt[:, g * HW + s * LANES:g * HW + (s + 1) * LANES]
            acc = None
            for e in range(2):
                hh = g * (SSM_HEADS // SSM_GROUPS) + 2 * s + e
                seg = a_cum[:, hh:hh + 1] - a_cum_t[hh:hh + 1, :]
                m = cb * jnp.exp(jnp.where(tril, seg, -jnp.inf))
                xm = jnp.where(lo_head if e == 0 else jnp.logical_not(lo_head), xd, 0.0)
                part = _dot(m.astype(BF16), xm.astype(BF16))
                acc = part if acc is None else acc + part
            slabs.append(acc)
        ys.append(jnp.concatenate(slabs, axis=-1) + y_off)
    y = jnp.concatenate(ys, axis=-1) + dskip_ref[...] * xs
    y = y * _silu(z_ref[...])
    outs = []
    for g in range(SSM_GROUPS):
        yg = y[:, g * HW:(g + 1) * HW]
        outs.append(yg * lax.rsqrt(jnp.mean(yg * yg, axis=-1, keepdims=True) + EPS))
    o_ref[...] = (jnp.concatenate(outs, axis=-1) * normw_ref[...]).astype(o_ref.dtype)


def _ssd(proj, conv_w, conv_b, dt_bias, a_log, d_skip, ssm_norm_w):
    T = proj.shape[0]
    C = SSD_CHUNK
    HW = SSM_WIDTH // SSM_GROUPS
    xbc0 = (4 * RET_WIDTH + SSM_WIDTH) // HW
    dt0 = (D_IN_PROJ - SSM_HEADS) // LANES
    pad = lambda v: jnp.zeros((1, LANES), F32).at[0, :SSM_HEADS].set(v.astype(F32))
    a_neg = pad(-jnp.exp(a_log.astype(F32)))
    expand_np = np.zeros((LANES, SSM_WIDTH), np.float32)
    for hh in range(SSM_HEADS):
        expand_np[hh, hh * SSM_HEAD_DIM:(hh + 1) * SSM_HEAD_DIM] = 1.0
    expand = jnp.asarray(expand_np, BF16)
    dskip_exp = jnp.repeat(d_skip.astype(F32), SSM_HEAD_DIM).reshape(1, SSM_WIDTH)
    const = lambda shape: pl.BlockSpec(shape, lambda c: (0, 0))
    return pl.pallas_call(
        _ssd_kernel,
        grid=(T // C,),
        in_specs=[pl.BlockSpec((C, HW), lambda c: (c, xbc0)),
                  pl.BlockSpec((C, HW), lambda c: (c, xbc0 + 1)),
                  pl.BlockSpec((C, HW), lambda c: (c, xbc0 + 2)),
                  pl.BlockSpec((C, SSM_WIDTH), lambda c: (c, 4 * RET_WIDTH // SSM_WIDTH)),
                  pl.BlockSpec((C, LANES), lambda c: (c, dt0)),
                  const((CONV_WIDTH, XBC_WIDTH)), const((1, XBC_WIDTH)),
                  const((1, LANES)), const((1, LANES)),
                  const((1, SSM_WIDTH)), const((1, SSM_WIDTH)),
                  const((LANES, SSM_WIDTH))],
        out_specs=pl.BlockSpec((C, SSM_WIDTH), lambda c: (c, 0)),
        out_shape=jax.ShapeDtypeStruct((T, SSM_WIDTH), BF16),
        scratch_shapes=[pltpu.VMEM((C + 8, XBC_WIDTH), F32),
                        pltpu.VMEM((SSM_GROUPS, SSM_STATE, HW), F32)],
        compiler_params=_params(("arbitrary",)),
        name="ssd",
    )(proj, proj, proj, proj, proj, conv_w, conv_b.reshape(1, XBC_WIDTH), pad(dt_bias), a_neg,
      dskip_exp, ssm_norm_w.reshape(1, SSM_WIDTH), expand)


def _outproj_kernel(ret_ref, ssm_ref, w_ref, x_ref, lnw_ref, wr_ref, br_ref, hs_ref, up_ref, lg_ref):
    tm = x_ref.shape[0]
    h = (x_ref[...] + _dot(ret_ref[...], w_ref[0:RET_WIDTH, :])
         + _dot(ssm_ref[...], w_ref[RET_WIDTH:D_MODEL, :]))
    for s in range(ROW_SLABS):
        hs_ref[pl.ds(s, tm, stride=ROW_SLABS), :] = h[:, s * LANES:(s + 1) * LANES]
    u = h * lax.rsqrt(jnp.mean(h * h, axis=-1, keepdims=True) + EPS) * lnw_ref[...]
    packed = _pack_bf16_pairs(u[:, :D_MODEL // 2], u[:, D_MODEL // 2:])
    for s in range(PACK_SLABS):
        up_ref[pl.ds(s, tm, stride=PACK_SLABS), :] = packed[:, s * LANES:(s + 1) * LANES]
    E = N_EXPERTS
    uh, um, ul = _split3(u)
    ph = _dot(uh, wr_ref[...])
    pm = _dot(um, wr_ref[...])
    pw = _dot(ul, wr_ref[...])
    lg = (ph[:, 0:E] + (ph[:, E:2 * E] + pm[:, 0:E])
          + (ph[:, 2 * E:3 * E] + pm[:, E:2 * E] + pw[:, 0:E]))
    lg_ref[...] = lg + br_ref[...]


def _outproj(ret, ssm, w_out_bf16, x2, ln_w, w_router, b_router):
    T = x2.shape[0]
    tm = 512
    return pl.pallas_call(
        _outproj_kernel,
        grid=(T // tm,),
        in_specs=[pl.BlockSpec((tm, RET_WIDTH), lambda i: (i, 0)),
                  pl.BlockSpec((tm, SSM_WIDTH), lambda i: (i, 0)),
                  pl.BlockSpec((D_MODEL, D_MODEL), lambda i: (0, 0)),
                  pl.BlockSpec((tm, D_MODEL), lambda i: (i, 0)),
                  pl.BlockSpec((1, D_MODEL), lambda i: (0, 0)),
                  pl.BlockSpec((D_MODEL, 3 * N_EXPERTS), lambda i: (0, 0)),
                  pl.BlockSpec((1, N_EXPERTS), lambda i: (0, 0))],
        out_specs=[pl.BlockSpec((tm * ROW_SLABS, LANES), lambda i: (i, 0)),
                   pl.BlockSpec((tm * PACK_SLABS, LANES), lambda i: (i, 0)),
                   pl.BlockSpec((tm, N_EXPERTS), lambda i: (i, 0))],
        out_shape=[jax.ShapeDtypeStruct((T * ROW_SLABS, LANES), F32),
                   jax.ShapeDtypeStruct((T * PACK_SLABS, LANES), jnp.uint32),
                   jax.ShapeDtypeStruct((T, N_EXPERTS), F32)],
        compiler_params=_params(("parallel",)),
        name="outproj",
    )(ret, ssm, w_out_bf16, x2, ln_w.reshape(1, D_MODEL), jnp.concatenate(_split3(w_router), axis=1),
      b_router.reshape(1, N_EXPERTS))


def _expert_kernel(n_ff, n_items, aliased, *refs):
    (item_e, item_row, item_nsub, item_nzero, item_live, item_src, sorted_tok,
     u_hbm, wg_hbm, wd_hbm, bga_ref, bgb_ref, bd_ref) = refs[:13]
    (y_hbm, stage, xbuf, acc, ybuf, wgbuf, wdbuf, wa_s, wb_s, wd_s,
     sem_in, sem_out, sem_w) = refs[14:] if aliased else refs[13:]
    i = pl.program_id(0)
    f = pl.program_id(1)

    step = i * n_ff + f
    live_steps = item_live[0] * n_ff
    for k in range(1, n_items):
        live_steps = live_steps + item_live[k] * n_ff
    W2_ = 2 * FF_TILE

    def tile_copies(g):
        e = item_e[g // n_ff]
        ft = g % n_ff
        slot = g % WEIGHT_SLOTS
        cols = lambda b: pl.ds(pl.multiple_of(b * W2_, W2_), W2_)
        rows = lambda b: pl.ds(pl.multiple_of(b * FF_TILE, FF_TILE), FF_TILE)
        return [pltpu.make_async_copy(wg_hbm.at[e, :, cols(ft)], wgbuf.at[slot, 0], sem_w.at[slot]),
                pltpu.make_async_copy(wg_hbm.at[e, :, cols(n_ff + ft)], wgbuf.at[slot, 1], sem_w.at[slot]),
                pltpu.make_async_copy(wd_hbm.at[e, rows(ft), :], wdbuf.at[slot, 0], sem_w.at[slot]),
                pltpu.make_async_copy(wd_hbm.at[e, rows(n_ff + ft), :], wdbuf.at[slot, 1], sem_w.at[slot])]

    @pl.when(step == 0)
    def _():
        for g in range(WEIGHT_SLOTS - 1):
            @pl.when(g < live_steps)
            def _(g=g):
                for c in tile_copies(g):
                    c.start()

    @pl.when(step < live_steps)
    def _():
        for c in tile_copies(step):
            c.wait()

        @pl.when(step + WEIGHT_SLOTS - 1 < live_steps)
        def _():
            for c in tile_copies(step + WEIGHT_SLOTS - 1):
                c.start()

    wslot = step % WEIGHT_SLOTS
    wga_ref = wgbuf.at[wslot, 0]
    wgb_ref = wgbuf.at[wslot, 1]
    wda_ref = wdbuf.at[wslot, 0]
    wdb_ref = wdbuf.at[wslot, 1]
    nsub = item_nsub[i]
    nzero = item_nzero[i]
    row0 = item_row[i]
    SUB = ROW_BLK
    YS = SUB * ROW_SLABS
    PS = PACK_SLABS
    XS = SUB * PS

    def y_copy(slot, j):
        dst = y_hbm.at[pl.ds(pl.multiple_of((row0 + j * SUB) * ROW_SLABS, YS), YS)]
        return pltpu.make_async_copy(ybuf.at[pl.ds(pl.multiple_of(slot * YS, YS), YS)], dst, sem_out.at[slot])

    def gather_copy(tok, r):
        src = u_hbm.at[pl.ds(pl.multiple_of(tok * PS, PS), PS)]
        return pltpu.make_async_copy(src, stage.at[pl.ds(pl.multiple_of(r * PS, PS), PS)], sem_in)

    n_assign = sorted_tok.shape[0]
    n_sub_max = ITEM_ROWS // SUB
    covers = [jnp.logical_and(nsub > lo, nsub <= m) for lo, m in zip((0,) + ITEM_VARIANTS[:-1], ITEM_VARIANTS)]
    CH = ITEM_ROWS // n_ff
    nxt = jnp.minimum(i + 1, n_items - 1)

    def issue_rows(item, lo, count, unrolled):
        src0 = item_src[item]

        def one(r, p):
            gather_copy(sorted_tok[jnp.minimum(src0 + r, n_assign - 1)], r).start(priority=p)

        if unrolled:
            for k in range(count):
                one(lo + k, k % 2)
        else:
            def body(q, c):
                for p in range(2):
                    one(lo + 2 * q + p, p)
                return c
            lax.fori_loop(0, count // 2, body, 0)

    def wait_rows():
        for _ in range(n_sub_max):
            pltpu.make_async_copy(u_hbm.at[pl.ds(0, XS)], stage.at[pl.ds(0, XS)], sem_in).wait()

    @pl.when(jnp.logical_and(i == 0, f == 0))
    def _():
        issue_rows(0, 0, ITEM_ROWS, False)

    lane = lax.broadcasted_iota(jnp.int32, (1, 2 * FF_TILE), 1)
    even = (lane % 2) == 0
    bga = bga_ref[...]
    bgb = bgb_ref[...]
    W2 = 2 * FF_TILE

    @pl.when(f == 0)
    def _():
        wait_rows()
        half = D_MODEL // 2

        def unpack(j, c):
            rows = pl.ds(pl.multiple_of(j * SUB, SUB), SUB)
            for s in range(PS):
                p = stage[pl.ds(j * XS + s, SUB, stride=PS), :]
                lo = lax.bitcast_convert_type(p << 16, F32)
                hi = lax.bitcast_convert_type(p & jnp.uint32(0xFFFF0000), F32)
                xbuf[rows, s * LANES:(s + 1) * LANES] = lo.astype(BF16)
                xbuf[rows, half + s * LANES:half + (s + 1) * LANES] = hi.astype(BF16)
            acc[rows, :] = jnp.zeros((SUB, D_MODEL), F32)
            return c
        lax.fori_loop(0, sum(jnp.where(c, m, 0) for m, c in zip(ITEM_VARIANTS, covers)), unpack, 0)

    @pl.when(nsub == 0)
    def _():
        issue_rows(nxt, f * CH, CH, False)

    def block(rows):
        xs = xbuf[0:rows, :]
        wa_s[...] = wga_ref[...].astype(BF16)
        ga = _dot(xs, wa_s[...]) + bga
        issue_rows(nxt, f * CH, CH, True)
        wb_s[...] = wgb_ref[...].astype(BF16)
        gb = _dot(xs, wb_s[...]) + bgb
        wd_s[...] = pltpu.bitcast(_pack_bf16_pairs(wda_ref[...], wdb_ref[...]), BF16)
        gate = jnp.where(even, ga, pltpu.roll(gb, 1, 1))
        up = jnp.where(even, pltpu.roll(ga, W2 - 1, 1), gb)
        gate = jnp.minimum(gate, SWIGLU_LIMIT)
        up = jnp.clip(up, -SWIGLU_LIMIT, SWIGLU_LIMIT)
        act = ((up + 1.0) * (gate * jax.nn.sigmoid(gate * SWIGLU_ALPHA))).astype(BF16)
        for n in range(0, D_MODEL, DOWN_TILE):
            acc[0:rows, n:n + DOWN_TILE] += _dot(act, wd_s[:, n:n + DOWN_TILE])

    for m, cover in zip(ITEM_VARIANTS, covers):
        @pl.when(cover)
        def _(m=m):
            block(m * SUB)

    @pl.when(f == n_ff - 1)
    def _():
        @pl.when(i == n_items - 1)
        def _():
            wait_rows()

        def stage_out(q, c):
            slot = q % 2

            @pl.when(q >= 2)
            def _():
                y_copy(slot, 0).wait()
            val = acc[pl.ds(pl.multiple_of(q * SUB, SUB), SUB), :] + bd_ref[...]
            for s in range(ROW_SLABS):
                ybuf[pl.ds(slot * YS + s, SUB, stride=ROW_SLABS), :] = val[:, s * LANES:(s + 1) * LANES]
            y_copy(slot, q).start()
            return c
        lax.fori_loop(0, nsub, stage_out, 0)

        @pl.when(nsub > 0)
        def _():
            y_copy(0, 0).wait()

        @pl.when(nsub > 1)
        def _():
            y_copy(1, 0).wait()

    @pl.when(jnp.logical_and(f == n_ff - 1, nzero > 0))
    def _():
        ybuf[0:YS, :] = jnp.zeros((YS, LANES), F32)

        def start(j, c):
            y_copy(0, j).start()
            return c
        lax.fori_loop(0, nzero, start, 0)

        def wait(j, c):
            y_copy(0, 0).wait()
            return c
        lax.fori_loop(0, nzero, wait, 0)


def _experts_call(u_packed, sorted_tok, n_rows, items, w_gate_up, b_gate_up, w_down, b_down, y_in=None):
    n_ff = (D_FF // 2) // FF_TILE
    assert ITEM_ROWS % (2 * n_ff) == 0 and ITEM_VARIANTS[-1] * ROW_BLK == ITEM_ROWS
    W2 = 2 * FF_TILE
    item_e, item_row, item_nsub, item_nzero, item_live, item_src = items
    n_items = item_e.shape[0]
    aliased = y_in is not None

    def ff(i, f, live):
        return jnp.where(live[i] > 0, f, n_ff - 1)

    grid_spec = pltpu.PrefetchScalarGridSpec(
        num_scalar_prefetch=7,
        grid=(n_items, n_ff),
        in_specs=[
            pl.BlockSpec(memory_space=pl.ANY),
            pl.BlockSpec(memory_space=pl.ANY),
            pl.BlockSpec(memory_space=pl.ANY),
            pl.BlockSpec((None, 1, W2), lambda i, f, e, r, n, z, lv, sr, st: (e[i], 0, ff(i, f, lv))),
            pl.BlockSpec((None, 1, W2), lambda i, f, e, r, n, z, lv, sr, st: (e[i], 0, n_ff + ff(i, f, lv))),
            pl.BlockSpec((None, 1, D_MODEL), lambda i, f, e, r, n, z, lv, sr, st: (e[i], 0, 0)),
        ] + ([pl.BlockSpec(memory_space=pl.ANY)] if aliased else []),
        out_specs=pl.BlockSpec(memory_space=pl.ANY),
        scratch_shapes=[pltpu.VMEM((ITEM_ROWS * PACK_SLABS, LANES), jnp.uint32),
                        pltpu.VMEM((ITEM_ROWS, D_MODEL), BF16),
                        pltpu.VMEM((ITEM_ROWS, D_MODEL), F32),
                        pltpu.VMEM((2 * ROW_BLK * ROW_SLABS, LANES), F32),
                        pltpu.VMEM((WEIGHT_SLOTS, 2, D_MODEL, W2), F32),
                        pltpu.VMEM((WEIGHT_SLOTS, 2, FF_TILE, D_MODEL), F32),
                        pltpu.VMEM((D_MODEL, W2), BF16),
                        pltpu.VMEM((D_MODEL, W2), BF16),
                        pltpu.VMEM((W2, D_MODEL), BF16),
                        pltpu.SemaphoreType.DMA(()),
                        pltpu.SemaphoreType.DMA((2,)),
                        pltpu.SemaphoreType.DMA((WEIGHT_SLOTS,))],
    )
    args = (item_e, item_row, item_nsub, item_nzero, item_live, item_src, sorted_tok, u_packed,
            w_gate_up, w_down, b_gate_up.reshape(N_EXPERTS, 1, 2 * D_FF),
            b_gate_up.reshape(N_EXPERTS, 1, 2 * D_FF), b_down.reshape(N_EXPERTS, 1, D_MODEL))
    return pl.pallas_call(
        functools.partial(_expert_kernel, n_ff, n_items, aliased),
        grid_spec=grid_spec,
        out_shape=jax.ShapeDtypeStruct((n_rows * ROW_SLABS, LANES), F32),
        input_output_aliases={len(args): 0} if aliased else {},
        compiler_params=_params(("arbitrary", "arbitrary")),
        name="experts_overflow" if aliased else "experts",
    )(*args, *((y_in,) if aliased else ()))


def _experts(u_packed, sorted_tok, n_rows, items, n_used, w_gate_up, b_gate_up, w_down, b_down):
    n_main = N_EXPERTS + 1
    weights = (w_gate_up, b_gate_up, w_down, b_down)
    y = _experts_call(u_packed, sorted_tok, n_rows, tuple(a[:n_main] for a in items), *weights)
    rest = tuple(a[n_main:] for a in items)
    return lax.cond(n_used > n_main,
                    lambda y_: _experts_call(u_packed, sorted_tok, n_rows, rest, *weights, y_in=y_),
                    lambda y_: y_, y)


def _combine_kernel(tm, n_steps, dest_ref, w_ref, y_hbm, h_ref, lnw_ref, o_ref, buf, osum, sem):
    i = pl.program_id(0)

    def copy(step, slot, t, k):
        d = dest_ref[(step * tm + t) * TOP_K + k]
        return pltpu.make_async_copy(y_hbm.at[d], buf.at[slot, k * tm + t], sem.at[slot])

    def issue(step, slot):
        def body(t, c):
            for k in range(TOP_K):
                copy(step, slot, t, k).start(priority=k % 2)
            return c
        lax.fori_loop(0, tm, body, 0, unroll=4)

    @pl.when(i == 0)
    def _():
        issue(0, 0)

    @pl.when(i + 1 < n_steps)
    def _():
        issue(i + 1, (i + 1) % 2)

    slot = i % 2

    pltpu.make_async_copy(y_hbm.at[pl.ds(0, TOP_K * tm)], buf.at[slot], sem.at[slot]).wait()

    def token(t, c):
        a = h_ref[t]
        for k in range(TOP_K):
            a = a + w_ref[(i * tm + t) * TOP_K + k] * buf[slot, k * tm + t]
        osum[pl.ds(pl.multiple_of(t * ROW_SLABS, ROW_SLABS), ROW_SLABS), :] = a
        return c
    lax.fori_loop(0, tm, token, 0, unroll=4)

    h = osum[...].reshape(tm, ROW_SLABS, LANES)
    ms = jnp.mean(jnp.mean(h * h, axis=2, keepdims=True), axis=1, keepdims=True)
    osum[...] = (h * lax.rsqrt(ms + EPS) * lnw_ref[...]).reshape(tm * ROW_SLABS, LANES)
    for s in range(ROW_SLABS):
        o_ref[:, s * LANES:(s + 1) * LANES] = osum[pl.ds(s, tm, stride=ROW_SLABS), :]


def _combine(dest, wflat, y3, h3, ln_w):
    T = h3.shape[0]
    tm = 128
    n_steps = T // tm
    grid_spec = pltpu.PrefetchScalarGridSpec(
        num_scalar_prefetch=2,
        grid=(n_steps,),
        in_specs=[pl.BlockSpec(memory_space=pl.ANY),
                  pl.BlockSpec((tm, ROW_SLABS, LANES), lambda i, d, w: (i, 0, 0)),
                  pl.BlockSpec((1, ROW_SLABS, LANES), lambda i, d, w: (0, 0, 0))],
        out_specs=pl.BlockSpec((tm, D_MODEL), lambda i, d, w: (i, 0)),
        scratch_shapes=[pltpu.VMEM((2, TOP_K * tm, ROW_SLABS, LANES), F32),
                        pltpu.VMEM((tm * ROW_SLABS, LANES), F32),
                        pltpu.SemaphoreType.DMA((2,))],
    )
    return pl.pallas_call(
        functools.partial(_combine_kernel, tm, n_steps),
        grid_spec=grid_spec,
        out_shape=jax.ShapeDtypeStruct((T, D_MODEL), F32),
        compiler_params=_params(("arbitrary",)),
        name="combine",
    )(dest, wflat, y3, h3, ln_w.reshape(1, ROW_SLABS, LANES))


def _route(logits, n_rows, n_items):
    T = logits.shape[0]
    top_logits, top_idx = lax.top_k(logits, TOP_K)
    top_w = jax.nn.softmax(top_logits, axis=-1)
    e_flat = top_idx.reshape(-1).astype(jnp.int32)
    onehot = (e_flat[:, None] == jnp.arange(N_EXPERTS, dtype=jnp.int32)[None, :]).astype(jnp.int32)
    csum = jnp.cumsum(onehot, axis=0)
    counts = csum[-1]
    padded = (counts + ROW_BLK - 1) // ROW_BLK * ROW_BLK
    pend = jnp.cumsum(padded)
    pstart = pend - padded
    dest = jnp.sum(onehot * (pstart[None, :] + csum - onehot), axis=1).astype(jnp.int32)

    per_e = (padded + ITEM_ROWS - 1) // ITEM_ROWS
    iend = jnp.cumsum(per_e)
    istart = iend - per_e
    ii = jnp.arange(n_items, dtype=jnp.int32)
    total = iend[-1]
    live = (ii < total).astype(jnp.int32)
    ic = jnp.minimum(ii, total - 1)
    ie = jnp.minimum(jnp.sum(ic[:, None] >= iend[None, :], axis=1), N_EXPERTS - 1).astype(jnp.int32)
    within = ic - istart[ie]
    irow = (pstart[ie] + within * ITEM_ROWS).astype(jnp.int32)
    insub = jnp.minimum((padded[ie] - within * ITEM_ROWS) // ROW_BLK, ITEM_ROWS // ROW_BLK).astype(jnp.int32)
    insub = insub * live
    tail_rows = n_rows - pend[-1]
    tail = jnp.logical_and(ii == total, tail_rows > 0)
    inzero = jnp.where(tail, tail_rows // ROW_BLK, 0).astype(jnp.int32)
    irow = jnp.where(tail, pend[-1], irow).astype(jnp.int32)
    n_used = total + (tail_rows > 0).astype(jnp.int32)
    n_assign = T * TOP_K
    assert N_EXPERTS * n_assign < 2 ** 31
    order = jnp.sort(e_flat * n_assign + jnp.arange(n_assign, dtype=jnp.int32)) % n_assign
    sorted_tok = (order // TOP_K).astype(jnp.int32)
    cstart = jnp.cumsum(counts) - counts
    isrc = ((cstart[ie] + within * ITEM_ROWS) * live).astype(jnp.int32)
    return dest, top_w.reshape(-1).astype(F32), (ie, irow, insub, inzero, live, isrc), n_used, sorted_tok


def kernel(x, positions, ln_mix_w, w_in, conv_w, conv_b, dt_bias, a_log, d_skip, ssm_norm_w, w_out,
           ln_ffn_w, w_router, b_router, w_gate_up, b_gate_up, w_down, b_down, ln_final_w):
    B, L, _ = x.shape
    T = B * L
    assert B == 1 and T % 1024 == 0
    x2 = x.reshape(T, D_MODEL)
    half = RET_HEAD_DIM // 2
    inv_freq = (ROPE_BASE ** (-jnp.arange(half, dtype=F32) / half)).reshape(1, half)
    pos_col = positions.reshape(T, 1).astype(F32)

    proj = _inproj(x2, ln_mix_w[0], jnp.swapaxes(w_in[0], 0, 1))
    ret = _retention(proj, pos_col, inv_freq)
    ssm = _ssd(proj, conv_w[0], conv_b[0], dt_bias[0], a_log[0], d_skip[0], ssm_norm_w[0])
    h_slabs, u_packed, logits = _outproj(ret, ssm, w_out[0].astype(BF16), x2, ln_ffn_w[0], w_router[0],
                                         b_router[0])

    n_rows = -(-(T * TOP_K + N_EXPERTS * (ROW_BLK - 1)) // ROW_BLK) * ROW_BLK
    n_items = N_EXPERTS + 1 + n_rows // ITEM_ROWS
    dest, wflat, items, n_used, sorted_tok = _route(logits, n_rows, n_items)

    y_rows = _experts(u_packed, sorted_tok, n_rows, items, n_used, w_gate_up[0], b_gate_up[0], w_down[0],
                      b_down[0])
    out = _combine(dest, wflat, y_rows.reshape(n_rows, ROW_SLABS, LANES),
                   h_slabs.reshape(T, ROW_SLABS, LANES), ln_final_w)
    return out.reshape(B, L, D_MODEL)
```

```python
import functools

import numpy as np
import jax
import jax.numpy as jnp
from jax import lax
from jax.experimental import pallas as pl
from jax.experimental.pallas import tpu as pltpu

F32 = jnp.float32
BF16 = jnp.bfloat16

D_MODEL = 2048
RET_HEADS = 4
RET_HEAD_DIM = 256
RET_WIDTH = RET_HEADS * RET_HEAD_DIM
SSM_WIDTH = D_MODEL - RET_WIDTH
SSM_HEAD_DIM = 64
SSM_HEADS = SSM_WIDTH // SSM_HEAD_DIM
SSM_GROUPS = 2
SSM_STATE = 128
CONV_WIDTH = 4
XBC_WIDTH = SSM_WIDTH + 2 * SSM_GROUPS * SSM_STATE
D_IN_PROJ = 4 * RET_WIDTH + SSM_WIDTH + XBC_WIDTH + SSM_HEADS
ROPE_BASE = 10000.0
N_EXPERTS = 32
TOP_K = 4
D_FF = D_MODEL
SWIGLU_LIMIT = 7.0
SWIGLU_ALPHA = 1.702
EPS = 1e-6

LANES = 128
VMEM_LIMIT = 56 * 1024 * 1024

RET_CHUNK = 256
SSD_CHUNK = 128
ROW_BLK = 128
ITEM_ROWS = 1536
ITEM_VARIANTS = (8, 9, 10, 12)
FF_TILE = 128
DOWN_TILE = 512
WEIGHT_SLOTS = 3
ROW_SLABS = D_MODEL // LANES
PACK_SLABS = ROW_SLABS // 2


def _params(sem, **kw):
    return pltpu.CompilerParams(dimension_semantics=sem, vmem_limit_bytes=VMEM_LIMIT, **kw)


def _dot(a, b):
    return jnp.dot(a, b, preferred_element_type=F32)


def _dot_nt(a, b):
    return lax.dot_general(a, b, (((1,), (1,)), ((), ())), preferred_element_type=F32)


def _dot_tn(a, b):
    return lax.dot_general(a, b, (((0,), (0,)), ((), ())), preferred_element_type=F32)


def _split3(x):
    hi = x.astype(BF16)
    r = x - hi.astype(F32)
    mid = r.astype(BF16)
    lo = (r - mid.astype(F32)).astype(BF16)
    return hi, mid, lo


def _dot_exact_rhs01(x, m01):
    hi, mid, lo = _split3(x)
    return _dot(hi, m01) + _dot(mid, m01) + _dot(lo, m01)


def _dot_exact_lhs01(m01, x):
    hi, mid, lo = _split3(x)
    return _dot(m01, hi) + _dot(m01, mid) + _dot(m01, lo)


def _silu(x):
    return x * jax.nn.sigmoid(x)


def _pack_bf16_pairs(lo, hi):
    lo_bits = lax.bitcast_convert_type(lo.astype(BF16).astype(F32), jnp.uint32)
    hi_bits = lax.bitcast_convert_type(hi.astype(BF16).astype(F32), jnp.uint32)
    return hi_bits | (lo_bits >> 16)


INPROJ_CHUNKS = 4


def _inproj_kernel(x_hbm, lnw_ref, w_ref, o_ref, xs_ref, u_ref, sem):
    tm = u_ref.shape[0]
    rows = tm // INPROJ_CHUNKS
    i = pl.program_id(0)

    @pl.when(pl.program_id(1) == 0)
    def _():
        def copy(c):
            return pltpu.make_async_copy(x_hbm.at[pl.ds(i * tm + c * rows, rows)], xs_ref.at[c % 2], sem.at[c % 2])

        copy(0).start()
        copy(1).start()
        for c in range(INPROJ_CHUNKS):
            copy(c).wait()
            x = xs_ref[c % 2]
            ms = jnp.mean(x * x, axis=-1, keepdims=True)
            u_ref[c * rows:(c + 1) * rows, :] = (x * lax.rsqrt(ms + EPS) * lnw_ref[...]).astype(BF16)
            if c + 2 < INPROJ_CHUNKS:
                copy(c + 2).start()

    tn = o_ref.shape[1]
    tail = D_IN_PROJ % tn
    if 0 < tail <= LANES:
        last = pl.num_programs(1) - 1

        @pl.when(pl.program_id(1) < last)
        def _():
            o_ref[...] = _dot_nt(u_ref[...], w_ref[...].astype(BF16))

        @pl.when(pl.program_id(1) == last)
        def _():
            o_ref[:, 0:LANES] = _dot_nt(u_ref[...], w_ref[0:LANES, :].astype(BF16))
    else:
        o_ref[...] = _dot_nt(u_ref[...], w_ref[...].astype(BF16))


def _inproj(x2, ln_w, w_in_t):
    T = x2.shape[0]
    tm, tn = (2048 if T % 2048 == 0 else 1024), 512
    return pl.pallas_call(
        _inproj_kernel,
        grid=(T // tm, pl.cdiv(D_IN_PROJ, tn)),
        in_specs=[pl.BlockSpec(memory_space=pl.ANY),
                  pl.BlockSpec((1, D_MODEL), lambda i, j: (0, 0)),
                  pl.BlockSpec((tn, D_MODEL), lambda i, j: (j, 0))],
        out_specs=pl.BlockSpec((tm, tn), lambda i, j: (i, j)),
        out_shape=jax.ShapeDtypeStruct((T, D_IN_PROJ), F32),
        scratch_shapes=[pltpu.VMEM((2, tm // INPROJ_CHUNKS, D_MODEL), F32),
                        pltpu.VMEM((tm, D_MODEL), BF16),
                        pltpu.SemaphoreType.DMA((2,))],
        compiler_params=_params(("arbitrary", "arbitrary")),
        name="inproj",
    )(x2, ln_w.reshape(1, D_MODEL), w_in_t)


def _retention_tables():
    C = RET_CHUNK
    h = np.arange(RET_HEADS, dtype=np.float64)
    log_gamma = np.log1p(-np.exp2(-5.0 - h))
    idx = np.arange(C, dtype=np.float64)
    rel = idx[:, None] - idx[None, :]
    intra = np.where(rel >= 0, np.exp(log_gamma[:, None, None] * np.maximum(rel, 0.0)), 0.0)
    q_decay = np.exp(log_gamma[:, None] * (idx + 1.0))
    k_decay = np.exp(log_gamma[:, None] * (C - 1.0 - idx))
    chunk_decay = np.exp(log_gamma * C)
    qd = np.broadcast_to(q_decay[:, :, None], (RET_HEADS, C, RET_HEAD_DIM))
    kd = np.broadcast_to(k_decay[:, :, None], (RET_HEADS, C, RET_HEAD_DIM))
    return (jnp.asarray(intra, F32), jnp.asarray(qd, F32), jnp.asarray(kd, F32),
            [float(c) for c in chunk_decay])


def _retention_kernel(chunk_decay, pos_ref, invf_ref, q_ref, k_ref, v_ref, g_ref,
                      intra_ref, qd_ref, kd_ref, o_ref, state_ref):
    @pl.when(pl.program_id(0) == 0)
    def _():
        state_ref[...] = jnp.zeros_like(state_ref)

    half = RET_HEAD_DIM // 2
    ang = pos_ref[...] * invf_ref[...]
    cos = jnp.cos(ang)
    sin = jnp.sin(ang)

    def rope(t):
        t1, t2 = t[:, :half], t[:, half:]
        return jnp.concatenate([t1 * cos - t2 * sin, t2 * cos + t1 * sin], axis=-1)

    for h in range(RET_HEADS):
        sl = slice(h * RET_HEAD_DIM, (h + 1) * RET_HEAD_DIM)
        q = rope(q_ref[:, sl])
        k = rope(k_ref[:, sl]) * (RET_HEAD_DIM ** -0.5)
        v = v_ref[:, sl].astype(BF16)
        state = state_ref[h]
        scores = _dot_nt(q.astype(BF16), k.astype(BF16)) * intra_ref[h]
        inner = _dot(scores.astype(BF16), v)
        cross = _dot((q * qd_ref[h]).astype(BF16), state.astype(BF16))
        state_ref[h] = chunk_decay[h] * state + _dot_tn((k * kd_ref[h]).astype(BF16), v)
        o = inner + cross
        o = o * lax.rsqrt(jnp.mean(o * o, axis=-1, keepdims=True) + EPS)
        o_ref[:, sl] = (o * _silu(g_ref[:, sl])).astype(o_ref.dtype)


def _retention(proj, pos_col, inv_freq):
    T = proj.shape[0]
    C = RET_CHUNK
    intra, qd, kd, chunk_decay = _retention_tables()
    col = lambda j: pl.BlockSpec((C, RET_WIDTH), lambda c, j=j: (c, j))
    const3 = lambda shape: pl.BlockSpec(shape, lambda c: (0, 0, 0))
    return pl.pallas_call(
        functools.partial(_retention_kernel, chunk_decay),
        grid=(T // C,),
        in_specs=[pl.BlockSpec((C, 1), lambda c: (c, 0)),
                  pl.BlockSpec((1, RET_HEAD_DIM // 2), lambda c: (0, 0)),
                  col(0), col(1), col(2), col(3),
                  const3((RET_HEADS, C, C)),
                  const3((RET_HEADS, C, RET_HEAD_DIM)),
                  const3((RET_HEADS, C, RET_HEAD_DIM))],
        out_specs=pl.BlockSpec((C, RET_WIDTH), lambda c: (c, 0)),
        out_shape=jax.ShapeDtypeStruct((T, RET_WIDTH), BF16),
        scratch_shapes=[pltpu.VMEM((RET_HEADS, RET_HEAD_DIM, RET_HEAD_DIM), F32)],
        compiler_params=_params(("arbitrary",)),
        name="retention",
    )(pos_col, inv_freq, proj, proj, proj, proj, intra, qd, kd)


def _ssd_kernel(xs0_ref, xs1_ref, bc_ref, z_ref, dt_ref, convw_ref, convb_ref, dtb_ref, a_ref,
                dskip_ref, normw_ref, expand_ref, o_ref, ext_ref, state_ref):
    C = SSD_CHUNK
    HW = SSM_WIDTH // SSM_GROUPS
    CARRY = 8

    @pl.when(pl.program_id(0) == 0)
    def _():
        ext_ref[0:CARRY, :] = jnp.zeros((CARRY, XBC_WIDTH), F32)
        state_ref[...] = jnp.zeros_like(state_ref)

    ext_ref[CARRY:CARRY + C, 0:HW] = xs0_ref[...]
    ext_ref[CARRY:CARRY + C, HW:2 * HW] = xs1_ref[...]
    ext_ref[CARRY:CARRY + C, 2 * HW:3 * HW] = bc_ref[...]
    conv = convb_ref[...]
    for k in range(CONV_WIDTH):
        off = CARRY - (CONV_WIDTH - 1) + k
        conv = conv + convw_ref[k:k + 1, :] * ext_ref[off:off + C, :]
    ext_ref[0:CARRY, :] = ext_ref[C:C + CARRY, :]
    xbc = _silu(conv)
    xs = xbc[:, :SSM_WIDTH]

    lane = lax.broadcasted_iota(jnp.int32, (1, LANES), 1)
    dt_raw = jnp.where(lane < SSM_HEADS, dt_ref[...], 0.0) + dtb_ref[...]
    dt = jnp.maximum(dt_raw, 0.0) + jnp.log1p(jnp.exp(-jnp.abs(dt_raw)))
    dta = dt * a_ref[...]

    row = lax.broadcasted_iota(jnp.int32, (C, C), 0)
    colm = lax.broadcasted_iota(jnp.int32, (C, C), 1)
    tril = row >= colm
    a_cum = _dot_exact_lhs01(jnp.where(tril, 1.0, 0.0).astype(BF16), dta)
    a_cum_t = a_cum.T

    expand = expand_ref[...]
    a_exp = _dot_exact_rhs01(a_cum, expand)
    dt_exp = _dot_exact_rhs01(dt, expand)
    a_last = a_exp[C - 1:C, :]
    decay_in = jnp.exp(a_exp)
    decay_out = jnp.exp(a_last - a_exp)
    chunk_decay = jnp.exp(a_last)
    xdt = xs * dt_exp

    lane2 = lax.broadcasted_iota(jnp.int32, (1, LANES), 1)
    lo_head = lane2 < SSM_HEAD_DIM
    ys = []
    for g in range(SSM_GROUPS):
        gs = slice(g * HW, (g + 1) * HW)
        b_g = xbc[:, SSM_WIDTH + g * SSM_STATE:SSM_WIDTH + (g + 1) * SSM_STATE].astype(BF16)
        c0 = SSM_WIDTH + SSM_GROUPS * SSM_STATE
        c_g = xbc[:, c0 + g * SSM_STATE:c0 + (g + 1) * SSM_STATE].astype(BF16)
        cb = _dot_nt(c_g, b_g)
        state = state_ref[g]
        y_off = _dot(c_g, state.astype(BF16)) * decay_in[:, gs]
        xw = (xdt[:, gs] * decay_out[:, gs]).astype(BF16)
        state_ref[g] = chunk_decay[:, gs] * state + _dot_tn(b_g, xw)
        slabs = []
        for s in range(HW // LANES):
            xd = xdt[:, g * HW + s * LANES:g * HW + (s + 1) * LANES]
            acc = None
            for e in range(2):
                hh = g * (SSM_HEADS // SSM_GROUPS) + 2 * s + e
                seg = a_cum[:, hh:hh + 1] - a_cum_t[hh:hh + 1, :]
                m = cb * jnp.exp(jnp.where(tril, seg, -jnp.inf))
                xm = jnp.where(lo_head if e == 0 else jnp.logical_not(lo_head), xd, 0.0)
                part = _dot(m.astype(BF16), xm.astype(BF16))
                acc = part if acc is None else acc + part
            slabs.append(acc)
        ys.append(jnp.concatenate(slabs, axis=-1) + y_off)
    y = jnp.concatenate(ys, axis=-1) + dskip_ref[...] * xs
    y = y * _silu(z_ref[...])
    outs = []
    for g in range(SSM_GROUPS):
        yg = y[:, g * HW:(g + 1) * HW]
        outs.append(yg * lax.rsqrt(jnp.mean(yg * yg, axis=-1, keepdims=True) + EPS))
    o_ref[...] = (jnp.concatenate(outs, axis=-1) * normw_ref[...]).astype(o_ref.dtype)


def _ssd(proj, conv_w, conv_b, dt_bias, a_log, d_skip, ssm_norm_w):
    T = proj.shape[0]
    C = SSD_CHUNK
    HW = SSM_WIDTH // SSM_GROUPS
    xbc0 = (4 * RET_WIDTH + SSM_WIDTH) // HW
    dt0 = (D_IN_PROJ - SSM_HEADS) // LANES
    pad = lambda v: jnp.zeros((1, LANES), F32).at[0, :SSM_HEADS].set(v.astype(F32))
    a_neg = pad(-jnp.exp(a_log.astype(F32)))
    expand_np = np.zeros((LANES, SSM_WIDTH), np.float32)
    for hh in range(SSM_HEADS):
        expand_np[hh, hh * SSM_HEAD_DIM:(hh + 1) * SSM_HEAD_DIM] = 1.0
    expand = jnp.asarray(expand_np, BF16)
    dskip_exp = jnp.repeat(d_skip.astype(F32), SSM_HEAD_DIM).reshape(1, SSM_WIDTH)
    const = lambda shape: pl.BlockSpec(shape, lambda c: (0, 0))
    return pl.pallas_call(
        _ssd_kernel,
        grid=(T // C,),
        in_specs=[pl.BlockSpec((C, HW), lambda c: (c, xbc0)),
                  pl.BlockSpec((C, HW), lambda c: (c, xbc0 + 1)),
                  pl.BlockSpec((C, HW), lambda c: (c, xbc0 + 2)),
                  pl.BlockSpec((C, SSM_WIDTH), lambda c: (c, 4 * RET_WIDTH // SSM_WIDTH)),
                  pl.BlockSpec((C, LANES), lambda c: (c, dt0)),
                  const((CONV_WIDTH, XBC_WIDTH)), const((1, XBC_WIDTH)),
                  const((1, LANES)), const((1, LANES)),
                  const((1, SSM_WIDTH)), const((1, SSM_WIDTH)),
                  const((LANES, SSM_WIDTH))],
        out_specs=pl.BlockSpec((C, SSM_WIDTH), lambda c: (c, 0)),
        out_shape=jax.ShapeDtypeStruct((T, SSM_WIDTH), BF16),
        scratch_shapes=[pltpu.VMEM((C + 8, XBC_WIDTH), F32),
                        pltpu.VMEM((SSM_GROUPS, SSM_STATE, HW), F32)],
        compiler_params=_params(("arbitrary",)),
        name="ssd",
    )(proj, proj, proj, proj, proj, conv_w, conv_b.reshape(1, XBC_WIDTH), pad(dt_bias), a_neg,
      dskip_exp, ssm_norm_w.reshape(1, SSM_WIDTH), expand)


def _outproj_kernel(ret_ref, ssm_ref, w_ref, x_ref, lnw_ref, wr_ref, br_ref, hs_ref, up_ref, lg_ref):
    tm = x_ref.shape[0]
    h = (x_ref[...] + _dot(ret_ref[...], w_ref[0:RET_WIDTH, :])
         + _dot(ssm_ref[...], w_ref[RET_WIDTH:D_MODEL, :]))
    for s in range(ROW_SLABS):
        hs_ref[pl.ds(s, tm, stride=ROW_SLABS), :] = h[:, s * LANES:(s + 1) * LANES]
    u = h * lax.rsqrt(jnp.mean(h * h, axis=-1, keepdims=True) + EPS) * lnw_ref[...]
    packed = _pack_bf16_pairs(u[:, :D_MODEL // 2], u[:, D_MODEL // 2:])
    for s in range(PACK_SLABS):
        up_ref[pl.ds(s, tm, stride=PACK_SLABS), :] = packed[:, s * LANES:(s + 1) * LANES]
    E = N_EXPERTS
    uh, um, ul = _split3(u)
    ph = _dot(uh, wr_ref[...])
    pm = _dot(um, wr_ref[...])
    pw = _dot(ul, wr_ref[...])
    lg = (ph[:, 0:E] + (ph[:, E:2 * E] + pm[:, 0:E])
          + (ph[:, 2 * E:3 * E] + pm[:, E:2 * E] + pw[:, 0:E]))
    lg_ref[...] = lg + br_ref[...]


def _outproj(ret, ssm, w_out_bf16, x2, ln_w, w_router, b_router):
    T = x2.shape[0]
    tm = 512
    return pl.pallas_call(
        _outproj_kernel,
        grid=(T // tm,),
        in_specs=[pl.BlockSpec((tm, RET_WIDTH), lambda i: (i, 0)),
                  pl.BlockSpec((tm, SSM_WIDTH), lambda i: (i, 0)),
                  pl.BlockSpec((D_MODEL, D_MODEL), lambda i: (0, 0)),
                  pl.BlockSpec((tm, D_MODEL), lambda i: (i, 0)),
                  pl.BlockSpec((1, D_MODEL), lambda i: (0, 0)),
                  pl.BlockSpec((D_MODEL, 3 * N_EXPERTS), lambda i: (0, 0)),
                  pl.BlockSpec((1, N_EXPERTS), lambda i: (0, 0))],
        out_specs=[pl.BlockSpec((tm * ROW_SLABS, LANES), lambda i: (i, 0)),
                   pl.BlockSpec((tm * PACK_SLABS, LANES), lambda i: (i, 0)),
                   pl.BlockSpec((tm, N_EXPERTS), lambda i: (i, 0))],
        out_shape=[jax.ShapeDtypeStruct((T * ROW_SLABS, LANES), F32),
                   jax.ShapeDtypeStruct((T * PACK_SLABS, LANES), jnp.uint32),
                   jax.ShapeDtypeStruct((T, N_EXPERTS), F32)],
        compiler_params=_params(("parallel",)),
        name="outproj",
    )(ret, ssm, w_out_bf16, x2, ln_w.reshape(1, D_MODEL), jnp.concatenate(_split3(w_router), axis=1),
      b_router.reshape(1, N_EXPERTS))


def _expert_kernel(n_ff, n_items, aliased, *refs):
    (item_e, item_row, item_nsub, item_nzero, item_live, item_src, sorted_tok,
     u_hbm, wg_hbm, wd_hbm, bga_ref, bgb_ref, bd_ref) = refs[:13]
    (y_hbm, stage, xbuf, acc, ybuf, wgbuf, wdbuf, wa_s, wb_s, wd_s,
     sem_in, sem_out, sem_w) = refs[14:] if aliased else refs[13:]
    i = pl.program_id(0)
    f = pl.program_id(1)

    step = i * n_ff + f
    live_steps = item_live[0] * n_ff
    for k in range(1, n_items):
        live_steps = live_steps + item_live[k] * n_ff
    W2_ = 2 * FF_TILE

    def tile_copies(g):
        e = item_e[g // n_ff]
        ft = g % n_ff
        slot = g % WEIGHT_SLOTS
        cols = lambda b: pl.ds(pl.multiple_of(b * W2_, W2_), W2_)
        rows = lambda b: pl.ds(pl.multiple_of(b * FF_TILE, FF_TILE), FF_TILE)
        return [pltpu.make_async_copy(wg_hbm.at[e, :, cols(ft)], wgbuf.at[slot, 0], sem_w.at[slot]),
                pltpu.make_async_copy(wg_hbm.at[e, :, cols(n_ff + ft)], wgbuf.at[slot, 1], sem_w.at[slot]),
                pltpu.make_async_copy(wd_hbm.at[e, rows(ft), :], wdbuf.at[slot, 0], sem_w.at[slot]),
                pltpu.make_async_copy(wd_hbm.at[e, rows(n_ff + ft), :], wdbuf.at[slot, 1], sem_w.at[slot])]

    @pl.when(step == 0)
    def _():
        for g in range(WEIGHT_SLOTS - 1):
            @pl.when(g < live_steps)
            def _(g=g):
                for c in tile_copies(g):
                    c.start()

    @pl.when(step < live_steps)
    def _():
        for c in tile_copies(step):
            c.wait()

        @pl.when(step + WEIGHT_SLOTS - 1 < live_steps)
        def _():
            for c in tile_copies(step + WEIGHT_SLOTS - 1):
                c.start()

    wslot = step % WEIGHT_SLOTS
    wga_ref = wgbuf.at[wslot, 0]
    wgb_ref = wgbuf.at[wslot, 1]
    wda_ref = wdbuf.at[wslot, 0]
    wdb_ref = wdbuf.at[wslot, 1]
    nsub = item_nsub[i]
    nzero = item_nzero[i]
    row0 = item_row[i]
    SUB = ROW_BLK
    YS = SUB * ROW_SLABS
    PS = PACK_SLABS
    XS = SUB * PS

    def y_copy(slot, j):
        dst = y_hbm.at[pl.ds(pl.multiple_of((row0 + j * SUB) * ROW_SLABS, YS), YS)]
        return pltpu.make_async_copy(ybuf.at[pl.ds(pl.multiple_of(slot * YS, YS), YS)], dst, sem_out.at[slot])

    def gather_copy(tok, r):
        src = u_hbm.at[pl.ds(pl.multiple_of(tok * PS, PS), PS)]
        return pltpu.make_async_copy(src, stage.at[pl.ds(pl.multiple_of(r * PS, PS), PS)], sem_in)

    n_assign = sorted_tok.shape[0]
    covers = [jnp.logical_and(nsub > lo, nsub <= m) for lo, m in zip((0,) + ITEM_VARIANTS[:-1], ITEM_VARIANTS)]
    nxt = jnp.minimum(i + 1, n_items - 1)

    def gathered_subs(item):
        ns = item_nsub[item]
        size = sum(jnp.where(jnp.logical_and(ns > lo, ns <= m), m, 0)
                   for lo, m in zip((0,) + ITEM_VARIANTS[:-1], ITEM_VARIANTS))
        return jnp.maximum(size, n_ff)

    def issue_rows(item, lo, count, unrolled):
        src0 = item_src[item]

        def one(r, p):
            gather_copy(sorted_tok[jnp.minimum(src0 + r, n_assign - 1)], r).start(priority=p)

        if unrolled:
            for k in range(count):
                one(lo + k, k % 2)
        else:
            def body(q, c):
                for p in range(2):
                    one(lo + 2 * q + p, p)
                return c
            lax.fori_loop(0, count // 2, body, 0)

    def wait_rows(item):
        def body(j, c):
            pltpu.make_async_copy(u_hbm.at[pl.ds(0, XS)], stage.at[pl.ds(0, XS)], sem_in).wait()
            return c
        lax.fori_loop(0, gathered_subs(item), body, 0)

    @pl.when(jnp.logical_and(i == 0, f == 0))
    def _():
        issue_rows(0, 0, gathered_subs(0) * SUB, False)

    @pl.when(f == n_ff - 1)
    def _():
        issue_rows(nxt, n_ff * SUB, (gathered_subs(nxt) - n_ff) * SUB, False)

    lane = lax.broadcasted_iota(jnp.int32, (1, 2 * FF_TILE), 1)
    even = (lane % 2) == 0
    bga = bga_ref[...]
    bgb = bgb_ref[...]
    W2 = 2 * FF_TILE

    @pl.when(f == 0)
    def _():
        wait_rows(i)
        half = D_MODEL // 2

        def unpack(j, c):
            rows = pl.ds(pl.multiple_of(j * SUB, SUB), SUB)
            for s in range(PS):
                p = stage[pl.ds(j * XS + s, SUB, stride=PS), :]
                lo = lax.bitcast_convert_type(p << 16, F32)
                hi = lax.bitcast_convert_type(p & jnp.uint32(0xFFFF0000), F32)
                xbuf[rows, s * LANES:(s + 1) * LANES] = lo.astype(BF16)
                xbuf[rows, half + s * LANES:half + (s + 1) * LANES] = hi.astype(BF16)
            acc[rows, :] = jnp.zeros((SUB, D_MODEL), F32)
            return c
        lax.fori_loop(0, sum(jnp.where(c, m, 0) for m, c in zip(ITEM_VARIANTS, covers)), unpack, 0)

    @pl.when(nsub == 0)
    def _():
        issue_rows(nxt, f * SUB, SUB, False)

    def block(rows):
        xs = xbuf[0:rows, :]
        wa_s[...] = wga_ref[...].astype(BF16)
        ga = _dot(xs, wa_s[...]) + bga
        issue_rows(nxt, f * SUB, SUB, True)
        wb_s[...] = wgb_ref[...].astype(BF16)
        gb = _dot(xs, wb_s[...]) + bgb
        wd_s[...] = pltpu.bitcast(_pack_bf16_pairs(wda_ref[...], wdb_ref[...]), BF16)
        gate = jnp.where(even, ga, pltpu.roll(gb, 1, 1))
        up = jnp.where(even, pltpu.roll(ga, W2 - 1, 1), gb)
        gate = jnp.minimum(gate, SWIGLU_LIMIT)
        up = jnp.clip(up, -SWIGLU_LIMIT, SWIGLU_LIMIT)
        act = ((up + 1.0) * (gate * jax.nn.sigmoid(gate * SWIGLU_ALPHA))).astype(BF16)
        for n in range(0, D_MODEL, DOWN_TILE):
            acc[0:rows, n:n + DOWN_TILE] += _dot(act, wd_s[:, n:n + DOWN_TILE])

    for m, cover in zip(ITEM_VARIANTS, covers):
        @pl.when(cover)
        def _(m=m):
            block(m * SUB)

    @pl.when(f == n_ff - 1)
    def _():
        @pl.when(i == n_items - 1)
        def _():
            wait_rows(nxt)

        def stage_out(q, c):
            slot = q % 2

            @pl.when(q >= 2)
            def _():
                y_copy(slot, 0).wait()
            val = acc[pl.ds(pl.multiple_of(q * SUB, SUB), SUB), :] + bd_ref[...]
            for s in range(ROW_SLABS):
                ybuf[pl.ds(slot * YS + s, SUB, stride=ROW_SLABS), :] = val[:, s * LANES:(s + 1) * LANES]
            y_copy(slot, q).start()
            return c
        lax.fori_loop(0, nsub, stage_out, 0)

        @pl.when(nsub > 0)
        def _():
            y_copy(0, 0).wait()

        @pl.when(nsub > 1)
        def _():
            y_copy(1, 0).wait()

    @pl.when(jnp.logical_and(f == n_ff - 1, nzero > 0))
    def _():
        ybuf[0:YS, :] = jnp.zeros((YS, LANES), F32)

        def start(j, c):
            y_copy(0, j).start()
            return c
        lax.fori_loop(0, nzero, start, 0)

        def wait(j, c):
            y_copy(0, 0).wait()
            return c
        lax.fori_loop(0, nzero, wait, 0)


def _experts_call(u_packed, sorted_tok, n_rows, items, w_gate_up, b_gate_up, w_down, b_down, y_in=None):
    n_ff = (D_FF // 2) // FF_TILE
    assert ITEM_VARIANTS[0] >= n_ff and ITEM_VARIANTS[-1] * ROW_BLK == ITEM_ROWS
    W2 = 2 * FF_TILE
    item_e, item_row, item_nsub, item_nzero, item_live, item_src = items
    n_items = item_e.shape[0]
    aliased = y_in is not None

    def ff(i, f, live):
        return jnp.where(live[i] > 0, f, n_ff - 1)

    grid_spec = pltpu.PrefetchScalarGridSpec(
        num_scalar_prefetch=7,
        grid=(n_items, n_ff),
        in_specs=[
            pl.BlockSpec(memory_space=pl.ANY),
            pl.BlockSpec(memory_space=pl.ANY),
            pl.BlockSpec(memory_space=pl.ANY),
            pl.BlockSpec((None, 1, W2), lambda i, f, e, r, n, z, lv, sr, st: (e[i], 0, ff(i, f, lv))),
            pl.BlockSpec((None, 1, W2), lambda i, f, e, r, n, z, lv, sr, st: (e[i], 0, n_ff + ff(i, f, lv))),
            pl.BlockSpec((None, 1, D_MODEL), lambda i, f, e, r, n, z, lv, sr, st: (e[i], 0, 0)),
        ] + ([pl.BlockSpec(memory_space=pl.ANY)] if aliased else []),
        out_specs=pl.BlockSpec(memory_space=pl.ANY),
        scratch_shapes=[pltpu.VMEM((ITEM_ROWS * PACK_SLABS, LANES), jnp.uint32),
                        pltpu.VMEM((ITEM_ROWS, D_MODEL), BF16),
                        pltpu.VMEM((ITEM_ROWS, D_MODEL), F32),
                        pltpu.VMEM((2 * ROW_BLK * ROW_SLABS, LANES), F32),
                        pltpu.VMEM((WEIGHT_SLOTS, 2, D_MODEL, W2), F32),
                        pltpu.VMEM((WEIGHT_SLOTS, 2, FF_TILE, D_MODEL), F32),
                        pltpu.VMEM((D_MODEL, W2), BF16),
                        pltpu.VMEM((D_MODEL, W2), BF16),
                        pltpu.VMEM((W2, D_MODEL), BF16),
                        pltpu.SemaphoreType.DMA(()),
                        pltpu.SemaphoreType.DMA((2,)),
                        pltpu.SemaphoreType.DMA((WEIGHT_SLOTS,))],
    )
    args = (item_e, item_row, item_nsub, item_nzero, item_live, item_src, sorted_tok, u_packed,
            w_gate_up, w_down, b_gate_up.reshape(N_EXPERTS, 1, 2 * D_FF),
            b_gate_up.reshape(N_EXPERTS, 1, 2 * D_FF), b_down.reshape(N_EXPERTS, 1, D_MODEL))
    return pl.pallas_call(
        functools.partial(_expert_kernel, n_ff, n_items, aliased),
        grid_spec=grid_spec,
        out_shape=jax.ShapeDtypeStruct((n_rows * ROW_SLABS, LANES), F32),
        input_output_aliases={len(args): 0} if aliased else {},
        compiler_params=_params(("arbitrary", "arbitrary")),
        name="experts_overflow" if aliased else "experts",
    )(*args, *((y_in,) if aliased else ()))


def _experts(u_packed, sorted_tok, n_rows, items, n_used, w_gate_up, b_gate_up, w_down, b_down):
    n_main = N_EXPERTS + 1
    weights = (w_gate_up, b_gate_up, w_down, b_down)
    y = _experts_call(u_packed, sorted_tok, n_rows, tuple(a[:n_main] for a in items), *weights)
    rest = tuple(a[n_main:] for a in items)
    return lax.cond(n_used > n_main,
                    lambda y_: _experts_call(u_packed, sorted_tok, n_rows, rest, *weights, y_in=y_),
                    lambda y_: y_, y)


def _combine_kernel(tm, n_steps, dest_ref, w_ref, y_hbm, h_ref, lnw_ref, o_ref, buf, osum, sem):
    i = pl.program_id(0)

    def copy(step, slot, t, k):
        d = dest_ref[(step * tm + t) * TOP_K + k]
        return pltpu.make_async_copy(y_hbm.at[d], buf.at[slot, k * tm + t], sem.at[slot])

    def issue(step, slot):
        def body(t, c):
            for k in range(TOP_K):
                copy(step, slot, t, k).start(priority=k % 2)
            return c
        lax.fori_loop(0, tm, body, 0, unroll=4)

    @pl.when(i == 0)
    def _():
        issue(0, 0)

    @pl.when(i + 1 < n_steps)
    def _():
        issue(i + 1, (i + 1) % 2)

    slot = i % 2

    pltpu.make_async_copy(y_hbm.at[pl.ds(0, TOP_K * tm)], buf.at[slot], sem.at[slot]).wait()

    def token(t, c):
        a = h_ref[t]
        for k in range(TOP_K):
            a = a + w_ref[(i * tm + t) * TOP_K + k] * buf[slot, k * tm + t]
        osum[pl.ds(pl.multiple_of(t * ROW_SLABS, ROW_SLABS), ROW_SLABS), :] = a
        return c
    lax.fori_loop(0, tm, token, 0, unroll=4)

    h = osum[...].reshape(tm, ROW_SLABS, LANES)
    ms = jnp.mean(jnp.mean(h * h, axis=2, keepdims=True), axis=1, keepdims=True)
    osum[...] = (h * lax.rsqrt(ms + EPS) * lnw_ref[...]).reshape(tm * ROW_SLABS, LANES)
    for s in range(ROW_SLABS):
        o_ref[:, s * LANES:(s + 1) * LANES] = osum[pl.ds(s, tm, stride=ROW_SLABS), :]


def _combine(dest, wflat, y3, h3, ln_w):
    T = h3.shape[0]
    tm = 128
    n_steps = T // tm
    grid_spec = pltpu.PrefetchScalarGridSpec(
        num_scalar_prefetch=2,
        grid=(n_steps,),
        in_specs=[pl.BlockSpec(memory_space=pl.ANY),
                  pl.BlockSpec((tm, ROW_SLABS, LANES), lambda i, d, w: (i, 0, 0)),
                  pl.BlockSpec((1, ROW_SLABS, LANES), lambda i, d, w: (0, 0, 0))],
        out_specs=pl.BlockSpec((tm, D_MODEL), lambda i, d, w: (i, 0)),
        scratch_shapes=[pltpu.VMEM((2, TOP_K * tm, ROW_SLABS, LANES), F32),
                        pltpu.VMEM((tm * ROW_SLABS, LANES), F32),
                        pltpu.SemaphoreType.DMA((2,))],
    )
    return pl.pallas_call(
        functools.partial(_combine_kernel, tm, n_steps),
        grid_spec=grid_spec,
        out_shape=jax.ShapeDtypeStruct((T, D_MODEL), F32),
        compiler_params=_params(("arbitrary",)),
        name="combine",
    )(dest, wflat, y3, h3, ln_w.reshape(1, ROW_SLABS, LANES))


def _route(logits, n_rows, n_items):
    T = logits.shape[0]
    top_logits, top_idx = lax.top_k(logits, TOP_K)
    top_w = jax.nn.softmax(top_logits, axis=-1)
    e_flat = top_idx.reshape(-1).astype(jnp.int32)
    onehot = (e_flat[:, None] == jnp.arange(N_EXPERTS, dtype=jnp.int32)[None, :]).astype(jnp.int32)
    csum = jnp.cumsum(onehot, axis=0)
    counts = csum[-1]
    padded = (counts + ROW_BLK - 1) // ROW_BLK * ROW_BLK
    pend = jnp.cumsum(padded)
    pstart = pend - padded
    dest = jnp.sum(onehot * (pstart[None, :] + csum - onehot), axis=1).astype(jnp.int32)

    per_e = (padded + ITEM_ROWS - 1) // ITEM_ROWS
    iend = jnp.cumsum(per_e)
    istart = iend - per_e
    ii = jnp.arange(n_items, dtype=jnp.int32)
    total = iend[-1]
    live = (ii < total).astype(jnp.int32)
    ic = jnp.minimum(ii, total - 1)
    ie = jnp.minimum(jnp.sum(ic[:, None] >= iend[None, :], axis=1), N_EXPERTS - 1).astype(jnp.int32)
    within = ic - istart[ie]
    irow = (pstart[ie] + within * ITEM_ROWS).astype(jnp.int32)
    insub = jnp.minimum((padded[ie] - within * ITEM_ROWS) // ROW_BLK, ITEM_ROWS // ROW_BLK).astype(jnp.int32)
    insub = insub * live
    tail_rows = n_rows - pend[-1]
    tail = jnp.logical_and(ii == total, tail_rows > 0)
    inzero = jnp.where(tail, tail_rows // ROW_BLK, 0).astype(jnp.int32)
    irow = jnp.where(tail, pend[-1], irow).astype(jnp.int32)
    n_used = total + (tail_rows > 0).astype(jnp.int32)
    n_assign = T * TOP_K
    assert N_EXPERTS * n_assign < 2 ** 31
    order = jnp.sort(e_flat * n_assign + jnp.arange(n_assign, dtype=jnp.int32)) % n_assign
    sorted_tok = (order // TOP_K).astype(jnp.int32)
    cstart = jnp.cumsum(counts) - counts
    isrc = ((cstart[ie] + within * ITEM_ROWS) * live).astype(jnp.int32)
    return dest, top_w.reshape(-1).astype(F32), (ie, irow, insub, inzero, live, isrc), n_used, sorted_tok


def kernel(x, positions, ln_mix_w, w_in, conv_w, conv_b, dt_bias, a_log, d_skip, ssm_norm_w, w_out,
           ln_ffn_w, w_router, b_router, w_gate_up, b_gate_up, w_down, b_down, ln_final_w):
    B, L, _ = x.shape
    T = B * L
    assert B == 1 and T % 1024 == 0
    x2 = x.reshape(T, D_MODEL)
    half = RET_HEAD_DIM // 2
    inv_freq = (ROPE_BASE ** (-jnp.arange(half, dtype=F32) / half)).reshape(1, half)
    pos_col = positions.reshape(T, 1).astype(F32)

    proj = _inproj(x2, ln_mix_w[0], jnp.swapaxes(w_in[0], 0, 1))
    ret = _retention(proj, pos_col, inv_freq)
    ssm = _ssd(proj, conv_w[0], conv_b[0], dt_bias[0], a_log[0], d_skip[0], ssm_norm_w[0])
    h_slabs, u_packed, logits = _outproj(ret, ssm, w_out[0].astype(BF16), x2, ln_ffn_w[0], w_router[0],
                                         b_router[0])

    n_rows = -(-(T * TOP_K + N_EXPERTS * (ROW_BLK - 1)) // ROW_BLK) * ROW_BLK
    n_items = N_EXPERTS + 1 + n_rows // ITEM_ROWS
    dest, wflat, items, n_used, sorted_tok = _route(logits, n_rows, n_items)

    y_rows = _experts(u_packed, sorted_tok, n_rows, items, n_used, w_gate_up[0], b_gate_up[0], w_down[0],
                      b_down[0])
    out = _combine(dest, wflat, y_rows.reshape(n_rows, ROW_SLABS, LANES),
                   h_slabs.reshape(T, ROW_SLABS, LANES), ln_final_w)
    return out.reshape(B, L, D_MODEL)
```

```python
import functools

import numpy as np
import jax
import jax.numpy as jnp
from jax import lax
from jax.experimental import pallas as pl
from jax.experimental.pallas import tpu as pltpu

F32 = jnp.float32
BF16 = jnp.bfloat16

D_MODEL = 2048
RET_HEADS = 4
RET_HEAD_DIM = 256
RET_WIDTH = RET_HEADS * RET_HEAD_DIM
SSM_WIDTH = D_MODEL - RET_WIDTH
SSM_HEAD_DIM = 64
SSM_HEADS = SSM_WIDTH // SSM_HEAD_DIM
SSM_GROUPS = 2
SSM_STATE = 128
CONV_WIDTH = 4
XBC_WIDTH = SSM_WIDTH + 2 * SSM_GROUPS * SSM_STATE
D_IN_PROJ = 4 * RET_WIDTH + SSM_WIDTH + XBC_WIDTH + SSM_HEADS
ROPE_BASE = 10000.0
N_EXPERTS = 32
TOP_K = 4
D_FF = D_MODEL
SWIGLU_LIMIT = 7.0
SWIGLU_ALPHA = 1.702
EPS = 1e-6

LANES = 128
VMEM_LIMIT = 56 * 1024 * 1024

RET_CHUNK = 256
SSD_CHUNK = 128
ROW_BLK = 128
ITEM_ROWS = 1536
ITEM_VARIANTS = (8, 9, 10, 12)
FF_TILE = 128
DOWN_TILE = 512
WEIGHT_SLOTS = 3
ROW_SLABS = D_MODEL // LANES
PACK_SLABS = ROW_SLABS // 2


def _params(sem, **kw):
    return pltpu.CompilerParams(dimension_semantics=sem, vmem_limit_bytes=VMEM_LIMIT, **kw)


def _dot(a, b):
    return jnp.dot(a, b, preferred_element_type=F32)


def _dot_nt(a, b):
    return lax.dot_general(a, b, (((1,), (1,)), ((), ())), preferred_element_type=F32)


def _dot_tn(a, b):
    return lax.dot_general(a, b, (((0,), (0,)), ((), ())), preferred_element_type=F32)


def _split3(x):
    hi = x.astype(BF16)
    r = x - hi.astype(F32)
    mid = r.astype(BF16)
    lo = (r - mid.astype(F32)).astype(BF16)
    return hi, mid, lo


def _dot_exact_rhs01(x, m01):
    hi, mid, lo = _split3(x)
    return _dot(hi, m01) + _dot(mid, m01) + _dot(lo, m01)


def _dot_exact_lhs01(m01, x):
    hi, mid, lo = _split3(x)
    return _dot(m01, hi) + _dot(m01, mid) + _dot(m01, lo)


def _silu(x):
    return x * jax.nn.sigmoid(x)


def _pack_bf16_pairs(lo, hi):
    lo_bits = lax.bitcast_convert_type(lo.astype(BF16).astype(F32), jnp.uint32)
    hi_bits = lax.bitcast_convert_type(hi.astype(BF16).astype(F32), jnp.uint32)
    return hi_bits | (lo_bits >> 16)


INPROJ_CHUNKS = 4


def _inproj_kernel(x_hbm, lnw_ref, w_ref, o_ref, xs_ref, u_ref, sem):
    tm = u_ref.shape[0]
    rows = tm // INPROJ_CHUNKS
    i = pl.program_id(0)

    @pl.when(pl.program_id(1) == 0)
    def _():
        def copy(c):
            return pltpu.make_async_copy(x_hbm.at[pl.ds(i * tm + c * rows, rows)], xs_ref.at[c % 2], sem.at[c % 2])

        copy(0).start()
        copy(1).start()
        for c in range(INPROJ_CHUNKS):
            copy(c).wait()
            x = xs_ref[c % 2]
            ms = jnp.mean(x * x, axis=-1, keepdims=True)
            u_ref[c * rows:(c + 1) * rows, :] = (x * lax.rsqrt(ms + EPS) * lnw_ref[...]).astype(BF16)
            if c + 2 < INPROJ_CHUNKS:
                copy(c + 2).start()

    tn = o_ref.shape[1]
    tail = D_IN_PROJ % tn
    if 0 < tail <= LANES:
        last = pl.num_programs(1) - 1

        @pl.when(pl.program_id(1) < last)
        def _():
            o_ref[...] = _dot_nt(u_ref[...], w_ref[...].astype(BF16))

        @pl.when(pl.program_id(1) == last)
        def _():
            o_ref[:, 0:LANES] = _dot_nt(u_ref[...], w_ref[0:LANES, :].astype(BF16))
    else:
        o_ref[...] = _dot_nt(u_ref[...], w_ref[...].astype(BF16))


def _inproj(x2, ln_w, w_in_t):
    T = x2.shape[0]
    tm, tn = (2048 if T % 2048 == 0 else 1024), 512
    return pl.pallas_call(
        _inproj_kernel,
        grid=(T // tm, pl.cdiv(D_IN_PROJ, tn)),
        in_specs=[pl.BlockSpec(memory_space=pl.ANY),
                  pl.BlockSpec((1, D_MODEL), lambda i, j: (0, 0)),
                  pl.BlockSpec((tn, D_MODEL), lambda i, j: (j, 0))],
        out_specs=pl.BlockSpec((tm, tn), lambda i, j: (i, j)),
        out_shape=jax.ShapeDtypeStruct((T, D_IN_PROJ), F32),
        scratch_shapes=[pltpu.VMEM((2, tm // INPROJ_CHUNKS, D_MODEL), F32),
                        pltpu.VMEM((tm, D_MODEL), BF16),
                        pltpu.SemaphoreType.DMA((2,))],
        compiler_params=_params(("arbitrary", "arbitrary")),
        name="inproj",
    )(x2, ln_w.reshape(1, D_MODEL), w_in_t)


def _retention_tables():
    C = RET_CHUNK
    h = np.arange(RET_HEADS, dtype=np.float64)
    log_gamma = np.log1p(-np.exp2(-5.0 - h))
    idx = np.arange(C, dtype=np.float64)
    rel = idx[:, None] - idx[None, :]
    intra = np.where(rel >= 0, np.exp(log_gamma[:, None, None] * np.maximum(rel, 0.0)), 0.0)
    q_decay = np.exp(log_gamma[:, None] * (idx + 1.0))
    k_decay = np.exp(log_gamma[:, None] * (C - 1.0 - idx))
    chunk_decay = np.exp(log_gamma * C)
    qd = np.broadcast_to(q_decay[:, :, None], (RET_HEADS, C, RET_HEAD_DIM))
    kd = np.broadcast_to(k_decay[:, :, None], (RET_HEADS, C, RET_HEAD_DIM))
    return (jnp.asarray(intra, F32), jnp.asarray(qd, F32), jnp.asarray(kd, F32),
            [float(c) for c in chunk_decay])


def _retention_kernel(chunk_decay, pos_ref, invf_ref, q_ref, k_ref, v_ref, g_ref,
                      intra_ref, qd_ref, kd_ref, o_ref, state_ref):
    @pl.when(pl.program_id(0) == 0)
    def _():
        state_ref[...] = jnp.zeros_like(state_ref)

    half = RET_HEAD_DIM // 2
    ang = pos_ref[...] * invf_ref[...]
    cos = jnp.cos(ang)
    sin = jnp.sin(ang)

    def rope(t):
        t1, t2 = t[:, :half], t[:, half:]
        return jnp.concatenate([t1 * cos - t2 * sin, t2 * cos + t1 * sin], axis=-1)

    for h in range(RET_HEADS):
        sl = slice(h * RET_HEAD_DIM, (h + 1) * RET_HEAD_DIM)
        q = rope(q_ref[:, sl])
        k = rope(k_ref[:, sl]) * (RET_HEAD_DIM ** -0.5)
        v = v_ref[:, sl].astype(BF16)
        state = state_ref[h]
        scores = _dot_nt(q.astype(BF16), k.astype(BF16)) * intra_ref[h]
        inner = _dot(scores.astype(BF16), v)
        cross = _dot((q * qd_ref[h]).astype(BF16), state.astype(BF16))
        state_ref[h] = chunk_decay[h] * state + _dot_tn((k * kd_ref[h]).astype(BF16), v)
        o = inner + cross
        o = o * lax.rsqrt(jnp.mean(o * o, axis=-1, keepdims=True) + EPS)
        o_ref[:, sl] = (o * _silu(g_ref[:, sl])).astype(o_ref.dtype)


def _retention(proj, pos_col, inv_freq):
    T = proj.shape[0]
    C = RET_CHUNK
    intra, qd, kd, chunk_decay = _retention_tables()
    col = lambda j: pl.BlockSpec((C, RET_WIDTH), lambda c, j=j: (c, j))
    const3 = lambda shape: pl.BlockSpec(shape, lambda c: (0, 0, 0))
    return pl.pallas_call(
        functools.partial(_retention_kernel, chunk_decay),
        grid=(T // C,),
        in_specs=[pl.BlockSpec((C, 1), lambda c: (c, 0)),
                  pl.BlockSpec((1, RET_HEAD_DIM // 2), lambda c: (0, 0)),
                  col(0), col(1), col(2), col(3),
                  const3((RET_HEADS, C, C)),
                  const3((RET_HEADS, C, RET_HEAD_DIM)),
                  const3((RET_HEADS, C, RET_HEAD_DIM))],
        out_specs=pl.BlockSpec((C, RET_WIDTH), lambda c: (c, 0)),
        out_shape=jax.ShapeDtypeStruct((T, RET_WIDTH), BF16),
        scratch_shapes=[pltpu.VMEM((RET_HEADS, RET_HEAD_DIM, RET_HEAD_DIM), F32)],
        compiler_params=_params(("arbitrary",)),
        name="retention",
    )(pos_col, inv_freq, proj, proj, proj, proj, intra, qd, kd)


def _ssd_kernel(xs0_ref, xs1_ref, bc_ref, z_ref, dt_ref, convw_ref, convb_ref, dtb_ref, a_ref,
                dskip_ref, normw_ref, expand_ref, o_ref, ext_ref, state_ref):
    C = SSD_CHUNK
    HW = SSM_WIDTH // SSM_GROUPS
    CARRY = 8

    @pl.when(pl.program_id(0) == 0)
    def _():
        ext_ref[0:CARRY, :] = jnp.zeros((CARRY, XBC_WIDTH), F32)
        state_ref[...] = jnp.zeros_like(state_ref)

    ext_ref[CARRY:CARRY + C, 0:HW] = xs0_ref[...]
    ext_ref[CARRY:CARRY + C, HW:2 * HW] = xs1_ref[...]
    ext_ref[CARRY:CARRY + C, 2 * HW:3 * HW] = bc_ref[...]
    conv = convb_ref[...]
    for k in range(CONV_WIDTH):
        off = CARRY - (CONV_WIDTH - 1) + k
        conv = conv + convw_ref[k:k + 1, :] * ext_ref[off:off + C, :]
    ext_ref[0:CARRY, :] = ext_ref[C:C + CARRY, :]
    xbc = _silu(conv)
    xs = xbc[:, :SSM_WIDTH]

    lane = lax.broadcasted_iota(jnp.int32, (1, LANES), 1)
    dt_raw = jnp.where(lane < SSM_HEADS, dt_ref[...], 0.0) + dtb_ref[...]
    dt = jnp.maximum(dt_raw, 0.0) + jnp.log1p(jnp.exp(-jnp.abs(dt_raw)))
    dta = dt * a_ref[...]

    row = lax.broadcasted_iota(jnp.int32, (C, C), 0)
    colm = lax.broadcasted_iota(jnp.int32, (C, C), 1)
    tril = row >= colm
    a_cum = _dot_exact_lhs01(jnp.where(tril, 1.0, 0.0).astype(BF16), dta)
    a_cum_t = a_cum.T

    expand = expand_ref[...]
    a_exp = _dot_exact_rhs01(a_cum, expand)
    dt_exp = _dot_exact_rhs01(dt, expand)
    a_last = a_exp[C - 1:C, :]
    decay_in = jnp.exp(a_exp)
    decay_out = jnp.exp(a_last - a_exp)
    chunk_decay = jnp.exp(a_last)
    xdt = xs * dt_exp

    lane2 = lax.broadcasted_iota(jnp.int32, (1, LANES), 1)
    lo_head = lane2 < SSM_HEAD_DIM
    ys = []
    for g in range(SSM_GROUPS):
        gs = slice(g * HW, (g + 1) * HW)
        b_g = xbc[:, SSM_WIDTH + g * SSM_STATE:SSM_WIDTH + (g + 1) * SSM_STATE].astype(BF16)
        c0 = SSM_WIDTH + SSM_GROUPS * SSM_STATE
        c_g = xbc[:, c0 + g * SSM_STATE:c0 + (g + 1) * SSM_STATE].astype(BF16)
        cb = _dot_nt(c_g, b_g)
        state = state_ref[g]
        y_off = _dot(c_g, state.astype(BF16)) * decay_in[:, gs]
        xw = (xdt[:, gs] * decay_out[:, gs]).astype(BF16)
        state_ref[g] = chunk_decay[:, gs] * state + _dot_tn(b_g, xw)
        slabs = []
        for s in range(HW // LANES):
            xd = xdt[:, g * HW + s * LANES:g * HW + (s + 1) * LANES]
            acc = None
            for e in range(2):
                hh = g * (SSM_HEADS // SSM_GROUPS) + 2 * s + e
                seg = a_cum[:, hh:hh + 1] - a_cum_t[hh:hh + 1, :]
                m = cb * jnp.exp(jnp.where(tril, seg, -jnp.inf))
                xm = jnp.where(lo_head if e == 0 else jnp.logical_not(lo_head), xd, 0.0)
                part = _dot(m.astype(BF16), xm.astype(BF16))
                acc = part if acc is None else acc + part
            slabs.append(acc)
        ys.append(jnp.concatenate(slabs, axis=-1) + y_off)
    y = jnp.concatenate(ys, axis=-1) + dskip_ref[...] * xs
    y = y * _silu(z_ref[...])
    outs = []
    for g in range(SSM_GROUPS):
        yg = y[:, g * HW:(g + 1) * HW]
        outs.append(yg * lax.rsqrt(jnp.mean(yg * yg, axis=-1, keepdims=True) + EPS))
    o_ref[...] = (jnp.concatenate(outs, axis=-1) * normw_ref[...]).astype(o_ref.dtype)


def _ssd(proj, conv_w, conv_b, dt_bias, a_log, d_skip, ssm_norm_w):
    T = proj.shape[0]
    C = SSD_CHUNK
    HW = SSM_WIDTH // SSM_GROUPS
    xbc0 = (4 * RET_WIDTH + SSM_WIDTH) // HW
    dt0 = (D_IN_PROJ - SSM_HEADS) // LANES
    pad = lambda v: jnp.zeros((1, LANES), F32).at[0, :SSM_HEADS].set(v.astype(F32))
    a_neg = pad(-jnp.exp(a_log.astype(F32)))
    expand_np = np.zeros((LANES, SSM_WIDTH), np.float32)
    for hh in range(SSM_HEADS):
        expand_np[hh, hh * SSM_HEAD_DIM:(hh + 1) * SSM_HEAD_DIM] = 1.0
    expand = jnp.asarray(expand_np, BF16)
    dskip_exp = jnp.repeat(d_skip.astype(F32), SSM_HEAD_DIM).reshape(1, SSM_WIDTH)
    const = lambda shape: pl.BlockSpec(shape, lambda c: (0, 0))
    return pl.pallas_call(
        _ssd_kernel,
        grid=(T // C,),
        in_specs=[pl.BlockSpec((C, HW), lambda c: (c, xbc0)),
                  pl.BlockSpec((C, HW), lambda c: (c, xbc0 + 1)),
                  pl.BlockSpec((C, HW), lambda c: (c, xbc0 + 2)),
                  pl.BlockSpec((C, SSM_WIDTH), lambda c: (c, 4 * RET_WIDTH // SSM_WIDTH)),
                  pl.BlockSpec((C, LANES), lambda c: (c, dt0)),
                  const((CONV_WIDTH, XBC_WIDTH)), const((1, XBC_WIDTH)),
                  const((1, LANES)), const((1, LANES)),
                  const((1, SSM_WIDTH)), const((1, SSM_WIDTH)),
                  const((LANES, SSM_WIDTH))],
        out_specs=pl.BlockSpec((C, SSM_WIDTH), lambda c: (c, 0)),
        out_shape=jax.ShapeDtypeStruct((T, SSM_WIDTH), BF16),
        scratch_shapes=[pltpu.VMEM((C + 8, XBC_WIDTH), F32),
                        pltpu.VMEM((SSM_GROUPS, SSM_STATE, HW), F32)],
        compiler_params=_params(("arbitrary",)),
        name="ssd",
    )(proj, proj, proj, proj, proj, conv_w, conv_b.reshape(1, XBC_WIDTH), pad(dt_bias), a_neg,
      dskip_exp, ssm_norm_w.reshape(1, SSM_WIDTH), expand)


def _outproj_kernel(ret_ref, ssm_ref, w_ref, x_ref, lnw_ref, wr_ref, br_ref, hs_ref, up_ref, ti_ref, tw_ref):
    tm = x_ref.shape[0]
    h = (x_ref[...] + _dot(ret_ref[...], w_ref[0:RET_WIDTH, :])
         + _dot(ssm_ref[...], w_ref[RET_WIDTH:D_MODEL, :]))
    for s in range(ROW_SLABS):
        hs_ref[pl.ds(s, tm, stride=ROW_SLABS), :] = h[:, s * LANES:(s + 1) * LANES]
    u = h * lax.rsqrt(jnp.mean(h * h, axis=-1, keepdims=True) + EPS) * lnw_ref[...]
    packed = _pack_bf16_pairs(u[:, :D_MODEL // 2], u[:, D_MODEL // 2:])
    for s in range(PACK_SLABS):
        up_ref[pl.ds(s, tm, stride=PACK_SLABS), :] = packed[:, s * LANES:(s + 1) * LANES]
    E = N_EXPERTS
    uh, um, ul = _split3(u)
    ph = _dot_nt(wr_ref[...], uh)
    pm = _dot_nt(wr_ref[...], um)
    pw = _dot_nt(wr_ref[...], ul)
    lg = (ph[0:E] + (ph[E:2 * E] + pm[0:E]) + (ph[2 * E:3 * E] + pm[E:2 * E] + pw[0:E])) + br_ref[...]
    expert = lax.broadcasted_iota(jnp.int32, lg.shape, 0)
    slot = lax.broadcasted_iota(jnp.int32, (TOP_K, tm), 0)
    top_l = jnp.zeros((TOP_K, tm), F32)
    top_i = jnp.zeros((TOP_K, tm), jnp.int32)
    work = lg
    for k in range(TOP_K):
        best = jnp.max(work, axis=0, keepdims=True)
        which = jnp.min(jnp.where(work == best, expert, E), axis=0, keepdims=True)
        top_l = jnp.where(slot == k, best, top_l)
        top_i = jnp.where(slot == k, which, top_i)
        work = jnp.where(expert == which, -jnp.inf, work)
    p = jnp.exp(top_l - jnp.max(top_l, axis=0, keepdims=True))
    ti_ref[...] = top_i
    tw_ref[...] = p / jnp.sum(p, axis=0, keepdims=True)


def _outproj(ret, ssm, w_out_bf16, x2, ln_w, w_router, b_router):
    T = x2.shape[0]
    tm = 512
    return pl.pallas_call(
        _outproj_kernel,
        grid=(T // tm,),
        in_specs=[pl.BlockSpec((tm, RET_WIDTH), lambda i: (i, 0)),
                  pl.BlockSpec((tm, SSM_WIDTH), lambda i: (i, 0)),
                  pl.BlockSpec((D_MODEL, D_MODEL), lambda i: (0, 0)),
                  pl.BlockSpec((tm, D_MODEL), lambda i: (i, 0)),
                  pl.BlockSpec((1, D_MODEL), lambda i: (0, 0)),
                  pl.BlockSpec((3 * N_EXPERTS, D_MODEL), lambda i: (0, 0)),
                  pl.BlockSpec((N_EXPERTS, 1), lambda i: (0, 0))],
        out_specs=[pl.BlockSpec((tm * ROW_SLABS, LANES), lambda i: (i, 0)),
                   pl.BlockSpec((tm * PACK_SLABS, LANES), lambda i: (i, 0)),
                   pl.BlockSpec((TOP_K, tm), lambda i: (0, i)),
                   pl.BlockSpec((TOP_K, tm), lambda i: (0, i))],
        out_shape=[jax.ShapeDtypeStruct((T * ROW_SLABS, LANES), F32),
                   jax.ShapeDtypeStruct((T * PACK_SLABS, LANES), jnp.uint32),
                   jax.ShapeDtypeStruct((TOP_K, T), jnp.int32),
                   jax.ShapeDtypeStruct((TOP_K, T), F32)],
        compiler_params=_params(("parallel",)),
        name="outproj",
    )(ret, ssm, w_out_bf16, x2, ln_w.reshape(1, D_MODEL),
      jnp.concatenate([p.T for p in _split3(w_router)], axis=0), b_router.reshape(N_EXPERTS, 1))


def _expert_kernel(n_ff, n_items, aliased, *refs):
    (item_e, item_row, item_nsub, item_nzero, item_live, item_src, sorted_tok,
     u_hbm, wg_hbm, wd_hbm, bgu_ref, bdn_ref) = refs[:12]
    (y_hbm, stage, xbuf, acc, ybuf, wgbuf, wdbuf, wa_s, wb_s, wd_s,
     sem_in, sem_out, sem_w) = refs[13:] if aliased else refs[12:]
    i = pl.program_id(0)
    f = pl.program_id(1)

    step = i * n_ff + f
    live_steps = item_live[0] * n_ff
    for k in range(1, n_items):
        live_steps = live_steps + item_live[k] * n_ff
    W2_ = 2 * FF_TILE

    def tile_copies(g):
        e = item_e[g // n_ff]
        ft = g % n_ff
        slot = g % WEIGHT_SLOTS
        cols = lambda b: pl.ds(pl.multiple_of(b * W2_, W2_), W2_)
        rows = lambda b: pl.ds(pl.multiple_of(b * FF_TILE, FF_TILE), FF_TILE)
        return [pltpu.make_async_copy(wg_hbm.at[e, :, cols(ft)], wgbuf.at[slot, 0], sem_w.at[slot]),
                pltpu.make_async_copy(wg_hbm.at[e, :, cols(n_ff + ft)], wgbuf.at[slot, 1], sem_w.at[slot]),
                pltpu.make_async_copy(wd_hbm.at[e, rows(ft), :], wdbuf.at[slot, 0], sem_w.at[slot]),
                pltpu.make_async_copy(wd_hbm.at[e, rows(n_ff + ft), :], wdbuf.at[slot, 1], sem_w.at[slot])]

    @pl.when(step == 0)
    def _():
        for g in range(WEIGHT_SLOTS - 1):
            @pl.when(g < live_steps)
            def _(g=g):
                for c in tile_copies(g):
                    c.start()

    @pl.when(step < live_steps)
    def _():
        for c in tile_copies(step):
            c.wait()

        @pl.when(step + WEIGHT_SLOTS - 1 < live_steps)
        def _():
            for c in tile_copies(step + WEIGHT_SLOTS - 1):
                c.start()

    wslot = step % WEIGHT_SLOTS
    wga_ref = wgbuf.at[wslot, 0]
    wgb_ref = wgbuf.at[wslot, 1]
    wda_ref = wdbuf.at[wslot, 0]
    wdb_ref = wdbuf.at[wslot, 1]
    nsub = item_nsub[i]
    nzero = item_nzero[i]
    row0 = item_row[i]
    SUB = ROW_BLK
    YS = SUB * ROW_SLABS
    PS = PACK_SLABS
    XS = SUB * PS

    def y_copy(slot, j):
        dst = y_hbm.at[pl.ds(pl.multiple_of((row0 + j * SUB) * ROW_SLABS, YS), YS)]
        return pltpu.make_async_copy(ybuf.at[pl.ds(pl.multiple_of(slot * YS, YS), YS)], dst, sem_out.at[slot])

    def gather_copy(tok, r):
        src = u_hbm.at[pl.ds(pl.multiple_of(tok * PS, PS), PS)]
        return pltpu.make_async_copy(src, stage.at[pl.ds(pl.multiple_of(r * PS, PS), PS)], sem_in)

    n_assign = sorted_tok.shape[0]
    covers = [jnp.logical_and(nsub > lo, nsub <= m) for lo, m in zip((0,) + ITEM_VARIANTS[:-1], ITEM_VARIANTS)]
    nxt = jnp.minimum(i + 1, n_items - 1)

    def gathered_subs(item):
        ns = item_nsub[item]
        size = sum(jnp.where(jnp.logical_and(ns > lo, ns <= m), m, 0)
                   for lo, m in zip((0,) + ITEM_VARIANTS[:-1], ITEM_VARIANTS))
        return jnp.maximum(size, n_ff)

    def issue_rows(item, lo, count, unrolled):
        src0 = item_src[item]

        def one(r, p):
            gather_copy(sorted_tok[jnp.minimum(src0 + r, n_assign - 1)], r).start(priority=p)

        if unrolled:
            for k in range(count):
                one(lo + k, k % 2)
        else:
            def body(q, c):
                for p in range(2):
                    one(lo + 2 * q + p, p)
                return c
            lax.fori_loop(0, count // 2, body, 0)

    def wait_rows(item):
        def body(j, c):
            pltpu.make_async_copy(u_hbm.at[pl.ds(0, XS)], stage.at[pl.ds(0, XS)], sem_in).wait()
            return c
        lax.fori_loop(0, gathered_subs(item), body, 0)

    @pl.when(jnp.logical_and(i == 0, f == 0))
    def _():
        issue_rows(0, 0, gathered_subs(0) * SUB, False)

    @pl.when(f == n_ff - 1)
    def _():
        issue_rows(nxt, n_ff * SUB, (gathered_subs(nxt) - n_ff) * SUB, False)

    lane = lax.broadcasted_iota(jnp.int32, (1, 2 * FF_TILE), 1)
    even = (lane % 2) == 0
    e_row = pl.ds(item_e[i], 1)
    bga = bgu_ref[e_row, pl.ds(pl.multiple_of(f * (2 * FF_TILE), 2 * FF_TILE), 2 * FF_TILE)]
    bgb = bgu_ref[e_row, pl.ds(pl.multiple_of((n_ff + f) * (2 * FF_TILE), 2 * FF_TILE), 2 * FF_TILE)]
    W2 = 2 * FF_TILE

    @pl.when(f == 0)
    def _():
        wait_rows(i)
        half = D_MODEL // 2

        def unpack(j, c):
            rows = pl.ds(pl.multiple_of(j * SUB, SUB), SUB)
            for s in range(PS):
                p = stage[pl.ds(j * XS + s, SUB, stride=PS), :]
                lo = lax.bitcast_convert_type(p << 16, F32)
                hi = lax.bitcast_convert_type(p & jnp.uint32(0xFFFF0000), F32)
                xbuf[rows, s * LANES:(s + 1) * LANES] = lo.astype(BF16)
                xbuf[rows, half + s * LANES:half + (s + 1) * LANES] = hi.astype(BF16)
            acc[rows, :] = jnp.zeros((SUB, D_MODEL), F32)
            return c
        lax.fori_loop(0, sum(jnp.where(c, m, 0) for m, c in zip(ITEM_VARIANTS, covers)), unpack, 0)

    @pl.when(nsub == 0)
    def _():
        issue_rows(nxt, f * SUB, SUB, False)

    def block(rows):
        xs = xbuf[0:rows, :]
        wa_s[...] = wga_ref[...].astype(BF16)
        ga = _dot(xs, wa_s[...]) + bga
        issue_rows(nxt, f * SUB, SUB, True)
        wb_s[...] = wgb_ref[...].astype(BF16)
        gb = _dot(xs, wb_s[...]) + bgb
        wd_s[...] = pltpu.bitcast(_pack_bf16_pairs(wda_ref[...], wdb_ref[...]), BF16)
        gate = jnp.where(even, ga, pltpu.roll(gb, 1, 1))
        up = jnp.where(even, pltpu.roll(ga, W2 - 1, 1), gb)
        gate = jnp.minimum(gate, SWIGLU_LIMIT)
        up = jnp.clip(up, -SWIGLU_LIMIT, SWIGLU_LIMIT)
        act = ((up + 1.0) * (gate * jax.nn.sigmoid(gate * SWIGLU_ALPHA))).astype(BF16)
        for n in range(0, D_MODEL, DOWN_TILE):
            acc[0:rows, n:n + DOWN_TILE] += _dot(act, wd_s[:, n:n + DOWN_TILE])

    for m, cover in zip(ITEM_VARIANTS, covers):
        @pl.when(cover)
        def _(m=m):
            block(m * SUB)

    @pl.when(f == n_ff - 1)
    def _():
        @pl.when(i == n_items - 1)
        def _():
            wait_rows(nxt)

        def stage_out(q, c):
            slot = q % 2

            @pl.when(q >= 2)
            def _():
                y_copy(slot, 0).wait()
            val = acc[pl.ds(pl.multiple_of(q * SUB, SUB), SUB), :] + bdn_ref[e_row, :]
            for s in range(ROW_SLABS):
                ybuf[pl.ds(slot * YS + s, SUB, stride=ROW_SLABS), :] = val[:, s * LANES:(s + 1) * LANES]
            y_copy(slot, q).start()
            return c
        lax.fori_loop(0, nsub, stage_out, 0)

        @pl.when(nsub > 0)
        def _():
            y_copy(0, 0).wait()

        @pl.when(nsub > 1)
        def _():
            y_copy(1, 0).wait()

    @pl.when(jnp.logical_and(f == n_ff - 1, nzero > 0))
    def _():
        ybuf[0:YS, :] = jnp.zeros((YS, LANES), F32)

        def start(j, c):
            y_copy(0, j).start()
            return c
        lax.fori_loop(0, nzero, start, 0)

        def wait(j, c):
            y_copy(0, 0).wait()
            return c
        lax.fori_loop(0, nzero, wait, 0)


def _experts_call(u_packed, sorted_tok, n_rows, items, w_gate_up, b_gate_up, w_down, b_down, y_in=None):
    n_ff = (D_FF // 2) // FF_TILE
    assert ITEM_VARIANTS[0] >= n_ff and ITEM_VARIANTS[-1] * ROW_BLK == ITEM_ROWS
    W2 = 2 * FF_TILE
    item_e, item_row, item_nsub, item_nzero, item_live, item_src = items
    n_items = item_e.shape[0]
    aliased = y_in is not None

    whole = lambda shape: pl.BlockSpec(shape, lambda i, f, e, r, n, z, lv, sr, st: (0, 0))
    grid_spec = pltpu.PrefetchScalarGridSpec(
        num_scalar_prefetch=7,
        grid=(n_items, n_ff),
        in_specs=[
            pl.BlockSpec(memory_space=pl.ANY),
            pl.BlockSpec(memory_space=pl.ANY),
            pl.BlockSpec(memory_space=pl.ANY),
            whole((N_EXPERTS, 2 * D_FF)),
            whole((N_EXPERTS, D_MODEL)),
        ] + ([pl.BlockSpec(memory_space=pl.ANY)] if aliased else []),
        out_specs=pl.BlockSpec(memory_space=pl.ANY),
        scratch_shapes=[pltpu.VMEM((ITEM_ROWS * PACK_SLABS, LANES), jnp.uint32),
                        pltpu.VMEM((ITEM_ROWS, D_MODEL), BF16),
                        pltpu.VMEM((ITEM_ROWS, D_MODEL), F32),
                        pltpu.VMEM((2 * ROW_BLK * ROW_SLABS, LANES), F32),
                        pltpu.VMEM((WEIGHT_SLOTS, 2, D_MODEL, W2), F32),
                        pltpu.VMEM((WEIGHT_SLOTS, 2, FF_TILE, D_MODEL), F32),
                        pltpu.VMEM((D_MODEL, W2), BF16),
                        pltpu.VMEM((D_MODEL, W2), BF16),
                        pltpu.VMEM((W2, D_MODEL), BF16),
                        pltpu.SemaphoreType.DMA(()),
                        pltpu.SemaphoreType.DMA((2,)),
                        pltpu.SemaphoreType.DMA((WEIGHT_SLOTS,))],
    )
    args = (item_e, item_row, item_nsub, item_nzero, item_live, item_src, sorted_tok, u_packed,
            w_gate_up, w_down, b_gate_up, b_down)
    return pl.pallas_call(
        functools.partial(_expert_kernel, n_ff, n_items, aliased),
        grid_spec=grid_spec,
        out_shape=jax.ShapeDtypeStruct((n_rows * ROW_SLABS, LANES), F32),
        input_output_aliases={len(args): 0} if aliased else {},
        compiler_params=_params(("arbitrary", "arbitrary")),
        name="experts_overflow" if aliased else "experts",
    )(*args, *((y_in,) if aliased else ()))


def _experts(u_packed, sorted_tok, n_rows, items, n_used, w_gate_up, b_gate_up, w_down, b_down):
    n_main = N_EXPERTS + 1
    weights = (w_gate_up, b_gate_up, w_down, b_down)
    y = _experts_call(u_packed, sorted_tok, n_rows, tuple(a[:n_main] for a in items), *weights)
    rest = tuple(a[n_main:] for a in items)
    return lax.cond(n_used > n_main,
                    lambda y_: _experts_call(u_packed, sorted_tok, n_rows, rest, *weights, y_in=y_),
                    lambda y_: y_, y)


def _combine_kernel(tm, n_steps, dest_ref, w_ref, y_hbm, h_ref, lnw_ref, o_ref, buf, osum, sem):
    i = pl.program_id(0)

    def copy(step, slot, t, k):
        d = dest_ref[(step * tm + t) * TOP_K + k]
        return pltpu.make_async_copy(y_hbm.at[d], buf.at[slot, k * tm + t], sem.at[slot])

    def issue(step, slot):
        def body(t, c):
            for k in range(TOP_K):
                copy(step, slot, t, k).start(priority=k % 2)
            return c
        lax.fori_loop(0, tm, body, 0, unroll=4)

    @pl.when(i == 0)
    def _():
        issue(0, 0)

    @pl.when(i + 1 < n_steps)
    def _():
        issue(i + 1, (i + 1) % 2)

    slot = i % 2

    pltpu.make_async_copy(y_hbm.at[pl.ds(0, TOP_K * tm)], buf.at[slot], sem.at[slot]).wait()

    def token(t, c):
        a = h_ref[t]
        for k in range(TOP_K):
            a = a + w_ref[(i * tm + t) * TOP_K + k] * buf[slot, k * tm + t]
        osum[pl.ds(pl.multiple_of(t * ROW_SLABS, ROW_SLABS), ROW_SLABS), :] = a
        return c
    lax.fori_loop(0, tm, token, 0, unroll=4)

    h = osum[...].reshape(tm, ROW_SLABS, LANES)
    ms = jnp.mean(jnp.mean(h * h, axis=2, keepdims=True), axis=1, keepdims=True)
    osum[...] = (h * lax.rsqrt(ms + EPS) * lnw_ref[...]).reshape(tm * ROW_SLABS, LANES)
    for s in range(ROW_SLABS):
        o_ref[:, s * LANES:(s + 1) * LANES] = osum[pl.ds(s, tm, stride=ROW_SLABS), :]


def _combine(dest, wflat, y3, h3, ln_w):
    T = h3.shape[0]
    tm = 128
    n_steps = T // tm
    grid_spec = pltpu.PrefetchScalarGridSpec(
        num_scalar_prefetch=2,
        grid=(n_steps,),
        in_specs=[pl.BlockSpec(memory_space=pl.ANY),
                  pl.BlockSpec((tm, ROW_SLABS, LANES), lambda i, d, w: (i, 0, 0)),
                  pl.BlockSpec((1, ROW_SLABS, LANES), lambda i, d, w: (0, 0, 0))],
        out_specs=pl.BlockSpec((tm, D_MODEL), lambda i, d, w: (i, 0)),
        scratch_shapes=[pltpu.VMEM((2, TOP_K * tm, ROW_SLABS, LANES), F32),
                        pltpu.VMEM((tm * ROW_SLABS, LANES), F32),
                        pltpu.SemaphoreType.DMA((2,))],
    )
    return pl.pallas_call(
        functools.partial(_combine_kernel, tm, n_steps),
        grid_spec=grid_spec,
        out_shape=jax.ShapeDtypeStruct((T, D_MODEL), F32),
        compiler_params=_params(("arbitrary",)),
        name="combine",
    )(dest, wflat, y3, h3, ln_w.reshape(1, ROW_SLABS, LANES))


def _route(top_idx, top_w, n_rows, n_items):
    T = top_idx.shape[0]
    e_flat = top_idx.reshape(-1).astype(jnp.int32)
    onehot = (e_flat[:, None] == jnp.arange(N_EXPERTS, dtype=jnp.int32)[None, :]).astype(jnp.int32)
    csum = jnp.cumsum(onehot, axis=0)
    counts = csum[-1]
    padded = (counts + ROW_BLK - 1) // ROW_BLK * ROW_BLK
    pend = jnp.cumsum(padded)
    pstart = pend - padded
    dest = jnp.sum(onehot * (pstart[None, :] + csum - onehot), axis=1).astype(jnp.int32)

    per_e = (padded + ITEM_ROWS - 1) // ITEM_ROWS
    iend = jnp.cumsum(per_e)
    istart = iend - per_e
    ii = jnp.arange(n_items, dtype=jnp.int32)
    total = iend[-1]
    live = (ii < total).astype(jnp.int32)
    ic = jnp.minimum(ii, total - 1)
    ie = jnp.minimum(jnp.sum(ic[:, None] >= iend[None, :], axis=1), N_EXPERTS - 1).astype(jnp.int32)
    within = ic - istart[ie]
    irow = (pstart[ie] + within * ITEM_ROWS).astype(jnp.int32)
    insub = jnp.minimum((padded[ie] - within * ITEM_ROWS) // ROW_BLK, ITEM_ROWS // ROW_BLK).astype(jnp.int32)
    insub = insub * live
    tail_rows = n_rows - pend[-1]
    tail = jnp.logical_and(ii == total, tail_rows > 0)
    inzero = jnp.where(tail, tail_rows // ROW_BLK, 0).astype(jnp.int32)
    irow = jnp.where(tail, pend[-1], irow).astype(jnp.int32)
    n_used = total + (tail_rows > 0).astype(jnp.int32)
    n_assign = T * TOP_K
    assert N_EXPERTS * n_assign < 2 ** 31
    order = jnp.sort(e_flat * n_assign + jnp.arange(n_assign, dtype=jnp.int32)) % n_assign
    sorted_tok = (order // TOP_K).astype(jnp.int32)
    cstart = jnp.cumsum(counts) - counts
    isrc = ((cstart[ie] + within * ITEM_ROWS) * live).astype(jnp.int32)
    return dest, top_w.reshape(-1).astype(F32), (ie, irow, insub, inzero, live, isrc), n_used, sorted_tok


def kernel(x, positions, ln_mix_w, w_in, conv_w, conv_b, dt_bias, a_log, d_skip, ssm_norm_w, w_out,
           ln_ffn_w, w_router, b_router, w_gate_up, b_gate_up, w_down, b_down, ln_final_w):
    B, L, _ = x.shape
    T = B * L
    assert B == 1 and T % 1024 == 0
    x2 = x.reshape(T, D_MODEL)
    half = RET_HEAD_DIM // 2
    inv_freq = (ROPE_BASE ** (-jnp.arange(half, dtype=F32) / half)).reshape(1, half)
    pos_col = positions.reshape(T, 1).astype(F32)

    proj = _inproj(x2, ln_mix_w[0], jnp.swapaxes(w_in[0], 0, 1))
    ret = _retention(proj, pos_col, inv_freq)
    ssm = _ssd(proj, conv_w[0], conv_b[0], dt_bias[0], a_log[0], d_skip[0], ssm_norm_w[0])
    h_slabs, u_packed, top_idx, top_w = _outproj(ret, ssm, w_out[0].astype(BF16), x2, ln_ffn_w[0], w_router[0],
                                                 b_router[0])

    n_rows = -(-(T * TOP_K + N_EXPERTS * (ROW_BLK - 1)) // ROW_BLK) * ROW_BLK
    n_items = N_EXPERTS + 1 + n_rows // ITEM_ROWS
    dest, wflat, items, n_used, sorted_tok = _route(top_idx.T, top_w.T, n_rows, n_items)

    y_rows = _experts(u_packed, sorted_tok, n_rows, items, n_used, w_gate_up[0], b_gate_up[0], w_down[0],
                      b_down[0])
    out = _combine(dest, wflat, y_rows.reshape(n_rows, ROW_SLABS, LANES),
                   h_slabs.reshape(T, ROW_SLABS, LANES), ln_final_w)
    return out.reshape(B, L, D_MODEL)
```

```python
import functools

import numpy as np
import jax
import jax.numpy as jnp
from jax import lax
from jax.experimental import pallas as pl
from jax.experimental.pallas import tpu as pltpu

F32 = jnp.float32
BF16 = jnp.bfloat16

D_MODEL = 2048
RET_HEADS = 4
RET_HEAD_DIM = 256
RET_WIDTH = RET_HEADS * RET_HEAD_DIM
SSM_WIDTH = D_MODEL - RET_WIDTH
SSM_HEAD_DIM = 64
SSM_HEADS = SSM_WIDTH // SSM_HEAD_DIM
SSM_GROUPS = 2
SSM_STATE = 128
CONV_WIDTH = 4
XBC_WIDTH = SSM_WIDTH + 2 * SSM_GROUPS * SSM_STATE
D_IN_PROJ = 4 * RET_WIDTH + SSM_WIDTH + XBC_WIDTH + SSM_HEADS
ROPE_BASE = 10000.0
N_EXPERTS = 32
TOP_K = 4
D_FF = D_MODEL
SWIGLU_LIMIT = 7.0
SWIGLU_ALPHA = 1.702
EPS = 1e-6

LANES = 128
VMEM_LIMIT = 56 * 1024 * 1024

RET_CHUNK = 256
SSD_CHUNK = 128
ROW_BLK = 128
ITEM_ROWS = 1536
ITEM_VARIANTS = (8, 9, 10, 12)
FF_TILE = 128
DOWN_TILE = 512
WEIGHT_SLOTS = 3
ROW_SLABS = D_MODEL // LANES
PACK_SLABS = ROW_SLABS // 2


def _params(sem, **kw):
    return pltpu.CompilerParams(dimension_semantics=sem, vmem_limit_bytes=VMEM_LIMIT, **kw)


def _dot(a, b):
    return jnp.dot(a, b, preferred_element_type=F32)


def _dot_nt(a, b):
    return lax.dot_general(a, b, (((1,), (1,)), ((), ())), preferred_element_type=F32)


def _dot_tn(a, b):
    return lax.dot_general(a, b, (((0,), (0,)), ((), ())), preferred_element_type=F32)


def _split3(x):
    hi = x.astype(BF16)
    r = x - hi.astype(F32)
    mid = r.astype(BF16)
    lo = (r - mid.astype(F32)).astype(BF16)
    return hi, mid, lo


def _dot_exact_rhs01(x, m01):
    hi, mid, lo = _split3(x)
    return _dot(hi, m01) + _dot(mid, m01) + _dot(lo, m01)


def _dot_exact_lhs01(m01, x):
    hi, mid, lo = _split3(x)
    return _dot(m01, hi) + _dot(m01, mid) + _dot(m01, lo)


def _silu(x):
    return x * jax.nn.sigmoid(x)


def _pack_bf16_pairs(lo, hi):
    lo_bits = lax.bitcast_convert_type(lo.astype(BF16).astype(F32), jnp.uint32)
    hi_bits = lax.bitcast_convert_type(hi.astype(BF16).astype(F32), jnp.uint32)
    return hi_bits | (lo_bits >> 16)


INPROJ_CHUNKS = 4


def _inproj_kernel(x_hbm, lnw_ref, w_ref, o_ref, xs_ref, u_ref, sem):
    tm = u_ref.shape[0]
    rows = tm // INPROJ_CHUNKS
    i = pl.program_id(0)
    j = pl.program_id(1)

    def copy(tile, c):
        src = x_hbm.at[pl.ds(pl.multiple_of(tile * tm + c * rows, rows), rows)]
        return pltpu.make_async_copy(src, xs_ref.at[c % 2], sem.at[c % 2])

    @pl.when(jnp.logical_and(i == 0, j == 0))
    def _():
        copy(0, 0).start()
        copy(0, 1).start()

    @pl.when(j == 0)
    def _():
        for c in range(INPROJ_CHUNKS):
            copy(i, c).wait()
            x = xs_ref[c % 2]
            ms = jnp.mean(x * x, axis=-1, keepdims=True)
            u_ref[c * rows:(c + 1) * rows, :] = (x * lax.rsqrt(ms + EPS) * lnw_ref[...]).astype(BF16)
            if c + 2 < INPROJ_CHUNKS:
                copy(i, c + 2).start()

    @pl.when(jnp.logical_and(j == pl.num_programs(1) - 1, i + 1 < pl.num_programs(0)))
    def _():
        copy(i + 1, 0).start()
        copy(i + 1, 1).start()

    tn = o_ref.shape[1]
    tail = D_IN_PROJ % tn
    if 0 < tail <= LANES:
        last = pl.num_programs(1) - 1

        @pl.when(pl.program_id(1) < last)
        def _():
            o_ref[...] = _dot_nt(u_ref[...], w_ref[...].astype(BF16))

        @pl.when(pl.program_id(1) == last)
        def _():
            o_ref[:, 0:LANES] = _dot_nt(u_ref[...], w_ref[0:LANES, :].astype(BF16))
    else:
        o_ref[...] = _dot_nt(u_ref[...], w_ref[...].astype(BF16))


def _inproj(x2, ln_w, w_in_t):
    T = x2.shape[0]
    tm, tn = (2048 if T % 2048 == 0 else 1024), 512
    return pl.pallas_call(
        _inproj_kernel,
        grid=(T // tm, pl.cdiv(D_IN_PROJ, tn)),
        in_specs=[pl.BlockSpec(memory_space=pl.ANY),
                  pl.BlockSpec((1, D_MODEL), lambda i, j: (0, 0)),
                  pl.BlockSpec((tn, D_MODEL), lambda i, j: (j, 0))],
        out_specs=pl.BlockSpec((tm, tn), lambda i, j: (i, j)),
        out_shape=jax.ShapeDtypeStruct((T, D_IN_PROJ), F32),
        scratch_shapes=[pltpu.VMEM((2, tm // INPROJ_CHUNKS, D_MODEL), F32),
                        pltpu.VMEM((tm, D_MODEL), BF16),
                        pltpu.SemaphoreType.DMA((2,))],
        compiler_params=_params(("arbitrary", "arbitrary")),
        name="inproj",
    )(x2, ln_w.reshape(1, D_MODEL), w_in_t)


def _retention_tables():
    C = RET_CHUNK
    h = np.arange(RET_HEADS, dtype=np.float64)
    log_gamma = np.log1p(-np.exp2(-5.0 - h))
    idx = np.arange(C, dtype=np.float64)
    rel = idx[:, None] - idx[None, :]
    intra = np.where(rel >= 0, np.exp(log_gamma[:, None, None] * np.maximum(rel, 0.0)), 0.0)
    q_decay = np.exp(log_gamma[:, None] * (idx + 1.0))
    k_decay = np.exp(log_gamma[:, None] * (C - 1.0 - idx))
    chunk_decay = np.exp(log_gamma * C)
    qd = np.broadcast_to(q_decay[:, :, None], (RET_HEADS, C, RET_HEAD_DIM))
    kd = np.broadcast_to(k_decay[:, :, None], (RET_HEADS, C, RET_HEAD_DIM))
    return (jnp.asarray(intra, F32), jnp.asarray(qd, F32), jnp.asarray(kd, F32),
            [float(c) for c in chunk_decay])


def _retention_kernel(chunk_decay, pos_ref, invf_ref, q_ref, k_ref, v_ref, g_ref,
                      intra_ref, qd_ref, kd_ref, o_ref, state_ref):
    @pl.when(pl.program_id(0) == 0)
    def _():
        state_ref[...] = jnp.zeros_like(state_ref)

    half = RET_HEAD_DIM // 2
    ang = pos_ref[...] * invf_ref[...]
    cos = jnp.cos(ang)
    sin = jnp.sin(ang)

    def rope(t):
        t1, t2 = t[:, :half], t[:, half:]
        return jnp.concatenate([t1 * cos - t2 * sin, t2 * cos + t1 * sin], axis=-1)

    for h in range(RET_HEADS):
        sl = slice(h * RET_HEAD_DIM, (h + 1) * RET_HEAD_DIM)
        q = rope(q_ref[:, sl])
        k = rope(k_ref[:, sl]) * (RET_HEAD_DIM ** -0.5)
        v = v_ref[:, sl].astype(BF16)
        state = state_ref[h]
        scores = _dot_nt(q.astype(BF16), k.astype(BF16)) * intra_ref[h]
        inner = _dot(scores.astype(BF16), v)
        cross = _dot((q * qd_ref[h]).astype(BF16), state.astype(BF16))
        state_ref[h] = chunk_decay[h] * state + _dot_tn((k * kd_ref[h]).astype(BF16), v)
        o = inner + cross
        o = o * lax.rsqrt(jnp.mean(o * o, axis=-1, keepdims=True) + EPS)
        o_ref[:, sl] = (o * _silu(g_ref[:, sl])).astype(o_ref.dtype)


def _retention(proj, pos_col, inv_freq):
    T = proj.shape[0]
    C = RET_CHUNK
    intra, qd, kd, chunk_decay = _retention_tables()
    col = lambda j: pl.BlockSpec((C, RET_WIDTH), lambda c, j=j: (c, j))
    const3 = lambda shape: pl.BlockSpec(shape, lambda c: (0, 0, 0))
    return pl.pallas_call(
        functools.partial(_retention_kernel, chunk_decay),
        grid=(T // C,),
        in_specs=[pl.BlockSpec((C, 1), lambda c: (c, 0)),
                  pl.BlockSpec((1, RET_HEAD_DIM // 2), lambda c: (0, 0)),
                  col(0), col(1), col(2), col(3),
                  const3((RET_HEADS, C, C)),
                  const3((RET_HEADS, C, RET_HEAD_DIM)),
                  const3((RET_HEADS, C, RET_HEAD_DIM))],
        out_specs=pl.BlockSpec((C, RET_WIDTH), lambda c: (c, 0)),
        out_shape=jax.ShapeDtypeStruct((T, RET_WIDTH), BF16),
        scratch_shapes=[pltpu.VMEM((RET_HEADS, RET_HEAD_DIM, RET_HEAD_DIM), F32)],
        compiler_params=_params(("arbitrary",)),
        name="retention",
    )(pos_col, inv_freq, proj, proj, proj, proj, intra, qd, kd)


def _ssd_kernel(xs0_ref, xs1_ref, bc_ref, z_ref, dt_ref, convw_ref, convb_ref, dtb_ref, a_ref,
                dskip_ref, normw_ref, expand_ref, o_ref, ext_ref, state_ref):
    C = SSD_CHUNK
    HW = SSM_WIDTH // SSM_GROUPS
    CARRY = 8

    @pl.when(pl.program_id(0) == 0)
    def _():
        ext_ref[0:CARRY, :] = jnp.zeros((CARRY, XBC_WIDTH), F32)
        state_ref[...] = jnp.zeros_like(state_ref)

    ext_ref[CARRY:CARRY + C, 0:HW] = xs0_ref[...]
    ext_ref[CARRY:CARRY + C, HW:2 * HW] = xs1_ref[...]
    ext_ref[CARRY:CARRY + C, 2 * HW:3 * HW] = bc_ref[...]
    conv = convb_ref[...]
    for k in range(CONV_WIDTH):
        off = CARRY - (CONV_WIDTH - 1) + k
        conv = conv + convw_ref[k:k + 1, :] * ext_ref[off:off + C, :]
    ext_ref[0:CARRY, :] = ext_ref[C:C + CARRY, :]
    xbc = _silu(conv)
    xs = xbc[:, :SSM_WIDTH]

    lane = lax.broadcasted_iota(jnp.int32, (1, LANES), 1)
    dt_raw = jnp.where(lane < SSM_HEADS, dt_ref[...], 0.0) + dtb_ref[...]
    dt = jnp.maximum(dt_raw, 0.0) + jnp.log1p(jnp.exp(-jnp.abs(dt_raw)))
    dta = dt * a_ref[...]

    row = lax.broadcasted_iota(jnp.int32, (C, C), 0)
    colm = lax.broadcasted_iota(jnp.int32, (C, C), 1)
    tril = row >= colm
    a_cum = _dot_exact_lhs01(jnp.where(tril, 1.0, 0.0).astype(BF16), dta)
    a_cum_t = a_cum.T

    expand = expand_ref[...]
    a_exp = _dot_exact_rhs01(a_cum, expand)
    dt_exp = _dot_exact_rhs01(dt, expand)
    a_last = a_exp[C - 1:C, :]
    decay_in = jnp.exp(a_exp)
    decay_out = jnp.exp(a_last - a_exp)
    chunk_decay = jnp.exp(a_last)
    xdt = xs * dt_exp

    lane2 = lax.broadcasted_iota(jnp.int32, (1, LANES), 1)
    lo_head = lane2 < SSM_HEAD_DIM
    ys = []
    for g in range(SSM_GROUPS):
        gs = slice(g * HW, (g + 1) * HW)
        b_g = xbc[:, SSM_WIDTH + g * SSM_STATE:SSM_WIDTH + (g + 1) * SSM_STATE].astype(BF16)
        c0 = SSM_WIDTH + SSM_GROUPS * SSM_STATE
        c_g = xbc[:, c0 + g * SSM_STATE:c0 + (g + 1) * SSM_STATE].astype(BF16)
        cb = _dot_nt(c_g, b_g)
        state = state_ref[g]
        y_off = _dot(c_g, state.astype(BF16)) * decay_in[:, gs]
        xw = (xdt[:, gs] * decay_out[:, gs]).astype(BF16)
        state_ref[g] = chunk_decay[:, gs] * state + _dot_tn(b_g, xw)
        slabs = []
        for s in range(HW // LANES):
            xd = xdt[:, g * HW + s * LANES:g * HW + (s + 1) * LANES]
            acc = None
            for e in range(2):
                hh = g * (SSM_HEADS // SSM_GROUPS) + 2 * s + e
                seg = a_cum[:, hh:hh + 1] - a_cum_t[hh:hh + 1, :]
                m = cb * jnp.exp(jnp.where(tril, seg, -jnp.inf))
                xm = jnp.where(lo_head if e == 0 else jnp.logical_not(lo_head), xd, 0.0)
                part = _dot(m.astype(BF16), xm.astype(BF16))
                acc = part if acc is None else acc + part
            slabs.append(acc)
        ys.append(jnp.concatenate(slabs, axis=-1) + y_off)
    y = jnp.concatenate(ys, axis=-1) + dskip_ref[...] * xs
    y = y * _silu(z_ref[...])
    outs = []
    for g in range(SSM_GROUPS):
        yg = y[:, g * HW:(g + 1) * HW]
        outs.append(yg * lax.rsqrt(jnp.mean(yg * yg, axis=-1, keepdims=True) + EPS))
    o_ref[...] = (jnp.concatenate(outs, axis=-1) * normw_ref[...]).astype(o_ref.dtype)


def _ssd(proj, conv_w, conv_b, dt_bias, a_log, d_skip, ssm_norm_w):
    T = proj.shape[0]
    C = SSD_CHUNK
    HW = SSM_WIDTH // SSM_GROUPS
    xbc0 = (4 * RET_WIDTH + SSM_WIDTH) // HW
    dt0 = (D_IN_PROJ - SSM_HEADS) // LANES
    pad = lambda v: jnp.zeros((1, LANES), F32).at[0, :SSM_HEADS].set(v.astype(F32))
    a_neg = pad(-jnp.exp(a_log.astype(F32)))
    expand_np = np.zeros((LANES, SSM_WIDTH), np.float32)
    for hh in range(SSM_HEADS):
        expand_np[hh, hh * SSM_HEAD_DIM:(hh + 1) * SSM_HEAD_DIM] = 1.0
    expand = jnp.asarray(expand_np, BF16)
    dskip_exp = jnp.repeat(d_skip.astype(F32), SSM_HEAD_DIM).reshape(1, SSM_WIDTH)
    const = lambda shape: pl.BlockSpec(shape, lambda c: (0, 0))
    return pl.pallas_call(
        _ssd_kernel,
        grid=(T // C,),
        in_specs=[pl.BlockSpec((C, HW), lambda c: (c, xbc0)),
                  pl.BlockSpec((C, HW), lambda c: (c, xbc0 + 1)),
                  pl.BlockSpec((C, HW), lambda c: (c, xbc0 + 2)),
                  pl.BlockSpec((C, SSM_WIDTH), lambda c: (c, 4 * RET_WIDTH // SSM_WIDTH)),
                  pl.BlockSpec((C, LANES), lambda c: (c, dt0)),
                  const((CONV_WIDTH, XBC_WIDTH)), const((1, XBC_WIDTH)),
                  const((1, LANES)), const((1, LANES)),
                  const((1, SSM_WIDTH)), const((1, SSM_WIDTH)),
                  const((LANES, SSM_WIDTH))],
        out_specs=pl.BlockSpec((C, SSM_WIDTH), lambda c: (c, 0)),
        out_shape=jax.ShapeDtypeStruct((T, SSM_WIDTH), BF16),
        scratch_shapes=[pltpu.VMEM((C + 8, XBC_WIDTH), F32),
                        pltpu.VMEM((SSM_GROUPS, SSM_STATE, HW), F32)],
        compiler_params=_params(("arbitrary",)),
        name="ssd",
    )(proj, proj, proj, proj, proj, conv_w, conv_b.reshape(1, XBC_WIDTH), pad(dt_bias), a_neg,
      dskip_exp, ssm_norm_w.reshape(1, SSM_WIDTH), expand)


def _outproj_kernel(ret_ref, ssm_ref, w_ref, x_ref, lnw_ref, wr_ref, br_ref, hs_ref, up_ref, ti_ref, tw_ref):
    tm = x_ref.shape[0]
    h = (x_ref[...] + _dot(ret_ref[...], w_ref[0:RET_WIDTH, :])
         + _dot(ssm_ref[...], w_ref[RET_WIDTH:D_MODEL, :]))
    for s in range(ROW_SLABS):
        hs_ref[pl.ds(s, tm, stride=ROW_SLABS), :] = h[:, s * LANES:(s + 1) * LANES]
    u = h * lax.rsqrt(jnp.mean(h * h, axis=-1, keepdims=True) + EPS) * lnw_ref[...]
    packed = _pack_bf16_pairs(u[:, :D_MODEL // 2], u[:, D_MODEL // 2:])
    for s in range(PACK_SLABS):
        up_ref[pl.ds(s, tm, stride=PACK_SLABS), :] = packed[:, s * LANES:(s + 1) * LANES]
    E = N_EXPERTS
    uh, um, ul = _split3(u)
    ph = _dot_nt(wr_ref[...], uh)
    pm = _dot_nt(wr_ref[...], um)
    pw = _dot_nt(wr_ref[...], ul)
    lg = (ph[0:E] + (ph[E:2 * E] + pm[0:E]) + (ph[2 * E:3 * E] + pm[E:2 * E] + pw[0:E])) + br_ref[...]
    expert = lax.broadcasted_iota(jnp.int32, lg.shape, 0)
    slot = lax.broadcasted_iota(jnp.int32, (TOP_K, tm), 0)
    top_l = jnp.zeros((TOP_K, tm), F32)
    top_i = jnp.zeros((TOP_K, tm), jnp.int32)
    work = lg
    for k in range(TOP_K):
        best = jnp.max(work, axis=0, keepdims=True)
        which = jnp.min(jnp.where(work == best, expert, E), axis=0, keepdims=True)
        top_l = jnp.where(slot == k, best, top_l)
        top_i = jnp.where(slot == k, which, top_i)
        work = jnp.where(expert == which, -jnp.inf, work)
    p = jnp.exp(top_l - jnp.max(top_l, axis=0, keepdims=True))
    ti_ref[...] = top_i
    tw_ref[...] = p / jnp.sum(p, axis=0, keepdims=True)


def _outproj(ret, ssm, w_out_bf16, x2, ln_w, w_router, b_router):
    T = x2.shape[0]
    tm = 512
    return pl.pallas_call(
        _outproj_kernel,
        grid=(T // tm,),
        in_specs=[pl.BlockSpec((tm, RET_WIDTH), lambda i: (i, 0)),
                  pl.BlockSpec((tm, SSM_WIDTH), lambda i: (i, 0)),
                  pl.BlockSpec((D_MODEL, D_MODEL), lambda i: (0, 0)),
                  pl.BlockSpec((tm, D_MODEL), lambda i: (i, 0)),
                  pl.BlockSpec((1, D_MODEL), lambda i: (0, 0)),
                  pl.BlockSpec((3 * N_EXPERTS, D_MODEL), lambda i: (0, 0)),
                  pl.BlockSpec((N_EXPERTS, 1), lambda i: (0, 0))],
        out_specs=[pl.BlockSpec((tm * ROW_SLABS, LANES), lambda i: (i, 0)),
                   pl.BlockSpec((tm * PACK_SLABS, LANES), lambda i: (i, 0)),
                   pl.BlockSpec((TOP_K, tm), lambda i: (0, i)),
                   pl.BlockSpec((TOP_K, tm), lambda i: (0, i))],
        out_shape=[jax.ShapeDtypeStruct((T * ROW_SLABS, LANES), F32),
                   jax.ShapeDtypeStruct((T * PACK_SLABS, LANES), jnp.uint32),
                   jax.ShapeDtypeStruct((TOP_K, T), jnp.int32),
                   jax.ShapeDtypeStruct((TOP_K, T), F32)],
        compiler_params=_params(("parallel",)),
        name="outproj",
    )(ret, ssm, w_out_bf16, x2, ln_w.reshape(1, D_MODEL),
      jnp.concatenate([p.T for p in _split3(w_router)], axis=0), b_router.reshape(N_EXPERTS, 1))


def _expert_kernel(n_ff, n_items, aliased, *refs):
    (item_e, item_row, item_nsub, item_nzero, item_live, item_src, sorted_tok,
     u_hbm, wg_hbm, wd_hbm, bgu_ref, bdn_ref) = refs[:12]
    (y_hbm, stage, xbuf, acc, ybuf, wgbuf, wdbuf, wa_s, wb_s, wd_s,
     sem_in, sem_out, sem_w) = refs[13:] if aliased else refs[12:]
    i = pl.program_id(0)
    f = pl.program_id(1)

    step = i * n_ff + f
    live_steps = item_live[0] * n_ff
    for k in range(1, n_items):
        live_steps = live_steps + item_live[k] * n_ff
    W2_ = 2 * FF_TILE

    def tile_copies(g):
        e = item_e[g // n_ff]
        ft = g % n_ff
        slot = g % WEIGHT_SLOTS
        cols = lambda b: pl.ds(pl.multiple_of(b * W2_, W2_), W2_)
        rows = lambda b: pl.ds(pl.multiple_of(b * FF_TILE, FF_TILE), FF_TILE)
        return [pltpu.make_async_copy(wg_hbm.at[e, :, cols(ft)], wgbuf.at[slot, 0], sem_w.at[slot]),
                pltpu.make_async_copy(wg_hbm.at[e, :, cols(n_ff + ft)], wgbuf.at[slot, 1], sem_w.at[slot]),
                pltpu.make_async_copy(wd_hbm.at[e, rows(ft), :], wdbuf.at[slot, 0], sem_w.at[slot]),
                pltpu.make_async_copy(wd_hbm.at[e, rows(n_ff + ft), :], wdbuf.at[slot, 1], sem_w.at[slot])]

    @pl.when(step == 0)
    def _():
        for g in range(WEIGHT_SLOTS - 1):
            @pl.when(g < live_steps)
            def _(g=g):
                for c in tile_copies(g):
                    c.start()

    @pl.when(step < live_steps)
    def _():
        for c in tile_copies(step):
            c.wait()

        @pl.when(step + WEIGHT_SLOTS - 1 < live_steps)
        def _():
            for c in tile_copies(step + WEIGHT_SLOTS - 1):
                c.start()

    wslot = step % WEIGHT_SLOTS
    wga_ref = wgbuf.at[wslot, 0]
    wgb_ref = wgbuf.at[wslot, 1]
    wda_ref = wdbuf.at[wslot, 0]
    wdb_ref = wdbuf.at[wslot, 1]
    nsub = item_nsub[i]
    nzero = item_nzero[i]
    row0 = item_row[i]
    SUB = ROW_BLK
    YS = SUB * ROW_SLABS
    PS = PACK_SLABS
    XS = SUB * PS

    def y_copy(slot, j):
        dst = y_hbm.at[pl.ds(pl.multiple_of((row0 + j * SUB) * ROW_SLABS, YS), YS)]
        return pltpu.make_async_copy(ybuf.at[pl.ds(pl.multiple_of(slot * YS, YS), YS)], dst, sem_out.at[slot])

    def gather_copy(tok, r):
        src = u_hbm.at[pl.ds(pl.multiple_of(tok * PS, PS), PS)]
        return pltpu.make_async_copy(src, stage.at[pl.ds(pl.multiple_of(r * PS, PS), PS)], sem_in)

    n_assign = sorted_tok.shape[0]
    covers = [jnp.logical_and(nsub > lo, nsub <= m) for lo, m in zip((0,) + ITEM_VARIANTS[:-1], ITEM_VARIANTS)]
    nxt = jnp.minimum(i + 1, n_items - 1)

    def gathered_subs(item):
        ns = item_nsub[item]
        size = sum(jnp.where(jnp.logical_and(ns > lo, ns <= m), m, 0)
                   for lo, m in zip((0,) + ITEM_VARIANTS[:-1], ITEM_VARIANTS))
        return jnp.maximum(size, n_ff)

    def issue_rows(item, lo, count, unrolled):
        src0 = item_src[item]

        def one(r, p):
            gather_copy(sorted_tok[jnp.minimum(src0 + r, n_assign - 1)], r).start(priority=p)

        if unrolled:
            for k in range(count):
                one(lo + k, k % 2)
        else:
            def body(q, c):
                for p in range(2):
                    one(lo + 2 * q + p, p)
                return c
            lax.fori_loop(0, count // 2, body, 0)

    def wait_rows(item):
        def body(j, c):
            pltpu.make_async_copy(u_hbm.at[pl.ds(0, XS)], stage.at[pl.ds(0, XS)], sem_in).wait()
            return c
        lax.fori_loop(0, gathered_subs(item), body, 0)

    @pl.when(jnp.logical_and(i == 0, f == 0))
    def _():
        issue_rows(0, 0, gathered_subs(0) * SUB, False)

    @pl.when(f == n_ff - 1)
    def _():
        issue_rows(nxt, n_ff * SUB, (gathered_subs(nxt) - n_ff) * SUB, False)

    lane = lax.broadcasted_iota(jnp.int32, (1, 2 * FF_TILE), 1)
    even = (lane % 2) == 0
    e_row = pl.ds(item_e[i], 1)
    bga = bgu_ref[e_row, pl.ds(pl.multiple_of(f * (2 * FF_TILE), 2 * FF_TILE), 2 * FF_TILE)]
    bgb = bgu_ref[e_row, pl.ds(pl.multiple_of((n_ff + f) * (2 * FF_TILE), 2 * FF_TILE), 2 * FF_TILE)]
    W2 = 2 * FF_TILE

    @pl.when(f == 0)
    def _():
        wait_rows(i)
        half = D_MODEL // 2

        def unpack(j, c):
            rows = pl.ds(pl.multiple_of(j * SUB, SUB), SUB)
            for s in range(PS):
                p = stage[pl.ds(j * XS + s, SUB, stride=PS), :]
                lo = lax.bitcast_convert_type(p << 16, F32)
                hi = lax.bitcast_convert_type(p & jnp.uint32(0xFFFF0000), F32)
                xbuf[rows, s * LANES:(s + 1) * LANES] = lo.astype(BF16)
                xbuf[rows, half + s * LANES:half + (s + 1) * LANES] = hi.astype(BF16)
            acc[rows, :] = jnp.zeros((SUB, D_MODEL), F32)
            return c
        lax.fori_loop(0, sum(jnp.where(c, m, 0) for m, c in zip(ITEM_VARIANTS, covers)), unpack, 0)

    @pl.when(nsub == 0)
    def _():
        issue_rows(nxt, f * SUB, SUB, False)

    def block(rows):
        xs = xbuf[0:rows, :]
        wa_s[...] = wga_ref[...].astype(BF16)
        ga = _dot(xs, wa_s[...]) + bga
        issue_rows(nxt, f * SUB, SUB, True)
        wb_s[...] = wgb_ref[...].astype(BF16)
        gb = _dot(xs, wb_s[...]) + bgb
        wd_s[...] = pltpu.bitcast(_pack_bf16_pairs(wda_ref[...], wdb_ref[...]), BF16)
        gate = jnp.where(even, ga, pltpu.roll(gb, 1, 1))
        up = jnp.where(even, pltpu.roll(ga, W2 - 1, 1), gb)
        gate = jnp.minimum(gate, SWIGLU_LIMIT)
        up = jnp.clip(up, -SWIGLU_LIMIT, SWIGLU_LIMIT)
        act = ((up + 1.0) * (gate * jax.nn.sigmoid(gate * SWIGLU_ALPHA))).astype(BF16)
        for n in range(0, D_MODEL, DOWN_TILE):
            acc[0:rows, n:n + DOWN_TILE] += _dot(act, wd_s[:, n:n + DOWN_TILE])

    for m, cover in zip(ITEM_VARIANTS, covers):
        @pl.when(cover)
        def _(m=m):
            block(m * SUB)

    @pl.when(f == n_ff - 1)
    def _():
        @pl.when(i == n_items - 1)
        def _():
            wait_rows(nxt)

        def stage_out(q, c):
            slot = q % 2

            @pl.when(q >= 2)
            def _():
                y_copy(slot, 0).wait()
            val = acc[pl.ds(pl.multiple_of(q * SUB, SUB), SUB), :] + bdn_ref[e_row, :]
            for s in range(ROW_SLABS):
                ybuf[pl.ds(slot * YS + s, SUB, stride=ROW_SLABS), :] = val[:, s * LANES:(s + 1) * LANES]
            y_copy(slot, q).start()
            return c
        lax.fori_loop(0, nsub, stage_out, 0)

        @pl.when(nsub > 0)
        def _():
            y_copy(0, 0).wait()

        @pl.when(nsub > 1)
        def _():
            y_copy(1, 0).wait()

    @pl.when(jnp.logical_and(f == n_ff - 1, nzero > 0))
    def _():
        ybuf[0:YS, :] = jnp.zeros((YS, LANES), F32)

        def start(j, c):
            y_copy(0, j).start()
            return c
        lax.fori_loop(0, nzero, start, 0)

        def wait(j, c):
            y_copy(0, 0).wait()
            return c
        lax.fori_loop(0, nzero, wait, 0)


def _experts_call(u_packed, sorted_tok, n_rows, items, w_gate_up, b_gate_up, w_down, b_down, y_in=None):
    n_ff = (D_FF // 2) // FF_TILE
    assert ITEM_VARIANTS[0] >= n_ff and ITEM_VARIANTS[-1] * ROW_BLK == ITEM_ROWS
    W2 = 2 * FF_TILE
    item_e, item_row, item_nsub, item_nzero, item_live, item_src = items
    n_items = item_e.shape[0]
    aliased = y_in is not None

    whole = lambda shape: pl.BlockSpec(shape, lambda i, f, e, r, n, z, lv, sr, st: (0, 0))
    grid_spec = pltpu.PrefetchScalarGridSpec(
        num_scalar_prefetch=7,
        grid=(n_items, n_ff),
        in_specs=[
            pl.BlockSpec(memory_space=pl.ANY),
            pl.BlockSpec(memory_space=pl.ANY),
            pl.BlockSpec(memory_space=pl.ANY),
            whole((N_EXPERTS, 2 * D_FF)),
            whole((N_EXPERTS, D_MODEL)),
        ] + ([pl.BlockSpec(memory_space=pl.ANY)] if aliased else []),
        out_specs=pl.BlockSpec(memory_space=pl.ANY),
        scratch_shapes=[pltpu.VMEM((ITEM_ROWS * PACK_SLABS, LANES), jnp.uint32),
                        pltpu.VMEM((ITEM_ROWS, D_MODEL), BF16),
                        pltpu.VMEM((ITEM_ROWS, D_MODEL), F32),
                        pltpu.VMEM((2 * ROW_BLK * ROW_SLABS, LANES), F32),
                        pltpu.VMEM((WEIGHT_SLOTS, 2, D_MODEL, W2), F32),
                        pltpu.VMEM((WEIGHT_SLOTS, 2, FF_TILE, D_MODEL), F32),
                        pltpu.VMEM((D_MODEL, W2), BF16),
                        pltpu.VMEM((D_MODEL, W2), BF16),
                        pltpu.VMEM((W2, D_MODEL), BF16),
                        pltpu.SemaphoreType.DMA(()),
                        pltpu.SemaphoreType.DMA((2,)),
                        pltpu.SemaphoreType.DMA((WEIGHT_SLOTS,))],
    )
    args = (item_e, item_row, item_nsub, item_nzero, item_live, item_src, sorted_tok, u_packed,
            w_gate_up, w_down, b_gate_up, b_down)
    return pl.pallas_call(
        functools.partial(_expert_kernel, n_ff, n_items, aliased),
        grid_spec=grid_spec,
        out_shape=jax.ShapeDtypeStruct((n_rows * ROW_SLABS, LANES), F32),
        input_output_aliases={len(args): 0} if aliased else {},
        compiler_params=_params(("arbitrary", "arbitrary")),
        name="experts_overflow" if aliased else "experts",
    )(*args, *((y_in,) if aliased else ()))


def _experts(u_packed, sorted_tok, n_rows, items, n_used, w_gate_up, b_gate_up, w_down, b_down):
    n_main = N_EXPERTS + 1
    weights = (w_gate_up, b_gate_up, w_down, b_down)
    y = _experts_call(u_packed, sorted_tok, n_rows, tuple(a[:n_main] for a in items), *weights)
    rest = tuple(a[n_main:] for a in items)
    return lax.cond(n_used > n_main,
                    lambda y_: _experts_call(u_packed, sorted_tok, n_rows, rest, *weights, y_in=y_),
                    lambda y_: y_, y)


def _combine_kernel(tm, n_steps, dest_ref, w_ref, y_hbm, h_ref, lnw_ref, o_ref, buf, osum, sem):
    i = pl.program_id(0)

    def copy(step, slot, t, k):
        d = dest_ref[(step * tm + t) * TOP_K + k]
        return pltpu.make_async_copy(y_hbm.at[d], buf.at[slot, k * tm + t], sem.at[slot])

    def issue(step, slot):
        def body(t, c):
            for k in range(TOP_K):
                copy(step, slot, t, k).start(priority=k % 2)
            return c
        lax.fori_loop(0, tm, body, 0, unroll=4)

    @pl.when(i == 0)
    def _():
        issue(0, 0)

    @pl.when(i + 1 < n_steps)
    def _():
        issue(i + 1, (i + 1) % 2)

    slot = i % 2

    pltpu.make_async_copy(y_hbm.at[pl.ds(0, TOP_K * tm)], buf.at[slot], sem.at[slot]).wait()

    def token(t, c):
        a = h_ref[t]
        for k in range(TOP_K):
            a = a + w_ref[(i * tm + t) * TOP_K + k] * buf[slot, k * tm + t]
        osum[pl.ds(pl.multiple_of(t * ROW_SLABS, ROW_SLABS), ROW_SLABS), :] = a
        return c
    lax.fori_loop(0, tm, token, 0, unroll=4)

    h = osum[...].reshape(tm, ROW_SLABS, LANES)
    ms = jnp.mean(jnp.mean(h * h, axis=2, keepdims=True), axis=1, keepdims=True)
    osum[...] = (h * lax.rsqrt(ms + EPS) * lnw_ref[...]).reshape(tm * ROW_SLABS, LANES)
    for s in range(ROW_SLABS):
        o_ref[:, s * LANES:(s + 1) * LANES] = osum[pl.ds(s, tm, stride=ROW_SLABS), :]


def _combine(dest, wflat, y3, h3, ln_w):
    T = h3.shape[0]
    tm = 256
    n_steps = T // tm
    grid_spec = pltpu.PrefetchScalarGridSpec(
        num_scalar_prefetch=2,
        grid=(n_steps,),
        in_specs=[pl.BlockSpec(memory_space=pl.ANY),
                  pl.BlockSpec((tm, ROW_SLABS, LANES), lambda i, d, w: (i, 0, 0)),
                  pl.BlockSpec((1, ROW_SLABS, LANES), lambda i, d, w: (0, 0, 0))],
        out_specs=pl.BlockSpec((tm, D_MODEL), lambda i, d, w: (i, 0)),
        scratch_shapes=[pltpu.VMEM((2, TOP_K * tm, ROW_SLABS, LANES), F32),
                        pltpu.VMEM((tm * ROW_SLABS, LANES), F32),
                        pltpu.SemaphoreType.DMA((2,))],
    )
    return pl.pallas_call(
        functools.partial(_combine_kernel, tm, n_steps),
        grid_spec=grid_spec,
        out_shape=jax.ShapeDtypeStruct((T, D_MODEL), F32),
        compiler_params=_params(("arbitrary",)),
        name="combine",
    )(dest, wflat, y3, h3, ln_w.reshape(1, ROW_SLABS, LANES))


def _route(top_idx, top_w, n_rows, n_items):
    T = top_idx.shape[0]
    e_flat = top_idx.reshape(-1).astype(jnp.int32)
    onehot = (e_flat[:, None] == jnp.arange(N_EXPERTS, dtype=jnp.int32)[None, :]).astype(jnp.int32)
    csum = jnp.cumsum(onehot, axis=0)
    counts = csum[-1]
    padded = (counts + ROW_BLK - 1) // ROW_BLK * ROW_BLK
    pend = jnp.cumsum(padded)
    pstart = pend - padded
    dest = jnp.sum(onehot * (pstart[None, :] + csum - onehot), axis=1).astype(jnp.int32)

    per_e = (padded + ITEM_ROWS - 1) // ITEM_ROWS
    iend = jnp.cumsum(per_e)
    istart = iend - per_e
    ii = jnp.arange(n_items, dtype=jnp.int32)
    total = iend[-1]
    live = (ii < total).astype(jnp.int32)
    ic = jnp.minimum(ii, total - 1)
    ie = jnp.minimum(jnp.sum(ic[:, None] >= iend[None, :], axis=1), N_EXPERTS - 1).astype(jnp.int32)
    within = ic - istart[ie]
    irow = (pstart[ie] + within * ITEM_ROWS).astype(jnp.int32)
    insub = jnp.minimum((padded[ie] - within * ITEM_ROWS) // ROW_BLK, ITEM_ROWS // ROW_BLK).astype(jnp.int32)
    insub = insub * live
    tail_rows = n_rows - pend[-1]
    tail = jnp.logical_and(ii == total, tail_rows > 0)
    inzero = jnp.where(tail, tail_rows // ROW_BLK, 0).astype(jnp.int32)
    irow = jnp.where(tail, pend[-1], irow).astype(jnp.int32)
    n_used = total + (tail_rows > 0).astype(jnp.int32)
    n_assign = T * TOP_K
    assert N_EXPERTS * n_assign < 2 ** 31
    order = jnp.sort(e_flat * n_assign + jnp.arange(n_assign, dtype=jnp.int32)) % n_assign
    sorted_tok = (order // TOP_K).astype(jnp.int32)
    cstart = jnp.cumsum(counts) - counts
    isrc = ((cstart[ie] + within * ITEM_ROWS) * live).astype(jnp.int32)
    return dest, top_w.reshape(-1).astype(F32), (ie, irow, insub, inzero, live, isrc), n_used, sorted_tok


def kernel(x, positions, ln_mix_w, w_in, conv_w, conv_b, dt_bias, a_log, d_skip, ssm_norm_w, w_out,
           ln_ffn_w, w_router, b_router, w_gate_up, b_gate_up, w_down, b_down, ln_final_w):
    B, L, _ = x.shape
    T = B * L
    assert B == 1 and T % 1024 == 0
    x2 = x.reshape(T, D_MODEL)
    half = RET_HEAD_DIM // 2
    inv_freq = (ROPE_BASE ** (-jnp.arange(half, dtype=F32) / half)).reshape(1, half)
    pos_col = positions.reshape(T, 1).astype(F32)

    proj = _inproj(x2, ln_mix_w[0], jnp.swapaxes(w_in[0], 0, 1))
    ret = _retention(proj, pos_col, inv_freq)
    ssm = _ssd(proj, conv_w[0], conv_b[0], dt_bias[0], a_log[0], d_skip[0], ssm_norm_w[0])
    h_slabs, u_packed, top_idx, top_w = _outproj(ret, ssm, w_out[0].astype(BF16), x2, ln_ffn_w[0], w_router[0],
                                                 b_router[0])

    n_rows = -(-(T * TOP_K + N_EXPERTS * (ROW_BLK - 1)) // ROW_BLK) * ROW_BLK
    n_items = N_EXPERTS + 1 + n_rows // ITEM_ROWS
    dest, wflat, items, n_used, sorted_tok = _route(top_idx.T, top_w.T, n_rows, n_items)

    y_rows = _experts(u_packed, sorted_tok, n_rows, items, n_used, w_gate_up[0], b_gate_up[0], w_down[0],
                      b_down[0])
    out = _combine(dest, wflat, y_rows.reshape(n_rows, ROW_SLABS, LANES),
                   h_slabs.reshape(T, ROW_SLABS, LANES), ln_final_w)
    return out.reshape(B, L, D_MODEL)
```

```python
import functools

import numpy as np
import jax
import jax.numpy as jnp
from jax import lax
from jax.experimental import pallas as pl
from jax.experimental.pallas import tpu as pltpu

F32 = jnp.float32
BF16 = jnp.bfloat16

D_MODEL = 2048
RET_HEADS = 4
RET_HEAD_DIM = 256
RET_WIDTH = RET_HEADS * RET_HEAD_DIM
SSM_WIDTH = D_MODEL - RET_WIDTH
SSM_HEAD_DIM = 64
SSM_HEADS = SSM_WIDTH // SSM_HEAD_DIM
SSM_GROUPS = 2
SSM_STATE = 128
CONV_WIDTH = 4
XBC_WIDTH = SSM_WIDTH + 2 * SSM_GROUPS * SSM_STATE
D_IN_PROJ = 4 * RET_WIDTH + SSM_WIDTH + XBC_WIDTH + SSM_HEADS
ROPE_BASE = 10000.0
N_EXPERTS = 32
TOP_K = 4
D_FF = D_MODEL
SWIGLU_LIMIT = 7.0
SWIGLU_ALPHA = 1.702
EPS = 1e-6

LANES = 128
VMEM_LIMIT = 56 * 1024 * 1024

RET_CHUNK = 256
SSD_CHUNK = 128
ROW_BLK = 128
ITEM_ROWS = 1536
ITEM_VARIANTS = (8, 9, 10, 12)
FF_TILE = 128
DOWN_TILE = 512
WEIGHT_SLOTS = 3
Y_SLOTS = 4
ROW_SLABS = D_MODEL // LANES
PACK_SLABS = ROW_SLABS // 2


def _params(sem, **kw):
    return pltpu.CompilerParams(dimension_semantics=sem, vmem_limit_bytes=VMEM_LIMIT, **kw)


def _dot(a, b):
    return jnp.dot(a, b, preferred_element_type=F32)


def _dot_nt(a, b):
    return lax.dot_general(a, b, (((1,), (1,)), ((), ())), preferred_element_type=F32)


def _dot_tn(a, b):
    return lax.dot_general(a, b, (((0,), (0,)), ((), ())), preferred_element_type=F32)


def _split3(x):
    hi = x.astype(BF16)
    r = x - hi.astype(F32)
    mid = r.astype(BF16)
    lo = (r - mid.astype(F32)).astype(BF16)
    return hi, mid, lo


def _dot_exact_rhs01(x, m01):
    hi, mid, lo = _split3(x)
    return _dot(hi, m01) + _dot(mid, m01) + _dot(lo, m01)


def _dot_exact_lhs01(m01, x):
    hi, mid, lo = _split3(x)
    return _dot(m01, hi) + _dot(m01, mid) + _dot(m01, lo)


def _silu(x):
    return x * jax.nn.sigmoid(x)


def _pack_bf16_pairs(lo, hi):
    lo_bits = lax.bitcast_convert_type(lo.astype(BF16).astype(F32), jnp.uint32)
    hi_bits = lax.bitcast_convert_type(hi.astype(BF16).astype(F32), jnp.uint32)
    return hi_bits | (lo_bits >> 16)


INPROJ_CHUNKS = 4


def _inproj_kernel(x_hbm, lnw_ref, w_ref, o_ref, xs_ref, u_ref, sem):
    tm = u_ref.shape[0]
    rows = tm // INPROJ_CHUNKS
    i = pl.program_id(0)

    @pl.when(pl.program_id(1) == 0)
    def _():
        def copy(c):
            return pltpu.make_async_copy(x_hbm.at[pl.ds(i * tm + c * rows, rows)], xs_ref.at[c % 2], sem.at[c % 2])

        copy(0).start()
        copy(1).start()
        for c in range(INPROJ_CHUNKS):
            copy(c).wait()
            x = xs_ref[c % 2]
            ms = jnp.mean(x * x, axis=-1, keepdims=True)
            u_ref[c * rows:(c + 1) * rows, :] = (x * lax.rsqrt(ms + EPS) * lnw_ref[...]).astype(BF16)
            if c + 2 < INPROJ_CHUNKS:
                copy(c + 2).start()

    tn = o_ref.shape[1]
    tail = D_IN_PROJ % tn
    if 0 < tail <= LANES:
        last = pl.num_programs(1) - 1

        @pl.when(pl.program_id(1) < last)
        def _():
            o_ref[...] = _dot_nt(u_ref[...], w_ref[...].astype(BF16))

        @pl.when(pl.program_id(1) == last)
        def _():
            o_ref[:, 0:LANES] = _dot_nt(u_ref[...], w_ref[0:LANES, :].astype(BF16))
    else:
        o_ref[...] = _dot_nt(u_ref[...], w_ref[...].astype(BF16))


def _inproj(x2, ln_w, w_in_t):
    T = x2.shape[0]
    tm, tn = (2048 if T % 2048 == 0 else 1024), 512
    return pl.pallas_call(
        _inproj_kernel,
        grid=(T // tm, pl.cdiv(D_IN_PROJ, tn)),
        in_specs=[pl.BlockSpec(memory_space=pl.ANY),
                  pl.BlockSpec((1, D_MODEL), lambda i, j: (0, 0)),
                  pl.BlockSpec((tn, D_MODEL), lambda i, j: (j, 0))],
        out_specs=pl.BlockSpec((tm, tn), lambda i, j: (i, j)),
        out_shape=jax.ShapeDtypeStruct((T, D_IN_PROJ), F32),
        scratch_shapes=[pltpu.VMEM((2, tm // INPROJ_CHUNKS, D_MODEL), F32),
                        pltpu.VMEM((tm, D_MODEL), BF16),
                        pltpu.SemaphoreType.DMA((2,))],
        compiler_params=_params(("arbitrary", "arbitrary")),
        name="inproj",
    )(x2, ln_w.reshape(1, D_MODEL), w_in_t)


def _retention_tables():
    C = RET_CHUNK
    h = np.arange(RET_HEADS, dtype=np.float64)
    log_gamma = np.log1p(-np.exp2(-5.0 - h))
    idx = np.arange(C, dtype=np.float64)
    rel = idx[:, None] - idx[None, :]
    intra = np.where(rel >= 0, np.exp(log_gamma[:, None, None] * np.maximum(rel, 0.0)), 0.0)
    q_decay = np.exp(log_gamma[:, None] * (idx + 1.0))
    k_decay = np.exp(log_gamma[:, None] * (C - 1.0 - idx))
    chunk_decay = np.exp(log_gamma * C)
    qd = np.broadcast_to(q_decay[:, :, None], (RET_HEADS, C, RET_HEAD_DIM))
    kd = np.broadcast_to(k_decay[:, :, None], (RET_HEADS, C, RET_HEAD_DIM))
    return (jnp.asarray(intra, F32), jnp.asarray(qd, F32), jnp.asarray(kd, F32),
            [float(c) for c in chunk_decay])


def _retention_kernel(chunk_decay, pos_ref, invf_ref, q_ref, k_ref, v_ref, g_ref,
                      intra_ref, qd_ref, kd_ref, o_ref, state_ref):
    @pl.when(pl.program_id(0) == 0)
    def _():
        state_ref[...] = jnp.zeros_like(state_ref)

    half = RET_HEAD_DIM // 2
    ang = pos_ref[...] * invf_ref[...]
    cos = jnp.cos(ang)
    sin = jnp.sin(ang)

    def rope(t):
        t1, t2 = t[:, :half], t[:, half:]
        return jnp.concatenate([t1 * cos - t2 * sin, t2 * cos + t1 * sin], axis=-1)

    for h in range(RET_HEADS):
        sl = slice(h * RET_HEAD_DIM, (h + 1) * RET_HEAD_DIM)
        q = rope(q_ref[:, sl])
        k = rope(k_ref[:, sl]) * (RET_HEAD_DIM ** -0.5)
        v = v_ref[:, sl].astype(BF16)
        state = state_ref[h]
        scores = _dot_nt(q.astype(BF16), k.astype(BF16)) * intra_ref[h]
        inner = _dot(scores.astype(BF16), v)
        cross = _dot((q * qd_ref[h]).astype(BF16), state.astype(BF16))
        state_ref[h] = chunk_decay[h] * state + _dot_tn((k * kd_ref[h]).astype(BF16), v)
        o = inner + cross
        o = o * lax.rsqrt(jnp.mean(o * o, axis=-1, keepdims=True) + EPS)
        o_ref[:, sl] = (o * _silu(g_ref[:, sl])).astype(o_ref.dtype)


def _retention(proj, pos_col, inv_freq):
    T = proj.shape[0]
    C = RET_CHUNK
    intra, qd, kd, chunk_decay = _retention_tables()
    col = lambda j: pl.BlockSpec((C, RET_WIDTH), lambda c, j=j: (c, j))
    const3 = lambda shape: pl.BlockSpec(shape, lambda c: (0, 0, 0))
    return pl.pallas_call(
        functools.partial(_retention_kernel, chunk_decay),
        grid=(T // C,),
        in_specs=[pl.BlockSpec((C, 1), lambda c: (c, 0)),
                  pl.BlockSpec((1, RET_HEAD_DIM // 2), lambda c: (0, 0)),
                  col(0), col(1), col(2), col(3),
                  const3((RET_HEADS, C, C)),
                  const3((RET_HEADS, C, RET_HEAD_DIM)),
                  const3((RET_HEADS, C, RET_HEAD_DIM))],
        out_specs=pl.BlockSpec((C, RET_WIDTH), lambda c: (c, 0)),
        out_shape=jax.ShapeDtypeStruct((T, RET_WIDTH), BF16),
        scratch_shapes=[pltpu.VMEM((RET_HEADS, RET_HEAD_DIM, RET_HEAD_DIM), F32)],
        compiler_params=_params(("arbitrary",)),
        name="retention",
    )(pos_col, inv_freq, proj, proj, proj, proj, intra, qd, kd)


def _ssd_kernel(xs0_ref, xs1_ref, bc_ref, z_ref, dt_ref, convw_ref, convb_ref, dtb_ref, a_ref,
                dskip_ref, normw_ref, expand_ref, o_ref, ext_ref, state_ref):
    C = SSD_CHUNK
    HW = SSM_WIDTH // SSM_GROUPS
    CARRY = 8

    @pl.when(pl.program_id(0) == 0)
    def _():
        ext_ref[0:CARRY, :] = jnp.zeros((CARRY, XBC_WIDTH), F32)
        state_ref[...] = jnp.zeros_like(state_ref)

    ext_ref[CARRY:CARRY + C, 0:HW] = xs0_ref[...]
    ext_ref[CARRY:CARRY + C, HW:2 * HW] = xs1_ref[...]
    ext_ref[CARRY:CARRY + C, 2 * HW:3 * HW] = bc_ref[...]
    conv = convb_ref[...]
    for k in range(CONV_WIDTH):
        off = CARRY - (CONV_WIDTH - 1) + k
        conv = conv + convw_ref[k:k + 1, :] * ext_ref[off:off + C, :]
    ext_ref[0:CARRY, :] = ext_ref[C:C + CARRY, :]
    xbc = _silu(conv)
    xs = xbc[:, :SSM_WIDTH]

    lane = lax.broadcasted_iota(jnp.int32, (1, LANES), 1)
    dt_raw = jnp.where(lane < SSM_HEADS, dt_ref[...], 0.0) + dtb_ref[...]
    dt = jnp.maximum(dt_raw, 0.0) + jnp.log1p(jnp.exp(-jnp.abs(dt_raw)))
    dta = dt * a_ref[...]

    row = lax.broadcasted_iota(jnp.int32, (C, C), 0)
    colm = lax.broadcasted_iota(jnp.int32, (C, C), 1)
    tril = row >= colm
    a_cum = _dot_exact_lhs01(jnp.where(tril, 1.0, 0.0).astype(BF16), dta)
    a_cum_t = a_cum.T

    expand = expand_ref[...]
    a_exp = _dot_exact_rhs01(a_cum, expand)
    dt_exp = _dot_exact_rhs01(dt, expand)
    a_last = a_exp[C - 1:C, :]
    decay_in = jnp.exp(a_exp)
    decay_out = jnp.exp(a_last - a_exp)
    chunk_decay = jnp.exp(a_last)
    xdt = xs * dt_exp

    lane2 = lax.broadcasted_iota(jnp.int32, (1, LANES), 1)
    lo_head = lane2 < SSM_HEAD_DIM
    ys = []
    for g in range(SSM_GROUPS):
        gs = slice(g * HW, (g + 1) * HW)
        b_g = xbc[:, SSM_WIDTH + g * SSM_STATE:SSM_WIDTH + (g + 1) * SSM_STATE].astype(BF16)
        c0 = SSM_WIDTH + SSM_GROUPS * SSM_STATE
        c_g = xbc[:, c0 + g * SSM_STATE:c0 + (g + 1) * SSM_STATE].astype(BF16)
        cb = _dot_nt(c_g, b_g)
        state = state_ref[g]
        y_off = _dot(c_g, state.astype(BF16)) * decay_in[:, gs]
        xw = (xdt[:, gs] * decay_out[:, gs]).astype(BF16)
        state_ref[g] = chunk_decay[:, gs] * state + _dot_tn(b_g, xw)
        slabs = []
        for s in range(HW // LANES):
            xd = xdt[:, g * HW + s * LANES:g * HW + (s + 1) * LANES]
            acc = None
            for e in range(2):
                hh = g * (SSM_HEADS // SSM_GROUPS) + 2 * s + e
                seg = a_cum[:, hh:hh + 1] - a_cum_t[hh:hh + 1, :]
                m = cb * jnp.exp(jnp.where(tril, seg, -jnp.inf))
                xm = jnp.where(lo_head if e == 0 else jnp.logical_not(lo_head), xd, 0.0)
                part = _dot(m.astype(BF16), xm.astype(BF16))
                acc = part if acc is None else acc + part
            slabs.append(acc)
        ys.append(jnp.concatenate(slabs, axis=-1) + y_off)
    y = jnp.concatenate(ys, axis=-1) + dskip_ref[...] * xs
    y = y * _silu(z_ref[...])
    outs = []
    for g in range(SSM_GROUPS):
        yg = y[:, g * HW:(g + 1) * HW]
        outs.append(yg * lax.rsqrt(jnp.mean(yg * yg, axis=-1, keepdims=True) + EPS))
    o_ref[...] = (jnp.concatenate(outs, axis=-1) * normw_ref[...]).astype(o_ref.dtype)


def _ssd(proj, conv_w, conv_b, dt_bias, a_log, d_skip, ssm_norm_w):
    T = proj.shape[0]
    C = SSD_CHUNK
    HW = SSM_WIDTH // SSM_GROUPS
    xbc0 = (4 * RET_WIDTH + SSM_WIDTH) // HW
    dt0 = (D_IN_PROJ - SSM_HEADS) // LANES
    pad = lambda v: jnp.zeros((1, LANES), F32).at[0, :SSM_HEADS].set(v.astype(F32))
    a_neg = pad(-jnp.exp(a_log.astype(F32)))
    expand_np = np.zeros((LANES, SSM_WIDTH), np.float32)
    for hh in range(SSM_HEADS):
        expand_np[hh, hh * SSM_HEAD_DIM:(hh + 1) * SSM_HEAD_DIM] = 1.0
    expand = jnp.asarray(expand_np, BF16)
    dskip_exp = jnp.repeat(d_skip.astype(F32), SSM_HEAD_DIM).reshape(1, SSM_WIDTH)
    const = lambda shape: pl.BlockSpec(shape, lambda c: (0, 0))
    return pl.pallas_call(
        _ssd_kernel,
        grid=(T // C,),
        in_specs=[pl.BlockSpec((C, HW), lambda c: (c, xbc0)),
                  pl.BlockSpec((C, HW), lambda c: (c, xbc0 + 1)),
                  pl.BlockSpec((C, HW), lambda c: (c, xbc0 + 2)),
                  pl.BlockSpec((C, SSM_WIDTH), lambda c: (c, 4 * RET_WIDTH // SSM_WIDTH)),
                  pl.BlockSpec((C, LANES), lambda c: (c, dt0)),
                  const((CONV_WIDTH, XBC_WIDTH)), const((1, XBC_WIDTH)),
                  const((1, LANES)), const((1, LANES)),
                  const((1, SSM_WIDTH)), const((1, SSM_WIDTH)),
                  const((LANES, SSM_WIDTH))],
        out_specs=pl.BlockSpec((C, SSM_WIDTH), lambda c: (c, 0)),
        out_shape=jax.ShapeDtypeStruct((T, SSM_WIDTH), BF16),
        scratch_shapes=[pltpu.VMEM((C + 8, XBC_WIDTH), F32),
                        pltpu.VMEM((SSM_GROUPS, SSM_STATE, HW), F32)],
        compiler_params=_params(("arbitrary",)),
        name="ssd",
    )(proj, proj, proj, proj, proj, conv_w, conv_b.reshape(1, XBC_WIDTH), pad(dt_bias), a_neg,
      dskip_exp, ssm_norm_w.reshape(1, SSM_WIDTH), expand)


def _outproj_kernel(ret_ref, ssm_ref, w_ref, x_ref, lnw_ref, wr_ref, br_ref, hs_ref, up_ref, ti_ref, tw_ref):
    tm = x_ref.shape[0]
    h = (x_ref[...] + _dot(ret_ref[...], w_ref[0:RET_WIDTH, :])
         + _dot(ssm_ref[...], w_ref[RET_WIDTH:D_MODEL, :]))
    for s in range(ROW_SLABS):
        hs_ref[pl.ds(s, tm, stride=ROW_SLABS), :] = h[:, s * LANES:(s + 1) * LANES]
    u = h * lax.rsqrt(jnp.mean(h * h, axis=-1, keepdims=True) + EPS) * lnw_ref[...]
    packed = _pack_bf16_pairs(u[:, :D_MODEL // 2], u[:, D_MODEL // 2:])
    for s in range(PACK_SLABS):
        up_ref[pl.ds(s, tm, stride=PACK_SLABS), :] = packed[:, s * LANES:(s + 1) * LANES]
    E = N_EXPERTS
    uh, um, ul = _split3(u)
    ph = _dot_nt(wr_ref[...], uh)
    pm = _dot_nt(wr_ref[...], um)
    pw = _dot_nt(wr_ref[...], ul)
    lg = (ph[0:E] + (ph[E:2 * E] + pm[0:E]) + (ph[2 * E:3 * E] + pm[E:2 * E] + pw[0:E])) + br_ref[...]
    expert = lax.broadcasted_iota(jnp.int32, lg.shape, 0)
    slot = lax.broadcasted_iota(jnp.int32, (TOP_K, tm), 0)
    top_l = jnp.zeros((TOP_K, tm), F32)
    top_i = jnp.zeros((TOP_K, tm), jnp.int32)
    work = lg
    for k in range(TOP_K):
        best = jnp.max(work, axis=0, keepdims=True)
        which = jnp.min(jnp.where(work == best, expert, E), axis=0, keepdims=True)
        top_l = jnp.where(slot == k, best, top_l)
        top_i = jnp.where(slot == k, which, top_i)
        work = jnp.where(expert == which, -jnp.inf, work)
    p = jnp.exp(top_l - jnp.max(top_l, axis=0, keepdims=True))
    ti_ref[...] = top_i
    tw_ref[...] = p / jnp.sum(p, axis=0, keepdims=True)


def _outproj(ret, ssm, w_out_bf16, x2, ln_w, w_router, b_router):
    T = x2.shape[0]
    tm = 512
    return pl.pallas_call(
        _outproj_kernel,
        grid=(T // tm,),
        in_specs=[pl.BlockSpec((tm, RET_WIDTH), lambda i: (i, 0)),
                  pl.BlockSpec((tm, SSM_WIDTH), lambda i: (i, 0)),
                  pl.BlockSpec((D_MODEL, D_MODEL), lambda i: (0, 0)),
                  pl.BlockSpec((tm, D_MODEL), lambda i: (i, 0)),
                  pl.BlockSpec((1, D_MODEL), lambda i: (0, 0)),
                  pl.BlockSpec((3 * N_EXPERTS, D_MODEL), lambda i: (0, 0)),
                  pl.BlockSpec((N_EXPERTS, 1), lambda i: (0, 0))],
        out_specs=[pl.BlockSpec((tm * ROW_SLABS, LANES), lambda i: (i, 0)),
                   pl.BlockSpec((tm * PACK_SLABS, LANES), lambda i: (i, 0)),
                   pl.BlockSpec((TOP_K, tm), lambda i: (0, i)),
                   pl.BlockSpec((TOP_K, tm), lambda i: (0, i))],
        out_shape=[jax.ShapeDtypeStruct((T * ROW_SLABS, LANES), F32),
                   jax.ShapeDtypeStruct((T * PACK_SLABS, LANES), jnp.uint32),
                   jax.ShapeDtypeStruct((TOP_K, T), jnp.int32),
                   jax.ShapeDtypeStruct((TOP_K, T), F32)],
        compiler_params=_params(("parallel",)),
        name="outproj",
    )(ret, ssm, w_out_bf16, x2, ln_w.reshape(1, D_MODEL),
      jnp.concatenate([p.T for p in _split3(w_router)], axis=0), b_router.reshape(N_EXPERTS, 1))


def _expert_kernel(n_ff, n_items, aliased, *refs):
    (item_e, item_row, item_nsub, item_nzero, item_live, item_src, sorted_tok,
     u_hbm, wg_hbm, wd_hbm, bgu_ref, bdn_ref) = refs[:12]
    (y_hbm, stage, xbuf, acc, ybuf, wgbuf, wdbuf, wa_s, wb_s, wd_s,
     sem_in, sem_out, sem_w) = refs[13:] if aliased else refs[12:]
    i = pl.program_id(0)
    f = pl.program_id(1)

    step = i * n_ff + f
    live_steps = item_live[0] * n_ff
    for k in range(1, n_items):
        live_steps = live_steps + item_live[k] * n_ff
    W2_ = 2 * FF_TILE

    def tile_copies(g):
        e = item_e[g // n_ff]
        ft = g % n_ff
        slot = g % WEIGHT_SLOTS
        cols = lambda b: pl.ds(pl.multiple_of(b * W2_, W2_), W2_)
        rows = lambda b: pl.ds(pl.multiple_of(b * FF_TILE, FF_TILE), FF_TILE)
        return [pltpu.make_async_copy(wg_hbm.at[e, :, cols(ft)], wgbuf.at[slot, 0], sem_w.at[slot]),
                pltpu.make_async_copy(wg_hbm.at[e, :, cols(n_ff + ft)], wgbuf.at[slot, 1], sem_w.at[slot]),
                pltpu.make_async_copy(wd_hbm.at[e, rows(ft), :], wdbuf.at[slot, 0], sem_w.at[slot]),
                pltpu.make_async_copy(wd_hbm.at[e, rows(n_ff + ft), :], wdbuf.at[slot, 1], sem_w.at[slot])]

    @pl.when(step == 0)
    def _():
        for g in range(WEIGHT_SLOTS - 1):
            @pl.when(g < live_steps)
            def _(g=g):
                for c in tile_copies(g):
                    c.start()

    @pl.when(step < live_steps)
    def _():
        for c in tile_copies(step):
            c.wait()

        @pl.when(step + WEIGHT_SLOTS - 1 < live_steps)
        def _():
            for c in tile_copies(step + WEIGHT_SLOTS - 1):
                c.start()

    wslot = step % WEIGHT_SLOTS
    wga_ref = wgbuf.at[wslot, 0]
    wgb_ref = wgbuf.at[wslot, 1]
    wda_ref = wdbuf.at[wslot, 0]
    wdb_ref = wdbuf.at[wslot, 1]
    nsub = item_nsub[i]
    nzero = item_nzero[i]
    row0 = item_row[i]
    SUB = ROW_BLK
    YS = SUB * ROW_SLABS
    PS = PACK_SLABS
    XS = SUB * PS

    def y_copy(slot, j):
        dst = y_hbm.at[pl.ds(pl.multiple_of((row0 + j * SUB) * ROW_SLABS, YS), YS)]
        return pltpu.make_async_copy(ybuf.at[pl.ds(pl.multiple_of(slot * YS, YS), YS)], dst, sem_out.at[slot])

    def gather_copy(tok, r):
        src = u_hbm.at[pl.ds(pl.multiple_of(tok * PS, PS), PS)]
        return pltpu.make_async_copy(src, stage.at[pl.ds(pl.multiple_of(r * PS, PS), PS)], sem_in)

    n_assign = sorted_tok.shape[0]
    covers = [jnp.logical_and(nsub > lo, nsub <= m) for lo, m in zip((0,) + ITEM_VARIANTS[:-1], ITEM_VARIANTS)]
    nxt = jnp.minimum(i + 1, n_items - 1)

    def gathered_subs(item):
        ns = item_nsub[item]
        size = sum(jnp.where(jnp.logical_and(ns > lo, ns <= m), m, 0)
                   for lo, m in zip((0,) + ITEM_VARIANTS[:-1], ITEM_VARIANTS))
        return jnp.maximum(size, n_ff)

    def issue_rows(item, lo, count, unrolled):
        src0 = item_src[item]

        def one(r, p):
            gather_copy(sorted_tok[jnp.minimum(src0 + r, n_assign - 1)], r).start(priority=p)

        if unrolled:
            for k in range(count):
                one(lo + k, k % 2)
        else:
            def body(q, c):
                for p in range(2):
                    one(lo + 2 * q + p, p)
                return c
            lax.fori_loop(0, count // 2, body, 0)

    def wait_rows(item):
        def body(j, c):
            pltpu.make_async_copy(u_hbm.at[pl.ds(0, XS)], stage.at[pl.ds(0, XS)], sem_in).wait()
            return c
        lax.fori_loop(0, gathered_subs(item), body, 0)

    @pl.when(jnp.logical_and(i == 0, f == 0))
    def _():
        issue_rows(0, 0, gathered_subs(0) * SUB, False)

    @pl.when(f == n_ff - 1)
    def _():
        issue_rows(nxt, n_ff * SUB, (gathered_subs(nxt) - n_ff) * SUB, False)

    lane = lax.broadcasted_iota(jnp.int32, (1, 2 * FF_TILE), 1)
    even = (lane % 2) == 0
    e_row = pl.ds(item_e[i], 1)
    bga = bgu_ref[e_row, pl.ds(pl.multiple_of(f * (2 * FF_TILE), 2 * FF_TILE), 2 * FF_TILE)]
    bgb = bgu_ref[e_row, pl.ds(pl.multiple_of((n_ff + f) * (2 * FF_TILE), 2 * FF_TILE), 2 * FF_TILE)]
    W2 = 2 * FF_TILE

    @pl.when(f == 0)
    def _():
        wait_rows(i)
        half = D_MODEL // 2

        def unpack(j, c):
            rows = pl.ds(pl.multiple_of(j * SUB, SUB), SUB)
            for s in range(PS):
                p = stage[pl.ds(j * XS + s, SUB, stride=PS), :]
                lo = lax.bitcast_convert_type(p << 16, F32)
                hi = lax.bitcast_convert_type(p & jnp.uint32(0xFFFF0000), F32)
                xbuf[rows, s * LANES:(s + 1) * LANES] = lo.astype(BF16)
                xbuf[rows, half + s * LANES:half + (s + 1) * LANES] = hi.astype(BF16)
            acc[rows, :] = jnp.zeros((SUB, D_MODEL), F32)
            return c
        lax.fori_loop(0, sum(jnp.where(c, m, 0) for m, c in zip(ITEM_VARIANTS, covers)), unpack, 0)

    @pl.when(nsub == 0)
    def _():
        issue_rows(nxt, f * SUB, SUB, False)

    def block(rows):
        xs = xbuf[0:rows, :]
        wa_s[...] = wga_ref[...].astype(BF16)
        ga = _dot(xs, wa_s[...]) + bga
        issue_rows(nxt, f * SUB, SUB, True)
        wb_s[...] = wgb_ref[...].astype(BF16)
        gb = _dot(xs, wb_s[...]) + bgb
        wd_s[...] = pltpu.bitcast(_pack_bf16_pairs(wda_ref[...], wdb_ref[...]), BF16)
        gate = jnp.where(even, ga, pltpu.roll(gb, 1, 1))
        up = jnp.where(even, pltpu.roll(ga, W2 - 1, 1), gb)
        gate = jnp.minimum(gate, SWIGLU_LIMIT)
        up = jnp.clip(up, -SWIGLU_LIMIT, SWIGLU_LIMIT)
        act = ((up + 1.0) * (gate * jax.nn.sigmoid(gate * SWIGLU_ALPHA))).astype(BF16)
        for n in range(0, D_MODEL, DOWN_TILE):
            acc[0:rows, n:n + DOWN_TILE] += _dot(act, wd_s[:, n:n + DOWN_TILE])

    for m, cover in zip(ITEM_VARIANTS, covers):
        @pl.when(cover)
        def _(m=m):
            block(m * SUB)

    @pl.when(f == n_ff - 1)
    def _():
        @pl.when(i == n_items - 1)
        def _():
            wait_rows(nxt)

        def stage_out(q, c):
            slot = q % Y_SLOTS

            @pl.when(q >= Y_SLOTS)
            def _():
                y_copy(slot, 0).wait()
            val = acc[pl.ds(pl.multiple_of(q * SUB, SUB), SUB), :] + bdn_ref[e_row, :]
            for s in range(ROW_SLABS):
                ybuf[pl.ds(slot * YS + s, SUB, stride=ROW_SLABS), :] = val[:, s * LANES:(s + 1) * LANES]
            y_copy(slot, q).start()
            return c
        lax.fori_loop(0, nsub, stage_out, 0)

        for k in range(Y_SLOTS):
            @pl.when(nsub > k)
            def _(k=k):
                y_copy(k, 0).wait()

    @pl.when(jnp.logical_and(f == n_ff - 1, nzero > 0))
    def _():
        ybuf[0:YS, :] = jnp.zeros((YS, LANES), F32)

        def start(j, c):
            y_copy(0, j).start()
            return c
        lax.fori_loop(0, nzero, start, 0)

        def wait(j, c):
            y_copy(0, 0).wait()
            return c
        lax.fori_loop(0, nzero, wait, 0)


def _experts_call(u_packed, sorted_tok, n_rows, items, w_gate_up, b_gate_up, w_down, b_down, y_in=None):
    n_ff = (D_FF // 2) // FF_TILE
    assert ITEM_VARIANTS[0] >= n_ff and ITEM_VARIANTS[-1] * ROW_BLK == ITEM_ROWS
    W2 = 2 * FF_TILE
    item_e, item_row, item_nsub, item_nzero, item_live, item_src = items
    n_items = item_e.shape[0]
    aliased = y_in is not None

    whole = lambda shape: pl.BlockSpec(shape, lambda i, f, e, r, n, z, lv, sr, st: (0, 0))
    grid_spec = pltpu.PrefetchScalarGridSpec(
        num_scalar_prefetch=7,
        grid=(n_items, n_ff),
        in_specs=[
            pl.BlockSpec(memory_space=pl.ANY),
            pl.BlockSpec(memory_space=pl.ANY),
            pl.BlockSpec(memory_space=pl.ANY),
            whole((N_EXPERTS, 2 * D_FF)),
            whole((N_EXPERTS, D_MODEL)),
        ] + ([pl.BlockSpec(memory_space=pl.ANY)] if aliased else []),
        out_specs=pl.BlockSpec(memory_space=pl.ANY),
        scratch_shapes=[pltpu.VMEM((ITEM_ROWS * PACK_SLABS, LANES), jnp.uint32),
                        pltpu.VMEM((ITEM_ROWS, D_MODEL), BF16),
                        pltpu.VMEM((ITEM_ROWS, D_MODEL), F32),
                        pltpu.VMEM((Y_SLOTS * ROW_BLK * ROW_SLABS, LANES), F32),
                        pltpu.VMEM((WEIGHT_SLOTS, 2, D_MODEL, W2), F32),
                        pltpu.VMEM((WEIGHT_SLOTS, 2, FF_TILE, D_MODEL), F32),
                        pltpu.VMEM((D_MODEL, W2), BF16),
                        pltpu.VMEM((D_MODEL, W2), BF16),
                        pltpu.VMEM((W2, D_MODEL), BF16),
                        pltpu.SemaphoreType.DMA(()),
                        pltpu.SemaphoreType.DMA((Y_SLOTS,)),
                        pltpu.SemaphoreType.DMA((WEIGHT_SLOTS,))],
    )
    args = (item_e, item_row, item_nsub, item_nzero, item_live, item_src, sorted_tok, u_packed,
            w_gate_up, w_down, b_gate_up, b_down)
    return pl.pallas_call(
        functools.partial(_expert_kernel, n_ff, n_items, aliased),
        grid_spec=grid_spec,
        out_shape=jax.ShapeDtypeStruct((n_rows * ROW_SLABS, LANES), F32),
        input_output_aliases={len(args): 0} if aliased else {},
        compiler_params=_params(("arbitrary", "arbitrary")),
        name="experts_overflow" if aliased else "experts",
    )(*args, *((y_in,) if aliased else ()))


def _experts(u_packed, sorted_tok, n_rows, items, n_used, w_gate_up, b_gate_up, w_down, b_down):
    n_main = N_EXPERTS + 1
    weights = (w_gate_up, b_gate_up, w_down, b_down)
    y = _experts_call(u_packed, sorted_tok, n_rows, tuple(a[:n_main] for a in items), *weights)
    rest = tuple(a[n_main:] for a in items)
    return lax.cond(n_used > n_main,
                    lambda y_: _experts_call(u_packed, sorted_tok, n_rows, rest, *weights, y_in=y_),
                    lambda y_: y_, y)


def _combine_kernel(tm, n_steps, dest_ref, w_ref, y_hbm, h_ref, lnw_ref, o_ref, buf, osum, sem):
    i = pl.program_id(0)

    def copy(step, slot, t, k):
        d = dest_ref[(step * tm + t) * TOP_K + k]
        return pltpu.make_async_copy(y_hbm.at[d], buf.at[slot, k * tm + t], sem.at[slot])

    def issue(step, slot):
        def body(t, c):
            for k in range(TOP_K):
                copy(step, slot, t, k).start(priority=k % 2)
            return c
        lax.fori_loop(0, tm, body, 0, unroll=4)

    @pl.when(i == 0)
    def _():
        issue(0, 0)

    @pl.when(i + 1 < n_steps)
    def _():
        issue(i + 1, (i + 1) % 2)

    slot = i % 2

    pltpu.make_async_copy(y_hbm.at[pl.ds(0, TOP_K * tm)], buf.at[slot], sem.at[slot]).wait()

    def token(t, c):
        a = h_ref[t]
        for k in range(TOP_K):
            a = a + w_ref[(i * tm + t) * TOP_K + k] * buf[slot, k * tm + t]
        osum[pl.ds(pl.multiple_of(t * ROW_SLABS, ROW_SLABS), ROW_SLABS), :] = a
        return c
    lax.fori_loop(0, tm, token, 0, unroll=4)

    h = osum[...].reshape(tm, ROW_SLABS, LANES)
    ms = jnp.mean(jnp.mean(h * h, axis=2, keepdims=True), axis=1, keepdims=True)
    osum[...] = (h * lax.rsqrt(ms + EPS) * lnw_ref[...]).reshape(tm * ROW_SLABS, LANES)
    for s in range(ROW_SLABS):
        o_ref[:, s * LANES:(s + 1) * LANES] = osum[pl.ds(s, tm, stride=ROW_SLABS), :]


def _combine(dest, wflat, y3, h3, ln_w):
    T = h3.shape[0]
    tm = 128
    n_steps = T // tm
    grid_spec = pltpu.PrefetchScalarGridSpec(
        num_scalar_prefetch=2,
        grid=(n_steps,),
        in_specs=[pl.BlockSpec(memory_space=pl.ANY),
                  pl.BlockSpec((tm, ROW_SLABS, LANES), lambda i, d, w: (i, 0, 0)),
                  pl.BlockSpec((1, ROW_SLABS, LANES), lambda i, d, w: (0, 0, 0))],
        out_specs=pl.BlockSpec((tm, D_MODEL), lambda i, d, w: (i, 0)),
        scratch_shapes=[pltpu.VMEM((2, TOP_K * tm, ROW_SLABS, LANES), F32),
                        pltpu.VMEM((tm * ROW_SLABS, LANES), F32),
                        pltpu.SemaphoreType.DMA((2,))],
    )
    return pl.pallas_call(
        functools.partial(_combine_kernel, tm, n_steps),
        grid_spec=grid_spec,
        out_shape=jax.ShapeDtypeStruct((T, D_MODEL), F32),
        compiler_params=_params(("arbitrary",)),
        name="combine",
    )(dest, wflat, y3, h3, ln_w.reshape(1, ROW_SLABS, LANES))


def _route(top_idx, top_w, n_rows, n_items):
    T = top_idx.shape[0]
    e_flat = top_idx.reshape(-1).astype(jnp.int32)
    onehot = (e_flat[:, None] == jnp.arange(N_EXPERTS, dtype=jnp.int32)[None, :]).astype(jnp.int32)
    csum = jnp.cumsum(onehot, axis=0)
    counts = csum[-1]
    padded = (counts + ROW_BLK - 1) // ROW_BLK * ROW_BLK
    pend = jnp.cumsum(padded)
    pstart = pend - padded
    dest = jnp.sum(onehot * (pstart[None, :] + csum - onehot), axis=1).astype(jnp.int32)

    per_e = (padded + ITEM_ROWS - 1) // ITEM_ROWS
    iend = jnp.cumsum(per_e)
    istart = iend - per_e
    ii = jnp.arange(n_items, dtype=jnp.int32)
    total = iend[-1]
    live = (ii < total).astype(jnp.int32)
    ic = jnp.minimum(ii, total - 1)
    ie = jnp.minimum(jnp.sum(ic[:, None] >= iend[None, :], axis=1), N_EXPERTS - 1).astype(jnp.int32)
    within = ic - istart[ie]
    irow = (pstart[ie] + within * ITEM_ROWS).astype(jnp.int32)
    insub = jnp.minimum((padded[ie] - within * ITEM_ROWS) // ROW_BLK, ITEM_ROWS // ROW_BLK).astype(jnp.int32)
    insub = insub * live
    tail_rows = n_rows - pend[-1]
    tail = jnp.logical_and(ii == total, tail_rows > 0)
    inzero = jnp.where(tail, tail_rows // ROW_BLK, 0).astype(jnp.int32)
    irow = jnp.where(tail, pend[-1], irow).astype(jnp.int32)
    n_used = total + (tail_rows > 0).astype(jnp.int32)
    n_assign = T * TOP_K
    assert N_EXPERTS * n_assign < 2 ** 31
    order = jnp.sort(e_flat * n_assign + jnp.arange(n_assign, dtype=jnp.int32)) % n_assign
    sorted_tok = (order // TOP_K).astype(jnp.int32)
    cstart = jnp.cumsum(counts) - counts
    isrc = ((cstart[ie] + within * ITEM_ROWS) * live).astype(jnp.int32)
    return dest, top_w.reshape(-1).astype(F32), (ie, irow, insub, inzero, live, isrc), n_used, sorted_tok


def kernel(x, positions, ln_mix_w, w_in, conv_w, conv_b, dt_bias, a_log, d_skip, ssm_norm_w, w_out,
           ln_ffn_w, w_router, b_router, w_gate_up, b_gate_up, w_down, b_down, ln_final_w):
    B, L, _ = x.shape
    T = B * L
    assert B == 1 and T % 1024 == 0
    x2 = x.reshape(T, D_MODEL)
    half = RET_HEAD_DIM // 2
    inv_freq = (ROPE_BASE ** (-jnp.arange(half, dtype=F32) / half)).reshape(1, half)
    pos_col = positions.reshape(T, 1).astype(F32)

    proj = _inproj(x2, ln_mix_w[0], jnp.swapaxes(w_in[0], 0, 1))
    ret = _retention(proj, pos_col, inv_freq)
    ssm = _ssd(proj, conv_w[0], conv_b[0], dt_bias[0], a_log[0], d_skip[0], ssm_norm_w[0])
    h_slabs, u_packed, top_idx, top_w = _outproj(ret, ssm, w_out[0].astype(BF16), x2, ln_ffn_w[0], w_router[0],
                                                 b_router[0])

    n_rows = -(-(T * TOP_K + N_EXPERTS * (ROW_BLK - 1)) // ROW_BLK) * ROW_BLK
    n_items = N_EXPERTS + 1 + n_rows // ITEM_ROWS
    dest, wflat, items, n_used, sorted_tok = _route(top_idx.T, top_w.T, n_rows, n_items)

    y_rows = _experts(u_packed, sorted_tok, n_rows, items, n_used, w_gate_up[0], b_gate_up[0], w_down[0],
                      b_down[0])
    out = _combine(dest, wflat, y_rows.reshape(n_rows, ROW_SLABS, LANES),
                   h_slabs.reshape(T, ROW_SLABS, LANES), ln_final_w)
    return out.reshape(B, L, D_MODEL)
```

```python
import functools

import numpy as np
import jax
import jax.numpy as jnp
from jax import lax
from jax.experimental import pallas as pl
from jax.experimental.pallas import tpu as pltpu

F32 = jnp.float32
BF16 = jnp.bfloat16

D_MODEL = 2048
RET_HEADS = 4
RET_HEAD_DIM = 256
RET_WIDTH = RET_HEADS * RET_HEAD_DIM
SSM_WIDTH = D_MODEL - RET_WIDTH
SSM_HEAD_DIM = 64
SSM_HEADS = SSM_WIDTH // SSM_HEAD_DIM
SSM_GROUPS = 2
SSM_STATE = 128
CONV_WIDTH = 4
XBC_WIDTH = SSM_WIDTH + 2 * SSM_GROUPS * SSM_STATE
D_IN_PROJ = 4 * RET_WIDTH + SSM_WIDTH + XBC_WIDTH + SSM_HEADS
ROPE_BASE = 10000.0
N_EXPERTS = 32
TOP_K = 4
D_FF = D_MODEL
SWIGLU_LIMIT = 7.0
SWIGLU_ALPHA = 1.702
EPS = 1e-6

LANES = 128
VMEM_LIMIT = 56 * 1024 * 1024

RET_CHUNK = 256
SSD_CHUNK = 128
ROW_BLK = 128
ITEM_ROWS = 1536
ITEM_VARIANTS = (8, 9, 10, 12)
FF_TILE = 128
DOWN_TILE = 512
WEIGHT_SLOTS = 3
Y_SLOTS = 8
ROW_SLABS = D_MODEL // LANES
PACK_SLABS = ROW_SLABS // 2


def _params(sem, **kw):
    return pltpu.CompilerParams(dimension_semantics=sem, vmem_limit_bytes=VMEM_LIMIT, **kw)


def _dot(a, b):
    return jnp.dot(a, b, preferred_element_type=F32)


def _dot_nt(a, b):
    return lax.dot_general(a, b, (((1,), (1,)), ((), ())), preferred_element_type=F32)


def _dot_tn(a, b):
    return lax.dot_general(a, b, (((0,), (0,)), ((), ())), preferred_element_type=F32)


def _split3(x):
    hi = x.astype(BF16)
    r = x - hi.astype(F32)
    mid = r.astype(BF16)
    lo = (r - mid.astype(F32)).astype(BF16)
    return hi, mid, lo


def _dot_exact_rhs01(x, m01):
    hi, mid, lo = _split3(x)
    return _dot(hi, m01) + _dot(mid, m01) + _dot(lo, m01)


def _dot_exact_lhs01(m01, x):
    hi, mid, lo = _split3(x)
    return _dot(m01, hi) + _dot(m01, mid) + _dot(m01, lo)


def _silu(x):
    return x * jax.nn.sigmoid(x)


def _pack_bf16_pairs(lo, hi):
    lo_bits = lax.bitcast_convert_type(lo.astype(BF16).astype(F32), jnp.uint32)
    hi_bits = lax.bitcast_convert_type(hi.astype(BF16).astype(F32), jnp.uint32)
    return hi_bits | (lo_bits >> 16)


INPROJ_CHUNKS = 4


def _inproj_kernel(x_hbm, lnw_ref, w_ref, o_ref, xs_ref, u_ref, sem):
    tm = u_ref.shape[0]
    rows = tm // INPROJ_CHUNKS
    i = pl.program_id(0)
    j = pl.program_id(1)

    def copy(tile, c):
        src = x_hbm.at[pl.ds(pl.multiple_of(tile * tm + c * rows, rows), rows)]
        return pltpu.make_async_copy(src, xs_ref.at[c], sem.at[c])

    @pl.when(jnp.logical_and(i == 0, j == 0))
    def _():
        for c in range(INPROJ_CHUNKS):
            copy(0, c).start()

    @pl.when(j == 0)
    def _():
        for c in range(INPROJ_CHUNKS):
            copy(i, c).wait()
            x = xs_ref[c]
            ms = jnp.mean(x * x, axis=-1, keepdims=True)
            u_ref[c * rows:(c + 1) * rows, :] = (x * lax.rsqrt(ms + EPS) * lnw_ref[...]).astype(BF16)

    @pl.when(jnp.logical_and(j == pl.num_programs(1) - 1, i + 1 < pl.num_programs(0)))
    def _():
        for c in range(INPROJ_CHUNKS):
            copy(i + 1, c).start()

    tn = o_ref.shape[1]
    tail = D_IN_PROJ % tn
    if 0 < tail <= LANES:
        last = pl.num_programs(1) - 1

        @pl.when(pl.program_id(1) < last)
        def _():
            o_ref[...] = _dot_nt(u_ref[...], w_ref[...].astype(BF16))

        @pl.when(pl.program_id(1) == last)
        def _():
            o_ref[:, 0:LANES] = _dot_nt(u_ref[...], w_ref[0:LANES, :].astype(BF16))
    else:
        o_ref[...] = _dot_nt(u_ref[...], w_ref[...].astype(BF16))


def _inproj(x2, ln_w, w_in_t):
    T = x2.shape[0]
    tm, tn = (2048 if T % 2048 == 0 else 1024), 512
    return pl.pallas_call(
        _inproj_kernel,
        grid=(T // tm, pl.cdiv(D_IN_PROJ, tn)),
        in_specs=[pl.BlockSpec(memory_space=pl.ANY),
                  pl.BlockSpec((1, D_MODEL), lambda i, j: (0, 0)),
                  pl.BlockSpec((tn, D_MODEL), lambda i, j: (j, 0))],
        out_specs=pl.BlockSpec((tm, tn), lambda i, j: (i, j)),
        out_shape=jax.ShapeDtypeStruct((T, D_IN_PROJ), F32),
        scratch_shapes=[pltpu.VMEM((INPROJ_CHUNKS, tm // INPROJ_CHUNKS, D_MODEL), F32),
                        pltpu.VMEM((tm, D_MODEL), BF16),
                        pltpu.SemaphoreType.DMA((INPROJ_CHUNKS,))],
        compiler_params=_params(("arbitrary", "arbitrary")),
        name="inproj",
    )(x2, ln_w.reshape(1, D_MODEL), w_in_t)


def _retention_tables():
    C = RET_CHUNK
    h = np.arange(RET_HEADS, dtype=np.float64)
    log_gamma = np.log1p(-np.exp2(-5.0 - h))
    idx = np.arange(C, dtype=np.float64)
    rel = idx[:, None] - idx[None, :]
    intra = np.where(rel >= 0, np.exp(log_gamma[:, None, None] * np.maximum(rel, 0.0)), 0.0)
    q_decay = np.exp(log_gamma[:, None] * (idx + 1.0))
    k_decay = np.exp(log_gamma[:, None] * (C - 1.0 - idx))
    chunk_decay = np.exp(log_gamma * C)
    qd = np.broadcast_to(q_decay[:, :, None], (RET_HEADS, C, RET_HEAD_DIM))
    kd = np.broadcast_to(k_decay[:, :, None], (RET_HEADS, C, RET_HEAD_DIM))
    return (jnp.asarray(intra, F32), jnp.asarray(qd, F32), jnp.asarray(kd, F32),
            [float(c) for c in chunk_decay])


def _retention_kernel(chunk_decay, pos_ref, invf_ref, q_ref, k_ref, v_ref, g_ref,
                      intra_ref, qd_ref, kd_ref, o_ref, state_ref):
    @pl.when(pl.program_id(0) == 0)
    def _():
        state_ref[...] = jnp.zeros_like(state_ref)

    half = RET_HEAD_DIM // 2
    ang = pos_ref[...] * invf_ref[...]
    cos = jnp.cos(ang)
    sin = jnp.sin(ang)

    def rope(t):
        t1, t2 = t[:, :half], t[:, half:]
        return jnp.concatenate([t1 * cos - t2 * sin, t2 * cos + t1 * sin], axis=-1)

    for h in range(RET_HEADS):
        sl = slice(h * RET_HEAD_DIM, (h + 1) * RET_HEAD_DIM)
        q = rope(q_ref[:, sl])
        k = rope(k_ref[:, sl]) * (RET_HEAD_DIM ** -0.5)
        v = v_ref[:, sl].astype(BF16)
        state = state_ref[h]
        scores = _dot_nt(q.astype(BF16), k.astype(BF16)) * intra_ref[h]
        inner = _dot(scores.astype(BF16), v)
        cross = _dot((q * qd_ref[h]).astype(BF16), state.astype(BF16))
        state_ref[h] = chunk_decay[h] * state + _dot_tn((k * kd_ref[h]).astype(BF16), v)
        o = inner + cross
        o = o * lax.rsqrt(jnp.mean(o * o, axis=-1, keepdims=True) + EPS)
        o_ref[:, sl] = (o * _silu(g_ref[:, sl])).astype(o_ref.dtype)


def _retention(proj, pos_col, inv_freq):
    T = proj.shape[0]
    C = RET_CHUNK
    intra, qd, kd, chunk_decay = _retention_tables()
    col = lambda j: pl.BlockSpec((C, RET_WIDTH), lambda c, j=j: (c, j))
    const3 = lambda shape: pl.BlockSpec(shape, lambda c: (0, 0, 0))
    return pl.pallas_call(
        functools.partial(_retention_kernel, chunk_decay),
        grid=(T // C,),
        in_specs=[pl.BlockSpec((C, 1), lambda c: (c, 0)),
                  pl.BlockSpec((1, RET_HEAD_DIM // 2), lambda c: (0, 0)),
                  col(0), col(1), col(2), col(3),
                  const3((RET_HEADS, C, C)),
                  const3((RET_HEADS, C, RET_HEAD_DIM)),
                  const3((RET_HEADS, C, RET_HEAD_DIM))],
        out_specs=pl.BlockSpec((C, RET_WIDTH), lambda c: (c, 0)),
        out_shape=jax.ShapeDtypeStruct((T, RET_WIDTH), BF16),
        scratch_shapes=[pltpu.VMEM((RET_HEADS, RET_HEAD_DIM, RET_HEAD_DIM), F32)],
        compiler_params=_params(("arbitrary",)),
        name="retention",
    )(pos_col, inv_freq, proj, proj, proj, proj, intra, qd, kd)


def _ssd_kernel(xs0_ref, xs1_ref, bc_ref, z_ref, dt_ref, convw_ref, convb_ref, dtb_ref, a_ref,
                dskip_ref, normw_ref, expand_ref, o_ref, ext_ref, state_ref):
    C = SSD_CHUNK
    HW = SSM_WIDTH // SSM_GROUPS
    CARRY = 8

    @pl.when(pl.program_id(0) == 0)
    def _():
        ext_ref[0:CARRY, :] = jnp.zeros((CARRY, XBC_WIDTH), F32)
        state_ref[...] = jnp.zeros_like(state_ref)

    ext_ref[CARRY:CARRY + C, 0:HW] = xs0_ref[...]
    ext_ref[CARRY:CARRY + C, HW:2 * HW] = xs1_ref[...]
    ext_ref[CARRY:CARRY + C, 2 * HW:3 * HW] = bc_ref[...]
    conv = convb_ref[...]
    for k in range(CONV_WIDTH):
        off = CARRY - (CONV_WIDTH - 1) + k
        conv = conv + convw_ref[k:k + 1, :] * ext_ref[off:off + C, :]
    ext_ref[0:CARRY, :] = ext_ref[C:C + CARRY, :]
    xbc = _silu(conv)
    xs = xbc[:, :SSM_WIDTH]

    lane = lax.broadcasted_iota(jnp.int32, (1, LANES), 1)
    dt_raw = jnp.where(lane < SSM_HEADS, dt_ref[...], 0.0) + dtb_ref[...]
    dt = jnp.maximum(dt_raw, 0.0) + jnp.log1p(jnp.exp(-jnp.abs(dt_raw)))
    dta = dt * a_ref[...]

    row = lax.broadcasted_iota(jnp.int32, (C, C), 0)
    colm = lax.broadcasted_iota(jnp.int32, (C, C), 1)
    tril = row >= colm
    a_cum = _dot_exact_lhs01(jnp.where(tril, 1.0, 0.0).astype(BF16), dta)
    a_cum_t = a_cum.T

    expand = expand_ref[...]
    a_exp = _dot_exact_rhs01(a_cum, expand)
    dt_exp = _dot_exact_rhs01(dt, expand)
    a_last = a_exp[C - 1:C, :]
    decay_in = jnp.exp(a_exp)
    decay_out = jnp.exp(a_last - a_exp)
    chunk_decay = jnp.exp(a_last)
    xdt = xs * dt_exp

    lane2 = lax.broadcasted_iota(jnp.int32, (1, LANES), 1)
    lo_head = lane2 < SSM_HEAD_DIM
    ys = []
    for g in range(SSM_GROUPS):
        gs = slice(g * HW, (g + 1) * HW)
        b_g = xbc[:, SSM_WIDTH + g * SSM_STATE:SSM_WIDTH + (g + 1) * SSM_STATE].astype(BF16)
        c0 = SSM_WIDTH + SSM_GROUPS * SSM_STATE
        c_g = xbc[:, c0 + g * SSM_STATE:c0 + (g + 1) * SSM_STATE].astype(BF16)
        cb = _dot_nt(c_g, b_g)
        state = state_ref[g]
        y_off = _dot(c_g, state.astype(BF16)) * decay_in[:, gs]
        xw = (xdt[:, gs] * decay_out[:, gs]).astype(BF16)
        state_ref[g] = chunk_decay[:, gs] * state + _dot_tn(b_g, xw)
        slabs = []
        for s in range(HW // LANES):
            xd = xdt[:, g * HW + s * LANES:g * HW + (s + 1) * LANES]
            acc = None
            for e in range(2):
                hh = g * (SSM_HEADS // SSM_GROUPS) + 2 * s + e
                seg = a_cum[:, hh:hh + 1] - a_cum_t[hh:hh + 1, :]
                m = cb * jnp.exp(jnp.where(tril, seg, -jnp.inf))
                xm = jnp.where(lo_head if e == 0 else jnp.logical_not(lo_head), xd, 0.0)
                part = _dot(m.astype(BF16), xm.astype(BF16))
                acc = part if acc is None else acc + part
            slabs.append(acc)
        ys.append(jnp.concatenate(slabs, axis=-1) + y_off)
    y = jnp.concatenate(ys, axis=-1) + dskip_ref[...] * xs
    y = y * _silu(z_ref[...])
    outs = []
    for g in range(SSM_GROUPS):
        yg = y[:, g * HW:(g + 1) * HW]
        outs.append(yg * lax.rsqrt(jnp.mean(yg * yg, axis=-1, keepdims=True) + EPS))
    o_ref[...] = (jnp.concatenate(outs, axis=-1) * normw_ref[...]).astype(o_ref.dtype)


def _ssd(proj, conv_w, conv_b, dt_bias, a_log, d_skip, ssm_norm_w):
    T = proj.shape[0]
    C = SSD_CHUNK
    HW = SSM_WIDTH // SSM_GROUPS
    xbc0 = (4 * RET_WIDTH + SSM_WIDTH) // HW
    dt0 = (D_IN_PROJ - SSM_HEADS) // LANES
    pad = lambda v: jnp.zeros((1, LANES), F32).at[0, :SSM_HEADS].set(v.astype(F32))
    a_neg = pad(-jnp.exp(a_log.astype(F32)))
    expand_np = np.zeros((LANES, SSM_WIDTH), np.float32)
    for hh in range(SSM_HEADS):
        expand_np[hh, hh * SSM_HEAD_DIM:(hh + 1) * SSM_HEAD_DIM] = 1.0
    expand = jnp.asarray(expand_np, BF16)
    dskip_exp = jnp.repeat(d_skip.astype(F32), SSM_HEAD_DIM).reshape(1, SSM_WIDTH)
    const = lambda shape: pl.BlockSpec(shape, lambda c: (0, 0))
    return pl.pallas_call(
        _ssd_kernel,
        grid=(T // C,),
        in_specs=[pl.BlockSpec((C, HW), lambda c: (c, xbc0)),
                  pl.BlockSpec((C, HW), lambda c: (c, xbc0 + 1)),
                  pl.BlockSpec((C, HW), lambda c: (c, xbc0 + 2)),
                  pl.BlockSpec((C, SSM_WIDTH), lambda c: (c, 4 * RET_WIDTH // SSM_WIDTH)),
                  pl.BlockSpec((C, LANES), lambda c: (c, dt0)),
                  const((CONV_WIDTH, XBC_WIDTH)), const((1, XBC_WIDTH)),
                  const((1, LANES)), const((1, LANES)),
                  const((1, SSM_WIDTH)), const((1, SSM_WIDTH)),
                  const((LANES, SSM_WIDTH))],
        out_specs=pl.BlockSpec((C, SSM_WIDTH), lambda c: (c, 0)),
        out_shape=jax.ShapeDtypeStruct((T, SSM_WIDTH), BF16),
        scratch_shapes=[pltpu.VMEM((C + 8, XBC_WIDTH), F32),
                        pltpu.VMEM((SSM_GROUPS, SSM_STATE, HW), F32)],
        compiler_params=_params(("arbitrary",)),
        name="ssd",
    )(proj, proj, proj, proj, proj, conv_w, conv_b.reshape(1, XBC_WIDTH), pad(dt_bias), a_neg,
      dskip_exp, ssm_norm_w.reshape(1, SSM_WIDTH), expand)


def _outproj_kernel(ret_ref, ssm_ref, w_ref, x_ref, lnw_ref, wr_ref, br_ref, hs_ref, up_ref, ti_ref, tw_ref):
    tm = x_ref.shape[0]
    h = (x_ref[...] + _dot(ret_ref[...], w_ref[0:RET_WIDTH, :])
         + _dot(ssm_ref[...], w_ref[RET_WIDTH:D_MODEL, :]))
    for s in range(ROW_SLABS):
        hs_ref[pl.ds(s, tm, stride=ROW_SLABS), :] = h[:, s * LANES:(s + 1) * LANES]
    u = h * lax.rsqrt(jnp.mean(h * h, axis=-1, keepdims=True) + EPS) * lnw_ref[...]
    packed = _pack_bf16_pairs(u[:, :D_MODEL // 2], u[:, D_MODEL // 2:])
    for s in range(PACK_SLABS):
        up_ref[pl.ds(s, tm, stride=PACK_SLABS), :] = packed[:, s * LANES:(s + 1) * LANES]
    E = N_EXPERTS
    uh, um, ul = _split3(u)
    ph = _dot_nt(wr_ref[...], uh)
    pm = _dot_nt(wr_ref[...], um)
    pw = _dot_nt(wr_ref[...], ul)
    lg = (ph[0:E] + (ph[E:2 * E] + pm[0:E]) + (ph[2 * E:3 * E] + pm[E:2 * E] + pw[0:E])) + br_ref[...]
    expert = lax.broadcasted_iota(jnp.int32, lg.shape, 0)
    slot = lax.broadcasted_iota(jnp.int32, (TOP_K, tm), 0)
    top_l = jnp.zeros((TOP_K, tm), F32)
    top_i = jnp.zeros((TOP_K, tm), jnp.int32)
    work = lg
    for k in range(TOP_K):
        best = jnp.max(work, axis=0, keepdims=True)
        which = jnp.min(jnp.where(work == best, expert, E), axis=0, keepdims=True)
        top_l = jnp.where(slot == k, best, top_l)
        top_i = jnp.where(slot == k, which, top_i)
        work = jnp.where(expert == which, -jnp.inf, work)
    p = jnp.exp(top_l - jnp.max(top_l, axis=0, keepdims=True))
    ti_ref[...] = top_i
    tw_ref[...] = p / jnp.sum(p, axis=0, keepdims=True)


def _outproj(ret, ssm, w_out_bf16, x2, ln_w, w_router, b_router):
    T = x2.shape[0]
    tm = 512
    return pl.pallas_call(
        _outproj_kernel,
        grid=(T // tm,),
        in_specs=[pl.BlockSpec((tm, RET_WIDTH), lambda i: (i, 0)),
                  pl.BlockSpec((tm, SSM_WIDTH), lambda i: (i, 0)),
                  pl.BlockSpec((D_MODEL, D_MODEL), lambda i: (0, 0)),
                  pl.BlockSpec((tm, D_MODEL), lambda i: (i, 0)),
                  pl.BlockSpec((1, D_MODEL), lambda i: (0, 0)),
                  pl.BlockSpec((3 * N_EXPERTS, D_MODEL), lambda i: (0, 0)),
                  pl.BlockSpec((N_EXPERTS, 1), lambda i: (0, 0))],
        out_specs=[pl.BlockSpec((tm * ROW_SLABS, LANES), lambda i: (i, 0)),
                   pl.BlockSpec((tm * PACK_SLABS, LANES), lambda i: (i, 0)),
                   pl.BlockSpec((TOP_K, tm), lambda i: (0, i)),
                   pl.BlockSpec((TOP_K, tm), lambda i: (0, i))],
        out_shape=[jax.ShapeDtypeStruct((T * ROW_SLABS, LANES), F32),
                   jax.ShapeDtypeStruct((T * PACK_SLABS, LANES), jnp.uint32),
                   jax.ShapeDtypeStruct((TOP_K, T), jnp.int32),
                   jax.ShapeDtypeStruct((TOP_K, T), F32)],
        compiler_params=_params(("parallel",)),
        name="outproj",
    )(ret, ssm, w_out_bf16, x2, ln_w.reshape(1, D_MODEL),
      jnp.concatenate([p.T for p in _split3(w_router)], axis=0), b_router.reshape(N_EXPERTS, 1))


def _expert_kernel(n_ff, n_items, aliased, *refs):
    (item_e, item_row, item_nsub, item_nzero, item_live, item_src, sorted_tok,
     u_hbm, wg_hbm, wd_hbm, bgu_ref, bdn_ref) = refs[:12]
    (y_hbm, stage, xbuf, acc, ybuf, wgbuf, wdbuf, wa_s, wb_s, wd_s,
     sem_in, sem_out, sem_w) = refs[13:] if aliased else refs[12:]
    i = pl.program_id(0)
    f = pl.program_id(1)

    step = i * n_ff + f
    live_steps = item_live[0] * n_ff
    for k in range(1, n_items):
        live_steps = live_steps + item_live[k] * n_ff
    W2_ = 2 * FF_TILE

    def tile_copies(g):
        e = item_e[g // n_ff]
        ft = g % n_ff
        slot = g % WEIGHT_SLOTS
        cols = lambda b: pl.ds(pl.multiple_of(b * W2_, W2_), W2_)
        rows = lambda b: pl.ds(pl.multiple_of(b * FF_TILE, FF_TILE), FF_TILE)
        return [pltpu.make_async_copy(wg_hbm.at[e, :, cols(ft)], wgbuf.at[slot, 0], sem_w.at[slot]),
                pltpu.make_async_copy(wg_hbm.at[e, :, cols(n_ff + ft)], wgbuf.at[slot, 1], sem_w.at[slot]),
                pltpu.make_async_copy(wd_hbm.at[e, rows(ft), :], wdbuf.at[slot, 0], sem_w.at[slot]),
                pltpu.make_async_copy(wd_hbm.at[e, rows(n_ff + ft), :], wdbuf.at[slot, 1], sem_w.at[slot])]

    @pl.when(step == 0)
    def _():
        for g in range(WEIGHT_SLOTS - 1):
            @pl.when(g < live_steps)
            def _(g=g):
                for c in tile_copies(g):
                    c.start()

    @pl.when(step < live_steps)
    def _():
        for c in tile_copies(step):
            c.wait()

        @pl.when(step + WEIGHT_SLOTS - 1 < live_steps)
        def _():
            for c in tile_copies(step + WEIGHT_SLOTS - 1):
                c.start()

    wslot = step % WEIGHT_SLOTS
    wga_ref = wgbuf.at[wslot, 0]
    wgb_ref = wgbuf.at[wslot, 1]
    wda_ref = wdbuf.at[wslot, 0]
    wdb_ref = wdbuf.at[wslot, 1]
    nsub = item_nsub[i]
    nzero = item_nzero[i]
    row0 = item_row[i]
    SUB = ROW_BLK
    YS = SUB * ROW_SLABS
    PS = PACK_SLABS
    XS = SUB * PS

    def y_copy(slot, j):
        dst = y_hbm.at[pl.ds(pl.multiple_of((row0 + j * SUB) * ROW_SLABS, YS), YS)]
        return pltpu.make_async_copy(ybuf.at[pl.ds(pl.multiple_of(slot * YS, YS), YS)], dst, sem_out.at[slot])

    def gather_copy(tok, r):
        src = u_hbm.at[pl.ds(pl.multiple_of(tok * PS, PS), PS)]
        return pltpu.make_async_copy(src, stage.at[pl.ds(pl.multiple_of(r * PS, PS), PS)], sem_in)

    n_assign = sorted_tok.shape[0]
    covers = [jnp.logical_and(nsub > lo, nsub <= m) for lo, m in zip((0,) + ITEM_VARIANTS[:-1], ITEM_VARIANTS)]
    nxt = jnp.minimum(i + 1, n_items - 1)

    def gathered_subs(item):
        ns = item_nsub[item]
        size = sum(jnp.where(jnp.logical_and(ns > lo, ns <= m), m, 0)
                   for lo, m in zip((0,) + ITEM_VARIANTS[:-1], ITEM_VARIANTS))
        return jnp.maximum(size, n_ff)

    def issue_rows(item, lo, count, unrolled):
        src0 = item_src[item]

        def one(r, p):
            gather_copy(sorted_tok[jnp.minimum(src0 + r, n_assign - 1)], r).start(priority=p)

        if unrolled:
            for k in range(count):
                one(lo + k, k % 2)
        else:
            def body(q, c):
                for p in range(2):
                    one(lo + 2 * q + p, p)
                return c
            lax.fori_loop(0, count // 2, body, 0)

    def wait_rows(item):
        def body(j, c):
            pltpu.make_async_copy(u_hbm.at[pl.ds(0, XS)], stage.at[pl.ds(0, XS)], sem_in).wait()
            return c
        lax.fori_loop(0, gathered_subs(item), body, 0)

    @pl.when(jnp.logical_and(i == 0, f == 0))
    def _():
        issue_rows(0, 0, gathered_subs(0) * SUB, False)

    @pl.when(f == n_ff - 1)
    def _():
        issue_rows(nxt, n_ff * SUB, (gathered_subs(nxt) - n_ff) * SUB, False)

    lane = lax.broadcasted_iota(jnp.int32, (1, 2 * FF_TILE), 1)
    even = (lane % 2) == 0
    e_row = pl.ds(item_e[i], 1)
    bga = bgu_ref[e_row, pl.ds(pl.multiple_of(f * (2 * FF_TILE), 2 * FF_TILE), 2 * FF_TILE)]
    bgb = bgu_ref[e_row, pl.ds(pl.multiple_of((n_ff + f) * (2 * FF_TILE), 2 * FF_TILE), 2 * FF_TILE)]
    W2 = 2 * FF_TILE

    @pl.when(f == 0)
    def _():
        wait_rows(i)
        half = D_MODEL // 2

        def unpack(j, c):
            rows = pl.ds(pl.multiple_of(j * SUB, SUB), SUB)
            for s in range(PS):
                p = stage[pl.ds(j * XS + s, SUB, stride=PS), :]
                lo = lax.bitcast_convert_type(p << 16, F32)
                hi = lax.bitcast_convert_type(p & jnp.uint32(0xFFFF0000), F32)
                xbuf[rows, s * LANES:(s + 1) * LANES] = lo.astype(BF16)
                xbuf[rows, half + s * LANES:half + (s + 1) * LANES] = hi.astype(BF16)
            acc[rows, :] = jnp.zeros((SUB, D_MODEL), F32)
            return c
        lax.fori_loop(0, sum(jnp.where(c, m, 0) for m, c in zip(ITEM_VARIANTS, covers)), unpack, 0)

    @pl.when(nsub == 0)
    def _():
        issue_rows(nxt, f * SUB, SUB, False)

    def block(rows):
        xs = xbuf[0:rows, :]
        wa_s[...] = wga_ref[...].astype(BF16)
        ga = _dot(xs, wa_s[...]) + bga
        issue_rows(nxt, f * SUB, SUB, True)
        wb_s[...] = wgb_ref[...].astype(BF16)
        gb = _dot(xs, wb_s[...]) + bgb
        wd_s[...] = pltpu.bitcast(_pack_bf16_pairs(wda_ref[...], wdb_ref[...]), BF16)
        gate = jnp.where(even, ga, pltpu.roll(gb, 1, 1))
        up = jnp.where(even, pltpu.roll(ga, W2 - 1, 1), gb)
        gate = jnp.minimum(gate, SWIGLU_LIMIT)
        up = jnp.clip(up, -SWIGLU_LIMIT, SWIGLU_LIMIT)
        act = ((up + 1.0) * (gate * jax.nn.sigmoid(gate * SWIGLU_ALPHA))).astype(BF16)
        for n in range(0, D_MODEL, DOWN_TILE):
            acc[0:rows, n:n + DOWN_TILE] += _dot(act, wd_s[:, n:n + DOWN_TILE])

    for m, cover in zip(ITEM_VARIANTS, covers):
        @pl.when(cover)
        def _(m=m):
            block(m * SUB)

    @pl.when(f == n_ff - 1)
    def _():
        @pl.when(i == n_items - 1)
        def _():
            wait_rows(nxt)

        def stage_out(q, c):
            slot = q % Y_SLOTS

            @pl.when(q >= Y_SLOTS)
            def _():
                y_copy(slot, 0).wait()
            val = acc[pl.ds(pl.multiple_of(q * SUB, SUB), SUB), :] + bdn_ref[e_row, :]
            for s in range(ROW_SLABS):
                ybuf[pl.ds(slot * YS + s, SUB, stride=ROW_SLABS), :] = val[:, s * LANES:(s + 1) * LANES]
            y_copy(slot, q).start()
            return c
        lax.fori_loop(0, nsub, stage_out, 0)

        for k in range(Y_SLOTS):
            @pl.when(nsub > k)
            def _(k=k):
                y_copy(k, 0).wait()

    @pl.when(jnp.logical_and(f == n_ff - 1, nzero > 0))
    def _():
        ybuf[0:YS, :] = jnp.zeros((YS, LANES), F32)

        def start(j, c):
            y_copy(0, j).start()
            return c
        lax.fori_loop(0, nzero, start, 0)

        def wait(j, c):
            y_copy(0, 0).wait()
            return c
        lax.fori_loop(0, nzero, wait, 0)


def _experts_call(u_packed, sorted_tok, n_rows, items, w_gate_up, b_gate_up, w_down, b_down, y_in=None):
    n_ff = (D_FF // 2) // FF_TILE
    assert ITEM_VARIANTS[0] >= n_ff and ITEM_VARIANTS[-1] * ROW_BLK == ITEM_ROWS
    W2 = 2 * FF_TILE
    item_e, item_row, item_nsub, item_nzero, item_live, item_src = items
    n_items = item_e.shape[0]
    aliased = y_in is not None

    whole = lambda shape: pl.BlockSpec(shape, lambda i, f, e, r, n, z, lv, sr, st: (0, 0))
    grid_spec = pltpu.PrefetchScalarGridSpec(
        num_scalar_prefetch=7,
        grid=(n_items, n_ff),
        in_specs=[
            pl.BlockSpec(memory_space=pl.ANY),
            pl.BlockSpec(memory_space=pl.ANY),
            pl.BlockSpec(memory_space=pl.ANY),
            whole((N_EXPERTS, 2 * D_FF)),
            whole((N_EXPERTS, D_MODEL)),
        ] + ([pl.BlockSpec(memory_space=pl.ANY)] if aliased else []),
        out_specs=pl.BlockSpec(memory_space=pl.ANY),
        scratch_shapes=[pltpu.VMEM((ITEM_ROWS * PACK_SLABS, LANES), jnp.uint32),
                        pltpu.VMEM((ITEM_ROWS, D_MODEL), BF16),
                        pltpu.VMEM((ITEM_ROWS, D_MODEL), F32),
                        pltpu.VMEM((Y_SLOTS * ROW_BLK * ROW_SLABS, LANES), F32),
                        pltpu.VMEM((WEIGHT_SLOTS, 2, D_MODEL, W2), F32),
                        pltpu.VMEM((WEIGHT_SLOTS, 2, FF_TILE, D_MODEL), F32),
                        pltpu.VMEM((D_MODEL, W2), BF16),
                        pltpu.VMEM((D_MODEL, W2), BF16),
                        pltpu.VMEM((W2, D_MODEL), BF16),
                        pltpu.SemaphoreType.DMA(()),
                        pltpu.SemaphoreType.DMA((Y_SLOTS,)),
                        pltpu.SemaphoreType.DMA((WEIGHT_SLOTS,))],
    )
    args = (item_e, item_row, item_nsub, item_nzero, item_live, item_src, sorted_tok, u_packed,
            w_gate_up, w_down, b_gate_up, b_down)
    return pl.pallas_call(
        functools.partial(_expert_kernel, n_ff, n_items, aliased),
        grid_spec=grid_spec,
        out_shape=jax.ShapeDtypeStruct((n_rows * ROW_SLABS, LANES), F32),
        input_output_aliases={len(args): 0} if aliased else {},
        compiler_params=_params(("arbitrary", "arbitrary")),
        name="experts_overflow" if aliased else "experts",
    )(*args, *((y_in,) if aliased else ()))


def _experts(u_packed, sorted_tok, n_rows, items, n_used, w_gate_up, b_gate_up, w_down, b_down):
    n_main = N_EXPERTS + 1
    weights = (w_gate_up, b_gate_up, w_down, b_down)
    y = _experts_call(u_packed, sorted_tok, n_rows, tuple(a[:n_main] for a in items), *weights)
    rest = tuple(a[n_main:] for a in items)
    return lax.cond(n_used > n_main,
                    lambda y_: _experts_call(u_packed, sorted_tok, n_rows, rest, *weights, y_in=y_),
                    lambda y_: y_, y)


def _combine_kernel(tm, n_steps, dest_ref, w_ref, y_hbm, h_ref, lnw_ref, o_ref, buf, osum, sem):
    i = pl.program_id(0)

    def copy(step, slot, t, k):
        d = dest_ref[(step * tm + t) * TOP_K + k]
        return pltpu.make_async_copy(y_hbm.at[d], buf.at[slot, k * tm + t], sem.at[slot])

    def issue(step, slot):
        def body(t, c):
            for k in range(TOP_K):
                copy(step, slot, t, k).start(priority=k % 2)
            return c
        lax.fori_loop(0, tm, body, 0, unroll=4)

    @pl.when(i == 0)
    def _():
        issue(0, 0)

    @pl.when(i + 1 < n_steps)
    def _():
        issue(i + 1, (i + 1) % 2)

    slot = i % 2

    pltpu.make_async_copy(y_hbm.at[pl.ds(0, TOP_K * tm)], buf.at[slot], sem.at[slot]).wait()

    def token(t, c):
        a = h_ref[t]
        for k in range(TOP_K):
            a = a + w_ref[(i * tm + t) * TOP_K + k] * buf[slot, k * tm + t]
        osum[pl.ds(pl.multiple_of(t * ROW_SLABS, ROW_SLABS), ROW_SLABS), :] = a
        return c
    lax.fori_loop(0, tm, token, 0, unroll=4)

    h = osum[...].reshape(tm, ROW_SLABS, LANES)
    ms = jnp.mean(jnp.mean(h * h, axis=2, keepdims=True), axis=1, keepdims=True)
    osum[...] = (h * lax.rsqrt(ms + EPS) * lnw_ref[...]).reshape(tm * ROW_SLABS, LANES)
    for s in range(ROW_SLABS):
        o_ref[:, s * LANES:(s + 1) * LANES] = osum[pl.ds(s, tm, stride=ROW_SLABS), :]


def _combine(dest, wflat, y3, h3, ln_w):
    T = h3.shape[0]
    tm = 128
    n_steps = T // tm
    grid_spec = pltpu.PrefetchScalarGridSpec(
        num_scalar_prefetch=2,
        grid=(n_steps,),
        in_specs=[pl.BlockSpec(memory_space=pl.ANY),
                  pl.BlockSpec((tm, ROW_SLABS, LANES), lambda i, d, w: (i, 0, 0)),
                  pl.BlockSpec((1, ROW_SLABS, LANES), lambda i, d, w: (0, 0, 0))],
        out_specs=pl.BlockSpec((tm, D_MODEL), lambda i, d, w: (i, 0)),
        scratch_shapes=[pltpu.VMEM((2, TOP_K * tm, ROW_SLABS, LANES), F32),
                        pltpu.VMEM((tm * ROW_SLABS, LANES), F32),
                        pltpu.SemaphoreType.DMA((2,))],
    )
    return pl.pallas_call(
        functools.partial(_combine_kernel, tm, n_steps),
        grid_spec=grid_spec,
        out_shape=jax.ShapeDtypeStruct((T, D_MODEL), F32),
        compiler_params=_params(("arbitrary",)),
        name="combine",
    )(dest, wflat, y3, h3, ln_w.reshape(1, ROW_SLABS, LANES))


def _route(top_idx, top_w, n_rows, n_items):
    T = top_idx.shape[0]
    e_flat = top_idx.reshape(-1).astype(jnp.int32)
    onehot = (e_flat[:, None] == jnp.arange(N_EXPERTS, dtype=jnp.int32)[None, :]).astype(jnp.int32)
    csum = jnp.cumsum(onehot, axis=0)
    counts = csum[-1]
    padded = (counts + ROW_BLK - 1) // ROW_BLK * ROW_BLK
    pend = jnp.cumsum(padded)
    pstart = pend - padded
    dest = jnp.sum(onehot * (pstart[None, :] + csum - onehot), axis=1).astype(jnp.int32)

    per_e = (padded + ITEM_ROWS - 1) // ITEM_ROWS
    iend = jnp.cumsum(per_e)
    istart = iend - per_e
    ii = jnp.arange(n_items, dtype=jnp.int32)
    total = iend[-1]
    live = (ii < total).astype(jnp.int32)
    ic = jnp.minimum(ii, total - 1)
    ie = jnp.minimum(jnp.sum(ic[:, None] >= iend[None, :], axis=1), N_EXPERTS - 1).astype(jnp.int32)
    within = ic - istart[ie]
    irow = (pstart[ie] + within * ITEM_ROWS).astype(jnp.int32)
    insub = jnp.minimum((padded[ie] - within * ITEM_ROWS) // ROW_BLK, ITEM_ROWS // ROW_BLK).astype(jnp.int32)
    insub = insub * live
    tail_rows = n_rows - pend[-1]
    tail = jnp.logical_and(ii == total, tail_rows > 0)
    inzero = jnp.where(tail, tail_rows // ROW_BLK, 0).astype(jnp.int32)
    irow = jnp.where(tail, pend[-1], irow).astype(jnp.int32)
    n_used = total + (tail_rows > 0).astype(jnp.int32)
    n_assign = T * TOP_K
    assert N_EXPERTS * n_assign < 2 ** 31
    order = jnp.sort(e_flat * n_assign + jnp.arange(n_assign, dtype=jnp.int32)) % n_assign
    sorted_tok = (order // TOP_K).astype(jnp.int32)
    cstart = jnp.cumsum(counts) - counts
    isrc = ((cstart[ie] + within * ITEM_ROWS) * live).astype(jnp.int32)
    return dest, top_w.reshape(-1).astype(F32), (ie, irow, insub, inzero, live, isrc), n_used, sorted_tok


def kernel(x, positions, ln_mix_w, w_in, conv_w, conv_b, dt_bias, a_log, d_skip, ssm_norm_w, w_out,
           ln_ffn_w, w_router, b_router, w_gate_up, b_gate_up, w_down, b_down, ln_final_w):
    B, L, _ = x.shape
    T = B * L
    assert B == 1 and T % 1024 == 0
    x2 = x.reshape(T, D_MODEL)
    half = RET_HEAD_DIM // 2
    inv_freq = (ROPE_BASE ** (-jnp.arange(half, dtype=F32) / half)).reshape(1, half)
    pos_col = positions.reshape(T, 1).astype(F32)

    proj = _inproj(x2, ln_mix_w[0], jnp.swapaxes(w_in[0], 0, 1))
    ret = _retention(proj, pos_col, inv_freq)
    ssm = _ssd(proj, conv_w[0], conv_b[0], dt_bias[0], a_log[0], d_skip[0], ssm_norm_w[0])
    h_slabs, u_packed, top_idx, top_w = _outproj(ret, ssm, w_out[0].astype(BF16), x2, ln_ffn_w[0], w_router[0],
                                                 b_router[0])

    n_rows = -(-(T * TOP_K + N_EXPERTS * (ROW_BLK - 1)) // ROW_BLK) * ROW_BLK
    n_items = N_EXPERTS + 1 + n_rows // ITEM_ROWS
    dest, wflat, items, n_used, sorted_tok = _route(top_idx.T, top_w.T, n_rows, n_items)

    y_rows = _experts(u_packed, sorted_tok, n_rows, items, n_used, w_gate_up[0], b_gate_up[0], w_down[0],
                      b_down[0])
    out = _combine(dest, wflat, y_rows.reshape(n_rows, ROW_SLABS, LANES),
                   h_slabs.reshape(T, ROW_SLABS, LANES), ln_final_w)
    return out.reshape(B, L, D_MODEL)
```

```python
import functools

import numpy as np
import jax
import jax.numpy as jnp
from jax import lax
from jax.experimental import pallas as pl
from jax.experimental.pallas import tpu as pltpu

F32 = jnp.float32
BF16 = jnp.bfloat16

D_MODEL = 2048
RET_HEADS = 4
RET_HEAD_DIM = 256
RET_WIDTH = RET_HEADS * RET_HEAD_DIM
SSM_WIDTH = D_MODEL - RET_WIDTH
SSM_HEAD_DIM = 64
SSM_HEADS = SSM_WIDTH // SSM_HEAD_DIM
SSM_GROUPS = 2
SSM_STATE = 128
CONV_WIDTH = 4
XBC_WIDTH = SSM_WIDTH + 2 * SSM_GROUPS * SSM_STATE
D_IN_PROJ = 4 * RET_WIDTH + SSM_WIDTH + XBC_WIDTH + SSM_HEADS
ROPE_BASE = 10000.0
N_EXPERTS = 32
TOP_K = 4
D_FF = D_MODEL
SWIGLU_LIMIT = 7.0
SWIGLU_ALPHA = 1.702
EPS = 1e-6

LANES = 128
VMEM_LIMIT = 56 * 1024 * 1024

RET_CHUNK = 256
SSD_CHUNK = 128
ROW_BLK = 128
ITEM_ROWS = 1536
ITEM_VARIANTS = (8, 9, 10, 12)
FF_TILE = 128
DOWN_TILE = 512
WEIGHT_SLOTS = 3
Y_SLOTS = 8
ROW_SLABS = D_MODEL // LANES
PACK_SLABS = ROW_SLABS // 2


def _params(sem, **kw):
    return pltpu.CompilerParams(dimension_semantics=sem, vmem_limit_bytes=VMEM_LIMIT, **kw)


def _dot(a, b):
    return jnp.dot(a, b, preferred_element_type=F32)


def _dot_nt(a, b):
    return lax.dot_general(a, b, (((1,), (1,)), ((), ())), preferred_element_type=F32)


def _dot_tn(a, b):
    return lax.dot_general(a, b, (((0,), (0,)), ((), ())), preferred_element_type=F32)


def _split3(x):
    hi = x.astype(BF16)
    r = x - hi.astype(F32)
    mid = r.astype(BF16)
    lo = (r - mid.astype(F32)).astype(BF16)
    return hi, mid, lo


def _dot_exact_rhs01(x, m01):
    hi, mid, lo = _split3(x)
    return _dot(hi, m01) + _dot(mid, m01) + _dot(lo, m01)


def _dot_exact_lhs01(m01, x):
    hi, mid, lo = _split3(x)
    return _dot(m01, hi) + _dot(m01, mid) + _dot(m01, lo)


def _silu(x):
    return x * jax.nn.sigmoid(x)


def _pack_bf16_pairs(lo, hi):
    lo_bits = lax.bitcast_convert_type(lo.astype(BF16).astype(F32), jnp.uint32)
    hi_bits = lax.bitcast_convert_type(hi.astype(BF16).astype(F32), jnp.uint32)
    return hi_bits | (lo_bits >> 16)


INPROJ_CHUNKS = 4


def _inproj_kernel(x_hbm, lnw_ref, w_ref, o_ref, xs_ref, u_ref, sem):
    tm = u_ref.shape[0]
    rows = tm // INPROJ_CHUNKS
    i = pl.program_id(0)
    j = pl.program_id(1)

    def copy(tile, c):
        src = x_hbm.at[pl.ds(pl.multiple_of(tile * tm + c * rows, rows), rows)]
        return pltpu.make_async_copy(src, xs_ref.at[c], sem.at[c])

    @pl.when(jnp.logical_and(i == 0, j == 0))
    def _():
        for c in range(INPROJ_CHUNKS):
            copy(0, c).start()

    @pl.when(j == 0)
    def _():
        for c in range(INPROJ_CHUNKS):
            copy(i, c).wait()
            x = xs_ref[c]
            ms = jnp.mean(x * x, axis=-1, keepdims=True)
            u_ref[c * rows:(c + 1) * rows, :] = (x * lax.rsqrt(ms + EPS) * lnw_ref[...]).astype(BF16)

    @pl.when(jnp.logical_and(j == pl.num_programs(1) - 1, i + 1 < pl.num_programs(0)))
    def _():
        for c in range(INPROJ_CHUNKS):
            copy(i + 1, c).start()

    tn = o_ref.shape[1]
    tail = D_IN_PROJ % tn
    if 0 < tail <= LANES:
        last = pl.num_programs(1) - 1

        @pl.when(pl.program_id(1) < last)
        def _():
            o_ref[...] = _dot_nt(u_ref[...], w_ref[...].astype(BF16))

        @pl.when(pl.program_id(1) == last)
        def _():
            o_ref[:, 0:LANES] = _dot_nt(u_ref[...], w_ref[0:LANES, :].astype(BF16))
    else:
        o_ref[...] = _dot_nt(u_ref[...], w_ref[...].astype(BF16))


def _inproj(x2, ln_w, w_in_t):
    T = x2.shape[0]
    tm, tn = (2048 if T % 2048 == 0 else 1024), 512
    return pl.pallas_call(
        _inproj_kernel,
        grid=(T // tm, pl.cdiv(D_IN_PROJ, tn)),
        in_specs=[pl.BlockSpec(memory_space=pl.ANY),
                  pl.BlockSpec((1, D_MODEL), lambda i, j: (0, 0)),
                  pl.BlockSpec((tn, D_MODEL), lambda i, j: (j, 0))],
        out_specs=pl.BlockSpec((tm, tn), lambda i, j: (i, j)),
        out_shape=jax.ShapeDtypeStruct((T, D_IN_PROJ), F32),
        scratch_shapes=[pltpu.VMEM((INPROJ_CHUNKS, tm // INPROJ_CHUNKS, D_MODEL), F32),
                        pltpu.VMEM((tm, D_MODEL), BF16),
                        pltpu.SemaphoreType.DMA((INPROJ_CHUNKS,))],
        compiler_params=_params(("arbitrary", "arbitrary")),
        name="inproj",
    )(x2, ln_w.reshape(1, D_MODEL), w_in_t)


def _retention_tables():
    C = RET_CHUNK
    h = np.arange(RET_HEADS, dtype=np.float64)
    log_gamma = np.log1p(-np.exp2(-5.0 - h))
    idx = np.arange(C, dtype=np.float64)
    rel = idx[:, None] - idx[None, :]
    intra = np.where(rel >= 0, np.exp(log_gamma[:, None, None] * np.maximum(rel, 0.0)), 0.0)
    q_decay = np.exp(log_gamma[:, None] * (idx + 1.0))
    k_decay = np.exp(log_gamma[:, None] * (C - 1.0 - idx))
    chunk_decay = np.exp(log_gamma * C)
    qd = np.broadcast_to(q_decay[:, :, None], (RET_HEADS, C, RET_HEAD_DIM))
    kd = np.broadcast_to(k_decay[:, :, None], (RET_HEADS, C, RET_HEAD_DIM))
    return (jnp.asarray(intra, F32), jnp.asarray(qd, F32), jnp.asarray(kd, F32),
            [float(c) for c in chunk_decay])


def _retention_kernel(chunk_decay, pos_ref, invf_ref, q_ref, k_ref, v_ref, g_ref,
                      intra_ref, qd_ref, kd_ref, o_ref, state_ref):
    @pl.when(pl.program_id(0) == 0)
    def _():
        state_ref[...] = jnp.zeros_like(state_ref)

    half = RET_HEAD_DIM // 2
    ang = pos_ref[...] * invf_ref[...]
    cos = jnp.cos(ang)
    sin = jnp.sin(ang)

    def rope(t):
        t1, t2 = t[:, :half], t[:, half:]
        return jnp.concatenate([t1 * cos - t2 * sin, t2 * cos + t1 * sin], axis=-1)

    for h in range(RET_HEADS):
        sl = slice(h * RET_HEAD_DIM, (h + 1) * RET_HEAD_DIM)
        q = rope(q_ref[:, sl])
        k = rope(k_ref[:, sl]) * (RET_HEAD_DIM ** -0.5)
        v = v_ref[:, sl].astype(BF16)
        state = state_ref[h]
        scores = _dot_nt(q.astype(BF16), k.astype(BF16)) * intra_ref[h]
        inner = _dot(scores.astype(BF16), v)
        cross = _dot((q * qd_ref[h]).astype(BF16), state.astype(BF16))
        state_ref[h] = chunk_decay[h] * state + _dot_tn((k * kd_ref[h]).astype(BF16), v)
        o = inner + cross
        o = o * lax.rsqrt(jnp.mean(o * o, axis=-1, keepdims=True) + EPS)
        o_ref[:, sl] = (o * _silu(g_ref[:, sl])).astype(o_ref.dtype)


def _retention(proj, pos_col, inv_freq):
    T = proj.shape[0]
    C = RET_CHUNK
    intra, qd, kd, chunk_decay = _retention_tables()
    col = lambda j: pl.BlockSpec((C, RET_WIDTH), lambda c, j=j: (c, j))
    const3 = lambda shape: pl.BlockSpec(shape, lambda c: (0, 0, 0))
    return pl.pallas_call(
        functools.partial(_retention_kernel, chunk_decay),
        grid=(T // C,),
        in_specs=[pl.BlockSpec((C, 1), lambda c: (c, 0)),
                  pl.BlockSpec((1, RET_HEAD_DIM // 2), lambda c: (0, 0)),
                  col(0), col(1), col(2), col(3),
                  const3((RET_HEADS, C, C)),
                  const3((RET_HEADS, C, RET_HEAD_DIM)),
                  const3((RET_HEADS, C, RET_HEAD_DIM))],
        out_specs=pl.BlockSpec((C, RET_WIDTH), lambda c: (c, 0)),
        out_shape=jax.ShapeDtypeStruct((T, RET_WIDTH), BF16),
        scratch_shapes=[pltpu.VMEM((RET_HEADS, RET_HEAD_DIM, RET_HEAD_DIM), F32)],
        compiler_params=_params(("arbitrary",)),
        name="retention",
    )(pos_col, inv_freq, proj, proj, proj, proj, intra, qd, kd)


def _ssd_kernel(xs0_ref, xs1_ref, bc_ref, z_ref, dt_ref, convw_ref, convb_ref, dtb_ref, a_ref,
                dskip_ref, normw_ref, expand_ref, o_ref, ext_ref, state_ref):
    C = SSD_CHUNK
    HW = SSM_WIDTH // SSM_GROUPS
    CARRY = 8

    @pl.when(pl.program_id(0) == 0)
    def _():
        ext_ref[0:CARRY, :] = jnp.zeros((CARRY, XBC_WIDTH), F32)
        state_ref[...] = jnp.zeros_like(state_ref)

    ext_ref[CARRY:CARRY + C, 0:HW] = xs0_ref[...]
    ext_ref[CARRY:CARRY + C, HW:2 * HW] = xs1_ref[...]
    ext_ref[CARRY:CARRY + C, 2 * HW:3 * HW] = bc_ref[...]
    conv = convb_ref[...]
    for k in range(CONV_WIDTH):
        off = CARRY - (CONV_WIDTH - 1) + k
        conv = conv + convw_ref[k:k + 1, :] * ext_ref[off:off + C, :]
    ext_ref[0:CARRY, :] = ext_ref[C:C + CARRY, :]
    xbc = _silu(conv)
    xs = xbc[:, :SSM_WIDTH]

    lane = lax.broadcasted_iota(jnp.int32, (1, LANES), 1)
    dt_raw = jnp.where(lane < SSM_HEADS, dt_ref[...], 0.0) + dtb_ref[...]
    dt = jnp.maximum(dt_raw, 0.0) + jnp.log1p(jnp.exp(-jnp.abs(dt_raw)))
    dta = dt * a_ref[...]

    row = lax.broadcasted_iota(jnp.int32, (C, C), 0)
    colm = lax.broadcasted_iota(jnp.int32, (C, C), 1)
    tril = row >= colm
    a_cum = _dot_exact_lhs01(jnp.where(tril, 1.0, 0.0).astype(BF16), dta)
    a_cum_t = a_cum.T

    expand = expand_ref[...]
    a_exp = _dot_exact_rhs01(a_cum, expand)
    dt_exp = _dot_exact_rhs01(dt, expand)
    a_last = a_exp[C - 1:C, :]
    decay_in = jnp.exp(a_exp)
    decay_out = jnp.exp(a_last - a_exp)
    chunk_decay = jnp.exp(a_last)
    xdt = xs * dt_exp

    lane2 = lax.broadcasted_iota(jnp.int32, (1, LANES), 1)
    lo_head = lane2 < SSM_HEAD_DIM
    ys = []
    for g in range(SSM_GROUPS):
        gs = slice(g * HW, (g + 1) * HW)
        b_g = xbc[:, SSM_WIDTH + g * SSM_STATE:SSM_WIDTH + (g + 1) * SSM_STATE].astype(BF16)
        c0 = SSM_WIDTH + SSM_GROUPS * SSM_STATE
        c_g = xbc[:, c0 + g * SSM_STATE:c0 + (g + 1) * SSM_STATE].astype(BF16)
        cb = _dot_nt(c_g, b_g)
        state = state_ref[g]
        y_off = _dot(c_g, state.astype(BF16)) * decay_in[:, gs]
        xw = (xdt[:, gs] * decay_out[:, gs]).astype(BF16)
        state_ref[g] = chunk_decay[:, gs] * state + _dot_tn(b_g, xw)
        slabs = []
        for s in range(HW // LANES):
            xd = xdt[:, g * HW + s * LANES:g * HW + (s + 1) * LANES]
            acc = None
            for e in range(2):
                hh = g * (SSM_HEADS // SSM_GROUPS) + 2 * s + e
                seg = a_cum[:, hh:hh + 1] - a_cum_t[hh:hh + 1, :]
                m = cb * jnp.exp(jnp.where(tril, seg, -jnp.inf))
                xm = jnp.where(lo_head if e == 0 else jnp.logical_not(lo_head), xd, 0.0)
                part = _dot(m.astype(BF16), xm.astype(BF16))
                acc = part if acc is None else acc + part
            slabs.append(acc)
        ys.append(jnp.concatenate(slabs, axis=-1) + y_off)
    y = jnp.concatenate(ys, axis=-1) + dskip_ref[...] * xs
    y = y * _silu(z_ref[...])
    outs = []
    for g in range(SSM_GROUPS):
        yg = y[:, g * HW:(g + 1) * HW]
        outs.append(yg * lax.rsqrt(jnp.mean(yg * yg, axis=-1, keepdims=True) + EPS))
    o_ref[...] = (jnp.concatenate(outs, axis=-1) * normw_ref[...]).astype(o_ref.dtype)


def _ssd(proj, conv_w, conv_b, dt_bias, a_log, d_skip, ssm_norm_w):
    T = proj.shape[0]
    C = SSD_CHUNK
    HW = SSM_WIDTH // SSM_GROUPS
    xbc0 = (4 * RET_WIDTH + SSM_WIDTH) // HW
    dt0 = (D_IN_PROJ - SSM_HEADS) // LANES
    pad = lambda v: jnp.zeros((1, LANES), F32).at[0, :SSM_HEADS].set(v.astype(F32))
    a_neg = pad(-jnp.exp(a_log.astype(F32)))
    expand_np = np.zeros((LANES, SSM_WIDTH), np.float32)
    for hh in range(SSM_HEADS):
        expand_np[hh, hh * SSM_HEAD_DIM:(hh + 1) * SSM_HEAD_DIM] = 1.0
    expand = jnp.asarray(expand_np, BF16)
    dskip_exp = jnp.repeat(d_skip.astype(F32), SSM_HEAD_DIM).reshape(1, SSM_WIDTH)
    const = lambda shape: pl.BlockSpec(shape, lambda c: (0, 0))
    return pl.pallas_call(
        _ssd_kernel,
        grid=(T // C,),
        in_specs=[pl.BlockSpec((C, HW), lambda c: (c, xbc0)),
                  pl.BlockSpec((C, HW), lambda c: (c, xbc0 + 1)),
                  pl.BlockSpec((C, HW), lambda c: (c, xbc0 + 2)),
                  pl.BlockSpec((C, SSM_WIDTH), lambda c: (c, 4 * RET_WIDTH // SSM_WIDTH)),
                  pl.BlockSpec((C, LANES), lambda c: (c, dt0)),
                  const((CONV_WIDTH, XBC_WIDTH)), const((1, XBC_WIDTH)),
                  const((1, LANES)), const((1, LANES)),
                  const((1, SSM_WIDTH)), const((1, SSM_WIDTH)),
                  const((LANES, SSM_WIDTH))],
        out_specs=pl.BlockSpec((C, SSM_WIDTH), lambda c: (c, 0)),
        out_shape=jax.ShapeDtypeStruct((T, SSM_WIDTH), BF16),
        scratch_shapes=[pltpu.VMEM((C + 8, XBC_WIDTH), F32),
                        pltpu.VMEM((SSM_GROUPS, SSM_STATE, HW), F32)],
        compiler_params=_params(("arbitrary",)),
        name="ssd",
    )(proj, proj, proj, proj, proj, conv_w, conv_b.reshape(1, XBC_WIDTH), pad(dt_bias), a_neg,
      dskip_exp, ssm_norm_w.reshape(1, SSM_WIDTH), expand)


def _outproj_kernel(ret_ref, ssm_ref, w_ref, x_ref, lnw_ref, wr_ref, br_ref, hs_ref, up_ref, ti_ref, tw_ref):
    tm = x_ref.shape[0]
    h = (x_ref[...] + _dot(ret_ref[...], w_ref[0:RET_WIDTH, :])
         + _dot(ssm_ref[...], w_ref[RET_WIDTH:D_MODEL, :]))
    for s in range(ROW_SLABS):
        hs_ref[pl.ds(s, tm, stride=ROW_SLABS), :] = h[:, s * LANES:(s + 1) * LANES]
    u = h * lax.rsqrt(jnp.mean(h * h, axis=-1, keepdims=True) + EPS) * lnw_ref[...]
    packed = _pack_bf16_pairs(u[:, :D_MODEL // 2], u[:, D_MODEL // 2:])
    for s in range(PACK_SLABS):
        up_ref[pl.ds(s, tm, stride=PACK_SLABS), :] = packed[:, s * LANES:(s + 1) * LANES]
    E = N_EXPERTS
    uh, um, ul = _split3(u)
    ph = _dot_nt(wr_ref[...], uh)
    pm = _dot_nt(wr_ref[...], um)
    pw = _dot_nt(wr_ref[...], ul)
    lg = (ph[0:E] + (ph[E:2 * E] + pm[0:E]) + (ph[2 * E:3 * E] + pm[E:2 * E] + pw[0:E])) + br_ref[...]
    expert = lax.broadcasted_iota(jnp.int32, lg.shape, 0)
    slot = lax.broadcasted_iota(jnp.int32, (TOP_K, tm), 0)
    top_l = jnp.zeros((TOP_K, tm), F32)
    top_i = jnp.zeros((TOP_K, tm), jnp.int32)
    work = lg
    for k in range(TOP_K):
        best = jnp.max(work, axis=0, keepdims=True)
        which = jnp.min(jnp.where(work == best, expert, E), axis=0, keepdims=True)
        top_l = jnp.where(slot == k, best, top_l)
        top_i = jnp.where(slot == k, which, top_i)
        work = jnp.where(expert == which, -jnp.inf, work)
    p = jnp.exp(top_l - jnp.max(top_l, axis=0, keepdims=True))
    ti_ref[...] = top_i
    tw_ref[...] = p / jnp.sum(p, axis=0, keepdims=True)


def _outproj(ret, ssm, w_out_bf16, x2, ln_w, w_router, b_router):
    T = x2.shape[0]
    tm = 512
    return pl.pallas_call(
        _outproj_kernel,
        grid=(T // tm,),
        in_specs=[pl.BlockSpec((tm, RET_WIDTH), lambda i: (i, 0)),
                  pl.BlockSpec((tm, SSM_WIDTH), lambda i: (i, 0)),
                  pl.BlockSpec((D_MODEL, D_MODEL), lambda i: (0, 0)),
                  pl.BlockSpec((tm, D_MODEL), lambda i: (i, 0)),
                  pl.BlockSpec((1, D_MODEL), lambda i: (0, 0)),
                  pl.BlockSpec((3 * N_EXPERTS, D_MODEL), lambda i: (0, 0)),
                  pl.BlockSpec((N_EXPERTS, 1), lambda i: (0, 0))],
        out_specs=[pl.BlockSpec((tm * ROW_SLABS, LANES), lambda i: (i, 0)),
                   pl.BlockSpec((tm * PACK_SLABS, LANES), lambda i: (i, 0)),
                   pl.BlockSpec((TOP_K, tm), lambda i: (0, i)),
                   pl.BlockSpec((TOP_K, tm), lambda i: (0, i))],
        out_shape=[jax.ShapeDtypeStruct((T * ROW_SLABS, LANES), F32),
                   jax.ShapeDtypeStruct((T * PACK_SLABS, LANES), jnp.uint32),
                   jax.ShapeDtypeStruct((TOP_K, T), jnp.int32),
                   jax.ShapeDtypeStruct((TOP_K, T), F32)],
        compiler_params=_params(("parallel",)),
        name="outproj",
    )(ret, ssm, w_out_bf16, x2, ln_w.reshape(1, D_MODEL),
      jnp.concatenate([p.T for p in _split3(w_router)], axis=0), b_router.reshape(N_EXPERTS, 1))


def _expert_kernel(n_ff, n_items, aliased, *refs):
    (item_e, item_row, item_nsub, item_nzero, item_live, item_src, sorted_tok,
     u_hbm, wg_hbm, wd_hbm, bgu_ref, bdn_ref) = refs[:12]
    (y_hbm, stage, xbuf, acc, ybuf, wgbuf, wdbuf, wa_s, wb_s, wd_s,
     sem_in, sem_out, sem_w) = refs[13:] if aliased else refs[12:]
    i = pl.program_id(0)
    f = pl.program_id(1)

    step = i * n_ff + f
    live_steps = item_live[0] * n_ff
    for k in range(1, n_items):
        live_steps = live_steps + item_live[k] * n_ff
    W2_ = 2 * FF_TILE

    def tile_copies(g):
        e = item_e[g // n_ff]
        ft = g % n_ff
        slot = g % WEIGHT_SLOTS
        cols = lambda b: pl.ds(pl.multiple_of(b * W2_, W2_), W2_)
        rows = lambda b: pl.ds(pl.multiple_of(b * FF_TILE, FF_TILE), FF_TILE)
        return [pltpu.make_async_copy(wg_hbm.at[e, :, cols(ft)], wgbuf.at[slot, 0], sem_w.at[slot]),
                pltpu.make_async_copy(wg_hbm.at[e, :, cols(n_ff + ft)], wgbuf.at[slot, 1], sem_w.at[slot]),
                pltpu.make_async_copy(wd_hbm.at[e, rows(ft), :], wdbuf.at[slot, 0], sem_w.at[slot]),
                pltpu.make_async_copy(wd_hbm.at[e, rows(n_ff + ft), :], wdbuf.at[slot, 1], sem_w.at[slot])]

    @pl.when(step == 0)
    def _():
        for g in range(WEIGHT_SLOTS - 1):
            @pl.when(g < live_steps)
            def _(g=g):
                for c in tile_copies(g):
                    c.start()

    @pl.when(step < live_steps)
    def _():
        for c in tile_copies(step):
            c.wait()

        @pl.when(step + WEIGHT_SLOTS - 1 < live_steps)
        def _():
            for c in tile_copies(step + WEIGHT_SLOTS - 1):
                c.start()

    wslot = step % WEIGHT_SLOTS
    wga_ref = wgbuf.at[wslot, 0]
    wgb_ref = wgbuf.at[wslot, 1]
    wda_ref = wdbuf.at[wslot, 0]
    wdb_ref = wdbuf.at[wslot, 1]
    nsub = item_nsub[i]
    nzero = item_nzero[i]
    row0 = item_row[i]
    SUB = ROW_BLK
    YS = SUB * ROW_SLABS
    PS = PACK_SLABS
    XS = SUB * PS

    def y_copy(slot, j):
        dst = y_hbm.at[pl.ds(pl.multiple_of((row0 + j * SUB) * ROW_SLABS, YS), YS)]
        return pltpu.make_async_copy(ybuf.at[pl.ds(pl.multiple_of(slot * YS, YS), YS)], dst, sem_out.at[slot])

    def gather_copy(tok, r):
        src = u_hbm.at[pl.ds(pl.multiple_of(tok * PS, PS), PS)]
        return pltpu.make_async_copy(src, stage.at[pl.ds(pl.multiple_of(r * PS, PS), PS)], sem_in)

    n_assign = sorted_tok.shape[0]
    covers = [jnp.logical_and(nsub > lo, nsub <= m) for lo, m in zip((0,) + ITEM_VARIANTS[:-1], ITEM_VARIANTS)]
    nxt = jnp.minimum(i + 1, n_items - 1)

    def gathered_subs(item):
        ns = item_nsub[item]
        size = sum(jnp.where(jnp.logical_and(ns > lo, ns <= m), m, 0)
                   for lo, m in zip((0,) + ITEM_VARIANTS[:-1], ITEM_VARIANTS))
        return jnp.maximum(size, n_ff)

    def issue_rows(item, lo, count, unrolled):
        src0 = item_src[item]

        def one(r, p):
            gather_copy(sorted_tok[jnp.minimum(src0 + r, n_assign - 1)], r).start(priority=p)

        if unrolled:
            for k in range(count):
                one(lo + k, k % 2)
        else:
            def body(q, c):
                for p in range(2):
                    one(lo + 2 * q + p, p)
                return c
            lax.fori_loop(0, count // 2, body, 0)

    def wait_rows(item):
        def body(j, c):
            pltpu.make_async_copy(u_hbm.at[pl.ds(0, XS)], stage.at[pl.ds(0, XS)], sem_in).wait()
            return c
        lax.fori_loop(0, gathered_subs(item), body, 0)

    @pl.when(jnp.logical_and(i == 0, f == 0))
    def _():
        issue_rows(0, 0, gathered_subs(0) * SUB, False)
        acc[...] = jnp.zeros(acc.shape, F32)

    @pl.when(f == n_ff - 1)
    def _():
        issue_rows(nxt, n_ff * SUB, (gathered_subs(nxt) - n_ff) * SUB, False)

    lane = lax.broadcasted_iota(jnp.int32, (1, 2 * FF_TILE), 1)
    even = (lane % 2) == 0
    e_row = pl.ds(item_e[i], 1)
    bga = bgu_ref[e_row, pl.ds(pl.multiple_of(f * (2 * FF_TILE), 2 * FF_TILE), 2 * FF_TILE)]
    bgb = bgu_ref[e_row, pl.ds(pl.multiple_of((n_ff + f) * (2 * FF_TILE), 2 * FF_TILE), 2 * FF_TILE)]
    W2 = 2 * FF_TILE

    @pl.when(f == 0)
    def _():
        wait_rows(i)
        half = D_MODEL // 2

        def unpack(j, c):
            rows = pl.ds(pl.multiple_of(j * SUB, SUB), SUB)
            for s in range(PS):
                p = stage[pl.ds(j * XS + s, SUB, stride=PS), :]
                lo = lax.bitcast_convert_type(p << 16, F32)
                hi = lax.bitcast_convert_type(p & jnp.uint32(0xFFFF0000), F32)
                xbuf[rows, s * LANES:(s + 1) * LANES] = lo.astype(BF16)
                xbuf[rows, half + s * LANES:half + (s + 1) * LANES] = hi.astype(BF16)
            return c
        lax.fori_loop(0, sum(jnp.where(c, m, 0) for m, c in zip(ITEM_VARIANTS, covers)), unpack, 0)

    @pl.when(nsub == 0)
    def _():
        issue_rows(nxt, f * SUB, SUB, False)

    def block(rows):
        xs = xbuf[0:rows, :]
        wa_s[...] = wga_ref[...].astype(BF16)
        ga = _dot(xs, wa_s[...]) + bga
        issue_rows(nxt, f * SUB, SUB, True)
        wb_s[...] = wgb_ref[...].astype(BF16)
        gb = _dot(xs, wb_s[...]) + bgb
        wd_s[...] = pltpu.bitcast(_pack_bf16_pairs(wda_ref[...], wdb_ref[...]), BF16)
        gate = jnp.where(even, ga, pltpu.roll(gb, 1, 1))
        up = jnp.where(even, pltpu.roll(ga, W2 - 1, 1), gb)
        gate = jnp.minimum(gate, SWIGLU_LIMIT)
        up = jnp.clip(up, -SWIGLU_LIMIT, SWIGLU_LIMIT)
        act = ((up + 1.0) * (gate * jax.nn.sigmoid(gate * SWIGLU_ALPHA))).astype(BF16)
        for n in range(0, D_MODEL, DOWN_TILE):
            part = _dot(act, wd_s[:, n:n + DOWN_TILE])
            acc[0:rows, n:n + DOWN_TILE] = jnp.where(f > 0, acc[0:rows, n:n + DOWN_TILE] + part, part)

    for m, cover in zip(ITEM_VARIANTS, covers):
        @pl.when(cover)
        def _(m=m):
            block(m * SUB)

    @pl.when(f == n_ff - 1)
    def _():
        @pl.when(i == n_items - 1)
        def _():
            wait_rows(nxt)

        def stage_out(q, c):
            slot = q % Y_SLOTS

            @pl.when(q >= Y_SLOTS)
            def _():
                y_copy(slot, 0).wait()
            val = acc[pl.ds(pl.multiple_of(q * SUB, SUB), SUB), :] + bdn_ref[e_row, :]
            for s in range(ROW_SLABS):
                ybuf[pl.ds(slot * YS + s, SUB, stride=ROW_SLABS), :] = val[:, s * LANES:(s + 1) * LANES]
            y_copy(slot, q).start()
            return c
        lax.fori_loop(0, nsub, stage_out, 0)

        for k in range(Y_SLOTS):
            @pl.when(nsub > k)
            def _(k=k):
                y_copy(k, 0).wait()

    @pl.when(jnp.logical_and(f == n_ff - 1, nzero > 0))
    def _():
        ybuf[0:YS, :] = jnp.zeros((YS, LANES), F32)

        def start(j, c):
            y_copy(0, j).start()
            return c
        lax.fori_loop(0, nzero, start, 0)

        def wait(j, c):
            y_copy(0, 0).wait()
            return c
        lax.fori_loop(0, nzero, wait, 0)


def _experts_call(u_packed, sorted_tok, n_rows, items, w_gate_up, b_gate_up, w_down, b_down, y_in=None):
    n_ff = (D_FF // 2) // FF_TILE
    assert ITEM_VARIANTS[0] >= n_ff and ITEM_VARIANTS[-1] * ROW_BLK == ITEM_ROWS
    W2 = 2 * FF_TILE
    item_e, item_row, item_nsub, item_nzero, item_live, item_src = items
    n_items = item_e.shape[0]
    aliased = y_in is not None

    whole = lambda shape: pl.BlockSpec(shape, lambda i, f, e, r, n, z, lv, sr, st: (0, 0))
    grid_spec = pltpu.PrefetchScalarGridSpec(
        num_scalar_prefetch=7,
        grid=(n_items, n_ff),
        in_specs=[
            pl.BlockSpec(memory_space=pl.ANY),
            pl.BlockSpec(memory_space=pl.ANY),
            pl.BlockSpec(memory_space=pl.ANY),
            whole((N_EXPERTS, 2 * D_FF)),
            whole((N_EXPERTS, D_MODEL)),
        ] + ([pl.BlockSpec(memory_space=pl.ANY)] if aliased else []),
        out_specs=pl.BlockSpec(memory_space=pl.ANY),
        scratch_shapes=[pltpu.VMEM((ITEM_ROWS * PACK_SLABS, LANES), jnp.uint32),
                        pltpu.VMEM((ITEM_ROWS, D_MODEL), BF16),
                        pltpu.VMEM((ITEM_ROWS, D_MODEL), F32),
                        pltpu.VMEM((Y_SLOTS * ROW_BLK * ROW_SLABS, LANES), F32),
                        pltpu.VMEM((WEIGHT_SLOTS, 2, D_MODEL, W2), F32),
                        pltpu.VMEM((WEIGHT_SLOTS, 2, FF_TILE, D_MODEL), F32),
                        pltpu.VMEM((D_MODEL, W2), BF16),
                        pltpu.VMEM((D_MODEL, W2), BF16),
                        pltpu.VMEM((W2, D_MODEL), BF16),
                        pltpu.SemaphoreType.DMA(()),
                        pltpu.SemaphoreType.DMA((Y_SLOTS,)),
                        pltpu.SemaphoreType.DMA((WEIGHT_SLOTS,))],
    )
    args = (item_e, item_row, item_nsub, item_nzero, item_live, item_src, sorted_tok, u_packed,
            w_gate_up, w_down, b_gate_up, b_down)
    return pl.pallas_call(
        functools.partial(_expert_kernel, n_ff, n_items, aliased),
        grid_spec=grid_spec,
        out_shape=jax.ShapeDtypeStruct((n_rows * ROW_SLABS, LANES), F32),
        input_output_aliases={len(args): 0} if aliased else {},
        compiler_params=_params(("arbitrary", "arbitrary")),
        name="experts_overflow" if aliased else "experts",
    )(*args, *((y_in,) if aliased else ()))


def _experts(u_packed, sorted_tok, n_rows, items, n_used, w_gate_up, b_gate_up, w_down, b_down):
    n_main = N_EXPERTS + 1
    weights = (w_gate_up, b_gate_up, w_down, b_down)
    y = _experts_call(u_packed, sorted_tok, n_rows, tuple(a[:n_main] for a in items), *weights)
    rest = tuple(a[n_main:] for a in items)
    return lax.cond(n_used > n_main,
                    lambda y_: _experts_call(u_packed, sorted_tok, n_rows, rest, *weights, y_in=y_),
                    lambda y_: y_, y)


def _combine_kernel(tm, n_steps, dest_ref, w_ref, y_hbm, h_ref, lnw_ref, o_ref, buf, osum, sem):
    i = pl.program_id(0)

    def copy(step, slot, t, k):
        d = dest_ref[(step * tm + t) * TOP_K + k]
        return pltpu.make_async_copy(y_hbm.at[d], buf.at[slot, k * tm + t], sem.at[slot])

    def issue(step, slot):
        def body(t, c):
            for k in range(TOP_K):
                copy(step, slot, t, k).start(priority=k % 2)
            return c
        lax.fori_loop(0, tm, body, 0, unroll=4)

    @pl.when(i == 0)
    def _():
        issue(0, 0)

    @pl.when(i + 1 < n_steps)
    def _():
        issue(i + 1, (i + 1) % 2)

    slot = i % 2

    pltpu.make_async_copy(y_hbm.at[pl.ds(0, TOP_K * tm)], buf.at[slot], sem.at[slot]).wait()

    def token(t, c):
        a = h_ref[t]
        for k in range(TOP_K):
            a = a + w_ref[(i * tm + t) * TOP_K + k] * buf[slot, k * tm + t]
        osum[pl.ds(pl.multiple_of(t * ROW_SLABS, ROW_SLABS), ROW_SLABS), :] = a
        return c
    lax.fori_loop(0, tm, token, 0, unroll=4)

    h = osum[...].reshape(tm, ROW_SLABS, LANES)
    ms = jnp.mean(jnp.mean(h * h, axis=2, keepdims=True), axis=1, keepdims=True)
    osum[...] = (h * lax.rsqrt(ms + EPS) * lnw_ref[...]).reshape(tm * ROW_SLABS, LANES)
    for s in range(ROW_SLABS):
        o_ref[:, s * LANES:(s + 1) * LANES] = osum[pl.ds(s, tm, stride=ROW_SLABS), :]


def _combine(dest, wflat, y3, h3, ln_w):
    T = h3.shape[0]
    tm = 128
    n_steps = T // tm
    grid_spec = pltpu.PrefetchScalarGridSpec(
        num_scalar_prefetch=2,
        grid=(n_steps,),
        in_specs=[pl.BlockSpec(memory_space=pl.ANY),
                  pl.BlockSpec((tm, ROW_SLABS, LANES), lambda i, d, w: (i, 0, 0)),
                  pl.BlockSpec((1, ROW_SLABS, LANES), lambda i, d, w: (0, 0, 0))],
        out_specs=pl.BlockSpec((tm, D_MODEL), lambda i, d, w: (i, 0)),
        scratch_shapes=[pltpu.VMEM((2, TOP_K * tm, ROW_SLABS, LANES), F32),
                        pltpu.VMEM((tm * ROW_SLABS, LANES), F32),
                        pltpu.SemaphoreType.DMA((2,))],
    )
    return pl.pallas_call(
        functools.partial(_combine_kernel, tm, n_steps),
        grid_spec=grid_spec,
        out_shape=jax.ShapeDtypeStruct((T, D_MODEL), F32),
        compiler_params=_params(("arbitrary",)),
        name="combine",
    )(dest, wflat, y3, h3, ln_w.reshape(1, ROW_SLABS, LANES))


def _route(top_idx, top_w, n_rows, n_items):
    T = top_idx.shape[0]
    e_flat = top_idx.reshape(-1).astype(jnp.int32)
    onehot = (e_flat[:, None] == jnp.arange(N_EXPERTS, dtype=jnp.int32)[None, :]).astype(jnp.int32)
    csum = jnp.cumsum(onehot, axis=0)
    counts = csum[-1]
    padded = (counts + ROW_BLK - 1) // ROW_BLK * ROW_BLK
    pend = jnp.cumsum(padded)
    pstart = pend - padded
    dest = jnp.sum(onehot * (pstart[None, :] + csum - onehot), axis=1).astype(jnp.int32)

    per_e = (padded + ITEM_ROWS - 1) // ITEM_ROWS
    iend = jnp.cumsum(per_e)
    istart = iend - per_e
    ii = jnp.arange(n_items, dtype=jnp.int32)
    total = iend[-1]
    live = (ii < total).astype(jnp.int32)
    ic = jnp.minimum(ii, total - 1)
    ie = jnp.minimum(jnp.sum(ic[:, None] >= iend[None, :], axis=1), N_EXPERTS - 1).astype(jnp.int32)
    within = ic - istart[ie]
    irow = (pstart[ie] + within * ITEM_ROWS).astype(jnp.int32)
    insub = jnp.minimum((padded[ie] - within * ITEM_ROWS) // ROW_BLK, ITEM_ROWS // ROW_BLK).astype(jnp.int32)
    insub = insub * live
    tail_rows = n_rows - pend[-1]
    tail = jnp.logical_and(ii == total, tail_rows > 0)
    inzero = jnp.where(tail, tail_rows // ROW_BLK, 0).astype(jnp.int32)
    irow = jnp.where(tail, pend[-1], irow).astype(jnp.int32)
    n_used = total + (tail_rows > 0).astype(jnp.int32)
    n_assign = T * TOP_K
    assert N_EXPERTS * n_assign < 2 ** 31
    order = jnp.sort(e_flat * n_assign + jnp.arange(n_assign, dtype=jnp.int32)) % n_assign
    sorted_tok = (order // TOP_K).astype(jnp.int32)
    cstart = jnp.cumsum(counts) - counts
    isrc = ((cstart[ie] + within * ITEM_ROWS) * live).astype(jnp.int32)
    return dest, top_w.reshape(-1).astype(F32), (ie, irow, insub, inzero, live, isrc), n_used, sorted_tok


def kernel(x, positions, ln_mix_w, w_in, conv_w, conv_b, dt_bias, a_log, d_skip, ssm_norm_w, w_out,
           ln_ffn_w, w_router, b_router, w_gate_up, b_gate_up, w_down, b_down, ln_final_w):
    B, L, _ = x.shape
    T = B * L
    assert B == 1 and T % 1024 == 0
    x2 = x.reshape(T, D_MODEL)
    half = RET_HEAD_DIM // 2
    inv_freq = (ROPE_BASE ** (-jnp.arange(half, dtype=F32) / half)).reshape(1, half)
    pos_col = positions.reshape(T, 1).astype(F32)

    proj = _inproj(x2, ln_mix_w[0], jnp.swapaxes(w_in[0], 0, 1))
    ret = _retention(proj, pos_col, inv_freq)
    ssm = _ssd(proj, conv_w[0], conv_b[0], dt_bias[0], a_log[0], d_skip[0], ssm_norm_w[0])
    h_slabs, u_packed, top_idx, top_w = _outproj(ret, ssm, w_out[0].astype(BF16), x2, ln_ffn_w[0], w_router[0],
                                                 b_router[0])

    n_rows = -(-(T * TOP_K + N_EXPERTS * (ROW_BLK - 1)) // ROW_BLK) * ROW_BLK
    n_items = N_EXPERTS + 1 + n_rows // ITEM_ROWS
    dest, wflat, items, n_used, sorted_tok = _route(top_idx.T, top_w.T, n_rows, n_items)

    y_rows = _experts(u_packed, sorted_tok, n_rows, items, n_used, w_gate_up[0], b_gate_up[0], w_down[0],
                      b_down[0])
    out = _combine(dest, wflat, y_rows.reshape(n_rows, ROW_SLABS, LANES),
                   h_slabs.reshape(T, ROW_SLABS, LANES), ln_final_w)
    return out.reshape(B, L, D_MODEL)
```

```python
import functools

import numpy as np
import jax
import jax.numpy as jnp
from jax import lax
from jax.experimental import pallas as pl
from jax.experimental.pallas import tpu as pltpu

F32 = jnp.float32
BF16 = jnp.bfloat16

D_MODEL = 2048
RET_HEADS = 4
RET_HEAD_DIM = 256
RET_WIDTH = RET_HEADS * RET_HEAD_DIM
SSM_WIDTH = D_MODEL - RET_WIDTH
SSM_HEAD_DIM = 64
SSM_HEADS = SSM_WIDTH // SSM_HEAD_DIM
SSM_GROUPS = 2
SSM_STATE = 128
CONV_WIDTH = 4
XBC_WIDTH = SSM_WIDTH + 2 * SSM_GROUPS * SSM_STATE
D_IN_PROJ = 4 * RET_WIDTH + SSM_WIDTH + XBC_WIDTH + SSM_HEADS
ROPE_BASE = 10000.0
N_EXPERTS = 32
TOP_K = 4
D_FF = D_MODEL
SWIGLU_LIMIT = 7.0
SWIGLU_ALPHA = 1.702
EPS = 1e-6

LANES = 128
VMEM_LIMIT = 56 * 1024 * 1024

RET_CHUNK = 256
SSD_CHUNK = 128
ROW_BLK = 128
ITEM_ROWS = 1536
ITEM_VARIANTS = (8, 9, 10, 12)
FF_TILE = 128
DOWN_TILE = 512
WEIGHT_SLOTS = 3
Y_SLOTS = 8
ROW_SLABS = D_MODEL // LANES
PACK_SLABS = ROW_SLABS // 2


def _params(sem, **kw):
    return pltpu.CompilerParams(dimension_semantics=sem, vmem_limit_bytes=VMEM_LIMIT, **kw)


def _dot(a, b):
    return jnp.dot(a, b, preferred_element_type=F32)


def _dot_nt(a, b):
    return lax.dot_general(a, b, (((1,), (1,)), ((), ())), preferred_element_type=F32)


def _dot_tn(a, b):
    return lax.dot_general(a, b, (((0,), (0,)), ((), ())), preferred_element_type=F32)


def _split3(x):
    hi = x.astype(BF16)
    r = x - hi.astype(F32)
    mid = r.astype(BF16)
    lo = (r - mid.astype(F32)).astype(BF16)
    return hi, mid, lo


def _dot_exact_rhs01(x, m01):
    hi, mid, lo = _split3(x)
    return _dot(hi, m01) + _dot(mid, m01) + _dot(lo, m01)


def _dot_exact_lhs01(m01, x):
    hi, mid, lo = _split3(x)
    return _dot(m01, hi) + _dot(m01, mid) + _dot(m01, lo)


def _silu(x):
    return x * jax.nn.sigmoid(x)


def _pack_bf16_pairs(lo, hi):
    lo_bits = lax.bitcast_convert_type(lo.astype(BF16).astype(F32), jnp.uint32)
    hi_bits = lax.bitcast_convert_type(hi.astype(BF16).astype(F32), jnp.uint32)
    return hi_bits | (lo_bits >> 16)


INPROJ_CHUNKS = 4


def _inproj_kernel(x_hbm, lnw_ref, w_ref, o_ref, xs_ref, u_ref, sem):
    tm = u_ref.shape[0]
    rows = tm // INPROJ_CHUNKS
    i = pl.program_id(0)
    j = pl.program_id(1)

    def copy(tile, c):
        src = x_hbm.at[pl.ds(pl.multiple_of(tile * tm + c * rows, rows), rows)]
        return pltpu.make_async_copy(src, xs_ref.at[c], sem.at[c])

    @pl.when(jnp.logical_and(i == 0, j == 0))
    def _():
        for c in range(INPROJ_CHUNKS):
            copy(0, c).start()

    @pl.when(j == 0)
    def _():
        for c in range(INPROJ_CHUNKS):
            copy(i, c).wait()
            x = xs_ref[c]
            ms = jnp.mean(x * x, axis=-1, keepdims=True)
            u_ref[c * rows:(c + 1) * rows, :] = (x * lax.rsqrt(ms + EPS) * lnw_ref[...]).astype(BF16)

    @pl.when(jnp.logical_and(j == pl.num_programs(1) - 1, i + 1 < pl.num_programs(0)))
    def _():
        for c in range(INPROJ_CHUNKS):
            copy(i + 1, c).start()

    tn = o_ref.shape[1]
    tail = D_IN_PROJ % tn
    if 0 < tail <= LANES:
        last = pl.num_programs(1) - 1

        @pl.when(pl.program_id(1) < last)
        def _():
            o_ref[...] = _dot_nt(u_ref[...], w_ref[...].astype(BF16))

        @pl.when(pl.program_id(1) == last)
        def _():
            o_ref[:, 0:LANES] = _dot_nt(u_ref[...], w_ref[0:LANES, :].astype(BF16))
    else:
        o_ref[...] = _dot_nt(u_ref[...], w_ref[...].astype(BF16))


def _inproj(x2, ln_w, w_in_t):
    T = x2.shape[0]
    tm, tn = (2048 if T % 2048 == 0 else 1024), 512
    return pl.pallas_call(
        _inproj_kernel,
        grid=(T // tm, pl.cdiv(D_IN_PROJ, tn)),
        in_specs=[pl.BlockSpec(memory_space=pl.ANY),
                  pl.BlockSpec((1, D_MODEL), lambda i, j: (0, 0)),
                  pl.BlockSpec((tn, D_MODEL), lambda i, j: (j, 0))],
        out_specs=pl.BlockSpec((tm, tn), lambda i, j: (i, j)),
        out_shape=jax.ShapeDtypeStruct((T, D_IN_PROJ), F32),
        scratch_shapes=[pltpu.VMEM((INPROJ_CHUNKS, tm // INPROJ_CHUNKS, D_MODEL), F32),
                        pltpu.VMEM((tm, D_MODEL), BF16),
                        pltpu.SemaphoreType.DMA((INPROJ_CHUNKS,))],
        compiler_params=_params(("arbitrary", "arbitrary")),
        name="inproj",
    )(x2, ln_w.reshape(1, D_MODEL), w_in_t)


def _retention_tables():
    C = RET_CHUNK
    h = np.arange(RET_HEADS, dtype=np.float64)
    log_gamma = np.log1p(-np.exp2(-5.0 - h))
    idx = np.arange(C, dtype=np.float64)
    rel = idx[:, None] - idx[None, :]
    intra = np.where(rel >= 0, np.exp(log_gamma[:, None, None] * np.maximum(rel, 0.0)), 0.0)
    q_decay = np.exp(log_gamma[:, None] * (idx + 1.0))
    k_decay = np.exp(log_gamma[:, None] * (C - 1.0 - idx))
    chunk_decay = np.exp(log_gamma * C)
    qd = np.broadcast_to(q_decay[:, :, None], (RET_HEADS, C, RET_HEAD_DIM))
    kd = np.broadcast_to(k_decay[:, :, None], (RET_HEADS, C, RET_HEAD_DIM))
    return (jnp.asarray(intra, F32), jnp.asarray(qd, F32), jnp.asarray(kd, F32),
            [float(c) for c in chunk_decay])


def _retention_kernel(chunk_decay, pos_ref, invf_ref, q_ref, k_ref, v_ref, g_ref,
                      intra_ref, qd_ref, kd_ref, o_ref, state_ref):
    @pl.when(pl.program_id(0) == 0)
    def _():
        state_ref[...] = jnp.zeros_like(state_ref)

    half = RET_HEAD_DIM // 2
    ang = pos_ref[...] * invf_ref[...]
    cos = jnp.cos(ang)
    sin = jnp.sin(ang)

    def rope(t):
        t1, t2 = t[:, :half], t[:, half:]
        return jnp.concatenate([t1 * cos - t2 * sin, t2 * cos + t1 * sin], axis=-1)

    for h in range(RET_HEADS):
        sl = slice(h * RET_HEAD_DIM, (h + 1) * RET_HEAD_DIM)
        q = rope(q_ref[:, sl])
        k = rope(k_ref[:, sl]) * (RET_HEAD_DIM ** -0.5)
        v = v_ref[:, sl].astype(BF16)
        state = state_ref[h]
        scores = _dot_nt(q.astype(BF16), k.astype(BF16)) * intra_ref[h]
        inner = _dot(scores.astype(BF16), v)
        cross = _dot((q * qd_ref[h]).astype(BF16), state.astype(BF16))
        state_ref[h] = chunk_decay[h] * state + _dot_tn((k * kd_ref[h]).astype(BF16), v)
        o = inner + cross
        o = o * lax.rsqrt(jnp.mean(o * o, axis=-1, keepdims=True) + EPS)
        o_ref[:, sl] = (o * _silu(g_ref[:, sl])).astype(o_ref.dtype)


def _retention(proj, pos_col, inv_freq):
    T = proj.shape[0]
    C = RET_CHUNK
    intra, qd, kd, chunk_decay = _retention_tables()
    col = lambda j: pl.BlockSpec((C, RET_WIDTH), lambda c, j=j: (c, j))
    const3 = lambda shape: pl.BlockSpec(shape, lambda c: (0, 0, 0))
    return pl.pallas_call(
        functools.partial(_retention_kernel, chunk_decay),
        grid=(T // C,),
        in_specs=[pl.BlockSpec((C, 1), lambda c: (c, 0)),
                  pl.BlockSpec((1, RET_HEAD_DIM // 2), lambda c: (0, 0)),
                  col(0), col(1), col(2), col(3),
                  const3((RET_HEADS, C, C)),
                  const3((RET_HEADS, C, RET_HEAD_DIM)),
                  const3((RET_HEADS, C, RET_HEAD_DIM))],
        out_specs=pl.BlockSpec((C, RET_WIDTH), lambda c: (c, 0)),
        out_shape=jax.ShapeDtypeStruct((T, RET_WIDTH), BF16),
        scratch_shapes=[pltpu.VMEM((RET_HEADS, RET_HEAD_DIM, RET_HEAD_DIM), F32)],
        compiler_params=_params(("arbitrary",)),
        name="retention",
    )(pos_col, inv_freq, proj, proj, proj, proj, intra, qd, kd)


def _ssd_kernel(xs0_ref, xs1_ref, bc_ref, z_ref, dt_ref, convw_ref, convb_ref, dtb_ref, a_ref,
                dskip_ref, normw_ref, expand_ref, o_ref, ext_ref, state_ref):
    C = SSD_CHUNK
    HW = SSM_WIDTH // SSM_GROUPS
    CARRY = 8

    @pl.when(pl.program_id(0) == 0)
    def _():
        ext_ref[0:CARRY, :] = jnp.zeros((CARRY, XBC_WIDTH), F32)
        state_ref[...] = jnp.zeros_like(state_ref)

    ext_ref[CARRY:CARRY + C, 0:HW] = xs0_ref[...]
    ext_ref[CARRY:CARRY + C, HW:2 * HW] = xs1_ref[...]
    ext_ref[CARRY:CARRY + C, 2 * HW:3 * HW] = bc_ref[...]
    conv = convb_ref[...]
    for k in range(CONV_WIDTH):
        off = CARRY - (CONV_WIDTH - 1) + k
        conv = conv + convw_ref[k:k + 1, :] * ext_ref[off:off + C, :]
    ext_ref[0:CARRY, :] = ext_ref[C:C + CARRY, :]
    xbc = _silu(conv)
    xs = xbc[:, :SSM_WIDTH]

    lane = lax.broadcasted_iota(jnp.int32, (1, LANES), 1)
    dt_raw = jnp.where(lane < SSM_HEADS, dt_ref[...], 0.0) + dtb_ref[...]
    dt = jnp.maximum(dt_raw, 0.0) + jnp.log1p(jnp.exp(-jnp.abs(dt_raw)))
    dta = dt * a_ref[...]

    row = lax.broadcasted_iota(jnp.int32, (C, C), 0)
    colm = lax.broadcasted_iota(jnp.int32, (C, C), 1)
    tril = row >= colm
    a_cum = _dot_exact_lhs01(jnp.where(tril, 1.0, 0.0).astype(BF16), dta)
    a_cum_t = a_cum.T

    expand = expand_ref[...]
    a_exp = _dot_exact_rhs01(a_cum, expand)
    dt_exp = _dot_exact_rhs01(dt, expand)
    a_last = a_exp[C - 1:C, :]
    decay_in = jnp.exp(a_exp)
    decay_out = jnp.exp(a_last - a_exp)
    chunk_decay = jnp.exp(a_last)
    xdt = xs * dt_exp

    lane2 = lax.broadcasted_iota(jnp.int32, (1, LANES), 1)
    lo_head = lane2 < SSM_HEAD_DIM
    ys = []
    for g in range(SSM_GROUPS):
        gs = slice(g * HW, (g + 1) * HW)
        b_g = xbc[:, SSM_WIDTH + g * SSM_STATE:SSM_WIDTH + (g + 1) * SSM_STATE].astype(BF16)
        c0 = SSM_WIDTH + SSM_GROUPS * SSM_STATE
        c_g = xbc[:, c0 + g * SSM_STATE:c0 + (g + 1) * SSM_STATE].astype(BF16)
        cb = _dot_nt(c_g, b_g)
        state = state_ref[g]
        y_off = _dot(c_g, state.astype(BF16)) * decay_in[:, gs]
        xw = (xdt[:, gs] * decay_out[:, gs]).astype(BF16)
        state_ref[g] = chunk_decay[:, gs] * state + _dot_tn(b_g, xw)
        slabs = []
        for s in range(HW // LANES):
            xd = xdt[:, g * HW + s * LANES:g * HW + (s + 1) * LANES]
            acc = None
            for e in range(2):
                hh = g * (SSM_HEADS // SSM_GROUPS) + 2 * s + e
                seg = a_cum[:, hh:hh + 1] - a_cum_t[hh:hh + 1, :]
                m = cb * jnp.exp(jnp.where(tril, seg, -jnp.inf))
                xm = jnp.where(lo_head if e == 0 else jnp.logical_not(lo_head), xd, 0.0)
                part = _dot(m.astype(BF16), xm.astype(BF16))
                acc = part if acc is None else acc + part
            slabs.append(acc)
        ys.append(jnp.concatenate(slabs, axis=-1) + y_off)
    y = jnp.concatenate(ys, axis=-1) + dskip_ref[...] * xs
    y = y * _silu(z_ref[...])
    outs = []
    for g in range(SSM_GROUPS):
        yg = y[:, g * HW:(g + 1) * HW]
        outs.append(yg * lax.rsqrt(jnp.mean(yg * yg, axis=-1, keepdims=True) + EPS))
    o_ref[...] = (jnp.concatenate(outs, axis=-1) * normw_ref[...]).astype(o_ref.dtype)


def _ssd(proj, conv_w, conv_b, dt_bias, a_log, d_skip, ssm_norm_w):
    T = proj.shape[0]
    C = SSD_CHUNK
    HW = SSM_WIDTH // SSM_GROUPS
    xbc0 = (4 * RET_WIDTH + SSM_WIDTH) // HW
    dt0 = (D_IN_PROJ - SSM_HEADS) // LANES
    pad = lambda v: jnp.zeros((1, LANES), F32).at[0, :SSM_HEADS].set(v.astype(F32))
    a_neg = pad(-jnp.exp(a_log.astype(F32)))
    expand_np = np.zeros((LANES, SSM_WIDTH), np.float32)
    for hh in range(SSM_HEADS):
        expand_np[hh, hh * SSM_HEAD_DIM:(hh + 1) * SSM_HEAD_DIM] = 1.0
    expand = jnp.asarray(expand_np, BF16)
    dskip_exp = jnp.repeat(d_skip.astype(F32), SSM_HEAD_DIM).reshape(1, SSM_WIDTH)
    const = lambda shape: pl.BlockSpec(shape, lambda c: (0, 0))
    return pl.pallas_call(
        _ssd_kernel,
        grid=(T // C,),
        in_specs=[pl.BlockSpec((C, HW), lambda c: (c, xbc0)),
                  pl.BlockSpec((C, HW), lambda c: (c, xbc0 + 1)),
                  pl.BlockSpec((C, HW), lambda c: (c, xbc0 + 2)),
                  pl.BlockSpec((C, SSM_WIDTH), lambda c: (c, 4 * RET_WIDTH // SSM_WIDTH)),
                  pl.BlockSpec((C, LANES), lambda c: (c, dt0)),
                  const((CONV_WIDTH, XBC_WIDTH)), const((1, XBC_WIDTH)),
                  const((1, LANES)), const((1, LANES)),
                  const((1, SSM_WIDTH)), const((1, SSM_WIDTH)),
                  const((LANES, SSM_WIDTH))],
        out_specs=pl.BlockSpec((C, SSM_WIDTH), lambda c: (c, 0)),
        out_shape=jax.ShapeDtypeStruct((T, SSM_WIDTH), BF16),
        scratch_shapes=[pltpu.VMEM((C + 8, XBC_WIDTH), F32),
                        pltpu.VMEM((SSM_GROUPS, SSM_STATE, HW), F32)],
        compiler_params=_params(("arbitrary",)),
        name="ssd",
    )(proj, proj, proj, proj, proj, conv_w, conv_b.reshape(1, XBC_WIDTH), pad(dt_bias), a_neg,
      dskip_exp, ssm_norm_w.reshape(1, SSM_WIDTH), expand)


def _outproj_kernel(ret_ref, ssm_ref, w_ref, x_ref, lnw_ref, wr_ref, br_ref, hs_ref, up_ref, ti_ref, tw_ref):
    tm = x_ref.shape[0]
    h = (x_ref[...] + _dot(ret_ref[...], w_ref[0:RET_WIDTH, :])
         + _dot(ssm_ref[...], w_ref[RET_WIDTH:D_MODEL, :]))
    for s in range(ROW_SLABS):
        hs_ref[pl.ds(s, tm, stride=ROW_SLABS), :] = h[:, s * LANES:(s + 1) * LANES]
    u = h * lax.rsqrt(jnp.mean(h * h, axis=-1, keepdims=True) + EPS) * lnw_ref[...]
    packed = _pack_bf16_pairs(u[:, :D_MODEL // 2], u[:, D_MODEL // 2:])
    for s in range(PACK_SLABS):
        up_ref[pl.ds(s, tm, stride=PACK_SLABS), :] = packed[:, s * LANES:(s + 1) * LANES]
    E = N_EXPERTS
    uh, um, ul = _split3(u)
    ph = _dot_nt(wr_ref[...], uh)
    pm = _dot_nt(wr_ref[...], um)
    pw = _dot_nt(wr_ref[...], ul)
    lg = (ph[0:E] + (ph[E:2 * E] + pm[0:E]) + (ph[2 * E:3 * E] + pm[E:2 * E] + pw[0:E])) + br_ref[...]
    expert = lax.broadcasted_iota(jnp.int32, lg.shape, 0)
    slot = lax.broadcasted_iota(jnp.int32, (TOP_K, tm), 0)
    top_l = jnp.zeros((TOP_K, tm), F32)
    top_i = jnp.zeros((TOP_K, tm), jnp.int32)
    work = lg
    for k in range(TOP_K):
        best = jnp.max(work, axis=0, keepdims=True)
        which = jnp.min(jnp.where(work == best, expert, E), axis=0, keepdims=True)
        top_l = jnp.where(slot == k, best, top_l)
        top_i = jnp.where(slot == k, which, top_i)
        work = jnp.where(expert == which, -jnp.inf, work)
    p = jnp.exp(top_l - jnp.max(top_l, axis=0, keepdims=True))
    ti_ref[...] = top_i
    tw_ref[...] = p / jnp.sum(p, axis=0, keepdims=True)


def _outproj(ret, ssm, w_out_bf16, x2, ln_w, w_router, b_router):
    T = x2.shape[0]
    tm = 512
    return pl.pallas_call(
        _outproj_kernel,
        grid=(T // tm,),
        in_specs=[pl.BlockSpec((tm, RET_WIDTH), lambda i: (i, 0)),
                  pl.BlockSpec((tm, SSM_WIDTH), lambda i: (i, 0)),
                  pl.BlockSpec((D_MODEL, D_MODEL), lambda i: (0, 0)),
                  pl.BlockSpec((tm, D_MODEL), lambda i: (i, 0)),
                  pl.BlockSpec((1, D_MODEL), lambda i: (0, 0)),
                  pl.BlockSpec((3 * N_EXPERTS, D_MODEL), lambda i: (0, 0)),
                  pl.BlockSpec((N_EXPERTS, 1), lambda i: (0, 0))],
        out_specs=[pl.BlockSpec((tm * ROW_SLABS, LANES), lambda i: (i, 0)),
                   pl.BlockSpec((tm * PACK_SLABS, LANES), lambda i: (i, 0)),
                   pl.BlockSpec((TOP_K, tm), lambda i: (0, i)),
                   pl.BlockSpec((TOP_K, tm), lambda i: (0, i))],
        out_shape=[jax.ShapeDtypeStruct((T * ROW_SLABS, LANES), F32),
                   jax.ShapeDtypeStruct((T * PACK_SLABS, LANES), jnp.uint32),
                   jax.ShapeDtypeStruct((TOP_K, T), jnp.int32),
                   jax.ShapeDtypeStruct((TOP_K, T), F32)],
        compiler_params=_params(("parallel",)),
        name="outproj",
    )(ret, ssm, w_out_bf16, x2, ln_w.reshape(1, D_MODEL),
      jnp.concatenate([p.T for p in _split3(w_router)], axis=0), b_router.reshape(N_EXPERTS, 1))


def _expert_kernel(n_ff, n_items, aliased, *refs):
    (item_e, item_row, item_nsub, item_nzero, item_live, item_src, sorted_tok,
     u_hbm, wg_hbm, wd_hbm, bgu_ref, bdn_ref) = refs[:12]
    (y_hbm, stage, xbuf, acc, ybuf, wgbuf, wdbuf, wa_s, wb_s, wd_s,
     sem_in, sem_out, sem_w) = refs[13:] if aliased else refs[12:]
    i = pl.program_id(0)
    f = pl.program_id(1)

    step = i * n_ff + f
    live_steps = item_live[0] * n_ff
    for k in range(1, n_items):
        live_steps = live_steps + item_live[k] * n_ff
    W2_ = 2 * FF_TILE

    def tile_copies(g):
        e = item_e[g // n_ff]
        ft = g % n_ff
        slot = g % WEIGHT_SLOTS
        cols = lambda b: pl.ds(pl.multiple_of(b * W2_, W2_), W2_)
        rows = lambda b: pl.ds(pl.multiple_of(b * FF_TILE, FF_TILE), FF_TILE)
        return [pltpu.make_async_copy(wg_hbm.at[e, :, cols(ft)], wgbuf.at[slot, 0], sem_w.at[slot]),
                pltpu.make_async_copy(wg_hbm.at[e, :, cols(n_ff + ft)], wgbuf.at[slot, 1], sem_w.at[slot]),
                pltpu.make_async_copy(wd_hbm.at[e, rows(ft), :], wdbuf.at[slot, 0], sem_w.at[slot]),
                pltpu.make_async_copy(wd_hbm.at[e, rows(n_ff + ft), :], wdbuf.at[slot, 1], sem_w.at[slot])]

    @pl.when(step == 0)
    def _():
        for g in range(WEIGHT_SLOTS - 1):
            @pl.when(g < live_steps)
            def _(g=g):
                for c in tile_copies(g):
                    c.start()

    @pl.when(step < live_steps)
    def _():
        for c in tile_copies(step):
            c.wait()

        @pl.when(step + WEIGHT_SLOTS - 1 < live_steps)
        def _():
            for c in tile_copies(step + WEIGHT_SLOTS - 1):
                c.start()

    wslot = step % WEIGHT_SLOTS
    wga_ref = wgbuf.at[wslot, 0]
    wgb_ref = wgbuf.at[wslot, 1]
    wda_ref = wdbuf.at[wslot, 0]
    wdb_ref = wdbuf.at[wslot, 1]
    nsub = item_nsub[i]
    nzero = item_nzero[i]
    row0 = item_row[i]
    SUB = ROW_BLK
    YS = SUB * ROW_SLABS
    PS = PACK_SLABS
    XS = SUB * PS

    def y_copy(slot, j):
        dst = y_hbm.at[pl.ds(pl.multiple_of((row0 + j * SUB) * ROW_SLABS, YS), YS)]
        return pltpu.make_async_copy(ybuf.at[pl.ds(pl.multiple_of(slot * YS, YS), YS)], dst, sem_out.at[slot])

    def gather_copy(tok, r):
        src = u_hbm.at[pl.ds(pl.multiple_of(tok * PS, PS), PS)]
        return pltpu.make_async_copy(src, stage.at[pl.ds(pl.multiple_of(r * PS, PS), PS)], sem_in)

    n_assign = sorted_tok.shape[0]
    covers = [jnp.logical_and(nsub > lo, nsub <= m) for lo, m in zip((0,) + ITEM_VARIANTS[:-1], ITEM_VARIANTS)]
    nxt = jnp.minimum(i + 1, n_items - 1)

    def gathered_subs(item):
        ns = item_nsub[item]
        size = sum(jnp.where(jnp.logical_and(ns > lo, ns <= m), m, 0)
                   for lo, m in zip((0,) + ITEM_VARIANTS[:-1], ITEM_VARIANTS))
        return jnp.maximum(size, n_ff)

    def issue_rows(item, lo, count, unrolled):
        src0 = item_src[item]

        def one(r, p):
            gather_copy(sorted_tok[jnp.minimum(src0 + r, n_assign - 1)], r).start(priority=p)

        if unrolled:
            for k in range(count):
                one(lo + k, k % 2)
        else:
            def body(q, c):
                for p in range(2):
                    one(lo + 2 * q + p, p)
                return c
            lax.fori_loop(0, count // 2, body, 0)

    def wait_rows(item):
        def body(j, c):
            pltpu.make_async_copy(u_hbm.at[pl.ds(0, XS)], stage.at[pl.ds(0, XS)], sem_in).wait()
            return c
        lax.fori_loop(0, gathered_subs(item), body, 0)

    @pl.when(jnp.logical_and(i == 0, f == 0))
    def _():
        issue_rows(0, 0, gathered_subs(0) * SUB, False)
        acc[...] = jnp.zeros(acc.shape, F32)

    @pl.when(f == n_ff - 1)
    def _():
        issue_rows(nxt, n_ff * SUB, (gathered_subs(nxt) - n_ff) * SUB, False)

    lane = lax.broadcasted_iota(jnp.int32, (1, 2 * FF_TILE), 1)
    even = (lane % 2) == 0
    e_row = pl.ds(item_e[i], 1)
    bga = bgu_ref[e_row, pl.ds(pl.multiple_of(f * (2 * FF_TILE), 2 * FF_TILE), 2 * FF_TILE)]
    bgb = bgu_ref[e_row, pl.ds(pl.multiple_of((n_ff + f) * (2 * FF_TILE), 2 * FF_TILE), 2 * FF_TILE)]
    W2 = 2 * FF_TILE

    @pl.when(f == 0)
    def _():
        wait_rows(i)
        half = D_MODEL // 2

        def unpack(j, c):
            rows = pl.ds(pl.multiple_of(j * SUB, SUB), SUB)
            for s in range(PS):
                p = stage[pl.ds(j * XS + s, SUB, stride=PS), :]
                lo = lax.bitcast_convert_type(p << 16, F32)
                hi = lax.bitcast_convert_type(p & jnp.uint32(0xFFFF0000), F32)
                xbuf[rows, s * LANES:(s + 1) * LANES] = lo.astype(BF16)
                xbuf[rows, half + s * LANES:half + (s + 1) * LANES] = hi.astype(BF16)
            return c
        lax.fori_loop(0, sum(jnp.where(c, m, 0) for m, c in zip(ITEM_VARIANTS, covers)), unpack, 0)

    @pl.when(nsub == 0)
    def _():
        issue_rows(nxt, f * SUB, SUB, False)

    def block(rows):
        xs = xbuf[0:rows, :]
        wa_s[...] = wga_ref[...].astype(BF16)
        ga = _dot(xs, wa_s[...]) + bga
        issue_rows(nxt, f * SUB, SUB, True)
        wb_s[...] = wgb_ref[...].astype(BF16)
        gb = _dot(xs, wb_s[...]) + bgb
        wd_s[...] = pltpu.bitcast(_pack_bf16_pairs(wda_ref[...], wdb_ref[...]), BF16)
        gate = jnp.where(even, ga, pltpu.roll(gb, 1, 1))
        up = jnp.where(even, pltpu.roll(ga, W2 - 1, 1), gb)
        gate = jnp.minimum(gate, SWIGLU_LIMIT)
        up = jnp.clip(up, -SWIGLU_LIMIT, SWIGLU_LIMIT)
        act = ((up + 1.0) * (gate * jax.nn.sigmoid(gate * SWIGLU_ALPHA))).astype(BF16)
        for n in range(0, D_MODEL, DOWN_TILE):
            part = _dot(act, wd_s[:, n:n + DOWN_TILE])
            acc[0:rows, n:n + DOWN_TILE] = jnp.where(f > 0, acc[0:rows, n:n + DOWN_TILE] + part, part)

    for m, cover in zip(ITEM_VARIANTS, covers):
        @pl.when(cover)
        def _(m=m):
            block(m * SUB)

    @pl.when(f == n_ff - 1)
    def _():
        @pl.when(i == n_items - 1)
        def _():
            wait_rows(nxt)

        def stage_out(q, c):
            slot = q % Y_SLOTS

            @pl.when(q >= Y_SLOTS)
            def _():
                y_copy(slot, 0).wait()
            val = acc[pl.ds(pl.multiple_of(q * SUB, SUB), SUB), :] + bdn_ref[e_row, :]
            for s in range(ROW_SLABS):
                ybuf[pl.ds(slot * YS + s, SUB, stride=ROW_SLABS), :] = val[:, s * LANES:(s + 1) * LANES]
            y_copy(slot, q).start()
            return c
        lax.fori_loop(0, nsub, stage_out, 0)

        for k in range(Y_SLOTS):
            @pl.when(nsub > k)
            def _(k=k):
                y_copy(k, 0).wait()

    @pl.when(jnp.logical_and(f == n_ff - 1, nzero > 0))
    def _():
        ybuf[0:YS, :] = jnp.zeros((YS, LANES), F32)

        def start(j, c):
            y_copy(0, j).start()
            return c
        lax.fori_loop(0, nzero, start, 0)

        def wait(j, c):
            y_copy(0, 0).wait()
            return c
        lax.fori_loop(0, nzero, wait, 0)


def _experts_call(u_packed, sorted_tok, n_rows, items, w_gate_up, b_gate_up, w_down, b_down, y_in=None):
    n_ff = (D_FF // 2) // FF_TILE
    assert ITEM_VARIANTS[0] >= n_ff and ITEM_VARIANTS[-1] * ROW_BLK == ITEM_ROWS
    W2 = 2 * FF_TILE
    item_e, item_row, item_nsub, item_nzero, item_live, item_src = items
    n_items = item_e.shape[0]
    aliased = y_in is not None

    whole = lambda shape: pl.BlockSpec(shape, lambda i, f, e, r, n, z, lv, sr, st: (0, 0))
    grid_spec = pltpu.PrefetchScalarGridSpec(
        num_scalar_prefetch=7,
        grid=(n_items, n_ff),
        in_specs=[
            pl.BlockSpec(memory_space=pl.ANY),
            pl.BlockSpec(memory_space=pl.ANY),
            pl.BlockSpec(memory_space=pl.ANY),
            whole((N_EXPERTS, 2 * D_FF)),
            whole((N_EXPERTS, D_MODEL)),
        ] + ([pl.BlockSpec(memory_space=pl.ANY)] if aliased else []),
        out_specs=pl.BlockSpec(memory_space=pl.ANY),
        scratch_shapes=[pltpu.VMEM((ITEM_ROWS * PACK_SLABS, LANES), jnp.uint32),
                        pltpu.VMEM((ITEM_ROWS, D_MODEL), BF16),
                        pltpu.VMEM((ITEM_ROWS, D_MODEL), F32),
                        pltpu.VMEM((Y_SLOTS * ROW_BLK * ROW_SLABS, LANES), F32),
                        pltpu.VMEM((WEIGHT_SLOTS, 2, D_MODEL, W2), F32),
                        pltpu.VMEM((WEIGHT_SLOTS, 2, FF_TILE, D_MODEL), F32),
                        pltpu.VMEM((D_MODEL, W2), BF16),
                        pltpu.VMEM((D_MODEL, W2), BF16),
                        pltpu.VMEM((W2, D_MODEL), BF16),
                        pltpu.SemaphoreType.DMA(()),
                        pltpu.SemaphoreType.DMA((Y_SLOTS,)),
                        pltpu.SemaphoreType.DMA((WEIGHT_SLOTS,))],
    )
    args = (item_e, item_row, item_nsub, item_nzero, item_live, item_src, sorted_tok, u_packed,
            w_gate_up, w_down, b_gate_up, b_down)
    return pl.pallas_call(
        functools.partial(_expert_kernel, n_ff, n_items, aliased),
        grid_spec=grid_spec,
        out_shape=jax.ShapeDtypeStruct((n_rows * ROW_SLABS, LANES), F32),
        input_output_aliases={len(args): 0} if aliased else {},
        compiler_params=_params(("arbitrary", "arbitrary")),
        name="experts_overflow" if aliased else "experts",
    )(*args, *((y_in,) if aliased else ()))


def _experts(u_packed, sorted_tok, n_rows, items, n_used, w_gate_up, b_gate_up, w_down, b_down):
    n_main = N_EXPERTS + 1
    weights = (w_gate_up, b_gate_up, w_down, b_down)
    y = _experts_call(u_packed, sorted_tok, n_rows, tuple(a[:n_main] for a in items), *weights)
    rest = tuple(a[n_main:] for a in items)
    return lax.cond(n_used > n_main,
                    lambda y_: _experts_call(u_packed, sorted_tok, n_rows, rest, *weights, y_in=y_),
                    lambda y_: y_, y)


def _combine_kernel(tm, n_steps, dest_ref, w_ref, y_hbm, h_ref, lnw_ref, o_ref, buf, osum, sem):
    i = pl.program_id(0)

    def copy(step, slot, t, k):
        d = dest_ref[(step * tm + t) * TOP_K + k]
        return pltpu.make_async_copy(y_hbm.at[d], buf.at[slot, k * tm + t], sem.at[slot])

    def issue(step, slot):
        def body(t, c):
            for k in range(TOP_K):
                copy(step, slot, t, k).start(priority=k % 2)
            return c
        lax.fori_loop(0, tm, body, 0, unroll=8)

    @pl.when(i == 0)
    def _():
        issue(0, 0)

    @pl.when(i + 1 < n_steps)
    def _():
        issue(i + 1, (i + 1) % 2)

    slot = i % 2

    pltpu.make_async_copy(y_hbm.at[pl.ds(0, TOP_K * tm)], buf.at[slot], sem.at[slot]).wait()

    def token(t, c):
        a = h_ref[t]
        for k in range(TOP_K):
            a = a + w_ref[(i * tm + t) * TOP_K + k] * buf[slot, k * tm + t]
        osum[pl.ds(pl.multiple_of(t * ROW_SLABS, ROW_SLABS), ROW_SLABS), :] = a
        return c
    lax.fori_loop(0, tm, token, 0, unroll=8)

    h = osum[...].reshape(tm, ROW_SLABS, LANES)
    ms = jnp.mean(jnp.mean(h * h, axis=2, keepdims=True), axis=1, keepdims=True)
    osum[...] = (h * lax.rsqrt(ms + EPS) * lnw_ref[...]).reshape(tm * ROW_SLABS, LANES)
    for s in range(ROW_SLABS):
        o_ref[:, s * LANES:(s + 1) * LANES] = osum[pl.ds(s, tm, stride=ROW_SLABS), :]


def _combine(dest, wflat, y3, h3, ln_w):
    T = h3.shape[0]
    tm = 128
    n_steps = T // tm
    grid_spec = pltpu.PrefetchScalarGridSpec(
        num_scalar_prefetch=2,
        grid=(n_steps,),
        in_specs=[pl.BlockSpec(memory_space=pl.ANY),
                  pl.BlockSpec((tm, ROW_SLABS, LANES), lambda i, d, w: (i, 0, 0)),
                  pl.BlockSpec((1, ROW_SLABS, LANES), lambda i, d, w: (0, 0, 0))],
        out_specs=pl.BlockSpec((tm, D_MODEL), lambda i, d, w: (i, 0)),
        scratch_shapes=[pltpu.VMEM((2, TOP_K * tm, ROW_SLABS, LANES), F32),
                        pltpu.VMEM((tm * ROW_SLABS, LANES), F32),
                        pltpu.SemaphoreType.DMA((2,))],
    )
    return pl.pallas_call(
        functools.partial(_combine_kernel, tm, n_steps),
        grid_spec=grid_spec,
        out_shape=jax.ShapeDtypeStruct((T, D_MODEL), F32),
        compiler_params=_params(("arbitrary",)),
        name="combine",
    )(dest, wflat, y3, h3, ln_w.reshape(1, ROW_SLABS, LANES))


def _route(top_idx, top_w, n_rows, n_items):
    T = top_idx.shape[0]
    e_flat = top_idx.reshape(-1).astype(jnp.int32)
    onehot = (e_flat[:, None] == jnp.arange(N_EXPERTS, dtype=jnp.int32)[None, :]).astype(jnp.int32)
    csum = jnp.cumsum(onehot, axis=0)
    counts = csum[-1]
    padded = (counts + ROW_BLK - 1) // ROW_BLK * ROW_BLK
    pend = jnp.cumsum(padded)
    pstart = pend - padded
    dest = jnp.sum(onehot * (pstart[None, :] + csum - onehot), axis=1).astype(jnp.int32)

    per_e = (padded + ITEM_ROWS - 1) // ITEM_ROWS
    iend = jnp.cumsum(per_e)
    istart = iend - per_e
    ii = jnp.arange(n_items, dtype=jnp.int32)
    total = iend[-1]
    live = (ii < total).astype(jnp.int32)
    ic = jnp.minimum(ii, total - 1)
    ie = jnp.minimum(jnp.sum(ic[:, None] >= iend[None, :], axis=1), N_EXPERTS - 1).astype(jnp.int32)
    within = ic - istart[ie]
    irow = (pstart[ie] + within * ITEM_ROWS).astype(jnp.int32)
    insub = jnp.minimum((padded[ie] - within * ITEM_ROWS) // ROW_BLK, ITEM_ROWS // ROW_BLK).astype(jnp.int32)
    insub = insub * live
    tail_rows = n_rows - pend[-1]
    tail = jnp.logical_and(ii == total, tail_rows > 0)
    inzero = jnp.where(tail, tail_rows // ROW_BLK, 0).astype(jnp.int32)
    irow = jnp.where(tail, pend[-1], irow).astype(jnp.int32)
    n_used = total + (tail_rows > 0).astype(jnp.int32)
    n_assign = T * TOP_K
    assert N_EXPERTS * n_assign < 2 ** 31
    order = jnp.sort(e_flat * n_assign + jnp.arange(n_assign, dtype=jnp.int32)) % n_assign
    sorted_tok = (order // TOP_K).astype(jnp.int32)
    cstart = jnp.cumsum(counts) - counts
    isrc = ((cstart[ie] + within * ITEM_ROWS) * live).astype(jnp.int32)
    return dest, top_w.reshape(-1).astype(F32), (ie, irow, insub, inzero, live, isrc), n_used, sorted_tok


def kernel(x, positions, ln_mix_w, w_in, conv_w, conv_b, dt_bias, a_log, d_skip, ssm_norm_w, w_out,
           ln_ffn_w, w_router, b_router, w_gate_up, b_gate_up, w_down, b_down, ln_final_w):
    B, L, _ = x.shape
    T = B * L
    assert B == 1 and T % 1024 == 0
    x2 = x.reshape(T, D_MODEL)
    half = RET_HEAD_DIM // 2
    inv_freq = (ROPE_BASE ** (-jnp.arange(half, dtype=F32) / half)).reshape(1, half)
    pos_col = positions.reshape(T, 1).astype(F32)

    proj = _inproj(x2, ln_mix_w[0], jnp.swapaxes(w_in[0], 0, 1))
    ret = _retention(proj, pos_col, inv_freq)
    ssm = _ssd(proj, conv_w[0], conv_b[0], dt_bias[0], a_log[0], d_skip[0], ssm_norm_w[0])
    h_slabs, u_packed, top_idx, top_w = _outproj(ret, ssm, w_out[0].astype(BF16), x2, ln_ffn_w[0], w_router[0],
                                                 b_router[0])

    n_rows = -(-(T * TOP_K + N_EXPERTS * (ROW_BLK - 1)) // ROW_BLK) * ROW_BLK
    n_items = N_EXPERTS + 1 + n_rows // ITEM_ROWS
    dest, wflat, items, n_used, sorted_tok = _route(top_idx.T, top_w.T, n_rows, n_items)

    y_rows = _experts(u_packed, sorted_tok, n_rows, items, n_used, w_gate_up[0], b_gate_up[0], w_down[0],
                      b_down[0])
    out = _combine(dest, wflat, y_rows.reshape(n_rows, ROW_SLABS, LANES),
                   h_slabs.reshape(T, ROW_SLABS, LANES), ln_final_w)
    return out.reshape(B, L, D_MODEL)
```

```python
import functools

import numpy as np
import jax
import jax.numpy as jnp
from jax import lax
from jax.experimental import pallas as pl
from jax.experimental.pallas import tpu as pltpu

F32 = jnp.float32
BF16 = jnp.bfloat16

D_MODEL = 2048
RET_HEADS = 4
RET_HEAD_DIM = 256
RET_WIDTH = RET_HEADS * RET_HEAD_DIM
SSM_WIDTH = D_MODEL - RET_WIDTH
SSM_HEAD_DIM = 64
SSM_HEADS = SSM_WIDTH // SSM_HEAD_DIM
SSM_GROUPS = 2
SSM_STATE = 128
CONV_WIDTH = 4
XBC_WIDTH = SSM_WIDTH + 2 * SSM_GROUPS * SSM_STATE
D_IN_PROJ = 4 * RET_WIDTH + SSM_WIDTH + XBC_WIDTH + SSM_HEADS
ROPE_BASE = 10000.0
N_EXPERTS = 32
TOP_K = 4
D_FF = D_MODEL
SWIGLU_LIMIT = 7.0
SWIGLU_ALPHA = 1.702
EPS = 1e-6

LANES = 128
VMEM_LIMIT = 56 * 1024 * 1024

RET_CHUNK = 256
SSD_CHUNK = 128
ROW_BLK = 128
ITEM_ROWS = 1536
ITEM_VARIANTS = (8, 9, 10, 12)
FF_TILE = 128
DOWN_TILE = 512
WEIGHT_SLOTS = 3
Y_SLOTS = 8
ROW_SLABS = D_MODEL // LANES
PACK_SLABS = ROW_SLABS // 2


def _params(sem, **kw):
    return pltpu.CompilerParams(dimension_semantics=sem, vmem_limit_bytes=VMEM_LIMIT, **kw)


def _dot(a, b):
    return jnp.dot(a, b, preferred_element_type=F32)


def _dot_nt(a, b):
    return lax.dot_general(a, b, (((1,), (1,)), ((), ())), preferred_element_type=F32)


def _dot_tn(a, b):
    return lax.dot_general(a, b, (((0,), (0,)), ((), ())), preferred_element_type=F32)


def _split3(x):
    hi = x.astype(BF16)
    r = x - hi.astype(F32)
    mid = r.astype(BF16)
    lo = (r - mid.astype(F32)).astype(BF16)
    return hi, mid, lo


def _dot_exact_rhs01(x, m01):
    hi, mid, lo = _split3(x)
    return _dot(hi, m01) + _dot(mid, m01) + _dot(lo, m01)


def _dot_exact_lhs01(m01, x):
    hi, mid, lo = _split3(x)
    return _dot(m01, hi) + _dot(m01, mid) + _dot(m01, lo)


def _silu(x):
    return x * jax.nn.sigmoid(x)


def _pack_bf16_pairs(lo, hi):
    lo_bits = lax.bitcast_convert_type(lo.astype(BF16).astype(F32), jnp.uint32)
    hi_bits = lax.bitcast_convert_type(hi.astype(BF16).astype(F32), jnp.uint32)
    return hi_bits | (lo_bits >> 16)


INPROJ_CHUNKS = 4


def _inproj_kernel(x_hbm, lnw_ref, w_ref, o_ref, xs_ref, u_ref, sem):
    tm = u_ref.shape[0]
    rows = tm // INPROJ_CHUNKS
    i = pl.program_id(0)
    j = pl.program_id(1)

    def copy(tile, c):
        src = x_hbm.at[pl.ds(pl.multiple_of(tile * tm + c * rows, rows), rows)]
        return pltpu.make_async_copy(src, xs_ref.at[c], sem.at[c])

    @pl.when(jnp.logical_and(i == 0, j == 0))
    def _():
        for c in range(INPROJ_CHUNKS):
            copy(0, c).start()

    @pl.when(j == 0)
    def _():
        for c in range(INPROJ_CHUNKS):
            copy(i, c).wait()
            x = xs_ref[c]
            ms = jnp.mean(x * x, axis=-1, keepdims=True)
            u_ref[c * rows:(c + 1) * rows, :] = (x * lax.rsqrt(ms + EPS) * lnw_ref[...]).astype(BF16)

    @pl.when(jnp.logical_and(j == pl.num_programs(1) - 1, i + 1 < pl.num_programs(0)))
    def _():
        for c in range(INPROJ_CHUNKS):
            copy(i + 1, c).start()

    tn = o_ref.shape[1]
    tail = D_IN_PROJ % tn
    if 0 < tail <= LANES:
        last = pl.num_programs(1) - 1

        @pl.when(pl.program_id(1) < last)
        def _():
            o_ref[...] = _dot_nt(u_ref[...], w_ref[...].astype(BF16))

        @pl.when(pl.program_id(1) == last)
        def _():
            o_ref[:, 0:LANES] = _dot_nt(u_ref[...], w_ref[0:LANES, :].astype(BF16))
    else:
        o_ref[...] = _dot_nt(u_ref[...], w_ref[...].astype(BF16))


def _inproj(x2, ln_w, w_in_t):
    T = x2.shape[0]
    tm, tn = (2048 if T % 2048 == 0 else 1024), 512
    return pl.pallas_call(
        _inproj_kernel,
        grid=(T // tm, pl.cdiv(D_IN_PROJ, tn)),
        in_specs=[pl.BlockSpec(memory_space=pl.ANY),
                  pl.BlockSpec((1, D_MODEL), lambda i, j: (0, 0)),
                  pl.BlockSpec((tn, D_MODEL), lambda i, j: (j, 0))],
        out_specs=pl.BlockSpec((tm, tn), lambda i, j: (i, j)),
        out_shape=jax.ShapeDtypeStruct((T, D_IN_PROJ), F32),
        scratch_shapes=[pltpu.VMEM((INPROJ_CHUNKS, tm // INPROJ_CHUNKS, D_MODEL), F32),
                        pltpu.VMEM((tm, D_MODEL), BF16),
                        pltpu.SemaphoreType.DMA((INPROJ_CHUNKS,))],
        compiler_params=_params(("arbitrary", "arbitrary")),
        name="inproj",
    )(x2, ln_w.reshape(1, D_MODEL), w_in_t)


def _retention_tables():
    C = RET_CHUNK
    h = np.arange(RET_HEADS, dtype=np.float64)
    log_gamma = np.log1p(-np.exp2(-5.0 - h))
    idx = np.arange(C, dtype=np.float64)
    rel = idx[:, None] - idx[None, :]
    intra = np.where(rel >= 0, np.exp(log_gamma[:, None, None] * np.maximum(rel, 0.0)), 0.0)
    q_decay = np.exp(log_gamma[:, None] * (idx + 1.0))
    k_decay = np.exp(log_gamma[:, None] * (C - 1.0 - idx))
    chunk_decay = np.exp(log_gamma * C)
    qd = np.broadcast_to(q_decay[:, :, None], (RET_HEADS, C, RET_HEAD_DIM))
    kd = np.broadcast_to(k_decay[:, :, None], (RET_HEADS, C, RET_HEAD_DIM))
    return (jnp.asarray(intra, F32), jnp.asarray(qd, F32), jnp.asarray(kd, F32),
            [float(c) for c in chunk_decay])


def _retention_kernel(chunk_decay, pos_ref, invf_ref, q_ref, k_ref, v_ref, g_ref,
                      intra_ref, qd_ref, kd_ref, o_ref, state_ref):
    @pl.when(pl.program_id(0) == 0)
    def _():
        state_ref[...] = jnp.zeros_like(state_ref)

    half = RET_HEAD_DIM // 2
    ang = pos_ref[...] * invf_ref[...]
    cos = jnp.cos(ang)
    sin = jnp.sin(ang)

    def rope(t):
        t1, t2 = t[:, :half], t[:, half:]
        return jnp.concatenate([t1 * cos - t2 * sin, t2 * cos + t1 * sin], axis=-1)

    for h in range(RET_HEADS):
        sl = slice(h * RET_HEAD_DIM, (h + 1) * RET_HEAD_DIM)
        q = rope(q_ref[:, sl])
        k = rope(k_ref[:, sl]) * (RET_HEAD_DIM ** -0.5)
        v = v_ref[:, sl].astype(BF16)
        state = state_ref[h]
        scores = _dot_nt(q.astype(BF16), k.astype(BF16)) * intra_ref[h]
        inner = _dot(scores.astype(BF16), v)
        cross = _dot((q * qd_ref[h]).astype(BF16), state.astype(BF16))
        state_ref[h] = chunk_decay[h] * state + _dot_tn((k * kd_ref[h]).astype(BF16), v)
        o = inner + cross
        o = o * lax.rsqrt(jnp.mean(o * o, axis=-1, keepdims=True) + EPS)
        o_ref[:, sl] = (o * _silu(g_ref[:, sl])).astype(o_ref.dtype)


def _retention(proj, pos_col, inv_freq):
    T = proj.shape[0]
    C = RET_CHUNK
    intra, qd, kd, chunk_decay = _retention_tables()
    col = lambda j: pl.BlockSpec((C, RET_WIDTH), lambda c, j=j: (c, j))
    const3 = lambda shape: pl.BlockSpec(shape, lambda c: (0, 0, 0))
    return pl.pallas_call(
        functools.partial(_retention_kernel, chunk_decay),
        grid=(T // C,),
        in_specs=[pl.BlockSpec((C, 1), lambda c: (c, 0)),
                  pl.BlockSpec((1, RET_HEAD_DIM // 2), lambda c: (0, 0)),
                  col(0), col(1), col(2), col(3),
                  const3((RET_HEADS, C, C)),
                  const3((RET_HEADS, C, RET_HEAD_DIM)),
                  const3((RET_HEADS, C, RET_HEAD_DIM))],
        out_specs=pl.BlockSpec((C, RET_WIDTH), lambda c: (c, 0)),
        out_shape=jax.ShapeDtypeStruct((T, RET_WIDTH), BF16),
        scratch_shapes=[pltpu.VMEM((RET_HEADS, RET_HEAD_DIM, RET_HEAD_DIM), F32)],
        compiler_params=_params(("arbitrary",)),
        name="retention",
    )(pos_col, inv_freq, proj, proj, proj, proj, intra, qd, kd)


def _ssd_kernel(xs0_ref, xs1_ref, bc_ref, z_ref, dt_ref, convw_ref, convb_ref, dtb_ref, a_ref,
                dskip_ref, normw_ref, expand_ref, o_ref, ext_ref, state_ref):
    C = SSD_CHUNK
    HW = SSM_WIDTH // SSM_GROUPS
    CARRY = 8

    @pl.when(pl.program_id(0) == 0)
    def _():
        ext_ref[0:CARRY, :] = jnp.zeros((CARRY, XBC_WIDTH), F32)
        state_ref[...] = jnp.zeros_like(state_ref)

    ext_ref[CARRY:CARRY + C, 0:HW] = xs0_ref[...]
    ext_ref[CARRY:CARRY + C, HW:2 * HW] = xs1_ref[...]
    ext_ref[CARRY:CARRY + C, 2 * HW:3 * HW] = bc_ref[...]
    conv = convb_ref[...]
    for k in range(CONV_WIDTH):
        off = CARRY - (CONV_WIDTH - 1) + k
        conv = conv + convw_ref[k:k + 1, :] * ext_ref[off:off + C, :]
    ext_ref[0:CARRY, :] = ext_ref[C:C + CARRY, :]
    xbc = _silu(conv)
    xs = xbc[:, :SSM_WIDTH]

    lane = lax.broadcasted_iota(jnp.int32, (1, LANES), 1)
    dt_raw = jnp.where(lane < SSM_HEADS, dt_ref[...], 0.0) + dtb_ref[...]
    dt = jnp.maximum(dt_raw, 0.0) + jnp.log1p(jnp.exp(-jnp.abs(dt_raw)))
    dta = dt * a_ref[...]

    row = lax.broadcasted_iota(jnp.int32, (C, C), 0)
    colm = lax.broadcasted_iota(jnp.int32, (C, C), 1)
    tril = row >= colm
    a_cum = _dot_exact_lhs01(jnp.where(tril, 1.0, 0.0).astype(BF16), dta)
    a_cum_t = a_cum.T

    expand = expand_ref[...]
    a_exp = _dot_exact_rhs01(a_cum, expand)
    dt_exp = _dot_exact_rhs01(dt, expand)
    a_last = a_exp[C - 1:C, :]
    decay_in = jnp.exp(a_exp)
    decay_out = jnp.exp(a_last - a_exp)
    chunk_decay = jnp.exp(a_last)
    xdt = xs * dt_exp

    lane2 = lax.broadcasted_iota(jnp.int32, (1, LANES), 1)
    lo_head = lane2 < SSM_HEAD_DIM
    ys = []
    for g in range(SSM_GROUPS):
        gs = slice(g * HW, (g + 1) * HW)
        b_g = xbc[:, SSM_WIDTH + g * SSM_STATE:SSM_WIDTH + (g + 1) * SSM_STATE].astype(BF16)
        c0 = SSM_WIDTH + SSM_GROUPS * SSM_STATE
        c_g = xbc[:, c0 + g * SSM_STATE:c0 + (g + 1) * SSM_STATE].astype(BF16)
        cb = _dot_nt(c_g, b_g)
        state = state_ref[g]
        y_off = _dot(c_g, state.astype(BF16)) * decay_in[:, gs]
        xw = (xdt[:, gs] * decay_out[:, gs]).astype(BF16)
        state_ref[g] = chunk_decay[:, gs] * state + _dot_tn(b_g, xw)
        slabs = []
        for s in range(HW // LANES):
            xd = xdt[:, g * HW + s * LANES:g * HW + (s + 1) * LANES]
            acc = None
            for e in range(2):
                hh = g * (SSM_HEADS // SSM_GROUPS) + 2 * s + e
                seg = a_cum[:, hh:hh + 1] - a_cum_t[hh:hh + 1, :]
                m = cb * jnp.exp(jnp.where(tril, seg, -jnp.inf))
                xm = jnp.where(lo_head if e == 0 else jnp.logical_not(lo_head), xd, 0.0)
                part = _dot(m.astype(BF16), xm.astype(BF16))
                acc = part if acc is None else acc + part
            slabs.append(acc)
        ys.append(jnp.concatenate(slabs, axis=-1) + y_off)
    y = jnp.concatenate(ys, axis=-1) + dskip_ref[...] * xs
    y = y * _silu(z_ref[...])
    outs = []
    for g in range(SSM_GROUPS):
        yg = y[:, g * HW:(g + 1) * HW]
        outs.append(yg * lax.rsqrt(jnp.mean(yg * yg, axis=-1, keepdims=True) + EPS))
    o_ref[...] = (jnp.concatenate(outs, axis=-1) * normw_ref[...]).astype(o_ref.dtype)


def _ssd(proj, conv_w, conv_b, dt_bias, a_log, d_skip, ssm_norm_w):
    T = proj.shape[0]
    C = SSD_CHUNK
    HW = SSM_WIDTH // SSM_GROUPS
    xbc0 = (4 * RET_WIDTH + SSM_WIDTH) // HW
    dt0 = (D_IN_PROJ - SSM_HEADS) // LANES
    pad = lambda v: jnp.zeros((1, LANES), F32).at[0, :SSM_HEADS].set(v.astype(F32))
    a_neg = pad(-jnp.exp(a_log.astype(F32)))
    expand_np = np.zeros((LANES, SSM_WIDTH), np.float32)
    for hh in range(SSM_HEADS):
        expand_np[hh, hh * SSM_HEAD_DIM:(hh + 1) * SSM_HEAD_DIM] = 1.0
    expand = jnp.asarray(expand_np, BF16)
    dskip_exp = jnp.repeat(d_skip.astype(F32), SSM_HEAD_DIM).reshape(1, SSM_WIDTH)
    const = lambda shape: pl.BlockSpec(shape, lambda c: (0, 0))
    return pl.pallas_call(
        _ssd_kernel,
        grid=(T // C,),
        in_specs=[pl.BlockSpec((C, HW), lambda c: (c, xbc0)),
                  pl.BlockSpec((C, HW), lambda c: (c, xbc0 + 1)),
                  pl.BlockSpec((C, HW), lambda c: (c, xbc0 + 2)),
                  pl.BlockSpec((C, SSM_WIDTH), lambda c: (c, 4 * RET_WIDTH // SSM_WIDTH)),
                  pl.BlockSpec((C, LANES), lambda c: (c, dt0)),
                  const((CONV_WIDTH, XBC_WIDTH)), const((1, XBC_WIDTH)),
                  const((1, LANES)), const((1, LANES)),
                  const((1, SSM_WIDTH)), const((1, SSM_WIDTH)),
                  const((LANES, SSM_WIDTH))],
        out_specs=pl.BlockSpec((C, SSM_WIDTH), lambda c: (c, 0)),
        out_shape=jax.ShapeDtypeStruct((T, SSM_WIDTH), BF16),
        scratch_shapes=[pltpu.VMEM((C + 8, XBC_WIDTH), F32),
                        pltpu.VMEM((SSM_GROUPS, SSM_STATE, HW), F32)],
        compiler_params=_params(("arbitrary",)),
        name="ssd",
    )(proj, proj, proj, proj, proj, conv_w, conv_b.reshape(1, XBC_WIDTH), pad(dt_bias), a_neg,
      dskip_exp, ssm_norm_w.reshape(1, SSM_WIDTH), expand)


def _outproj_kernel(ret_ref, ssm_ref, w_ref, x_ref, lnw_ref, wr_ref, br_ref, hs_ref, up_ref, ti_ref, tw_ref):
    tm = x_ref.shape[0]
    h = (x_ref[...] + _dot(ret_ref[...], w_ref[0:RET_WIDTH, :])
         + _dot(ssm_ref[...], w_ref[RET_WIDTH:D_MODEL, :]))
    for s in range(ROW_SLABS):
        hs_ref[pl.ds(s, tm, stride=ROW_SLABS), :] = h[:, s * LANES:(s + 1) * LANES]
    u = h * lax.rsqrt(jnp.mean(h * h, axis=-1, keepdims=True) + EPS) * lnw_ref[...]
    packed = _pack_bf16_pairs(u[:, :D_MODEL // 2], u[:, D_MODEL // 2:])
    for s in range(PACK_SLABS):
        up_ref[pl.ds(s, tm, stride=PACK_SLABS), :] = packed[:, s * LANES:(s + 1) * LANES]
    E = N_EXPERTS
    uh, um, ul = _split3(u)
    ph = _dot_nt(wr_ref[...], uh)
    pm = _dot_nt(wr_ref[...], um)
    pw = _dot_nt(wr_ref[...], ul)
    lg = (ph[0:E] + (ph[E:2 * E] + pm[0:E]) + (ph[2 * E:3 * E] + pm[E:2 * E] + pw[0:E])) + br_ref[...]
    expert = lax.broadcasted_iota(jnp.int32, lg.shape, 0)
    slot = lax.broadcasted_iota(jnp.int32, (TOP_K, tm), 0)
    top_l = jnp.zeros((TOP_K, tm), F32)
    top_i = jnp.zeros((TOP_K, tm), jnp.int32)
    work = lg
    for k in range(TOP_K):
        best = jnp.max(work, axis=0, keepdims=True)
        which = jnp.min(jnp.where(work == best, expert, E), axis=0, keepdims=True)
        top_l = jnp.where(slot == k, best, top_l)
        top_i = jnp.where(slot == k, which, top_i)
        work = jnp.where(expert == which, -jnp.inf, work)
    p = jnp.exp(top_l - jnp.max(top_l, axis=0, keepdims=True))
    ti_ref[...] = top_i
    tw_ref[...] = p / jnp.sum(p, axis=0, keepdims=True)


def _outproj(ret, ssm, w_out_bf16, x2, ln_w, w_router, b_router):
    T = x2.shape[0]
    tm = 512
    return pl.pallas_call(
        _outproj_kernel,
        grid=(T // tm,),
        in_specs=[pl.BlockSpec((tm, RET_WIDTH), lambda i: (i, 0)),
                  pl.BlockSpec((tm, SSM_WIDTH), lambda i: (i, 0)),
                  pl.BlockSpec((D_MODEL, D_MODEL), lambda i: (0, 0)),
                  pl.BlockSpec((tm, D_MODEL), lambda i: (i, 0)),
                  pl.BlockSpec((1, D_MODEL), lambda i: (0, 0)),
                  pl.BlockSpec((3 * N_EXPERTS, D_MODEL), lambda i: (0, 0)),
                  pl.BlockSpec((N_EXPERTS, 1), lambda i: (0, 0))],
        out_specs=[pl.BlockSpec((tm * ROW_SLABS, LANES), lambda i: (i, 0)),
                   pl.BlockSpec((tm * PACK_SLABS, LANES), lambda i: (i, 0)),
                   pl.BlockSpec((TOP_K, tm), lambda i: (0, i)),
                   pl.BlockSpec((TOP_K, tm), lambda i: (0, i))],
        out_shape=[jax.ShapeDtypeStruct((T * ROW_SLABS, LANES), F32),
                   jax.ShapeDtypeStruct((T * PACK_SLABS, LANES), jnp.uint32),
                   jax.ShapeDtypeStruct((TOP_K, T), jnp.int32),
                   jax.ShapeDtypeStruct((TOP_K, T), F32)],
        compiler_params=_params(("parallel",)),
        name="outproj",
    )(ret, ssm, w_out_bf16, x2, ln_w.reshape(1, D_MODEL),
      jnp.concatenate([p.T for p in _split3(w_router)], axis=0), b_router.reshape(N_EXPERTS, 1))


def _expert_kernel(n_ff, n_items, aliased, *refs):
    (item_e, item_row, item_nsub, item_nzero, item_live, item_src, sorted_tok,
     u_hbm, wg_hbm, wd_hbm, bgu_ref, bdn_ref) = refs[:12]
    (y_hbm, stage, xbuf, acc, ybuf, wgbuf, wdbuf, wa_s, wb_s, wd_s,
     sem_in, sem_out, sem_w) = refs[13:] if aliased else refs[12:]
    i = pl.program_id(0)
    f = pl.program_id(1)

    step = i * n_ff + f
    live_steps = item_live[0] * n_ff
    for k in range(1, n_items):
        live_steps = live_steps + item_live[k] * n_ff
    W2_ = 2 * FF_TILE

    def tile_copies(g):
        e = item_e[g // n_ff]
        ft = g % n_ff
        slot = g % WEIGHT_SLOTS
        cols = lambda b: pl.ds(pl.multiple_of(b * W2_, W2_), W2_)
        rows = lambda b: pl.ds(pl.multiple_of(b * FF_TILE, FF_TILE), FF_TILE)
        return [pltpu.make_async_copy(wg_hbm.at[e, :, cols(ft)], wgbuf.at[slot, 0], sem_w.at[slot]),
                pltpu.make_async_copy(wg_hbm.at[e, :, cols(n_ff + ft)], wgbuf.at[slot, 1], sem_w.at[slot]),
                pltpu.make_async_copy(wd_hbm.at[e, rows(ft), :], wdbuf.at[slot, 0], sem_w.at[slot]),
                pltpu.make_async_copy(wd_hbm.at[e, rows(n_ff + ft), :], wdbuf.at[slot, 1], sem_w.at[slot])]

    @pl.when(step == 0)
    def _():
        for g in range(WEIGHT_SLOTS - 1):
            @pl.when(g < live_steps)
            def _(g=g):
                for c in tile_copies(g):
                    c.start()

    @pl.when(step < live_steps)
    def _():
        for c in tile_copies(step):
            c.wait()

        @pl.when(step + WEIGHT_SLOTS - 1 < live_steps)
        def _():
            for c in tile_copies(step + WEIGHT_SLOTS - 1):
                c.start()

    wslot = step % WEIGHT_SLOTS
    wga_ref = wgbuf.at[wslot, 0]
    wgb_ref = wgbuf.at[wslot, 1]
    wda_ref = wdbuf.at[wslot, 0]
    wdb_ref = wdbuf.at[wslot, 1]
    nsub = item_nsub[i]
    nzero = item_nzero[i]
    row0 = item_row[i]
    SUB = ROW_BLK
    YS = SUB * ROW_SLABS
    PS = PACK_SLABS
    XS = SUB * PS

    def y_copy(slot, j):
        dst = y_hbm.at[pl.ds(pl.multiple_of((row0 + j * SUB) * ROW_SLABS, YS), YS)]
        return pltpu.make_async_copy(ybuf.at[pl.ds(pl.multiple_of(slot * YS, YS), YS)], dst, sem_out.at[slot])

    def gather_copy(tok, r):
        src = u_hbm.at[pl.ds(pl.multiple_of(tok * PS, PS), PS)]
        return pltpu.make_async_copy(src, stage.at[pl.ds(pl.multiple_of(r * PS, PS), PS)], sem_in)

    n_assign = sorted_tok.shape[0]
    covers = [jnp.logical_and(nsub > lo, nsub <= m) for lo, m in zip((0,) + ITEM_VARIANTS[:-1], ITEM_VARIANTS)]
    nxt = jnp.minimum(i + 1, n_items - 1)

    def gathered_subs(item):
        ns = item_nsub[item]
        size = sum(jnp.where(jnp.logical_and(ns > lo, ns <= m), m, 0)
                   for lo, m in zip((0,) + ITEM_VARIANTS[:-1], ITEM_VARIANTS))
        return jnp.maximum(size, n_ff)

    def issue_rows(item, lo, count, unrolled):
        src0 = item_src[item]

        def one(r, p):
            gather_copy(sorted_tok[jnp.minimum(src0 + r, n_assign - 1)], r).start(priority=p)

        if unrolled:
            for k in range(count):
                one(lo + k, k % 2)
        else:
            def body(q, c):
                for p in range(2):
                    one(lo + 2 * q + p, p)
                return c
            lax.fori_loop(0, count // 2, body, 0)

    def wait_rows(item):
        def body(j, c):
            pltpu.make_async_copy(u_hbm.at[pl.ds(0, XS)], stage.at[pl.ds(0, XS)], sem_in).wait()
            return c
        lax.fori_loop(0, gathered_subs(item), body, 0)

    @pl.when(jnp.logical_and(i == 0, f == 0))
    def _():
        issue_rows(0, 0, gathered_subs(0) * SUB, False)
        acc[...] = jnp.zeros(acc.shape, F32)

    @pl.when(f == n_ff - 1)
    def _():
        issue_rows(nxt, n_ff * SUB, (gathered_subs(nxt) - n_ff) * SUB, False)

    lane = lax.broadcasted_iota(jnp.int32, (1, 2 * FF_TILE), 1)
    even = (lane % 2) == 0
    e_row = pl.ds(item_e[i], 1)
    bga = bgu_ref[e_row, pl.ds(pl.multiple_of(f * (2 * FF_TILE), 2 * FF_TILE), 2 * FF_TILE)]
    bgb = bgu_ref[e_row, pl.ds(pl.multiple_of((n_ff + f) * (2 * FF_TILE), 2 * FF_TILE), 2 * FF_TILE)]
    W2 = 2 * FF_TILE

    @pl.when(f == 0)
    def _():
        wait_rows(i)
        half = D_MODEL // 2

        def unpack(j, c):
            rows = pl.ds(pl.multiple_of(j * SUB, SUB), SUB)
            for s in range(PS):
                p = stage[pl.ds(j * XS + s, SUB, stride=PS), :]
                lo = lax.bitcast_convert_type(p << 16, F32)
                hi = lax.bitcast_convert_type(p & jnp.uint32(0xFFFF0000), F32)
                xbuf[rows, s * LANES:(s + 1) * LANES] = lo.astype(BF16)
                xbuf[rows, half + s * LANES:half + (s + 1) * LANES] = hi.astype(BF16)
            return c
        lax.fori_loop(0, sum(jnp.where(c, m, 0) for m, c in zip(ITEM_VARIANTS, covers)), unpack, 0)

    @pl.when(nsub == 0)
    def _():
        issue_rows(nxt, f * SUB, SUB, False)

    def block(rows):
        xs = xbuf[0:rows, :]
        wa_s[...] = wga_ref[...].astype(BF16)
        ga = _dot(xs, wa_s[...]) + bga
        issue_rows(nxt, f * SUB, SUB, True)
        wb_s[...] = wgb_ref[...].astype(BF16)
        gb = _dot(xs, wb_s[...]) + bgb
        wd_s[...] = pltpu.bitcast(_pack_bf16_pairs(wda_ref[...], wdb_ref[...]), BF16)
        gate = jnp.where(even, ga, pltpu.roll(gb, 1, 1))
        up = jnp.where(even, pltpu.roll(ga, W2 - 1, 1), gb)
        gate = jnp.minimum(gate, SWIGLU_LIMIT)
        up = jnp.clip(up, -SWIGLU_LIMIT, SWIGLU_LIMIT)
        act = ((up + 1.0) * (gate * jax.nn.sigmoid(gate * SWIGLU_ALPHA))).astype(BF16)
        for n in range(0, D_MODEL, DOWN_TILE):
            part = _dot(act, wd_s[:, n:n + DOWN_TILE])
            acc[0:rows, n:n + DOWN_TILE] = jnp.where(f > 0, acc[0:rows, n:n + DOWN_TILE] + part, part)

    for m, cover in zip(ITEM_VARIANTS, covers):
        @pl.when(cover)
        def _(m=m):
            block(m * SUB)

    @pl.when(f == n_ff - 1)
    def _():
        @pl.when(i == n_items - 1)
        def _():
            wait_rows(nxt)

        def stage_out(q, c):
            slot = q % Y_SLOTS

            @pl.when(q >= Y_SLOTS)
            def _():
                y_copy(slot, 0).wait()
            val = acc[pl.ds(pl.multiple_of(q * SUB, SUB), SUB), :] + bdn_ref[e_row, :]
            for s in range(ROW_SLABS):
                ybuf[pl.ds(slot * YS + s, SUB, stride=ROW_SLABS), :] = val[:, s * LANES:(s + 1) * LANES]
            y_copy(slot, q).start()
            return c
        lax.fori_loop(0, nsub, stage_out, 0)

        for k in range(Y_SLOTS):
            @pl.when(nsub > k)
            def _(k=k):
                y_copy(k, 0).wait()

    @pl.when(jnp.logical_and(f == n_ff - 1, nzero > 0))
    def _():
        ybuf[0:YS, :] = jnp.zeros((YS, LANES), F32)

        def start(j, c):
            y_copy(0, j).start()
            return c
        lax.fori_loop(0, nzero, start, 0)

        def wait(j, c):
            y_copy(0, 0).wait()
            return c
        lax.fori_loop(0, nzero, wait, 0)


def _experts_call(u_packed, sorted_tok, n_rows, items, w_gate_up, b_gate_up, w_down, b_down, y_in=None):
    n_ff = (D_FF // 2) // FF_TILE
    assert ITEM_VARIANTS[0] >= n_ff and ITEM_VARIANTS[-1] * ROW_BLK == ITEM_ROWS
    W2 = 2 * FF_TILE
    item_e, item_row, item_nsub, item_nzero, item_live, item_src = items
    n_items = item_e.shape[0]
    aliased = y_in is not None

    whole = lambda shape: pl.BlockSpec(shape, lambda i, f, e, r, n, z, lv, sr, st: (0, 0))
    grid_spec = pltpu.PrefetchScalarGridSpec(
        num_scalar_prefetch=7,
        grid=(n_items, n_ff),
        in_specs=[
            pl.BlockSpec(memory_space=pl.ANY),
            pl.BlockSpec(memory_space=pl.ANY),
            pl.BlockSpec(memory_space=pl.ANY),
            whole((N_EXPERTS, 2 * D_FF)),
            whole((N_EXPERTS, D_MODEL)),
        ] + ([pl.BlockSpec(memory_space=pl.ANY)] if aliased else []),
        out_specs=pl.BlockSpec(memory_space=pl.ANY),
        scratch_shapes=[pltpu.VMEM((ITEM_ROWS * PACK_SLABS, LANES), jnp.uint32),
                        pltpu.VMEM((ITEM_ROWS, D_MODEL), BF16),
                        pltpu.VMEM((ITEM_ROWS, D_MODEL), F32),
                        pltpu.VMEM((Y_SLOTS * ROW_BLK * ROW_SLABS, LANES), F32),
                        pltpu.VMEM((WEIGHT_SLOTS, 2, D_MODEL, W2), F32),
                        pltpu.VMEM((WEIGHT_SLOTS, 2, FF_TILE, D_MODEL), F32),
                        pltpu.VMEM((D_MODEL, W2), BF16),
                        pltpu.VMEM((D_MODEL, W2), BF16),
                        pltpu.VMEM((W2, D_MODEL), BF16),
                        pltpu.SemaphoreType.DMA(()),
                        pltpu.SemaphoreType.DMA((Y_SLOTS,)),
                        pltpu.SemaphoreType.DMA((WEIGHT_SLOTS,))],
    )
    args = (item_e, item_row, item_nsub, item_nzero, item_live, item_src, sorted_tok, u_packed,
            w_gate_up, w_down, b_gate_up, b_down)
    return pl.pallas_call(
        functools.partial(_expert_kernel, n_ff, n_items, aliased),
        grid_spec=grid_spec,
        out_shape=jax.ShapeDtypeStruct((n_rows * ROW_SLABS, LANES), F32),
        input_output_aliases={len(args): 0} if aliased else {},
        compiler_params=_params(("arbitrary", "arbitrary")),
        name="experts_overflow" if aliased else "experts",
    )(*args, *((y_in,) if aliased else ()))


def _experts(u_packed, sorted_tok, n_rows, items, n_used, w_gate_up, b_gate_up, w_down, b_down):
    n_main = N_EXPERTS + 1
    weights = (w_gate_up, b_gate_up, w_down, b_down)
    y = _experts_call(u_packed, sorted_tok, n_rows, tuple(a[:n_main] for a in items), *weights)
    rest = tuple(a[n_main:] for a in items)
    return lax.cond(n_used > n_main,
                    lambda y_: _experts_call(u_packed, sorted_tok, n_rows, rest, *weights, y_in=y_),
                    lambda y_: y_, y)


def _combine_kernel(tm, n_steps, dest_ref, w_ref, y_hbm, h_ref, lnw_ref, o_ref, buf, osum, sem):
    i = pl.program_id(0)

    def copy(step, slot, t, k):
        d = dest_ref[(step * tm + t) * TOP_K + k]
        return pltpu.make_async_copy(y_hbm.at[d], buf.at[slot, k * tm + t], sem.at[slot])

    def issue(step, slot):
        def body(t, c):
            for k in range(TOP_K):
                copy(step, slot, t, k).start(priority=k % 2)
            return c
        lax.fori_loop(0, tm, body, 0, unroll=16)

    @pl.when(i == 0)
    def _():
        issue(0, 0)

    @pl.when(i + 1 < n_steps)
    def _():
        issue(i + 1, (i + 1) % 2)

    slot = i % 2

    pltpu.make_async_copy(y_hbm.at[pl.ds(0, TOP_K * tm)], buf.at[slot], sem.at[slot]).wait()

    def token(t, c):
        a = h_ref[t]
        for k in range(TOP_K):
            a = a + w_ref[(i * tm + t) * TOP_K + k] * buf[slot, k * tm + t]
        osum[pl.ds(pl.multiple_of(t * ROW_SLABS, ROW_SLABS), ROW_SLABS), :] = a
        return c
    lax.fori_loop(0, tm, token, 0, unroll=16)

    h = osum[...].reshape(tm, ROW_SLABS, LANES)
    ms = jnp.mean(jnp.mean(h * h, axis=2, keepdims=True), axis=1, keepdims=True)
    osum[...] = (h * lax.rsqrt(ms + EPS) * lnw_ref[...]).reshape(tm * ROW_SLABS, LANES)
    for s in range(ROW_SLABS):
        o_ref[:, s * LANES:(s + 1) * LANES] = osum[pl.ds(s, tm, stride=ROW_SLABS), :]


def _combine(dest, wflat, y3, h3, ln_w):
    T = h3.shape[0]
    tm = 128
    n_steps = T // tm
    grid_spec = pltpu.PrefetchScalarGridSpec(
        num_scalar_prefetch=2,
        grid=(n_steps,),
        in_specs=[pl.BlockSpec(memory_space=pl.ANY),
                  pl.BlockSpec((tm, ROW_SLABS, LANES), lambda i, d, w: (i, 0, 0)),
                  pl.BlockSpec((1, ROW_SLABS, LANES), lambda i, d, w: (0, 0, 0))],
        out_specs=pl.BlockSpec((tm, D_MODEL), lambda i, d, w: (i, 0)),
        scratch_shapes=[pltpu.VMEM((2, TOP_K * tm, ROW_SLABS, LANES), F32),
                        pltpu.VMEM((tm * ROW_SLABS, LANES), F32),
                        pltpu.SemaphoreType.DMA((2,))],
    )
    return pl.pallas_call(
        functools.partial(_combine_kernel, tm, n_steps),
        grid_spec=grid_spec,
        out_shape=jax.ShapeDtypeStruct((T, D_MODEL), F32),
        compiler_params=_params(("arbitrary",)),
        name="combine",
    )(dest, wflat, y3, h3, ln_w.reshape(1, ROW_SLABS, LANES))


def _route(top_idx, top_w, n_rows, n_items):
    T = top_idx.shape[0]
    e_flat = top_idx.reshape(-1).astype(jnp.int32)
    onehot = (e_flat[:, None] == jnp.arange(N_EXPERTS, dtype=jnp.int32)[None, :]).astype(jnp.int32)
    csum = jnp.cumsum(onehot, axis=0)
    counts = csum[-1]
    padded = (counts + ROW_BLK - 1) // ROW_BLK * ROW_BLK
    pend = jnp.cumsum(padded)
    pstart = pend - padded
    dest = jnp.sum(onehot * (pstart[None, :] + csum - onehot), axis=1).astype(jnp.int32)

    per_e = (padded + ITEM_ROWS - 1) // ITEM_ROWS
    iend = jnp.cumsum(per_e)
    istart = iend - per_e
    ii = jnp.arange(n_items, dtype=jnp.int32)
    total = iend[-1]
    live = (ii < total).astype(jnp.int32)
    ic = jnp.minimum(ii, total - 1)
    ie = jnp.minimum(jnp.sum(ic[:, None] >= iend[None, :], axis=1), N_EXPERTS - 1).astype(jnp.int32)
    within = ic - istart[ie]
    irow = (pstart[ie] + within * ITEM_ROWS).astype(jnp.int32)
    insub = jnp.minimum((padded[ie] - within * ITEM_ROWS) // ROW_BLK, ITEM_ROWS // ROW_BLK).astype(jnp.int32)
    insub = insub * live
    tail_rows = n_rows - pend[-1]
    tail = jnp.logical_and(ii == total, tail_rows > 0)
    inzero = jnp.where(tail, tail_rows // ROW_BLK, 0).astype(jnp.int32)
    irow = jnp.where(tail, pend[-1], irow).astype(jnp.int32)
    n_used = total + (tail_rows > 0).astype(jnp.int32)
    n_assign = T * TOP_K
    assert N_EXPERTS * n_assign < 2 ** 31
    order = jnp.sort(e_flat * n_assign + jnp.arange(n_assign, dtype=jnp.int32)) % n_assign
    sorted_tok = (order // TOP_K).astype(jnp.int32)
    cstart = jnp.cumsum(counts) - counts
    isrc = ((cstart[ie] + within * ITEM_ROWS) * live).astype(jnp.int32)
    return dest, top_w.reshape(-1).astype(F32), (ie, irow, insub, inzero, live, isrc), n_used, sorted_tok


def kernel(x, positions, ln_mix_w, w_in, conv_w, conv_b, dt_bias, a_log, d_skip, ssm_norm_w, w_out,
           ln_ffn_w, w_router, b_router, w_gate_up, b_gate_up, w_down, b_down, ln_final_w):
    B, L, _ = x.shape
    T = B * L
    assert B == 1 and T % 1024 == 0
    x2 = x.reshape(T, D_MODEL)
    half = RET_HEAD_DIM // 2
    inv_freq = (ROPE_BASE ** (-jnp.arange(half, dtype=F32) / half)).reshape(1, half)
    pos_col = positions.reshape(T, 1).astype(F32)

    proj = _inproj(x2, ln_mix_w[0], jnp.swapaxes(w_in[0], 0, 1))
    ret = _retention(proj, pos_col, inv_freq)
    ssm = _ssd(proj, conv_w[0], conv_b[0], dt_bias[0], a_log[0], d_skip[0], ssm_norm_w[0])
    h_slabs, u_packed, top_idx, top_w = _outproj(ret, ssm, w_out[0].astype(BF16), x2, ln_ffn_w[0], w_router[0],
                                                 b_router[0])

    n_rows = -(-(T * TOP_K + N_EXPERTS * (ROW_BLK - 1)) // ROW_BLK) * ROW_BLK
    n_items = N_EXPERTS + 1 + n_rows // ITEM_ROWS
    dest, wflat, items, n_used, sorted_tok = _route(top_idx.T, top_w.T, n_rows, n_items)

    y_rows = _experts(u_packed, sorted_tok, n_rows, items, n_used, w_gate_up[0], b_gate_up[0], w_down[0],
                      b_down[0])
    out = _combine(dest, wflat, y_rows.reshape(n_rows, ROW_SLABS, LANES),
                   h_slabs.reshape(T, ROW_SLABS, LANES), ln_final_w)
    return out.reshape(B, L, D_MODEL)
```
